```python
import jax, jax.numpy as jnp
from jax import lax
import numpy as np

D_MODEL = 1024
BATCH = 8
SEQ = 4096
DEPTH = 1

G_GROUPS = 8
G_WIDTH = 512
G_HEAD = G_WIDTH // G_GROUPS
CHUNK = 128
R_WIDTH = 512
R_HEAD = 64
R_HEADS = R_WIDTH // R_HEAD
DECAY_LORA = 32
AAA_LORA = 32
GATE_LORA = 96
D_FF = 4 * D_MODEL
ALPHA = (2.0 * DEPTH) ** 0.25
BETA = (8.0 * DEPTH) ** -0.25
LN_EPS = 1e-5
GN_EPS = 64e-5
R_SHIFT_COLS = 3 * R_WIDTH + DECAY_LORA + AAA_LORA + GATE_LORA
IN_COLS = 2 * G_WIDTH + R_SHIFT_COLS + 2 * D_MODEL

kernel_name = 'hybrid_gmlp_rwkv7_deepnorm_adaln'


def _layer_norm(x, g, b, eps):
    xf = x.astype(jnp.float32)
    mu = xf.mean(-1, keepdims=True)
    var = jnp.square(xf - mu).mean(-1, keepdims=True)
    return (xf - mu) * lax.rsqrt(var + eps) * g + b


def _gmlp_branch(z, g_ln_v, b_ln_v, w_spatial, b_spatial):
    B, S, _ = z.shape
    z = jax.nn.gelu(z)
    u, v = jnp.split(z, 2, axis=-1)
    v = _layer_norm(v, g_ln_v, b_ln_v, LN_EPS)
    v = v.reshape(B, S // CHUNK, CHUNK, G_GROUPS, G_HEAD)
    mask = jnp.tril(jnp.ones((CHUNK, CHUNK), dtype=bool))
    ws = jnp.where(mask[None], w_spatial, 0.0)
    s = jnp.einsum('gts,bcsgd->bctgd', ws, v) + b_spatial.T[:, :, None]
    return u * s.reshape(B, S, G_WIDTH)


def _rwkv7_branch(z, mu_shift, w0, w_decay_up, a0, w_aaa_up, w_gate_up,
                  k_k, k_a, r_k, gn_gain, gn_bias):
    B, S, _ = z.shape
    prev = jnp.pad(z, ((0, 0), (1, 0), (0, 0)))[:, :-1]
    z = z + (prev - z) * mu_shift
    i1 = R_WIDTH
    i2 = 2 * R_WIDTH
    i3 = 3 * R_WIDTH
    i4 = i3 + DECAY_LORA
    i5 = i4 + AAA_LORA
    r, k, v, xw, xa, xg = jnp.split(z, [i1, i2, i3, i4, i5], axis=-1)
    w_log = -jax.nn.softplus(-(w0 + jnp.tanh(xw) @ w_decay_up)) - 0.5
    decay = jnp.exp(-jnp.exp(w_log.astype(jnp.float32)))
    a = jax.nn.sigmoid(a0 + xa @ w_aaa_up)
    g = jax.nn.sigmoid(xg) @ w_gate_up

    def heads(t):
        return t.astype(jnp.float32).reshape(B, S, R_HEADS, R_HEAD)

    kk = heads(k * k_k)
    kk = kk / jnp.maximum(jnp.linalg.norm(kk, axis=-1, keepdims=True), 1e-12)
    k = k * (1.0 + (a - 1.0) * k_a)
    r_h, k_h, v_h, a_h = heads(r), heads(k), heads(v), heads(a)

    def tm(t):
        return jnp.transpose(t, (1, 0, 2, 3))

    def step(state, inp):
        rt, wt, kt, vt, at, bt = inp
        sa = jnp.einsum('bhij,bhj->bhi', state, at)
        state = (state * wt[:, :, None, :] + sa[..., None] * bt[:, :, None, :]
                 + vt[..., None] * kt[:, :, None, :])
        yt = jnp.einsum('bhij,bhj->bhi', state, rt)
        return state, yt

    s0 = jnp.zeros((B, R_HEADS, R_HEAD, R_HEAD), jnp.float32)
    _, y = lax.scan(step, s0, (tm(r_h), tm(heads(decay)), tm(k_h), tm(v_h),
                               tm(-kk), tm(kk * a_h)))
    y = jnp.transpose(y, (1, 0, 2, 3))
    mu = y.mean(-1, keepdims=True)
    var = jnp.square(y - mu).mean(-1, keepdims=True)
    y = ((y - mu) * lax.rsqrt(var + GN_EPS)).reshape(B, S, R_WIDTH) * gn_gain + gn_bias
    bonus = (r_h * k_h * r_k).sum(-1, keepdims=True) * v_h
    return (y + bonus.reshape(B, S, R_WIDTH)) * g


def _fwd_setup_inputs(seed: int = 0) -> dict:
    key = jax.random.key(seed)
    ks = jax.random.split(key, 36)
    L = DEPTH

    def nrm(k, shape, scale):
        return jax.random.normal(k, shape, jnp.float32) * scale

    w_in = nrm(ks[4], (L, D_MODEL, IN_COLS), D_MODEL ** -0.5)
    v_lo = 2 * G_WIDTH + 2 * R_WIDTH
    w_in = w_in.at[:, :, v_lo:v_lo + R_WIDTH].multiply(BETA)
    return {
        'x': nrm(ks[0], (BATCH, SEQ, D_MODEL), 1.0),
        'c': nrm(ks[1], (BATCH, D_MODEL), 1.0),
        'w_ada': nrm(ks[2], (L, D_MODEL, 6 * D_MODEL), 0.5 * D_MODEL ** -0.5),
        'b_ada': nrm(ks[3], (L, 6 * D_MODEL), 0.02),
        'w_in': w_in,
        'b_in': nrm(ks[5], (L, IN_COLS), 0.02),
        'g_ln_v': 1.0 + nrm(ks[6], (L, G_WIDTH), 0.05),
        'b_ln_v': nrm(ks[7], (L, G_WIDTH), 0.02),
        'w_spatial': nrm(ks[8], (L, G_GROUPS, CHUNK, CHUNK), CHUNK ** -0.5),
        'b_spatial': 1.0 + nrm(ks[9], (L, G_GROUPS, CHUNK), 0.1),
        'mu_shift': jax.random.uniform(ks[10], (L, R_SHIFT_COLS), jnp.float32),
        'w0': jax.random.uniform(ks[11], (L, R_WIDTH), jnp.float32, minval=-5.0, maxval=0.0),
        'w_decay_up': nrm(ks[12], (L, DECAY_LORA, R_WIDTH), DECAY_LORA ** -0.5),
        'a0': nrm(ks[13], (L, R_WIDTH), 0.1),
        'w_aaa_up': nrm(ks[14], (L, AAA_LORA, R_WIDTH), AAA_LORA ** -0.5),
        'w_gate_up': nrm(ks[15], (L, GATE_LORA, R_WIDTH), GATE_LORA ** -0.5),
        'k_k': 0.85 + nrm(ks[16], (L, R_WIDTH), 0.05),
        'k_a': 1.0 + nrm(ks[17], (L, R_WIDTH), 0.05),
        'r_k': nrm(ks[18], (L, R_HEADS, R_HEAD), 0.1),
        'gn_gain': 1.0 + nrm(ks[19], (L, R_WIDTH), 0.05),
        'gn_bias': nrm(ks[20], (L, R_WIDTH), 0.02),
        'w_branch_a': nrm(ks[21], (L, G_WIDTH, D_MODEL), BETA * G_WIDTH ** -0.5),
        'w_branch_b': nrm(ks[22], (L, R_WIDTH, D_MODEL), BETA * R_WIDTH ** -0.5),
        'w_out': nrm(ks[23], (L, D_MODEL, D_MODEL), BETA * D_MODEL ** -0.5),
        'b_out': nrm(ks[24], (L, D_MODEL), 0.01),
        'ln1_g': 1.0 + nrm(ks[25], (L, D_MODEL), 0.05),
        'ln1_b': nrm(ks[26], (L, D_MODEL), 0.02),
        'w_ff1': nrm(ks[27], (L, D_MODEL, D_FF), BETA * D_MODEL ** -0.5),
        'b_ff1': nrm(ks[28], (L, D_FF), 0.01),
        'w_ff2': nrm(ks[29], (L, D_FF, D_MODEL), BETA * D_FF ** -0.5),
        'b_ff2': nrm(ks[30], (L, D_MODEL), 0.01),
        'ln2_g': 1.0 + nrm(ks[31], (L, D_MODEL), 0.05),
        'ln2_b': nrm(ks[32], (L, D_MODEL), 0.02),
    }


def _fwd_reference(x, c, w_ada, b_ada, w_in, b_in, g_ln_v, b_ln_v, w_spatial, b_spatial,
              mu_shift, w0, w_decay_up, a0, w_aaa_up, w_gate_up, k_k, k_a, r_k,
              gn_gain, gn_bias, w_branch_a, w_branch_b, w_out, b_out, ln1_g, ln1_b,
              w_ff1, b_ff1, w_ff2, b_ff2, ln2_g, ln2_b):
    out_dtype = x.dtype
    h_res = x.astype(jnp.float32)
    c_act = jax.nn.silu(c.astype(jnp.float32))
    g_end = 2 * G_WIDTH
    r_end = g_end + R_SHIFT_COLS
    for l in range(DEPTH):
        mod = c_act @ w_ada[l] + b_ada[l]
        sh1, sc1, gt1, sh2, sc2, gt2 = [m[:, None, :] for m in jnp.split(mod, 6, axis=-1)]
        h = h_res * (1.0 + sc1) + sh1
        proj = h @ w_in[l] + b_in[l]
        y_a = _gmlp_branch(proj[..., :g_end], g_ln_v[l], b_ln_v[l], w_spatial[l], b_spatial[l])
        y_b = _rwkv7_branch(proj[..., g_end:r_end], mu_shift[l], w0[l], w_decay_up[l], a0[l],
                            w_aaa_up[l], w_gate_up[l], k_k[l], k_a[l], r_k[l],
                            gn_gain[l], gn_bias[l])
        gate_a, gate_b = jnp.split(proj[..., r_end:], 2, axis=-1)
        merged = (jax.nn.sigmoid(gate_a) * (y_a @ w_branch_a[l])
                  + jax.nn.sigmoid(gate_b) * (y_b @ w_branch_b[l]))
        mix = merged @ w_out[l] + b_out[l]
        h_res = _layer_norm(ALPHA * h_res + gt1 * mix, ln1_g[l], ln1_b[l], LN_EPS)
        h = h_res * (1.0 + sc2) + sh2
        ff = jnp.square(jax.nn.relu(h @ w_ff1[l] + b_ff1[l])) @ w_ff2[l] + b_ff2[l]
        h_res = _layer_norm(ALPHA * h_res + gt2 * ff, ln2_g[l], ln2_b[l], LN_EPS)
    return h_res.astype(out_dtype)


import jax as _jax
import jax.numpy as _jnp

TWIN_FORMAT = 'train_step'
FWD_PARAMS = ['x', 'c', 'w_ada', 'b_ada', 'w_in', 'b_in', 'g_ln_v', 'b_ln_v', 'w_spatial', 'b_spatial', 'mu_shift', 'w0', 'w_decay_up', 'a0', 'w_aaa_up', 'w_gate_up', 'k_k', 'k_a', 'r_k', 'gn_gain', 'gn_bias', 'w_branch_a', 'w_branch_b', 'w_out', 'b_out', 'ln1_g', 'ln1_b', 'w_ff1', 'b_ff1', 'w_ff2', 'b_ff2', 'ln2_g', 'ln2_b']
TWIN_WEIGHTS = ['w_ada', 'b_ada', 'w_in', 'b_in', 'g_ln_v', 'b_ln_v', 'w_spatial', 'b_spatial', 'mu_shift', 'w0', 'w_decay_up', 'a0', 'w_aaa_up', 'w_gate_up', 'k_k', 'k_a', 'r_k', 'gn_gain', 'gn_bias', 'w_branch_a', 'w_branch_b', 'w_out', 'b_out', 'ln1_g', 'ln1_b', 'w_ff1', 'b_ff1', 'w_ff2', 'b_ff2', 'ln2_g', 'ln2_b']
TWIN_DIFF_INPUT = 'x'
TWIN_INPUTS = ['x', 'c', 'w_ada', 'b_ada', 'w_in', 'b_in', 'g_ln_v', 'b_ln_v', 'w_spatial', 'b_spatial', 'mu_shift', 'w0', 'w_decay_up', 'a0', 'w_aaa_up', 'w_gate_up', 'k_k', 'k_a', 'r_k', 'gn_gain', 'gn_bias', 'w_branch_a', 'w_branch_b', 'w_out', 'b_out', 'ln1_g', 'ln1_b', 'w_ff1', 'b_ff1', 'w_ff2', 'b_ff2', 'ln2_g', 'ln2_b', 'loss_target', 'm_w_ada', 'm_b_ada', 'm_w_in', 'm_b_in', 'm_g_ln_v', 'm_b_ln_v', 'm_w_spatial', 'm_b_spatial', 'm_mu_shift', 'm_w0', 'm_w_decay_up', 'm_a0', 'm_w_aaa_up', 'm_w_gate_up', 'm_k_k', 'm_k_a', 'm_r_k', 'm_gn_gain', 'm_gn_bias', 'm_w_branch_a', 'm_w_branch_b', 'm_w_out', 'm_b_out', 'm_ln1_g', 'm_ln1_b', 'm_w_ff1', 'm_b_ff1', 'm_w_ff2', 'm_b_ff2', 'm_ln2_g', 'm_ln2_b', 'v_w_ada', 'v_b_ada', 'v_w_in', 'v_b_in', 'v_g_ln_v', 'v_b_ln_v', 'v_w_spatial', 'v_b_spatial', 'v_mu_shift', 'v_w0', 'v_w_decay_up', 'v_a0', 'v_w_aaa_up', 'v_w_gate_up', 'v_k_k', 'v_k_a', 'v_r_k', 'v_gn_gain', 'v_gn_bias', 'v_w_branch_a', 'v_w_branch_b', 'v_w_out', 'v_b_out', 'v_ln1_g', 'v_ln1_b', 'v_w_ff1', 'v_b_ff1', 'v_w_ff2', 'v_b_ff2', 'v_ln2_g', 'v_ln2_b']
TWIN_OUTPUTS = ['loss', 'grad_x', 'grad_w_ada', 'grad_b_ada', 'grad_w_in', 'grad_b_in', 'grad_g_ln_v', 'grad_b_ln_v', 'grad_w_spatial', 'grad_b_spatial', 'grad_mu_shift', 'grad_w0', 'grad_w_decay_up', 'grad_a0', 'grad_w_aaa_up', 'grad_w_gate_up', 'grad_k_k', 'grad_k_a', 'grad_r_k', 'grad_gn_gain', 'grad_gn_bias', 'grad_w_branch_a', 'grad_w_branch_b', 'grad_w_out', 'grad_b_out', 'grad_ln1_g', 'grad_ln1_b', 'grad_w_ff1', 'grad_b_ff1', 'grad_w_ff2', 'grad_b_ff2', 'grad_ln2_g', 'grad_ln2_b', 'delta_w_ada', 'delta_b_ada', 'delta_w_in', 'delta_b_in', 'delta_g_ln_v', 'delta_b_ln_v', 'delta_w_spatial', 'delta_b_spatial', 'delta_mu_shift', 'delta_w0', 'delta_w_decay_up', 'delta_a0', 'delta_w_aaa_up', 'delta_w_gate_up', 'delta_k_k', 'delta_k_a', 'delta_r_k', 'delta_gn_gain', 'delta_gn_bias', 'delta_w_branch_a', 'delta_w_branch_b', 'delta_w_out', 'delta_b_out', 'delta_ln1_g', 'delta_ln1_b', 'delta_w_ff1', 'delta_b_ff1', 'delta_w_ff2', 'delta_b_ff2', 'delta_ln2_g', 'delta_ln2_b', 'new_m_w_ada', 'new_m_b_ada', 'new_m_w_in', 'new_m_b_in', 'new_m_g_ln_v', 'new_m_b_ln_v', 'new_m_w_spatial', 'new_m_b_spatial', 'new_m_mu_shift', 'new_m_w0', 'new_m_w_decay_up', 'new_m_a0', 'new_m_w_aaa_up', 'new_m_w_gate_up', 'new_m_k_k', 'new_m_k_a', 'new_m_r_k', 'new_m_gn_gain', 'new_m_gn_bias', 'new_m_w_branch_a', 'new_m_w_branch_b', 'new_m_w_out', 'new_m_b_out', 'new_m_ln1_g', 'new_m_ln1_b', 'new_m_w_ff1', 'new_m_b_ff1', 'new_m_w_ff2', 'new_m_b_ff2', 'new_m_ln2_g', 'new_m_ln2_b', 'new_v_w_ada', 'new_v_b_ada', 'new_v_w_in', 'new_v_b_in', 'new_v_g_ln_v', 'new_v_b_ln_v', 'new_v_w_spatial', 'new_v_b_spatial', 'new_v_mu_shift', 'new_v_w0', 'new_v_w_decay_up', 'new_v_a0', 'new_v_w_aaa_up', 'new_v_w_gate_up', 'new_v_k_k', 'new_v_k_a', 'new_v_r_k', 'new_v_gn_gain', 'new_v_gn_bias', 'new_v_w_branch_a', 'new_v_w_branch_b', 'new_v_w_out', 'new_v_b_out', 'new_v_ln1_g', 'new_v_ln1_b', 'new_v_w_ff1', 'new_v_b_ff1', 'new_v_w_ff2', 'new_v_b_ff2', 'new_v_ln2_g', 'new_v_ln2_b']
TWIN_LEAF_KINDS = {'loss': 'loss', 'grad_x': 'grad_x', 'grad_w_ada': 'grad_w', 'grad_b_ada': 'grad_w', 'grad_w_in': 'grad_w', 'grad_b_in': 'grad_w', 'grad_g_ln_v': 'grad_w', 'grad_b_ln_v': 'grad_w', 'grad_w_spatial': 'grad_w', 'grad_b_spatial': 'grad_w', 'grad_mu_shift': 'grad_w', 'grad_w0': 'grad_w', 'grad_w_decay_up': 'grad_w', 'grad_a0': 'grad_w', 'grad_w_aaa_up': 'grad_w', 'grad_w_gate_up': 'grad_w', 'grad_k_k': 'grad_w', 'grad_k_a': 'grad_w', 'grad_r_k': 'grad_w', 'grad_gn_gain': 'grad_w', 'grad_gn_bias': 'grad_w', 'grad_w_branch_a': 'grad_w', 'grad_w_branch_b': 'grad_w', 'grad_w_out': 'grad_w', 'grad_b_out': 'grad_w', 'grad_ln1_g': 'grad_w', 'grad_ln1_b': 'grad_w', 'grad_w_ff1': 'grad_w', 'grad_b_ff1': 'grad_w', 'grad_w_ff2': 'grad_w', 'grad_b_ff2': 'grad_w', 'grad_ln2_g': 'grad_w', 'grad_ln2_b': 'grad_w', 'delta_w_ada': 'delta_w', 'delta_b_ada': 'delta_w', 'delta_w_in': 'delta_w', 'delta_b_in': 'delta_w', 'delta_g_ln_v': 'delta_w', 'delta_b_ln_v': 'delta_w', 'delta_w_spatial': 'delta_w', 'delta_b_spatial': 'delta_w', 'delta_mu_shift': 'delta_w', 'delta_w0': 'delta_w', 'delta_w_decay_up': 'delta_w', 'delta_a0': 'delta_w', 'delta_w_aaa_up': 'delta_w', 'delta_w_gate_up': 'delta_w', 'delta_k_k': 'delta_w', 'delta_k_a': 'delta_w', 'delta_r_k': 'delta_w', 'delta_gn_gain': 'delta_w', 'delta_gn_bias': 'delta_w', 'delta_w_branch_a': 'delta_w', 'delta_w_branch_b': 'delta_w', 'delta_w_out': 'delta_w', 'delta_b_out': 'delta_w', 'delta_ln1_g': 'delta_w', 'delta_ln1_b': 'delta_w', 'delta_w_ff1': 'delta_w', 'delta_b_ff1': 'delta_w', 'delta_w_ff2': 'delta_w', 'delta_b_ff2': 'delta_w', 'delta_ln2_g': 'delta_w', 'delta_ln2_b': 'delta_w', 'new_m_w_ada': 'new_m', 'new_m_b_ada': 'new_m', 'new_m_w_in': 'new_m', 'new_m_b_in': 'new_m', 'new_m_g_ln_v': 'new_m', 'new_m_b_ln_v': 'new_m', 'new_m_w_spatial': 'new_m', 'new_m_b_spatial': 'new_m', 'new_m_mu_shift': 'new_m', 'new_m_w0': 'new_m', 'new_m_w_decay_up': 'new_m', 'new_m_a0': 'new_m', 'new_m_w_aaa_up': 'new_m', 'new_m_w_gate_up': 'new_m', 'new_m_k_k': 'new_m', 'new_m_k_a': 'new_m', 'new_m_r_k': 'new_m', 'new_m_gn_gain': 'new_m', 'new_m_gn_bias': 'new_m', 'new_m_w_branch_a': 'new_m', 'new_m_w_branch_b': 'new_m', 'new_m_w_out': 'new_m', 'new_m_b_out': 'new_m', 'new_m_ln1_g': 'new_m', 'new_m_ln1_b': 'new_m', 'new_m_w_ff1': 'new_m', 'new_m_b_ff1': 'new_m', 'new_m_w_ff2': 'new_m', 'new_m_b_ff2': 'new_m', 'new_m_ln2_g': 'new_m', 'new_m_ln2_b': 'new_m', 'new_v_w_ada': 'new_v', 'new_v_b_ada': 'new_v', 'new_v_w_in': 'new_v', 'new_v_b_in': 'new_v', 'new_v_g_ln_v': 'new_v', 'new_v_b_ln_v': 'new_v', 'new_v_w_spatial': 'new_v', 'new_v_b_spatial': 'new_v', 'new_v_mu_shift': 'new_v', 'new_v_w0': 'new_v', 'new_v_w_decay_up': 'new_v', 'new_v_a0': 'new_v', 'new_v_w_aaa_up': 'new_v', 'new_v_w_gate_up': 'new_v', 'new_v_k_k': 'new_v', 'new_v_k_a': 'new_v', 'new_v_r_k': 'new_v', 'new_v_gn_gain': 'new_v', 'new_v_gn_bias': 'new_v', 'new_v_w_branch_a': 'new_v', 'new_v_w_branch_b': 'new_v', 'new_v_w_out': 'new_v', 'new_v_b_out': 'new_v', 'new_v_ln1_g': 'new_v', 'new_v_ln1_b': 'new_v', 'new_v_w_ff1': 'new_v', 'new_v_b_ff1': 'new_v', 'new_v_w_ff2': 'new_v', 'new_v_b_ff2': 'new_v', 'new_v_ln2_g': 'new_v', 'new_v_ln2_b': 'new_v'}


def _forward(args):
    return _fwd_reference(*[args[k] for k in FWD_PARAMS])


def _output_shape():
    out = _jax.eval_shape(lambda: _forward(_fwd_setup_inputs(0)))
    return out.shape, out.dtype

N_MICROBATCH = 1
ADAM_LR = 0.001
ADAM_B1 = 0.9
ADAM_B2 = 0.999
ADAM_EPS = 1e-08
ADAM_WD = 0.01
ADAM_STEP = 10
PER_EXAMPLE_BATCH_AXIS = {'x': 0, 'c': 0, 'loss_target': 0}
SHARED_INPUTS = []
_WEIGHT_DTYPES = {'w_ada': _jnp.float32, 'b_ada': _jnp.float32, 'w_in': _jnp.float32, 'b_in': _jnp.float32, 'g_ln_v': _jnp.float32, 'b_ln_v': _jnp.float32, 'w_spatial': _jnp.float32, 'b_spatial': _jnp.float32, 'mu_shift': _jnp.float32, 'w0': _jnp.float32, 'w_decay_up': _jnp.float32, 'a0': _jnp.float32, 'w_aaa_up': _jnp.float32, 'w_gate_up': _jnp.float32, 'k_k': _jnp.float32, 'k_a': _jnp.float32, 'r_k': _jnp.float32, 'gn_gain': _jnp.float32, 'gn_bias': _jnp.float32, 'w_branch_a': _jnp.float32, 'w_branch_b': _jnp.float32, 'w_out': _jnp.float32, 'b_out': _jnp.float32, 'ln1_g': _jnp.float32, 'ln1_b': _jnp.float32, 'w_ff1': _jnp.float32, 'b_ff1': _jnp.float32, 'w_ff2': _jnp.float32, 'b_ff2': _jnp.float32, 'ln2_g': _jnp.float32, 'ln2_b': _jnp.float32}
MOMENT_SCALE = {'w_ada': 1.747928e-02, 'b_ada': 3.197843e-02, 'w_in': 7.363021e-03, 'b_in': 9.720048e-03, 'g_ln_v': 6.888623e-03, 'b_ln_v': 6.659875e-03, 'w_spatial': 4.626042e-03, 'b_spatial': 6.506560e-03, 'mu_shift': 1.096086e-02, 'w0': 3.552003e-03, 'w_decay_up': 8.816080e-04, 'a0': 2.729813e-03, 'w_aaa_up': 2.558881e-03, 'w_gate_up': 6.942768e-03, 'k_k': 1.186454e-02, 'k_a': 1.038008e-02, 'r_k': 1.321703e-02, 'gn_gain': 7.164067e-03, 'gn_bias': 1.402517e-02, 'w_branch_a': 1.457407e-02, 'w_branch_b': 8.491587e-03, 'w_out': 1.671018e-02, 'b_out': 7.379964e-02, 'ln1_g': 2.958338e+00, 'ln1_b': 4.597670e-01, 'w_ff1': 1.381427e-02, 'b_ff1': 1.415667e-02, 'w_ff2': 2.640656e-02, 'b_ff2': 7.259239e-02, 'ln2_g': 3.220567e+01, 'ln2_b': 8.365663e-01}


def _to_microbatches(a, axis):
    t = _jnp.moveaxis(a, axis, 0)
    t = t.reshape((N_MICROBATCH, t.shape[0] // N_MICROBATCH) + t.shape[1:])
    return _jnp.moveaxis(t, 1, axis + 1)


def setup_inputs(seed: int = 0) -> dict:
    inp = _fwd_setup_inputs(seed)
    key = _jax.random.fold_in(_jax.random.key(seed), 7919)
    shape, _ = _output_shape()
    out = dict(inp)
    out["loss_target"] = _jax.random.normal(_jax.random.fold_in(key, 0), shape, _jnp.float32)
    for i, name in enumerate(TWIN_WEIGHTS):
        w = inp[name].astype(_jnp.float32)
        if MOMENT_SCALE is None:
            s = _jnp.sqrt(_jnp.mean(_jnp.square(w)) + 1e-30)
        else:
            s = MOMENT_SCALE[name]
        km, kv = _jax.random.split(_jax.random.fold_in(key, i + 1))
        out[name] = w
        out["m_" + name] = s * _jax.random.normal(km, w.shape, _jnp.float32)
        out["v_" + name] = (s * s) * _jax.random.uniform(kv, w.shape, _jnp.float32, 0.5, 1.5)
    if N_MICROBATCH > 1:
        for name, axis in PER_EXAMPLE_BATCH_AXIS.items():
            out[name] = _to_microbatches(out[name], axis)
    return {'x': out['x'], 'c': out['c'], 'w_ada': out['w_ada'], 'b_ada': out['b_ada'], 'w_in': out['w_in'], 'b_in': out['b_in'], 'g_ln_v': out['g_ln_v'], 'b_ln_v': out['b_ln_v'], 'w_spatial': out['w_spatial'], 'b_spatial': out['b_spatial'], 'mu_shift': out['mu_shift'], 'w0': out['w0'], 'w_decay_up': out['w_decay_up'], 'a0': out['a0'], 'w_aaa_up': out['w_aaa_up'], 'w_gate_up': out['w_gate_up'], 'k_k': out['k_k'], 'k_a': out['k_a'], 'r_k': out['r_k'], 'gn_gain': out['gn_gain'], 'gn_bias': out['gn_bias'], 'w_branch_a': out['w_branch_a'], 'w_branch_b': out['w_branch_b'], 'w_out': out['w_out'], 'b_out': out['b_out'], 'ln1_g': out['ln1_g'], 'ln1_b': out['ln1_b'], 'w_ff1': out['w_ff1'], 'b_ff1': out['b_ff1'], 'w_ff2': out['w_ff2'], 'b_ff2': out['b_ff2'], 'ln2_g': out['ln2_g'], 'ln2_b': out['ln2_b'], 'loss_target': out['loss_target'], 'm_w_ada': out['m_w_ada'], 'm_b_ada': out['m_b_ada'], 'm_w_in': out['m_w_in'], 'm_b_in': out['m_b_in'], 'm_g_ln_v': out['m_g_ln_v'], 'm_b_ln_v': out['m_b_ln_v'], 'm_w_spatial': out['m_w_spatial'], 'm_b_spatial': out['m_b_spatial'], 'm_mu_shift': out['m_mu_shift'], 'm_w0': out['m_w0'], 'm_w_decay_up': out['m_w_decay_up'], 'm_a0': out['m_a0'], 'm_w_aaa_up': out['m_w_aaa_up'], 'm_w_gate_up': out['m_w_gate_up'], 'm_k_k': out['m_k_k'], 'm_k_a': out['m_k_a'], 'm_r_k': out['m_r_k'], 'm_gn_gain': out['m_gn_gain'], 'm_gn_bias': out['m_gn_bias'], 'm_w_branch_a': out['m_w_branch_a'], 'm_w_branch_b': out['m_w_branch_b'], 'm_w_out': out['m_w_out'], 'm_b_out': out['m_b_out'], 'm_ln1_g': out['m_ln1_g'], 'm_ln1_b': out['m_ln1_b'], 'm_w_ff1': out['m_w_ff1'], 'm_b_ff1': out['m_b_ff1'], 'm_w_ff2': out['m_w_ff2'], 'm_b_ff2': out['m_b_ff2'], 'm_ln2_g': out['m_ln2_g'], 'm_ln2_b': out['m_ln2_b'], 'v_w_ada': out['v_w_ada'], 'v_b_ada': out['v_b_ada'], 'v_w_in': out['v_w_in'], 'v_b_in': out['v_b_in'], 'v_g_ln_v': out['v_g_ln_v'], 'v_b_ln_v': out['v_b_ln_v'], 'v_w_spatial': out['v_w_spatial'], 'v_b_spatial': out['v_b_spatial'], 'v_mu_shift': out['v_mu_shift'], 'v_w0': out['v_w0'], 'v_w_decay_up': out['v_w_decay_up'], 'v_a0': out['v_a0'], 'v_w_aaa_up': out['v_w_aaa_up'], 'v_w_gate_up': out['v_w_gate_up'], 'v_k_k': out['v_k_k'], 'v_k_a': out['v_k_a'], 'v_r_k': out['v_r_k'], 'v_gn_gain': out['v_gn_gain'], 'v_gn_bias': out['v_gn_bias'], 'v_w_branch_a': out['v_w_branch_a'], 'v_w_branch_b': out['v_w_branch_b'], 'v_w_out': out['v_w_out'], 'v_b_out': out['v_b_out'], 'v_ln1_g': out['v_ln1_g'], 'v_ln1_b': out['v_ln1_b'], 'v_w_ff1': out['v_w_ff1'], 'v_b_ff1': out['v_b_ff1'], 'v_w_ff2': out['v_w_ff2'], 'v_b_ff2': out['v_b_ff2'], 'v_ln2_g': out['v_ln2_g'], 'v_ln2_b': out['v_ln2_b']}


def _loss(weights, diff, rest, loss_target):
    with _jax.named_scope("forward"):
        args = {**rest, TWIN_DIFF_INPUT: diff, **{k: w.astype(_WEIGHT_DTYPES[k]) for k, w in weights.items()}}
        y = _forward(args)
    with _jax.named_scope("loss_head"):
        err = _jnp.square(y.astype(_jnp.float32) - loss_target)
        return 0.5 * _jnp.sum(_jnp.mean(err, axis=-1)) if err.ndim else 0.5 * err


def _adamw(w, g, m, v):
    m = ADAM_B1 * m + (1.0 - ADAM_B1) * g
    v = ADAM_B2 * v + (1.0 - ADAM_B2) * _jnp.square(g)
    m_hat = m / (1.0 - ADAM_B1 ** ADAM_STEP)
    v_hat = v / (1.0 - ADAM_B2 ** ADAM_STEP)
    delta = -ADAM_LR * (m_hat / (_jnp.sqrt(v_hat) + ADAM_EPS) + ADAM_WD * w)
    return delta, m, v


def reference(x, c, w_ada, b_ada, w_in, b_in, g_ln_v, b_ln_v, w_spatial, b_spatial, mu_shift, w0, w_decay_up, a0, w_aaa_up, w_gate_up, k_k, k_a, r_k, gn_gain, gn_bias, w_branch_a, w_branch_b, w_out, b_out, ln1_g, ln1_b, w_ff1, b_ff1, w_ff2, b_ff2, ln2_g, ln2_b, loss_target, m_w_ada, m_b_ada, m_w_in, m_b_in, m_g_ln_v, m_b_ln_v, m_w_spatial, m_b_spatial, m_mu_shift, m_w0, m_w_decay_up, m_a0, m_w_aaa_up, m_w_gate_up, m_k_k, m_k_a, m_r_k, m_gn_gain, m_gn_bias, m_w_branch_a, m_w_branch_b, m_w_out, m_b_out, m_ln1_g, m_ln1_b, m_w_ff1, m_b_ff1, m_w_ff2, m_b_ff2, m_ln2_g, m_ln2_b, v_w_ada, v_b_ada, v_w_in, v_b_in, v_g_ln_v, v_b_ln_v, v_w_spatial, v_b_spatial, v_mu_shift, v_w0, v_w_decay_up, v_a0, v_w_aaa_up, v_w_gate_up, v_k_k, v_k_a, v_r_k, v_gn_gain, v_gn_bias, v_w_branch_a, v_w_branch_b, v_w_out, v_b_out, v_ln1_g, v_ln1_b, v_w_ff1, v_b_ff1, v_w_ff2, v_b_ff2, v_ln2_g, v_ln2_b):
    given = dict(x=x, c=c, w_ada=w_ada, b_ada=b_ada, w_in=w_in, b_in=b_in, g_ln_v=g_ln_v, b_ln_v=b_ln_v, w_spatial=w_spatial, b_spatial=b_spatial, mu_shift=mu_shift, w0=w0, w_decay_up=w_decay_up, a0=a0, w_aaa_up=w_aaa_up, w_gate_up=w_gate_up, k_k=k_k, k_a=k_a, r_k=r_k, gn_gain=gn_gain, gn_bias=gn_bias, w_branch_a=w_branch_a, w_branch_b=w_branch_b, w_out=w_out, b_out=b_out, ln1_g=ln1_g, ln1_b=ln1_b, w_ff1=w_ff1, b_ff1=b_ff1, w_ff2=w_ff2, b_ff2=b_ff2, ln2_g=ln2_g, ln2_b=ln2_b, loss_target=loss_target, m_w_ada=m_w_ada, m_b_ada=m_b_ada, m_w_in=m_w_in, m_b_in=m_b_in, m_g_ln_v=m_g_ln_v, m_b_ln_v=m_b_ln_v, m_w_spatial=m_w_spatial, m_b_spatial=m_b_spatial, m_mu_shift=m_mu_shift, m_w0=m_w0, m_w_decay_up=m_w_decay_up, m_a0=m_a0, m_w_aaa_up=m_w_aaa_up, m_w_gate_up=m_w_gate_up, m_k_k=m_k_k, m_k_a=m_k_a, m_r_k=m_r_k, m_gn_gain=m_gn_gain, m_gn_bias=m_gn_bias, m_w_branch_a=m_w_branch_a, m_w_branch_b=m_w_branch_b, m_w_out=m_w_out, m_b_out=m_b_out, m_ln1_g=m_ln1_g, m_ln1_b=m_ln1_b, m_w_ff1=m_w_ff1, m_b_ff1=m_b_ff1, m_w_ff2=m_w_ff2, m_b_ff2=m_b_ff2, m_ln2_g=m_ln2_g, m_ln2_b=m_ln2_b, v_w_ada=v_w_ada, v_b_ada=v_b_ada, v_w_in=v_w_in, v_b_in=v_b_in, v_g_ln_v=v_g_ln_v, v_b_ln_v=v_b_ln_v, v_w_spatial=v_w_spatial, v_b_spatial=v_b_spatial, v_mu_shift=v_mu_shift, v_w0=v_w0, v_w_decay_up=v_w_decay_up, v_a0=v_a0, v_w_aaa_up=v_w_aaa_up, v_w_gate_up=v_w_gate_up, v_k_k=v_k_k, v_k_a=v_k_a, v_r_k=v_r_k, v_gn_gain=v_gn_gain, v_gn_bias=v_gn_bias, v_w_branch_a=v_w_branch_a, v_w_branch_b=v_w_branch_b, v_w_out=v_w_out, v_b_out=v_b_out, v_ln1_g=v_ln1_g, v_ln1_b=v_ln1_b, v_w_ff1=v_w_ff1, v_b_ff1=v_b_ff1, v_w_ff2=v_w_ff2, v_b_ff2=v_b_ff2, v_ln2_g=v_ln2_g, v_ln2_b=v_ln2_b)
    weights = {n: given[n] for n in TWIN_WEIGHTS}
    shared = {n: given[n] for n in SHARED_INPUTS}
    per_example = {n: given[n] for n in ['x', 'c']}
    grad_fn = _jax.value_and_grad(_loss, argnums=(0, 1))

    def one_microbatch(ex, loss_target):
        ex = dict(ex)
        diff = ex.pop(TWIN_DIFF_INPUT)
        return grad_fn(weights, diff, {**shared, **ex}, loss_target)

    if N_MICROBATCH == 1:
        loss, (grad_w, grad_x) = one_microbatch(per_example, given["loss_target"])
    else:
        def body(carry, xs):
            loss_sum, grad_sum = carry
            l_k, (gw_k, gx_k) = one_microbatch(xs[0], xs[1])
            with _jax.named_scope("update"):
                return (loss_sum + l_k, _jax.tree.map(_jnp.add, grad_sum, gw_k)), gx_k

        init = (_jnp.zeros((), _jnp.float32), _jax.tree.map(_jnp.zeros_like, weights))
        (loss, grad_w), grad_x = _jax.lax.scan(body, init, (per_example, given["loss_target"]))
    with _jax.named_scope("update"):
        delta_w, new_m, new_v = {}, {}, {}
        for n in TWIN_WEIGHTS:
            delta_w[n], new_m[n], new_v[n] = _adamw(weights[n], grad_w[n], given["m_" + n], given["v_" + n])
    return (loss, grad_x, *[grad_w[n] for n in TWIN_WEIGHTS], *[delta_w[n] for n in TWIN_WEIGHTS],
            *[new_m[n] for n in TWIN_WEIGHTS], *[new_v[n] for n in TWIN_WEIGHTS])
```

```python
import functools

import numpy as np
import jax
import jax.numpy as jnp
from jax import lax
from jax.experimental import pallas as pl
from jax.experimental.pallas import tpu as pltpu

F32 = jnp.float32
MXU_DT = jnp.bfloat16
WIRE_DT = jnp.float32

D_MODEL = 1024
G_GROUPS = 8
G_WIDTH = 512
CHUNK = 128
R_WIDTH = 512
R_HEAD = 64
R_HEADS = 8
DECAY_LORA = 32
AAA_LORA = 32
GATE_LORA = 96
LORA = DECAY_LORA + AAA_LORA + GATE_LORA
LORA_PAD = 256
R_COLS = 3 * R_WIDTH + LORA
R_COLS_PAD = 3 * R_WIDTH + LORA_PAD
D_FF = 4 * D_MODEL
ALPHA = 2.0 ** 0.25
LN_EPS = 1e-5
GN_EPS = 64e-5
ADAM_LR = 0.001
ADAM_B1 = 0.9
ADAM_B2 = 0.999
ADAM_EPS = 1e-08
ADAM_WD = 0.01
ADAM_STEP = 10

LANES = 128
PACK_W = 512
PACK_Q = 2 * 16 * PACK_W
VMEM_LIMIT = 48 * 1024 * 1024
SCAN_T = 64

MESH_AXES = ("x", "y", "c")


def _dg(a, b, dims):
    return lax.dot_general(a.astype(MXU_DT), b.astype(MXU_DT), (dims, ((), ())),
                           preferred_element_type=F32)


@jax.custom_vjp
def _bdot(a, b):
    return _dg(a, b, ((1,), (0,)))


def _bdot_fwd(a, b):
    return _bdot(a, b), (a, b)


def _bdot_bwd(res, g):
    a, b = res
    return (_dg(g, b, ((1,), (1,))).astype(a.dtype), _dg(a, g, ((0,), (0,))).astype(b.dtype))


_bdot.defvjp(_bdot_fwd, _bdot_bwd)


def _split_dot(x, m, dims):
    hi = x.astype(jnp.bfloat16)
    lo = (x - hi.astype(F32)).astype(jnp.bfloat16)
    dn = (dims, ((), ()))
    return (lax.dot_general(hi, m, dn, preferred_element_type=F32)
            + lax.dot_general(lo, m, dn, preferred_element_type=F32))


@jax.custom_vjp
def _pdot(x, m):
    return _split_dot(x, m, ((1,), (0,)))


def _pdot_fwd(x, m):
    return _pdot(x, m), m


def _pdot_bwd(m, g):
    return _split_dot(g, m, ((1,), (1,))), None


_pdot.defvjp(_pdot_fwd, _pdot_bwd)


def _sigmoid(x):
    return 1.0 / (1.0 + jnp.exp(-x))


def _softplus(x):
    return jnp.maximum(x, 0.0) + jnp.log(1.0 + jnp.exp(-jnp.maximum(x, -x)))


def _gelu(x):
    return 0.5 * x * (1.0 + jnp.tanh(0.7978845608028654 * (x + 0.044715 * (x * x * x))))


def _ln(x, g, b, eps):
    mu = jnp.mean(x, axis=-1, keepdims=True)
    xc = x - mu
    var = jnp.mean(xc * xc, axis=-1, keepdims=True)
    return xc * lax.rsqrt(var + eps) * g + b


def _colsum(x):
    return jnp.sum(x, axis=0, keepdims=True)


def _pick(n, target, q=LANES):
    if n <= target:
        return n
    best = None
    for t in range(q, target + 1, q):
        if n % t == 0:
            best = t
    assert best is not None, (n, target)
    return best


def _mm(name, a, b, mode, out_dtype=F32, tm=1024, tn=1024, tk=512):
    if mode == "nn":
        (M, K), (K2, N) = a.shape, b.shape
    elif mode == "nt":
        (M, K), (N, K2) = a.shape, b.shape
    else:
        (K, M), (K2, N) = a.shape, b.shape
    assert K == K2, (name, a.shape, b.shape, mode)
    tm, tn, tk = _pick(M, tm, 8 if M < LANES else LANES), _pick(N, tn), _pick(K, tk)
    nk = K // tk
    dims = {"nn": ((1,), (0,)), "nt": ((1,), (1,)), "tn": ((0,), (0,))}[mode]

    def body(a_ref, b_ref, o_ref, acc_ref):
        k = pl.program_id(2)

        @pl.when(k == 0)
        def _():
            acc_ref[...] = jnp.zeros(acc_ref.shape, F32)

        acc_ref[...] += _dg(a_ref[...], b_ref[...], dims)

        @pl.when(k == nk - 1)
        def _():
            o_ref[...] = acc_ref[...].astype(o_ref.dtype)

    if mode == "nn":
        a_spec = pl.BlockSpec((tm, tk), lambda i, j, k: (i, k))
        b_spec = pl.BlockSpec((tk, tn), lambda i, j, k: (k, j))
    elif mode == "nt":
        a_spec = pl.BlockSpec((tm, tk), lambda i, j, k: (i, k))
        b_spec = pl.BlockSpec((tn, tk), lambda i, j, k: (j, k))
    else:
        a_spec = pl.BlockSpec((tk, tm), lambda i, j, k: (k, i))
        b_spec = pl.BlockSpec((tk, tn), lambda i, j, k: (k, j))
    return pl.pallas_call(
        body, name=name, grid=(M // tm, N // tn, nk),
        in_specs=[a_spec, b_spec],
        out_specs=pl.BlockSpec((tm, tn), lambda i, j, k: (i, j)),
        out_shape=jax.ShapeDtypeStruct((M, N), out_dtype),
        scratch_shapes=[pltpu.VMEM((tm, tn), F32)],
        compiler_params=pltpu.CompilerParams(
            dimension_semantics=("parallel", "parallel", "arbitrary"), vmem_limit_bytes=VMEM_LIMIT),
    )(a, b)


def _rows(name, fn, rows, params, outs, accs=(), tile=256):
    S = rows[0].shape[0]
    tile = min(tile, S)
    assert S % tile == 0, (name, S, tile)
    nr, npar, no, na = len(rows), len(params), len(outs), len(accs)

    def body(*refs):
        rin, pin = refs[:nr], refs[nr:nr + npar]
        oref, aref = refs[nr + npar:nr + npar + no], refs[nr + npar + no:]
        res = fn(*[r[...] for r in rin], *[p[...] for p in pin])
        if not isinstance(res, (tuple, list)):
            res = (res,)
        assert len(res) == no + na, (name, len(res), no, na)
        for ref, val in zip(oref, res[:no]):
            ref[...] = val.astype(ref.dtype)
        if na:
            @pl.when(pl.program_id(0) == 0)
            def _():
                for ref in aref:
                    ref[...] = jnp.zeros(ref.shape, ref.dtype)

            for ref, val in zip(aref, res[no:]):
                ref[...] += jnp.broadcast_to(val, ref.shape).astype(ref.dtype)

    def whole(shape):
        nd = len(shape)
        return pl.BlockSpec(tuple(shape), lambda i: (0,) * nd)

    in_specs = ([pl.BlockSpec((tile, r.shape[1]), lambda i: (i, 0)) for r in rows]
                + [whole(p.shape) for p in params])
    out_specs = ([pl.BlockSpec((tile, n), lambda i: (i, 0)) for n, _ in outs]
                 + [whole(s) for s in accs])
    out_shape = ([jax.ShapeDtypeStruct((S, n), dt) for n, dt in outs]
                 + [jax.ShapeDtypeStruct(tuple(s), F32) for s in accs])
    res = pl.pallas_call(
        body, name=name, grid=(S // tile,), in_specs=in_specs, out_specs=out_specs,
        out_shape=out_shape,
        compiler_params=pltpu.CompilerParams(
            dimension_semantics=("arbitrary",), vmem_limit_bytes=VMEM_LIMIT),
    )(*rows, *params)
    return res


def _modulate(x, sc, sh):
    return x * (1.0 + sc) + sh


def _gmlp_consts():
    lane = lax.broadcasted_iota(jnp.int32, (1, G_WIDTH), 1)
    gmask = [(lane // (G_WIDTH // G_GROUPS) == g).astype(F32) for g in range(G_GROUPS)]
    tril = (lax.broadcasted_iota(jnp.int32, (CHUNK, CHUNK), 0)
            >= lax.broadcasted_iota(jnp.int32, (CHUNK, CHUNK), 1))
    return gmask, tril


def _gmlp_core(gmask, tril, gsel, zu, zv, bu, bv, g, b, bst, *ws):
    u = _gelu(zu + bu)
    v = _ln(_gelu(zv + bv), g, b, LN_EPS)
    s = _pdot(bst, gsel)
    for gi in range(G_GROUPS):
        s = s + _bdot(jnp.where(tril, ws[gi], 0.0), v * gmask[gi])
    return u * s


def _pre_core(rowmask, bd, zr, zk, zv, zl, pr, pk, pv, pq, br, bk, bv, bl, mr, mk, mv, ml,
              w0, wd, a0, wa, wg, k_k, k_a):
    def mix(z, p, b, mu):
        zz = z + b
        return zz + ((p + b) * rowmask - zz) * mu

    r, k, v, l = mix(zr, pr, br, mr), mix(zk, pk, bk, mk), mix(zv, pv, bv, mv), mix(zl, pq, bl, ml)
    w_log = -_softplus(-(w0 + _bdot(jnp.tanh(l), wd))) - 0.5
    decay = jnp.exp(-jnp.exp(w_log))
    a = _sigmoid(a0 + _bdot(l, wa))
    g = _bdot(_sigmoid(l), wg)
    kk = k * k_k
    kkn = kk / jnp.maximum(jnp.sqrt(_pdot(kk * kk, bd)), 1e-12)
    k2 = k * (1.0 + (a - 1.0) * k_a)
    return r, decay, k2, v, -kkn, kkn * a, g


def _post_core(bd, y, r, k2, v, g, gain, bias, rk):
    inv = 1.0 / R_HEAD
    mu = _pdot(y, bd) * inv
    yc = y - mu
    var = _pdot(yc * yc, bd) * inv
    yn = yc * lax.rsqrt(var + GN_EPS) * gain + bias
    bonus = _pdot(r * k2 * rk, bd) * v
    return (yn + bonus) * g


def _merge_core(pa, pb, ga, gb, bga, bgb):
    return _sigmoid(ga + bga) * pa + _sigmoid(gb + bgb) * pb


def _ln1_core(x, mix, gt1, bout, g, b, sc2, sh2):
    h1 = _ln(ALPHA * x + gt1 * (mix + bout), g, b, LN_EPS)
    return h1, h1 * (1.0 + sc2) + sh2


def _ln2_loss_core(tgt, h1, ff, gt2, bff2, g, b):
    out = _ln(ALPHA * h1 + gt2 * (ff + bff2), g, b, LN_EPS)
    err = out - tgt
    return 0.5 * jnp.sum(err * err) * (1.0 / D_MODEL)


def _scan_consts():
    sub = lax.broadcasted_iota(jnp.int32, (R_HEAD, LANES), 0)
    lane = lax.broadcasted_iota(jnp.int32, (R_HEAD, LANES), 1)
    return lane < R_HEAD, sub == (lane & (R_HEAD - 1))


def _seg_sum(lo, xb):
    s_lo = jnp.sum(jnp.where(lo, xb, 0.0), axis=1, keepdims=True)
    s_hi = jnp.sum(jnp.where(lo, 0.0, xb), axis=1, keepdims=True)
    return jnp.where(lo, s_lo, s_hi)


def _col_of_row(lo, eye, row):
    return _seg_sum(lo, jnp.where(eye, jnp.broadcast_to(row, eye.shape), 0.0))


def _row_of_col(eye, colb):
    return jnp.sum(jnp.where(eye, colb, 0.0), axis=0, keepdims=True)


N_BLK = R_WIDTH // LANES
ROW_GROUP = 8


def _scan_fwd(r, w, k, v, a, b):
    S = r.shape[0]
    T = min(SCAN_T, S)
    nchunk = S // T

    def body(r_ref, w_ref, k_ref, v_ref, a_ref, b_ref, y_ref, ck_ref, st_ref):
        lo, eye = _scan_consts()

        @pl.when(pl.program_id(0) == 0)
        def _():
            st_ref[...] = jnp.zeros(st_ref.shape, F32)

        ck_ref[0] = st_ref[...]
        sub8 = lax.broadcasted_iota(jnp.int32, (ROW_GROUP, LANES), 0)

        def group(gi, state):
            base = pl.multiple_of(gi * ROW_GROUP, ROW_GROUP)
            state = list(state)
            for q in range(N_BLK):
                sl = slice(q * LANES, (q + 1) * LANES)
                r8, w8, k8, v8, a8, b8 = [ref[pl.ds(base, ROW_GROUP), sl]
                                          for ref in (r_ref, w_ref, k_ref, v_ref, a_ref, b_ref)]
                y8 = jnp.zeros((ROW_GROUP, LANES), F32)
                s = state[q]
                for i in range(ROW_GROUP):
                    row = lambda t8: t8[i:i + 1, :]
                    sa = _seg_sum(lo, s * row(a8))
                    vb = _col_of_row(lo, eye, row(v8))
                    s = s * row(w8) + sa * row(b8) + vb * row(k8)
                    y8 = jnp.where(sub8 == i, _row_of_col(eye, _seg_sum(lo, s * row(r8))), y8)
                y_ref[pl.ds(base, ROW_GROUP), sl] = y8
                state[q] = s
            return tuple(state)

        init = tuple(st_ref[:, q * LANES:(q + 1) * LANES] for q in range(N_BLK))
        fin = lax.fori_loop(0, T // ROW_GROUP, group, init)
        for q in range(N_BLK):
            st_ref[:, q * LANES:(q + 1) * LANES] = fin[q]

    blk = pl.BlockSpec((T, R_WIDTH), lambda i: (i, 0))
    return pl.pallas_call(
        body, name="scan_fwd", grid=(nchunk,), in_specs=[blk] * 6,
        out_specs=[blk, pl.BlockSpec((1, R_HEAD, R_WIDTH), lambda i: (i, 0, 0))],
        out_shape=[jax.ShapeDtypeStruct((S, R_WIDTH), F32),
                   jax.ShapeDtypeStruct((nchunk, R_HEAD, R_WIDTH), F32)],
        scratch_shapes=[pltpu.VMEM((R_HEAD, R_WIDTH), F32)],
        compiler_params=pltpu.CompilerParams(
            dimension_semantics=("arbitrary",), vmem_limit_bytes=VMEM_LIMIT),
    )(r, w, k, v, a, b)


def _scan_bwd(r, w, k, v, a, b, ck, dy):
    S = r.shape[0]
    T = min(SCAN_T, S)
    nchunk = S // T

    def body(r_ref, w_ref, k_ref, v_ref, a_ref, b_ref, ck_ref, dy_ref,
             dr_ref, dw_ref, dk_ref, dv_ref, da_ref, db_ref, s_buf, sa_buf, vb_buf, ds_ref):
        lo, eye = _scan_consts()

        @pl.when(pl.program_id(0) == 0)
        def _():
            ds_ref[...] = jnp.zeros(ds_ref.shape, F32)

        sub8 = lax.broadcasted_iota(jnp.int32, (ROW_GROUP, LANES), 0)

        def fgroup(gi, state):
            base = pl.multiple_of(gi * ROW_GROUP, ROW_GROUP)
            state = list(state)
            for q in range(N_BLK):
                sl = slice(q * LANES, (q + 1) * LANES)
                w8, k8, v8, a8, b8 = [ref[pl.ds(base, ROW_GROUP), sl]
                                      for ref in (w_ref, k_ref, v_ref, a_ref, b_ref)]
                s = state[q]
                for i in range(ROW_GROUP):
                    row = lambda t8: t8[i:i + 1, :]
                    sa = _seg_sum(lo, s * row(a8))
                    vb = _col_of_row(lo, eye, row(v8))
                    s_buf[base + i, :, sl] = s
                    sa_buf[base + i, :, sl] = sa
                    vb_buf[base + i, :, sl] = vb
                    s = s * row(w8) + sa * row(b8) + vb * row(k8)
                state[q] = s
            return tuple(state)

        lax.fori_loop(0, T // ROW_GROUP, fgroup,
                      tuple(ck_ref[0, :, q * LANES:(q + 1) * LANES] for q in range(N_BLK)))

        def bgroup(n, dstate):
            base = pl.multiple_of((T // ROW_GROUP - 1 - n) * ROW_GROUP, ROW_GROUP)
            dstate = list(dstate)
            for q in range(N_BLK):
                sl = slice(q * LANES, (q + 1) * LANES)
                r8, w8, k8, a8, b8, dy8 = [ref[pl.ds(base, ROW_GROUP), sl]
                                           for ref in (r_ref, w_ref, k_ref, a_ref, b_ref, dy_ref)]
                acc = {n_: jnp.zeros((ROW_GROUP, LANES), F32) for n_ in ("r", "w", "k", "v", "a", "b")}
                ds = dstate[q]
                for i in reversed(range(ROW_GROUP)):
                    row = lambda t8: t8[i:i + 1, :]
                    put = lambda n_, val: acc.__setitem__(n_, jnp.where(sub8 == i, val, acc[n_]))
                    sp, sa, vb = s_buf[base + i, :, sl], sa_buf[base + i, :, sl], vb_buf[base + i, :, sl]
                    wr, ar, br, kr, rr = row(w8), row(a8), row(b8), row(k8), row(r8)
                    st = sp * wr + sa * br + vb * kr
                    dyc = _col_of_row(lo, eye, row(dy8))
                    ds = ds + dyc * rr
                    put("r", _colsum(st * dyc))
                    put("w", _colsum(ds * sp))
                    put("b", _colsum(ds * sa))
                    put("k", _colsum(ds * vb))
                    dsa = _seg_sum(lo, ds * br)
                    put("v", _row_of_col(eye, _seg_sum(lo, ds * kr)))
                    put("a", _colsum(sp * dsa))
                    ds = ds * wr + dsa * ar
                for n_, ref in (("r", dr_ref), ("w", dw_ref), ("k", dk_ref), ("v", dv_ref), ("a", da_ref), ("b", db_ref)):
                    ref[pl.ds(base, ROW_GROUP), sl] = acc[n_]
                dstate[q] = ds
            return tuple(dstate)

        fin = lax.fori_loop(0, T // ROW_GROUP, bgroup,
                            tuple(ds_ref[:, q * LANES:(q + 1) * LANES] for q in range(N_BLK)))
        for q in range(N_BLK):
            ds_ref[:, q * LANES:(q + 1) * LANES] = fin[q]

    blk = pl.BlockSpec((T, R_WIDTH), lambda i: (nchunk - 1 - i, 0))
    ckb = pl.BlockSpec((1, R_HEAD, R_WIDTH), lambda i: (nchunk - 1 - i, 0, 0))
    return pl.pallas_call(
        body, name="scan_bwd", grid=(nchunk,), in_specs=[blk] * 6 + [ckb, blk],
        out_specs=[blk] * 6,
        out_shape=[jax.ShapeDtypeStruct((S, R_WIDTH), F32)] * 6,
        scratch_shapes=[pltpu.VMEM((T, R_HEAD, R_WIDTH), F32)] * 3 + [pltpu.VMEM((R_HEAD, R_WIDTH), F32)],
        compiler_params=pltpu.CompilerParams(
            dimension_semantics=("arbitrary",), vmem_limit_bytes=VMEM_LIMIT),
    )(r, w, k, v, a, b, ck, dy)


def _pad_in_cols(t, axis):
    cut = 2 * G_WIDTH + R_COLS
    lo, hi = lax.slice_in_dim(t, 0, cut, axis=axis), lax.slice_in_dim(t, cut, t.shape[axis], axis=axis)
    zshape = list(t.shape)
    zshape[axis] = LORA_PAD - LORA
    return jnp.concatenate([lo, jnp.zeros(zshape, t.dtype), hi], axis=axis)


def _unpad_in_cols(t, axis):
    cut = 2 * G_WIDTH + R_COLS
    return jnp.concatenate([lax.slice_in_dim(t, 0, cut, axis=axis),
                            lax.slice_in_dim(t, cut + LORA_PAD - LORA, t.shape[axis], axis=axis)], axis=axis)


def _pad_rows(t, lo, n):
    return jnp.zeros((n, t.shape[1]), t.dtype).at[lo:lo + t.shape[0]].set(t)


def _shift_down(t):
    return jnp.concatenate([jnp.zeros((1, t.shape[1]), t.dtype), t[:-1]], axis=0)


def _shift_up(t):
    return jnp.concatenate([t[1:], jnp.zeros((1, t.shape[1]), t.dtype)], axis=0)


def _local_step(x, c, tgt, W):
    S = x.shape[0]
    bf = MXU_DT
    G = {}

    hl = np.arange(R_WIDTH) // R_HEAD
    bd = jnp.asarray(hl[:, None] == hl[None, :], jnp.bfloat16)
    gsel = jnp.asarray(np.arange(LANES)[:, None] == (np.arange(G_WIDTH) // (G_WIDTH // G_GROUPS))[None, :],
                       jnp.bfloat16)
    w_in_p = _pad_in_cols(W["w_in"], 1)
    b_in_p = _pad_in_cols(W["b_in"], 1)
    c_g, c_r = 2 * G_WIDTH, 2 * G_WIDTH + R_COLS_PAD
    w_g, w_r, w_gate = w_in_p[:, :c_g], w_in_p[:, c_g:c_r], w_in_p[:, c_r:]
    b_g, b_r, b_gate = b_in_p[:, :c_g], b_in_p[:, c_g:c_r], b_in_p[:, c_r:]
    mu_p = jnp.concatenate([W["mu_shift"], jnp.zeros((1, LORA_PAD - LORA), F32)], axis=1)
    wd_p = _pad_rows(W["w_decay_up"].astype(F32), 0, LORA_PAD)
    wa_p = _pad_rows(W["w_aaa_up"].astype(F32), DECAY_LORA, LORA_PAD)
    wg_p = _pad_rows(W["w_gate_up"].astype(F32), DECAY_LORA + AAA_LORA, LORA_PAD)
    ws2 = W["w_spatial"].reshape(G_GROUPS * CHUNK, CHUNK)
    bst = jnp.zeros((CHUNK, LANES), F32).at[:, :G_GROUPS].set(W["b_spatial"].T)
    rk = W["r_k"].reshape(1, R_WIDTH)

    c8 = jnp.broadcast_to(c, (8, D_MODEL))
    (ca8,) = _rows("ada_silu", lambda cc: cc * _sigmoid(cc), [c8], [], [(D_MODEL, bf)], tile=8)
    mod_raw = _mm("ada_mm", ca8, W["w_ada"], "nn")
    (mod8,) = _rows("ada_bias", lambda m, bb: m + bb, [mod_raw], [W["b_ada"]], [(6 * D_MODEL, F32)], tile=8)
    sh1, sc1, gt1, sh2, sc2, gt2 = [mod8[0:1, i * D_MODEL:(i + 1) * D_MODEL] for i in range(6)]

    (h,) = _rows("mod1", _modulate, [x], [sc1, sh1], [(D_MODEL, bf)])
    proj_g = _mm("proj_g", h, w_g, "nn")
    proj_r = _mm("proj_r", h, w_r, "nn")
    proj_gate = _mm("proj_gate", h, w_gate, "nn")

    def split2(t):
        return t[:, :G_WIDTH], t[:, G_WIDTH:]

    def gmlp_fwd(z, bz, g, b, bst_, gsel_, ws_):
        gmask, tril = _gmlp_consts()
        (zu, zv), (bu, bv) = split2(z), split2(bz)
        wsl = [ws_[i * CHUNK:(i + 1) * CHUNK] for i in range(G_GROUPS)]
        return _gmlp_core(gmask, tril, gsel_, zu, zv, bu, bv, g, b, bst_, *wsl)

    (y_a,) = _rows("gmlp_fwd", gmlp_fwd, [proj_g], [b_g, W["g_ln_v"], W["b_ln_v"], bst, gsel, ws2],
                   [(G_WIDTH, bf)], tile=CHUNK)

    r_cuts = (0, R_WIDTH, 2 * R_WIDTH, 3 * R_WIDTH, R_COLS_PAD)

    def split4(t):
        return [t[:, r_cuts[i]:r_cuts[i + 1]] for i in range(4)]

    tile_pre = min(256, S)

    def rowmask_of():
        grow = pl.program_id(0) * tile_pre + lax.broadcasted_iota(jnp.int32, (tile_pre, 1), 0)
        return (grow > 0).astype(F32)

    pre_params = [b_r, mu_p, W["w0"], wd_p, W["a0"], wa_p, wg_p, W["k_k"], W["k_a"], bd]

    def pre_fwd(z, p, bz, mu, w0, wd, a0, wa, wg, k_k, k_a, bd_):
        return _pre_core(rowmask_of(), bd_, *split4(z), *split4(p), *split4(bz), *split4(mu),
                         w0, wd, a0, wa, wg, k_k, k_a)

    proj_r_prev = _shift_down(proj_r)
    s_r, s_w, s_k, s_v, s_a, s_b, s_g = _rows(
        "rwkv_pre_fwd", pre_fwd, [proj_r, proj_r_prev], pre_params, [(R_WIDTH, F32)] * 7, tile=tile_pre)
    y_scan, ckpt = _scan_fwd(s_r, s_w, s_k, s_v, s_a, s_b)

    def post_fwd(y, r, k2, v, g, gain, bias, rk_, bd_):
        return _post_core(bd_, y, r, k2, v, g, gain, bias, rk_)

    post_params = [W["gn_gain"], W["gn_bias"], rk, bd]
    (y_b,) = _rows("rwkv_post_fwd", post_fwd, [y_scan, s_r, s_k, s_v, s_g], post_params, [(R_WIDTH, bf)])
    p_a = _mm("branch_a", y_a, W["w_branch_a"], "nn")
    p_b = _mm("branch_b", y_b, W["w_branch_b"], "nn")

    def merge_fwd(pa, pb, gz, bgz):
        return _merge_core(pa, pb, gz[:, :D_MODEL], gz[:, D_MODEL:], bgz[:, :D_MODEL], bgz[:, D_MODEL:])

    (merged,) = _rows("merge_fwd", merge_fwd, [p_a, p_b, proj_gate], [b_gate], [(D_MODEL, bf)])
    mix = _mm("out_proj", merged, W["w_out"], "nn")
    ln1_params = [gt1, W["b_out"], W["ln1_g"], W["ln1_b"], sc2, sh2]
    h1, h2 = _rows("ln1_fwd", _ln1_core, [x, mix], ln1_params, [(D_MODEL, F32), (D_MODEL, bf)])

    a1 = _mm("ff1", h2, W["w_ff1"], "nn")
    (act,) = _rows("ff_act", lambda z, bb: jnp.square(jnp.maximum(z + bb, 0.0)), [a1], [W["b_ff1"]], [(D_FF, bf)])
    ff = _mm("ff2", act, W["w_ff2"], "nn")

    def ln2_loss(h1_, ff_, tg, gt2_, bff2, g, b):
        loss, vjp = jax.vjp(functools.partial(_ln2_loss_core, tg), h1_, ff_, gt2_, bff2, g, b)
        return vjp(jnp.ones((), F32)) + (loss,)

    ln2_params = [gt2, W["b_ff2"], W["ln2_g"], W["ln2_b"]]
    dh1, dff, dgt2, G["b_ff2"], G["ln2_g"], G["ln2_b"], loss_acc = _rows(
        "ln2_loss", ln2_loss, [h1, ff, tgt], ln2_params, [(D_MODEL, F32), (D_MODEL, bf)],
        accs=[(1, D_MODEL)] * 4 + [(1, LANES)])
    loss = loss_acc[0, 0]

    dact = _mm("ff2_dx", dff, W["w_ff2"], "nt")
    G["w_ff2"] = _mm("ff2_dw", act, dff, "tn")

    def act_bwd(z, da, bb):
        d = da * 2.0 * jnp.maximum(z + bb, 0.0)
        return d, _colsum(d)

    da1, G["b_ff1"] = _rows("ff_act_bwd", act_bwd, [a1, dact], [W["b_ff1"]], [(D_FF, bf)], accs=[(1, D_FF)])
    dh2 = _mm("ff1_dx", da1, W["w_ff1"], "nt")
    G["w_ff1"] = _mm("ff1_dw", h2, da1, "tn")

    def ln1_bwd(x_, mix_, dh1_, dh2_, *ps):
        _, vjp = jax.vjp(_ln1_core, x_, mix_, *ps)
        return vjp((dh1_, dh2_))

    dx_res, dmix, dgt1, G["b_out"], G["ln1_g"], G["ln1_b"], dsc2, dsh2 = _rows(
        "ln1_bwd", ln1_bwd, [x, mix, dh1, dh2], ln1_params, [(D_MODEL, F32), (D_MODEL, bf)],
        accs=[(1, D_MODEL)] * 6)

    dmerged = _mm("out_proj_dx", dmix, W["w_out"], "nt")
    G["w_out"] = _mm("out_proj_dw", merged, dmix, "tn")

    def merge_bwd(pa, pb, gz, dm, bgz):
        args = (pa.astype(F32), pb.astype(F32), gz[:, :D_MODEL], gz[:, D_MODEL:], bgz[:, :D_MODEL], bgz[:, D_MODEL:])
        _, vjp = jax.vjp(_merge_core, *args)
        dpa, dpb, dga, dgb, dbga, dbgb = vjp(dm)
        return dpa, dpb, jnp.concatenate([dga, dgb], axis=1), jnp.concatenate([dbga, dbgb], axis=1)

    dp_a, dp_b, dgates, db_gate = _rows(
        "merge_bwd", merge_bwd, [p_a, p_b, proj_gate, dmerged], [b_gate],
        [(D_MODEL, bf), (D_MODEL, bf), (2 * D_MODEL, F32)], accs=[(1, 2 * D_MODEL)])
    dy_a = _mm("branch_a_dx", dp_a, W["w_branch_a"], "nt")
    G["w_branch_a"] = _mm("branch_a_dw", y_a, dp_a, "tn")
    dy_b = _mm("branch_b_dx", dp_b, W["w_branch_b"], "nt")
    G["w_branch_b"] = _mm("branch_b_dw", y_b, dp_b, "tn")

    def post_bwd(y, r, k2, v, g, dyb, gain, bias, rk_, bd_):
        _, vjp = jax.vjp(functools.partial(_post_core, bd_), y, r, k2, v, g, gain, bias, rk_)
        return vjp(dyb)

    dy_scan, dr_p, dk_p, dv_p, dg_p, G["gn_gain"], G["gn_bias"], drk = _rows(
        "rwkv_post_bwd", post_bwd, [y_scan, s_r, s_k, s_v, s_g, dy_b], post_params,
        [(R_WIDTH, F32)] * 5, accs=[(1, R_WIDTH)] * 3)
    G["r_k"] = drk.reshape(R_HEADS, R_HEAD)
    dr_s, dw_s, dk_s, dv_s, da_s, db_s = _scan_bwd(s_r, s_w, s_k, s_v, s_a, s_b, ckpt, dy_scan)

    def pre_bwd(z, p, dr1, dr2, dw, dk1, dk2, dv1, dv2, da, db, dg,
                bz, mu, w0, wd, a0, wa, wg, k_k, k_a, bd_):
        prim = (*split4(z), *split4(p), *split4(bz), *split4(mu), w0, wd, a0, wa, wg, k_k, k_a)
        _, vjp = jax.vjp(functools.partial(_pre_core, rowmask_of(), bd_), *prim)
        d = vjp((dr1 + dr2, dw, dk1 + dk2, dv1 + dv2, da, db, dg))
        cat = lambda parts: jnp.concatenate(parts, axis=1)
        return (cat(d[0:4]), cat(d[4:8]), cat(d[8:12]), cat(d[12:16])) + tuple(d[16:])

    dz_r, dprev, db_r, dmu_p, G["w0"], dwd_p, G["a0"], dwa_p, dwg_p, G["k_k"], G["k_a"] = _rows(
        "rwkv_pre_bwd", pre_bwd,
        [proj_r, proj_r_prev, dr_s, dr_p, dw_s, dk_s, dk_p, dv_s, dv_p, da_s, db_s, dg_p],
        pre_params, [(R_COLS_PAD, F32)] * 2,
        accs=[(1, R_COLS_PAD), (1, R_COLS_PAD), (1, R_WIDTH), (LORA_PAD, R_WIDTH), (1, R_WIDTH),
              (LORA_PAD, R_WIDTH), (LORA_PAD, R_WIDTH), (1, R_WIDTH), (1, R_WIDTH)],
        tile=tile_pre)
    G["mu_shift"] = dmu_p[:, :R_COLS]
    G["w_decay_up"] = dwd_p[:DECAY_LORA]
    G["w_aaa_up"] = dwa_p[DECAY_LORA:DECAY_LORA + AAA_LORA]
    G["w_gate_up"] = dwg_p[DECAY_LORA + AAA_LORA:LORA]

    def gmlp_bwd(z, dya, bz, g, b, bst_, gsel_, ws_):
        gmask, tril = _gmlp_consts()
        (zu, zv), (bu, bv) = split2(z), split2(bz)
        wsl = [ws_[i * CHUNK:(i + 1) * CHUNK] for i in range(G_GROUPS)]
        _, vjp = jax.vjp(functools.partial(_gmlp_core, gmask, tril, gsel_), zu, zv, bu, bv, g, b, bst_, *wsl)
        d = vjp(dya)
        return (jnp.concatenate(d[0:2], axis=1), jnp.concatenate(d[2:4], axis=1), d[4], d[5], d[6],
                jnp.concatenate(d[7:], axis=0))

    dz_g, db_g, G["g_ln_v"], G["b_ln_v"], dbst, dws2 = _rows(
        "gmlp_bwd", gmlp_bwd, [proj_g, dy_a], [b_g, W["g_ln_v"], W["b_ln_v"], bst, gsel, ws2],
        [(2 * G_WIDTH, F32)],
        accs=[(1, 2 * G_WIDTH), (1, G_WIDTH), (1, G_WIDTH), (CHUNK, LANES), (G_GROUPS * CHUNK, CHUNK)],
        tile=CHUNK)
    G["w_spatial"] = dws2.reshape(G_GROUPS, CHUNK, CHUNK)
    G["b_spatial"] = dbst[:, :G_GROUPS].T

    def dproj_cat(dzg, dzr, dpv, dgz):
        return jnp.concatenate([dzg, dzr + dpv, dgz], axis=1)

    (dproj,) = _rows("dproj_cat", dproj_cat, [dz_g, dz_r, _shift_up(dprev), dgates], [],
                     [(2 * G_WIDTH + R_COLS_PAD + 2 * D_MODEL, bf)])
    dh = _mm("proj_dx", dproj, w_in_p, "nt")
    G["w_in"] = _unpad_in_cols(_mm("proj_dw", h, dproj, "tn"), 1)
    G["b_in"] = _unpad_in_cols(jnp.concatenate([db_g, db_r, db_gate], axis=1), 1)

    def mod1_bwd(x_, dh_, dxr, sc):
        return dh_ * (1.0 + sc) + dxr, _colsum(dh_ * x_), _colsum(dh_)

    grad_x, dsc1, dsh1 = _rows("mod1_bwd", mod1_bwd, [x, dh, dx_res], [sc1], [(D_MODEL, F32)],
                               accs=[(1, D_MODEL)] * 2)

    dmod = jnp.concatenate([dsh1, dsc1, dgt1, dsh2, dsc2, dgt2], axis=1)
    G["b_ada"] = dmod
    (G["w_ada"],) = _rows("ada_dw", lambda cc, dm: (cc * _sigmoid(cc)) * dm, [c.reshape(D_MODEL, 1)], [dmod],
                          [(6 * D_MODEL, F32)], tile=128)
    return loss, grad_x, G


SHARDED = (("w_ada", (D_MODEL, 6 * D_MODEL), 1), ("w_in", (D_MODEL, 2 * G_WIDTH + R_COLS + 2 * D_MODEL), 1),
           ("w_decay_up", (DECAY_LORA, R_WIDTH), 1), ("w_aaa_up", (AAA_LORA, R_WIDTH), 1),
           ("w_gate_up", (GATE_LORA, R_WIDTH), 1), ("w_branch_a", (G_WIDTH, D_MODEL), 1),
           ("w_branch_b", (R_WIDTH, D_MODEL), 1), ("w_out", (D_MODEL, D_MODEL), 0),
           ("w_ff1", (D_MODEL, D_FF), 1), ("w_ff2", (D_FF, D_MODEL), 0))
SMALL = (("b_ada", (1, 6 * D_MODEL)), ("b_in", (1, 2 * G_WIDTH + R_COLS + 2 * D_MODEL)),
         ("g_ln_v", (1, G_WIDTH)), ("b_ln_v", (1, G_WIDTH)), ("w_spatial", (G_GROUPS, CHUNK, CHUNK)),
         ("b_spatial", (G_GROUPS, CHUNK)), ("mu_shift", (1, R_COLS)), ("w0", (1, R_WIDTH)),
         ("a0", (1, R_WIDTH)), ("k_k", (1, R_WIDTH)), ("k_a", (1, R_WIDTH)), ("r_k", (R_HEADS, R_HEAD)),
         ("gn_gain", (1, R_WIDTH)), ("gn_bias", (1, R_WIDTH)), ("b_out", (1, D_MODEL)),
         ("ln1_g", (1, D_MODEL)), ("ln1_b", (1, D_MODEL)), ("b_ff1", (1, D_FF)), ("b_ff2", (1, D_MODEL)),
         ("ln2_g", (1, D_MODEL)), ("ln2_b", (1, D_MODEL)))
WEIGHT_ORDER = ("w_ada", "b_ada", "w_in", "b_in", "g_ln_v", "b_ln_v", "w_spatial", "b_spatial", "mu_shift",
                "w0", "w_decay_up", "a0", "w_aaa_up", "w_gate_up", "k_k", "k_a", "r_k", "gn_gain", "gn_bias",
                "w_branch_a", "w_branch_b", "w_out", "b_out", "ln1_g", "ln1_b", "w_ff1", "b_ff1", "w_ff2",
                "b_ff2", "ln2_g", "ln2_b")
N_CHIPS = 4


def _shard_shape(shape, axis):
    s = list(shape)
    s[axis] //= N_CHIPS
    return tuple(s)


def _numel(shape):
    return int(np.prod(shape))


def _round_up(n, q):
    return -(-n // q) * q


N_SHARDED = sum(_numel(_shard_shape(s, a)) for _, s, a in SHARDED)
N_SMALL = sum(_numel(s) for _, s in SMALL)
ROWS_W = _round_up(N_SHARDED, PACK_Q) // PACK_W
ROWS_G = _round_up(N_SHARDED + N_SMALL, PACK_Q) // PACK_W


def _pack_rows(parts, rows, dtype):
    flat = jnp.concatenate([p.reshape(-1).astype(dtype) for p in parts])
    flat = jnp.concatenate([flat, jnp.zeros((rows * PACK_W - flat.shape[0],), dtype)])
    return flat.reshape(rows, PACK_W)


def _pack_local(shards, small, rows, dtype):
    parts = [shards[n] for n, _, _ in SHARDED] + ([small[n] for n, _ in SMALL] if small is not None else [])
    return _pack_rows(parts, rows, dtype)


def _unpack_local(pack, with_small):
    flat = pack.reshape(-1)
    out, off = {}, 0
    for n, s, a in SHARDED:
        ss = _shard_shape(s, a)
        out[n] = flat[off:off + _numel(ss)].reshape(ss)
        off += _numel(ss)
    if with_small:
        for n, s in SMALL:
            out[n] = flat[off:off + _numel(s)].reshape(s)
            off += _numel(s)
    return out


def _unpack_gathered(gath):
    flat = gath.reshape(N_CHIPS, -1)
    out, off = {}, 0
    for n, s, a in SHARDED:
        ss = _shard_shape(s, a)
        piece = flat[:, off:off + _numel(ss)].reshape((N_CHIPS,) + ss)
        off += _numel(ss)
        if a == 0:
            out[n] = piece.reshape(s)
        else:
            out[n] = jnp.transpose(piece, (1, 0, 2)).reshape(s)
    return out


def _pack_grads(G):
    cols = []
    for n, s, a in SHARDED:
        g = G[n].astype(WIRE_DT)
        ss = _shard_shape(s, a)
        if a == 0:
            cols.append(g.reshape(N_CHIPS, -1))
        else:
            cols.append(jnp.transpose(g.reshape(s[0], N_CHIPS, ss[1]), (1, 0, 2)).reshape(N_CHIPS, -1))
    for n, s in SMALL:
        cols.append(jnp.broadcast_to(G[n].astype(WIRE_DT).reshape(1, -1), (N_CHIPS, _numel(s))))
    flat = jnp.concatenate(cols, axis=1)
    flat = jnp.concatenate([flat, jnp.zeros((N_CHIPS, ROWS_G * PACK_W - flat.shape[1]), WIRE_DT)], axis=1)
    return flat.reshape(N_CHIPS, ROWS_G, PACK_W)


ANY = pl.BlockSpec(memory_space=pl.ANY)
MESH = pl.DeviceIdType.MESH


def _place():
    x, y, c = lax.axis_index("x"), lax.axis_index("y"), lax.axis_index("c")
    chips = [(1 - x, y), (x, 1 - y), (1 - x, 1 - y)]
    return x, y, c, chips


def _remote(src, dst, send_sem, recv_sem, to):
    return pltpu.make_async_remote_copy(src_ref=src, dst_ref=dst, send_sem=send_sem, recv_sem=recv_sem,
                                        device_id=to, device_id_type=MESH)


def _all_gather_weights(pack):
    rows = pack.shape[0]
    H = rows // 2

    def body(p_ref, out_ref, send_sems, recv_sems, local_sems):
        x, y, c, chips = _place()
        s = 2 * x + y
        sibling = (x, y, 1 - c)
        own = [pltpu.make_async_copy(p_ref.at[pl.ds(hf * H, H)], out_ref.at[2 * s + hf], local_sems.at[hf])
               for hf in range(2)]
        for cp in own:
            cp.start()
        mine = p_ref.at[pl.ds(c * H, H)]
        first = [_remote(mine, out_ref.at[2 * s + c], send_sems.at[j], recv_sems.at[j], (*chip, c))
                 for j, chip in enumerate(chips)]
        for cp in first:
            cp.start()
        landed = [out_ref.at[2 * (2 * chip[0] + chip[1]) + c] for chip in chips]
        passed = [_remote(landed[j], landed[j], send_sems.at[3 + j], recv_sems.at[3 + j], sibling)
                  for j in range(3)]
        for j in range(3):
            _remote(mine, landed[j], send_sems.at[j], recv_sems.at[j], sibling).wait_recv()
            passed[j].start()
        for j, chip in enumerate(chips):
            got = out_ref.at[2 * (2 * chip[0] + chip[1]) + 1 - c]
            _remote(mine, got, send_sems.at[3 + j], recv_sems.at[3 + j], sibling).wait_recv()
        for cp in first + passed:
            cp.wait_send()
        for cp in own:
            cp.wait()

    return pl.pallas_call(
        body, name="ag_weights", in_specs=[ANY], out_specs=ANY,
        out_shape=jax.ShapeDtypeStruct((2 * N_CHIPS, H, PACK_W), pack.dtype),
        scratch_shapes=[pltpu.SemaphoreType.DMA((6,)), pltpu.SemaphoreType.DMA((6,)), pltpu.SemaphoreType.DMA((2,))],
    )(pack)


def _rs_sibling_in(gp):
    H = gp.shape[1] // 2

    def body(g_ref, out_ref, send_sems, recv_sems):
        x, y, c, _ = _place()
        cps = [_remote(g_ref.at[q, pl.ds((1 - c) * H, H)], out_ref.at[q], send_sems.at[q], recv_sems.at[q],
                       (x, y, 1 - c)) for q in range(N_CHIPS)]
        for cp in cps:
            cp.start()
        for cp in cps:
            cp.wait()

    return pl.pallas_call(
        body, name="rs_sibling_in", in_specs=[ANY], out_specs=ANY,
        out_shape=jax.ShapeDtypeStruct((N_CHIPS, H, PACK_W), gp.dtype),
        scratch_shapes=[pltpu.SemaphoreType.DMA((N_CHIPS,)), pltpu.SemaphoreType.DMA((N_CHIPS,))],
    )(gp)


def _rs_add_own(gp, got, c_arr, tr=912):
    H = got.shape[1]
    tr = _pick(H, tr, 8)
    nb = H // tr

    def body(c_ref, g_ref, r_ref, o_ref):
        o_ref[...] = g_ref[...].astype(F32) + r_ref[...].astype(F32)

    return pl.pallas_call(
        body, name="rs_add_own",
        grid_spec=pltpu.PrefetchScalarGridSpec(
            num_scalar_prefetch=1, grid=(N_CHIPS, nb),
            in_specs=[pl.BlockSpec((1, tr, PACK_W), lambda q, i, c_ref: (q, c_ref[0] * nb + i, 0)),
                      pl.BlockSpec((1, tr, PACK_W), lambda q, i, c_ref: (q, i, 0))],
            out_specs=pl.BlockSpec((1, tr, PACK_W), lambda q, i, c_ref: (q, i, 0))),
        out_shape=jax.ShapeDtypeStruct((N_CHIPS, H, PACK_W), WIRE_DT),
        compiler_params=pltpu.CompilerParams(dimension_semantics=("arbitrary", "arbitrary")),
    )(c_arr, gp, got)


def _rs_chips(part):
    def body(p_ref, out_ref, send_sems, recv_sems, local_sem):
        x, y, c, chips = _place()
        s = 2 * x + y
        own = pltpu.make_async_copy(p_ref.at[s], out_ref.at[s], local_sem)
        own.start()
        cps = [_remote(p_ref.at[2 * chip[0] + chip[1]], out_ref.at[s], send_sems.at[j], recv_sems.at[j], (*chip, c))
               for j, chip in enumerate(chips)]
        for cp in cps:
            cp.start()
        for j, chip in enumerate(chips):
            cps[j].wait_send()
            _remote(p_ref.at[s], out_ref.at[2 * chip[0] + chip[1]], send_sems.at[j], recv_sems.at[j], (*chip, c)).wait_recv()
        own.wait()

    return pl.pallas_call(
        body, name="rs_chips", in_specs=[ANY], out_specs=ANY,
        out_shape=jax.ShapeDtypeStruct(part.shape, part.dtype),
        scratch_shapes=[pltpu.SemaphoreType.DMA((3,)), pltpu.SemaphoreType.DMA((3,)), pltpu.SemaphoreType.DMA],
    )(part)


def _rs_add_chips(slots, tr=912):
    H = slots.shape[1]
    tr = _pick(H, tr, 8)

    def body(s_ref, o_ref):
        o_ref[...] = ((s_ref[0].astype(F32) + s_ref[1].astype(F32)) + s_ref[2].astype(F32)) + s_ref[3].astype(F32)

    return pl.pallas_call(
        body, name="rs_add_chips", grid=(H // tr,),
        in_specs=[pl.BlockSpec((N_CHIPS, tr, PACK_W), lambda i: (0, i, 0))],
        out_specs=pl.BlockSpec((tr, PACK_W), lambda i: (i, 0)),
        out_shape=jax.ShapeDtypeStruct((H, PACK_W), F32),
        compiler_params=pltpu.CompilerParams(dimension_semantics=("arbitrary",)),
    )(slots)


def _rs_sibling_out(half):
    H = half.shape[0]

    def body(h_ref, out_ref, send_sem, recv_sem, local_sem):
        x, y, c, _ = _place()
        own = pltpu.make_async_copy(h_ref, out_ref.at[pl.ds(c * H, H)], local_sem)
        own.start()
        cp = _remote(h_ref, out_ref.at[pl.ds(c * H, H)], send_sem, recv_sem, (x, y, 1 - c))
        cp.start()
        cp.wait_send()
        _remote(h_ref, out_ref.at[pl.ds((1 - c) * H, H)], send_sem, recv_sem, (x, y, 1 - c)).wait_recv()
        own.wait()

    return pl.pallas_call(
        body, name="rs_sibling_out", in_specs=[ANY], out_specs=ANY,
        out_shape=jax.ShapeDtypeStruct((2 * H, PACK_W), half.dtype),
        scratch_shapes=[pltpu.SemaphoreType.DMA, pltpu.SemaphoreType.DMA, pltpu.SemaphoreType.DMA],
    )(half)


def _adamw(w, g, m, v):
    def fn(w_, g_, m_, v_):
        m2 = ADAM_B1 * m_ + (1.0 - ADAM_B1) * g_
        v2 = ADAM_B2 * v_ + (1.0 - ADAM_B2) * (g_ * g_)
        m_hat = m2 / (1.0 - ADAM_B1 ** ADAM_STEP)
        v_hat = v2 / (1.0 - ADAM_B2 ** ADAM_STEP)
        return -ADAM_LR * (m_hat / (jnp.sqrt(v_hat) + ADAM_EPS) + ADAM_WD * w_), m2, v2

    return _rows("adamw", fn, [w, g, m, v], [], [(PACK_W, F32)] * 3, tile=_pick(w.shape[0], 1024, 8))


def kernel(x, c, w_ada, b_ada, w_in, b_in, g_ln_v, b_ln_v, w_spatial, b_spatial, mu_shift, w0, w_decay_up, a0, w_aaa_up, w_gate_up, k_k, k_a, r_k, gn_gain, gn_bias, w_branch_a, w_branch_b, w_out, b_out, ln1_g, ln1_b, w_ff1, b_ff1, w_ff2, b_ff2, ln2_g, ln2_b, loss_target, m_w_ada, m_b_ada, m_w_in, m_b_in, m_g_ln_v, m_b_ln_v, m_w_spatial, m_b_spatial, m_mu_shift, m_w0, m_w_decay_up, m_a0, m_w_aaa_up, m_w_gate_up, m_k_k, m_k_a, m_r_k, m_gn_gain, m_gn_bias, m_w_branch_a, m_w_branch_b, m_w_out, m_b_out, m_ln1_g, m_ln1_b, m_w_ff1, m_b_ff1, m_w_ff2, m_b_ff2, m_ln2_g, m_ln2_b, v_w_ada, v_b_ada, v_w_in, v_b_in, v_g_ln_v, v_b_ln_v, v_w_spatial, v_b_spatial, v_mu_shift, v_w0, v_w_decay_up, v_a0, v_w_aaa_up, v_w_gate_up, v_k_k, v_k_a, v_r_k, v_gn_gain, v_gn_bias, v_w_branch_a, v_w_branch_b, v_w_out, v_b_out, v_ln1_g, v_ln1_b, v_w_ff1, v_b_ff1, v_w_ff2, v_b_ff2, v_ln2_g, v_ln2_b):
    args = dict(locals())
    local_shape = {n: _shard_shape(s, a) for n, s, a in SHARDED}
    local_shape.update(dict(SMALL))
    wts = {n: args[n].reshape(local_shape[n]) for n in WEIGHT_ORDER}
    mom = {n: args["m_" + n].reshape(local_shape[n]) for n in WEIGHT_ORDER}
    var = {n: args["v_" + n].reshape(local_shape[n]) for n in WEIGHT_ORDER}
    small_names = [n for n, _ in SMALL]

    gath = _all_gather_weights(_pack_local(wts, None, ROWS_W, MXU_DT))
    W = _unpack_gathered(gath.reshape(N_CHIPS, ROWS_W, PACK_W))
    W.update({n: wts[n] for n in small_names})

    loss, grad_x, G = _local_step(x[0], c, loss_target[0], W)
    loss = lax.psum(loss, MESH_AXES)

    gp = _pack_grads(G)
    c_arr = lax.axis_index("c").astype(jnp.int32).reshape(1)
    part = _rs_add_own(gp, _rs_sibling_in(gp), c_arr)
    seg = _rs_sibling_out(_rs_add_chips(_rs_chips(part)))

    small_w = {n: wts[n] for n in small_names}
    w_pack = _pack_local(wts, small_w, ROWS_G, F32)
    m_pack = _pack_local(mom, {n: mom[n] for n in small_names}, ROWS_G, F32)
    v_pack = _pack_local(var, {n: var[n] for n in small_names}, ROWS_G, F32)
    delta, new_m, new_v = _adamw(w_pack, seg, m_pack, v_pack)
    outs = [_unpack_local(t, True) for t in (seg, delta, new_m, new_v)]
    res = [loss, grad_x[None]]
    for o in outs:
        res += [o[n].reshape(args[n].shape) for n in WEIGHT_ORDER]
    return tuple(res)
```

```python
import functools

import numpy as np
import jax
import jax.numpy as jnp
from jax import lax
from jax.experimental import pallas as pl
from jax.experimental.pallas import tpu as pltpu

F32 = jnp.float32
MXU_DT = jnp.bfloat16
WIRE_DT = jnp.float32

D_MODEL = 1024
G_GROUPS = 8
G_WIDTH = 512
CHUNK = 128
R_WIDTH = 512
R_HEAD = 64
R_HEADS = 8
DECAY_LORA = 32
AAA_LORA = 32
GATE_LORA = 96
LORA = DECAY_LORA + AAA_LORA + GATE_LORA
LORA_PAD = 256
R_COLS = 3 * R_WIDTH + LORA
R_COLS_PAD = 3 * R_WIDTH + LORA_PAD
D_FF = 4 * D_MODEL
ALPHA = 2.0 ** 0.25
LN_EPS = 1e-5
GN_EPS = 64e-5
ADAM_LR = 0.001
ADAM_B1 = 0.9
ADAM_B2 = 0.999
ADAM_EPS = 1e-08
ADAM_WD = 0.01
ADAM_STEP = 10

LANES = 128
PACK_W = 512
PACK_Q = 2 * 16 * PACK_W
VMEM_LIMIT = 48 * 1024 * 1024
SCAN_T = 64

MESH_AXES = ("x", "y", "c")


def _dg(a, b, dims):
    return lax.dot_general(a.astype(MXU_DT), b.astype(MXU_DT), (dims, ((), ())),
                           preferred_element_type=F32)


@jax.custom_vjp
def _bdot(a, b):
    return _dg(a, b, ((1,), (0,)))


def _bdot_fwd(a, b):
    return _bdot(a, b), (a, b)


def _bdot_bwd(res, g):
    a, b = res
    return (_dg(g, b, ((1,), (1,))).astype(a.dtype), _dg(a, g, ((0,), (0,))).astype(b.dtype))


_bdot.defvjp(_bdot_fwd, _bdot_bwd)


def _split_dot(x, m, dims):
    hi = x.astype(jnp.bfloat16)
    lo = (x - hi.astype(F32)).astype(jnp.bfloat16)
    dn = (dims, ((), ()))
    return (lax.dot_general(hi, m, dn, preferred_element_type=F32)
            + lax.dot_general(lo, m, dn, preferred_element_type=F32))


@jax.custom_vjp
def _pdot(x, m):
    return _split_dot(x, m, ((1,), (0,)))


def _pdot_fwd(x, m):
    return _pdot(x, m), m


def _pdot_bwd(m, g):
    return _split_dot(g, m, ((1,), (1,))), None


_pdot.defvjp(_pdot_fwd, _pdot_bwd)


def _sigmoid(x):
    return 1.0 / (1.0 + jnp.exp(-x))


def _softplus(x):
    return jnp.maximum(x, 0.0) + jnp.log(1.0 + jnp.exp(-jnp.maximum(x, -x)))


def _gelu(x):
    return 0.5 * x * (1.0 + jnp.tanh(0.7978845608028654 * (x + 0.044715 * (x * x * x))))


def _ln(x, g, b, eps):
    mu = jnp.mean(x, axis=-1, keepdims=True)
    xc = x - mu
    var = jnp.mean(xc * xc, axis=-1, keepdims=True)
    return xc * lax.rsqrt(var + eps) * g + b


def _colsum(x):
    return jnp.sum(x, axis=0, keepdims=True)


def _pick(n, target, q=LANES):
    if n <= target:
        return n
    best = None
    for t in range(q, target + 1, q):
        if n % t == 0:
            best = t
    assert best is not None, (n, target)
    return best


def _mm(name, a, b, mode, out_dtype=F32, out_split=1, tm=1024, tn=1024, tk=512):
    bs = b.shape[0] if b.ndim == 3 else 1
    br, bc = b.shape[-2:]
    if mode == "nn":
        (M, K), K2, N = a.shape, br, bc * bs
    elif mode == "nt":
        (M, K), N, K2 = a.shape, br, bc * bs
    else:
        assert bs == 1
        (K, M), K2, N = a.shape, br, bc
    assert K == K2, (name, a.shape, b.shape, mode)
    n_piece = N // max(bs if mode == "nn" else 1, out_split)
    k_piece = K // (bs if mode == "nt" else 1)
    tm, tn, tk = _pick(M, tm, 8 if M < LANES else LANES), _pick(n_piece, tn), _pick(k_piece, tk)
    nk, npj, npk = K // tk, n_piece // tn, k_piece // tk
    dims = {"nn": ((1,), (0,)), "nt": ((1,), (1,)), "tn": ((0,), (0,))}[mode]

    def body(a_ref, b_ref, o_ref, acc_ref):
        k = pl.program_id(2)

        @pl.when(k == 0)
        def _():
            acc_ref[...] = jnp.zeros(acc_ref.shape, F32)

        acc_ref[...] += _dg(a_ref[...], b_ref[0] if bs > 1 else b_ref[...], dims)

        @pl.when(k == nk - 1)
        def _():
            if out_split > 1:
                o_ref[0] = acc_ref[...].astype(o_ref.dtype)
            else:
                o_ref[...] = acc_ref[...].astype(o_ref.dtype)

    if mode == "nn":
        a_spec = pl.BlockSpec((tm, tk), lambda i, j, k: (i, k))
        b_spec = (pl.BlockSpec((tk, tn), lambda i, j, k: (k, j)) if bs == 1 else
                  pl.BlockSpec((1, tk, tn), lambda i, j, k: (j // npj, k, j % npj)))
    elif mode == "nt":
        a_spec = pl.BlockSpec((tm, tk), lambda i, j, k: (i, k))
        b_spec = (pl.BlockSpec((tn, tk), lambda i, j, k: (j, k)) if bs == 1 else
                  pl.BlockSpec((1, tn, tk), lambda i, j, k: (k // npk, j, k % npk)))
    else:
        a_spec = pl.BlockSpec((tk, tm), lambda i, j, k: (k, i))
        b_spec = pl.BlockSpec((tk, tn), lambda i, j, k: (k, j))
    if out_split > 1:
        o_spec = pl.BlockSpec((1, tm, tn), lambda i, j, k: (j // npj, i, j % npj))
        o_shape = jax.ShapeDtypeStruct((out_split, M, n_piece), out_dtype)
    else:
        o_spec = pl.BlockSpec((tm, tn), lambda i, j, k: (i, j))
        o_shape = jax.ShapeDtypeStruct((M, N), out_dtype)
    return pl.pallas_call(
        body, name=name, grid=(M // tm, N // tn, nk),
        in_specs=[a_spec, b_spec], out_specs=o_spec, out_shape=o_shape,
        scratch_shapes=[pltpu.VMEM((tm, tn), F32)],
        compiler_params=pltpu.CompilerParams(
            dimension_semantics=("parallel", "parallel", "arbitrary"), vmem_limit_bytes=VMEM_LIMIT),
    )(a, b)


def _rows(name, fn, rows, params, outs, accs=(), tile=256):
    S = rows[0].shape[0]
    tile = min(tile, S)
    assert S % tile == 0, (name, S, tile)
    nr, npar, no, na = len(rows), len(params), len(outs), len(accs)

    def body(*refs):
        rin, pin = refs[:nr], refs[nr:nr + npar]
        oref, aref = refs[nr + npar:nr + npar + no], refs[nr + npar + no:]
        res = fn(*[r[...] for r in rin], *[p[...] for p in pin])
        if not isinstance(res, (tuple, list)):
            res = (res,)
        assert len(res) == no + na, (name, len(res), no, na)
        for ref, val in zip(oref, res[:no]):
            ref[...] = val.astype(ref.dtype)
        if na:
            @pl.when(pl.program_id(0) == 0)
            def _():
                for ref in aref:
                    ref[...] = jnp.zeros(ref.shape, ref.dtype)

            for ref, val in zip(aref, res[no:]):
                ref[...] += jnp.broadcast_to(val, ref.shape).astype(ref.dtype)

    def whole(shape):
        nd = len(shape)
        return pl.BlockSpec(tuple(shape), lambda i: (0,) * nd)

    in_specs = ([pl.BlockSpec((tile, r.shape[1]), lambda i: (i, 0)) for r in rows]
                + [whole(p.shape) for p in params])
    out_specs = ([pl.BlockSpec((tile, n), lambda i: (i, 0)) for n, _ in outs]
                 + [whole(s) for s in accs])
    out_shape = ([jax.ShapeDtypeStruct((S, n), dt) for n, dt in outs]
                 + [jax.ShapeDtypeStruct(tuple(s), F32) for s in accs])
    res = pl.pallas_call(
        body, name=name, grid=(S // tile,), in_specs=in_specs, out_specs=out_specs,
        out_shape=out_shape,
        compiler_params=pltpu.CompilerParams(
            dimension_semantics=("arbitrary",), vmem_limit_bytes=VMEM_LIMIT),
    )(*rows, *params)
    return res


def _modulate(x, sc, sh):
    return x * (1.0 + sc) + sh


def _gmlp_consts():
    lane = lax.broadcasted_iota(jnp.int32, (1, G_WIDTH), 1)
    gmask = [(lane // (G_WIDTH // G_GROUPS) == g).astype(F32) for g in range(G_GROUPS)]
    tril = (lax.broadcasted_iota(jnp.int32, (CHUNK, CHUNK), 0)
            >= lax.broadcasted_iota(jnp.int32, (CHUNK, CHUNK), 1))
    return gmask, tril


def _gmlp_core(gmask, tril, gsel, zu, zv, bu, bv, g, b, bst, *ws):
    u = _gelu(zu + bu)
    v = _ln(_gelu(zv + bv), g, b, LN_EPS)
    s = _pdot(bst, gsel)
    for gi in range(G_GROUPS):
        s = s + _bdot(jnp.where(tril, ws[gi], 0.0), v * gmask[gi])
    return u * s


def _pre_core(rowmask, bd, zr, zk, zv, zl, pr, pk, pv, pq, br, bk, bv, bl, mr, mk, mv, ml,
              w0, wd, a0, wa, wg, k_k, k_a):
    def mix(z, p, b, mu):
        zz = z + b
        return zz + ((p + b) * rowmask - zz) * mu

    r, k, v, l = mix(zr, pr, br, mr), mix(zk, pk, bk, mk), mix(zv, pv, bv, mv), mix(zl, pq, bl, ml)
    w_log = -_softplus(-(w0 + _bdot(jnp.tanh(l), wd))) - 0.5
    decay = jnp.exp(-jnp.exp(w_log))
    a = _sigmoid(a0 + _bdot(l, wa))
    g = _bdot(_sigmoid(l), wg)
    kk = k * k_k
    kkn = kk / jnp.maximum(jnp.sqrt(_pdot(kk * kk, bd)), 1e-12)
    k2 = k * (1.0 + (a - 1.0) * k_a)
    return r, decay, k2, v, -kkn, kkn * a, g


def _post_core(bd, y, r, k2, v, g, gain, bias, rk):
    inv = 1.0 / R_HEAD
    mu = _pdot(y, bd) * inv
    yc = y - mu
    var = _pdot(yc * yc, bd) * inv
    yn = yc * lax.rsqrt(var + GN_EPS) * gain + bias
    bonus = _pdot(r * k2 * rk, bd) * v
    return (yn + bonus) * g


def _merge_core(pa, pb, ga, gb, bga, bgb):
    return _sigmoid(ga + bga) * pa + _sigmoid(gb + bgb) * pb


def _ln1_core(x, mix, gt1, bout, g, b, sc2, sh2):
    h1 = _ln(ALPHA * x + gt1 * (mix + bout), g, b, LN_EPS)
    return h1, h1 * (1.0 + sc2) + sh2


def _ln2_loss_core(tgt, h1, ff, gt2, bff2, g, b):
    out = _ln(ALPHA * h1 + gt2 * (ff + bff2), g, b, LN_EPS)
    err = out - tgt
    return 0.5 * jnp.sum(err * err) * (1.0 / D_MODEL)


def _scan_consts():
    sub = lax.broadcasted_iota(jnp.int32, (R_HEAD, LANES), 0)
    lane = lax.broadcasted_iota(jnp.int32, (R_HEAD, LANES), 1)
    return lane < R_HEAD, sub == (lane & (R_HEAD - 1))


def _seg_sum(lo, xb):
    s_lo = jnp.sum(jnp.where(lo, xb, 0.0), axis=1, keepdims=True)
    s_hi = jnp.sum(jnp.where(lo, 0.0, xb), axis=1, keepdims=True)
    return jnp.where(lo, s_lo, s_hi)


def _col_of_row(lo, eye, row):
    return _seg_sum(lo, jnp.where(eye, jnp.broadcast_to(row, eye.shape), 0.0))


def _row_of_col(eye, colb):
    return jnp.sum(jnp.where(eye, colb, 0.0), axis=0, keepdims=True)


N_BLK = R_WIDTH // LANES
ROW_GROUP = 8


def _scan_fwd(r, w, k, v, a, b):
    S = r.shape[0]
    T = min(SCAN_T, S)
    nchunk = S // T

    def body(r_ref, w_ref, k_ref, v_ref, a_ref, b_ref, y_ref, ck_ref, st_ref):
        lo, eye = _scan_consts()

        @pl.when(pl.program_id(0) == 0)
        def _():
            st_ref[...] = jnp.zeros(st_ref.shape, F32)

        ck_ref[0] = st_ref[...]
        sub8 = lax.broadcasted_iota(jnp.int32, (ROW_GROUP, LANES), 0)

        def group(gi, state):
            base = pl.multiple_of(gi * ROW_GROUP, ROW_GROUP)
            state = list(state)
            for q in range(N_BLK):
                sl = slice(q * LANES, (q + 1) * LANES)
                r8, w8, k8, v8, a8, b8 = [ref[pl.ds(base, ROW_GROUP), sl]
                                          for ref in (r_ref, w_ref, k_ref, v_ref, a_ref, b_ref)]
                y8 = jnp.zeros((ROW_GROUP, LANES), F32)
                s = state[q]
                for i in range(ROW_GROUP):
                    row = lambda t8: t8[i:i + 1, :]
                    sa = _seg_sum(lo, s * row(a8))
                    vb = _col_of_row(lo, eye, row(v8))
                    s = s * row(w8) + sa * row(b8) + vb * row(k8)
                    y8 = jnp.where(sub8 == i, _row_of_col(eye, _seg_sum(lo, s * row(r8))), y8)
                y_ref[pl.ds(base, ROW_GROUP), sl] = y8
                state[q] = s
            return tuple(state)

        init = tuple(st_ref[:, q * LANES:(q + 1) * LANES] for q in range(N_BLK))
        fin = lax.fori_loop(0, T // ROW_GROUP, group, init)
        for q in range(N_BLK):
            st_ref[:, q * LANES:(q + 1) * LANES] = fin[q]

    blk = pl.BlockSpec((T, R_WIDTH), lambda i: (i, 0))
    return pl.pallas_call(
        body, name="scan_fwd", grid=(nchunk,), in_specs=[blk] * 6,
        out_specs=[blk, pl.BlockSpec((1, R_HEAD, R_WIDTH), lambda i: (i, 0, 0))],
        out_shape=[jax.ShapeDtypeStruct((S, R_WIDTH), F32),
                   jax.ShapeDtypeStruct((nchunk, R_HEAD, R_WIDTH), F32)],
        scratch_shapes=[pltpu.VMEM((R_HEAD, R_WIDTH), F32)],
        compiler_params=pltpu.CompilerParams(
            dimension_semantics=("arbitrary",), vmem_limit_bytes=VMEM_LIMIT),
    )(r, w, k, v, a, b)


def _scan_bwd(r, w, k, v, a, b, ck, dy):
    S = r.shape[0]
    T = min(SCAN_T, S)
    nchunk = S // T

    def body(r_ref, w_ref, k_ref, v_ref, a_ref, b_ref, ck_ref, dy_ref,
             dr_ref, dw_ref, dk_ref, dv_ref, da_ref, db_ref, s_buf, sa_buf, vb_buf, ds_ref):
        lo, eye = _scan_consts()

        @pl.when(pl.program_id(0) == 0)
        def _():
            ds_ref[...] = jnp.zeros(ds_ref.shape, F32)

        sub8 = lax.broadcasted_iota(jnp.int32, (ROW_GROUP, LANES), 0)

        def fgroup(gi, state):
            base = pl.multiple_of(gi * ROW_GROUP, ROW_GROUP)
            state = list(state)
            for q in range(N_BLK):
                sl = slice(q * LANES, (q + 1) * LANES)
                w8, k8, v8, a8, b8 = [ref[pl.ds(base, ROW_GROUP), sl]
                                      for ref in (w_ref, k_ref, v_ref, a_ref, b_ref)]
                s = state[q]
                for i in range(ROW_GROUP):
                    row = lambda t8: t8[i:i + 1, :]
                    sa = _seg_sum(lo, s * row(a8))
                    vb = _col_of_row(lo, eye, row(v8))
                    s_buf[base + i, :, sl] = s
                    sa_buf[base + i, :, sl] = sa
                    vb_buf[base + i, :, sl] = vb
                    s = s * row(w8) + sa * row(b8) + vb * row(k8)
                state[q] = s
            return tuple(state)

        lax.fori_loop(0, T // ROW_GROUP, fgroup,
                      tuple(ck_ref[0, :, q * LANES:(q + 1) * LANES] for q in range(N_BLK)))

        def bgroup(n, dstate):
            base = pl.multiple_of((T // ROW_GROUP - 1 - n) * ROW_GROUP, ROW_GROUP)
            dstate = list(dstate)
            for q in range(N_BLK):
                sl = slice(q * LANES, (q + 1) * LANES)
                r8, w8, k8, a8, b8, dy8 = [ref[pl.ds(base, ROW_GROUP), sl]
                                           for ref in (r_ref, w_ref, k_ref, a_ref, b_ref, dy_ref)]
                acc = {n_: jnp.zeros((ROW_GROUP, LANES), F32) for n_ in ("r", "w", "k", "v", "a", "b")}
                ds = dstate[q]
                for i in reversed(range(ROW_GROUP)):
                    row = lambda t8: t8[i:i + 1, :]
                    put = lambda n_, val: acc.__setitem__(n_, jnp.where(sub8 == i, val, acc[n_]))
                    sp, sa, vb = s_buf[base + i, :, sl], sa_buf[base + i, :, sl], vb_buf[base + i, :, sl]
                    wr, ar, br, kr, rr = row(w8), row(a8), row(b8), row(k8), row(r8)
                    st = sp * wr + sa * br + vb * kr
                    dyc = _col_of_row(lo, eye, row(dy8))
                    ds = ds + dyc * rr
                    put("r", _colsum(st * dyc))
                    put("w", _colsum(ds * sp))
                    put("b", _colsum(ds * sa))
                    put("k", _colsum(ds * vb))
                    dsa = _seg_sum(lo, ds * br)
                    put("v", _row_of_col(eye, _seg_sum(lo, ds * kr)))
                    put("a", _colsum(sp * dsa))
                    ds = ds * wr + dsa * ar
                for n_, ref in (("r", dr_ref), ("w", dw_ref), ("k", dk_ref), ("v", dv_ref), ("a", da_ref), ("b", db_ref)):
                    ref[pl.ds(base, ROW_GROUP), sl] = acc[n_]
                dstate[q] = ds
            return tuple(dstate)

        fin = lax.fori_loop(0, T // ROW_GROUP, bgroup,
                            tuple(ds_ref[:, q * LANES:(q + 1) * LANES] for q in range(N_BLK)))
        for q in range(N_BLK):
            ds_ref[:, q * LANES:(q + 1) * LANES] = fin[q]

    blk = pl.BlockSpec((T, R_WIDTH), lambda i: (nchunk - 1 - i, 0))
    ckb = pl.BlockSpec((1, R_HEAD, R_WIDTH), lambda i: (nchunk - 1 - i, 0, 0))
    return pl.pallas_call(
        body, name="scan_bwd", grid=(nchunk,), in_specs=[blk] * 6 + [ckb, blk],
        out_specs=[blk] * 6,
        out_shape=[jax.ShapeDtypeStruct((S, R_WIDTH), F32)] * 6,
        scratch_shapes=[pltpu.VMEM((T, R_HEAD, R_WIDTH), F32)] * 3 + [pltpu.VMEM((R_HEAD, R_WIDTH), F32)],
        compiler_params=pltpu.CompilerParams(
            dimension_semantics=("arbitrary",), vmem_limit_bytes=VMEM_LIMIT),
    )(r, w, k, v, a, b, ck, dy)


def _pad_in_cols(t, axis):
    cut = 2 * G_WIDTH + R_COLS
    lo, hi = lax.slice_in_dim(t, 0, cut, axis=axis), lax.slice_in_dim(t, cut, t.shape[axis], axis=axis)
    zshape = list(t.shape)
    zshape[axis] = LORA_PAD - LORA
    return jnp.concatenate([lo, jnp.zeros(zshape, t.dtype), hi], axis=axis)


def _unpad_in_cols(t, axis):
    cut = 2 * G_WIDTH + R_COLS
    return jnp.concatenate([lax.slice_in_dim(t, 0, cut, axis=axis),
                            lax.slice_in_dim(t, cut + LORA_PAD - LORA, t.shape[axis], axis=axis)], axis=axis)


def _pad_rows(t, lo, n):
    return jnp.zeros((n, t.shape[1]), t.dtype).at[lo:lo + t.shape[0]].set(t)


def _join_cols(w3):
    p, k, n = w3.shape
    return jnp.transpose(w3, (1, 0, 2)).reshape(k, p * n)


def _split_cols(w):
    k, n = w.shape
    return jnp.transpose(w.reshape(k, N_CHIPS, n // N_CHIPS), (1, 0, 2))


def _ada_dw(ccol, dmod):
    n = dmod.shape[1] // N_CHIPS
    tile = 256

    def body(c_ref, d_ref, o_ref):
        cc = c_ref[...]
        o_ref[0] = (cc * _sigmoid(cc)) * d_ref[...]

    return pl.pallas_call(
        body, name="ada_dw", grid=(N_CHIPS, D_MODEL // tile),
        in_specs=[pl.BlockSpec((tile, 1), lambda q, i: (i, 0)), pl.BlockSpec((1, n), lambda q, i: (0, q))],
        out_specs=pl.BlockSpec((1, tile, n), lambda q, i: (q, i, 0)),
        out_shape=jax.ShapeDtypeStruct((N_CHIPS, D_MODEL, n), F32),
        compiler_params=pltpu.CompilerParams(dimension_semantics=("parallel", "parallel")),
    )(ccol, dmod)


def _shift_down(t):
    return jnp.concatenate([jnp.zeros((1, t.shape[1]), t.dtype), t[:-1]], axis=0)


def _shift_up(t):
    return jnp.concatenate([t[1:], jnp.zeros((1, t.shape[1]), t.dtype)], axis=0)


def _local_step(x, c, tgt, W):
    S = x.shape[0]
    bf = MXU_DT
    G = {}

    hl = np.arange(R_WIDTH) // R_HEAD
    bd = jnp.asarray(hl[:, None] == hl[None, :], jnp.bfloat16)
    gsel = jnp.asarray(np.arange(LANES)[:, None] == (np.arange(G_WIDTH) // (G_WIDTH // G_GROUPS))[None, :],
                       jnp.bfloat16)
    w_in_p = _pad_in_cols(_join_cols(W["w_in"]), 1)
    b_in_p = _pad_in_cols(W["b_in"], 1)
    c_g, c_r = 2 * G_WIDTH, 2 * G_WIDTH + R_COLS_PAD
    w_g, w_r, w_gate = w_in_p[:, :c_g], w_in_p[:, c_g:c_r], w_in_p[:, c_r:]
    b_g, b_r, b_gate = b_in_p[:, :c_g], b_in_p[:, c_g:c_r], b_in_p[:, c_r:]
    mu_p = jnp.concatenate([W["mu_shift"], jnp.zeros((1, LORA_PAD - LORA), F32)], axis=1)
    wd_p = _pad_rows(W["w_decay_up"].astype(F32), 0, LORA_PAD)
    wa_p = _pad_rows(W["w_aaa_up"].astype(F32), DECAY_LORA, LORA_PAD)
    wg_p = _pad_rows(W["w_gate_up"].astype(F32), DECAY_LORA + AAA_LORA, LORA_PAD)
    ws2 = W["w_spatial"].reshape(G_GROUPS * CHUNK, CHUNK)
    bst = jnp.zeros((CHUNK, LANES), F32).at[:, :G_GROUPS].set(W["b_spatial"].T)
    rk = W["r_k"].reshape(1, R_WIDTH)

    c8 = jnp.broadcast_to(c, (8, D_MODEL))
    (ca8,) = _rows("ada_silu", lambda cc: cc * _sigmoid(cc), [c8], [], [(D_MODEL, bf)], tile=8)
    mod_raw = _mm("ada_mm", ca8, W["w_ada"], "nn")
    (mod8,) = _rows("ada_bias", lambda m, bb: m + bb, [mod_raw], [W["b_ada"]], [(6 * D_MODEL, F32)], tile=8)
    sh1, sc1, gt1, sh2, sc2, gt2 = [mod8[0:1, i * D_MODEL:(i + 1) * D_MODEL] for i in range(6)]

    (h,) = _rows("mod1", _modulate, [x], [sc1, sh1], [(D_MODEL, bf)])
    proj_g = _mm("proj_g", h, w_g, "nn")
    proj_r = _mm("proj_r", h, w_r, "nn")
    proj_gate = _mm("proj_gate", h, w_gate, "nn")

    def split2(t):
        return t[:, :G_WIDTH], t[:, G_WIDTH:]

    def gmlp_fwd(z, bz, g, b, bst_, gsel_, ws_):
        gmask, tril = _gmlp_consts()
        (zu, zv), (bu, bv) = split2(z), split2(bz)
        wsl = [ws_[i * CHUNK:(i + 1) * CHUNK] for i in range(G_GROUPS)]
        return _gmlp_core(gmask, tril, gsel_, zu, zv, bu, bv, g, b, bst_, *wsl)

    (y_a,) = _rows("gmlp_fwd", gmlp_fwd, [proj_g], [b_g, W["g_ln_v"], W["b_ln_v"], bst, gsel, ws2],
                   [(G_WIDTH, bf)], tile=CHUNK)

    r_cuts = (0, R_WIDTH, 2 * R_WIDTH, 3 * R_WIDTH, R_COLS_PAD)

    def split4(t):
        return [t[:, r_cuts[i]:r_cuts[i + 1]] for i in range(4)]

    tile_pre = min(256, S)

    def rowmask_of():
        grow = pl.program_id(0) * tile_pre + lax.broadcasted_iota(jnp.int32, (tile_pre, 1), 0)
        return (grow > 0).astype(F32)

    pre_params = [b_r, mu_p, W["w0"], wd_p, W["a0"], wa_p, wg_p, W["k_k"], W["k_a"], bd]

    def pre_fwd(z, p, bz, mu, w0, wd, a0, wa, wg, k_k, k_a, bd_):
        return _pre_core(rowmask_of(), bd_, *split4(z), *split4(p), *split4(bz), *split4(mu),
                         w0, wd, a0, wa, wg, k_k, k_a)

    proj_r_prev = _shift_down(proj_r)
    s_r, s_w, s_k, s_v, s_a, s_b, s_g = _rows(
        "rwkv_pre_fwd", pre_fwd, [proj_r, proj_r_prev], pre_params, [(R_WIDTH, F32)] * 7, tile=tile_pre)
    y_scan, ckpt = _scan_fwd(s_r, s_w, s_k, s_v, s_a, s_b)

    def post_fwd(y, r, k2, v, g, gain, bias, rk_, bd_):
        return _post_core(bd_, y, r, k2, v, g, gain, bias, rk_)

    post_params = [W["gn_gain"], W["gn_bias"], rk, bd]
    (y_b,) = _rows("rwkv_post_fwd", post_fwd, [y_scan, s_r, s_k, s_v, s_g], post_params, [(R_WIDTH, bf)])
    p_a = _mm("branch_a", y_a, W["w_branch_a"], "nn")
    p_b = _mm("branch_b", y_b, W["w_branch_b"], "nn")

    def merge_fwd(pa, pb, gz, bgz):
        return _merge_core(pa, pb, gz[:, :D_MODEL], gz[:, D_MODEL:], bgz[:, :D_MODEL], bgz[:, D_MODEL:])

    (merged,) = _rows("merge_fwd", merge_fwd, [p_a, p_b, proj_gate], [b_gate], [(D_MODEL, bf)])
    mix = _mm("out_proj", merged, W["w_out"], "nn")
    ln1_params = [gt1, W["b_out"], W["ln1_g"], W["ln1_b"], sc2, sh2]
    h1, h2 = _rows("ln1_fwd", _ln1_core, [x, mix], ln1_params, [(D_MODEL, F32), (D_MODEL, bf)])

    a1 = _mm("ff1", h2, W["w_ff1"], "nn")
    (act,) = _rows("ff_act", lambda z, bb: jnp.square(jnp.maximum(z + bb, 0.0)), [a1], [W["b_ff1"]], [(D_FF, bf)])
    ff = _mm("ff2", act, W["w_ff2"], "nn")

    def ln2_loss(h1_, ff_, tg, gt2_, bff2, g, b):
        loss, vjp = jax.vjp(functools.partial(_ln2_loss_core, tg), h1_, ff_, gt2_, bff2, g, b)
        return vjp(jnp.ones((), F32)) + (loss,)

    ln2_params = [gt2, W["b_ff2"], W["ln2_g"], W["ln2_b"]]
    dh1, dff, dgt2, G["b_ff2"], G["ln2_g"], G["ln2_b"], loss_acc = _rows(
        "ln2_loss", ln2_loss, [h1, ff, tgt], ln2_params, [(D_MODEL, F32), (D_MODEL, bf)],
        accs=[(1, D_MODEL)] * 4 + [(1, LANES)])
    loss = loss_acc[0, 0]

    dact = _mm("ff2_dx", dff, W["w_ff2"], "nt")
    G["w_ff2"] = _mm("ff2_dw", act, dff, "tn").reshape(N_CHIPS, D_FF // N_CHIPS, D_MODEL)

    def act_bwd(z, da, bb):
        d = da * 2.0 * jnp.maximum(z + bb, 0.0)
        return d, _colsum(d)

    da1, G["b_ff1"] = _rows("ff_act_bwd", act_bwd, [a1, dact], [W["b_ff1"]], [(D_FF, bf)], accs=[(1, D_FF)])
    dh2 = _mm("ff1_dx", da1, W["w_ff1"], "nt")
    G["w_ff1"] = _mm("ff1_dw", h2, da1, "tn", out_split=N_CHIPS)

    def ln1_bwd(x_, mix_, dh1_, dh2_, *ps):
        _, vjp = jax.vjp(_ln1_core, x_, mix_, *ps)
        return vjp((dh1_, dh2_))

    dx_res, dmix, dgt1, G["b_out"], G["ln1_g"], G["ln1_b"], dsc2, dsh2 = _rows(
        "ln1_bwd", ln1_bwd, [x, mix, dh1, dh2], ln1_params, [(D_MODEL, F32), (D_MODEL, bf)],
        accs=[(1, D_MODEL)] * 6)

    dmerged = _mm("out_proj_dx", dmix, W["w_out"], "nt")
    G["w_out"] = _mm("out_proj_dw", merged, dmix, "tn").reshape(N_CHIPS, D_MODEL // N_CHIPS, D_MODEL)

    def merge_bwd(pa, pb, gz, dm, bgz):
        args = (pa.astype(F32), pb.astype(F32), gz[:, :D_MODEL], gz[:, D_MODEL:], bgz[:, :D_MODEL], bgz[:, D_MODEL:])
        _, vjp = jax.vjp(_merge_core, *args)
        dpa, dpb, dga, dgb, dbga, dbgb = vjp(dm)
        return dpa, dpb, jnp.concatenate([dga, dgb], axis=1), jnp.concatenate([dbga, dbgb], axis=1)

    dp_a, dp_b, dgates, db_gate = _rows(
        "merge_bwd", merge_bwd, [p_a, p_b, proj_gate, dmerged], [b_gate],
        [(D_MODEL, bf), (D_MODEL, bf), (2 * D_MODEL, F32)], accs=[(1, 2 * D_MODEL)])
    dy_a = _mm("branch_a_dx", dp_a, W["w_branch_a"], "nt")
    G["w_branch_a"] = _mm("branch_a_dw", y_a, dp_a, "tn", out_split=N_CHIPS)
    dy_b = _mm("branch_b_dx", dp_b, W["w_branch_b"], "nt")
    G["w_branch_b"] = _mm("branch_b_dw", y_b, dp_b, "tn", out_split=N_CHIPS)

    def post_bwd(y, r, k2, v, g, dyb, gain, bias, rk_, bd_):
        _, vjp = jax.vjp(functools.partial(_post_core, bd_), y, r, k2, v, g, gain, bias, rk_)
        return vjp(dyb)

    dy_scan, dr_p, dk_p, dv_p, dg_p, G["gn_gain"], G["gn_bias"], drk = _rows(
        "rwkv_post_bwd", post_bwd, [y_scan, s_r, s_k, s_v, s_g, dy_b], post_params,
        [(R_WIDTH, F32)] * 5, accs=[(1, R_WIDTH)] * 3)
    G["r_k"] = drk.reshape(R_HEADS, R_HEAD)
    dr_s, dw_s, dk_s, dv_s, da_s, db_s = _scan_bwd(s_r, s_w, s_k, s_v, s_a, s_b, ckpt, dy_scan)

    def pre_bwd(z, p, dr1, dr2, dw, dk1, dk2, dv1, dv2, da, db, dg,
                bz, mu, w0, wd, a0, wa, wg, k_k, k_a, bd_):
        prim = (*split4(z), *split4(p), *split4(bz), *split4(mu), w0, wd, a0, wa, wg, k_k, k_a)
        _, vjp = jax.vjp(functools.partial(_pre_core, rowmask_of(), bd_), *prim)
        d = vjp((dr1 + dr2, dw, dk1 + dk2, dv1 + dv2, da, db, dg))
        cat = lambda parts: jnp.concatenate(parts, axis=1)
        return (cat(d[0:4]), cat(d[4:8]), cat(d[8:12]), cat(d[12:16])) + tuple(d[16:])

    dz_r, dprev, db_r, dmu_p, G["w0"], dwd_p, G["a0"], dwa_p, dwg_p, G["k_k"], G["k_a"] = _rows(
        "rwkv_pre_bwd", pre_bwd,
        [proj_r, proj_r_prev, dr_s, dr_p, dw_s, dk_s, dk_p, dv_s, dv_p, da_s, db_s, dg_p],
        pre_params, [(R_COLS_PAD, F32)] * 2,
        accs=[(1, R_COLS_PAD), (1, R_COLS_PAD), (1, R_WIDTH), (LORA_PAD, R_WIDTH), (1, R_WIDTH),
              (LORA_PAD, R_WIDTH), (LORA_PAD, R_WIDTH), (1, R_WIDTH), (1, R_WIDTH)],
        tile=tile_pre)
    G["mu_shift"] = dmu_p[:, :R_COLS]
    G["w_decay_up"] = dwd_p[:DECAY_LORA]
    G["w_aaa_up"] = dwa_p[DECAY_LORA:DECAY_LORA + AAA_LORA]
    G["w_gate_up"] = dwg_p[DECAY_LORA + AAA_LORA:LORA]

    def gmlp_bwd(z, dya, bz, g, b, bst_, gsel_, ws_):
        gmask, tril = _gmlp_consts()
        (zu, zv), (bu, bv) = split2(z), split2(bz)
        wsl = [ws_[i * CHUNK:(i + 1) * CHUNK] for i in range(G_GROUPS)]
        _, vjp = jax.vjp(functools.partial(_gmlp_core, gmask, tril, gsel_), zu, zv, bu, bv, g, b, bst_, *wsl)
        d = vjp(dya)
        return (jnp.concatenate(d[0:2], axis=1), jnp.concatenate(d[2:4], axis=1), d[4], d[5], d[6],
                jnp.concatenate(d[7:], axis=0))

    dz_g, db_g, G["g_ln_v"], G["b_ln_v"], dbst, dws2 = _rows(
        "gmlp_bwd", gmlp_bwd, [proj_g, dy_a], [b_g, W["g_ln_v"], W["b_ln_v"], bst, gsel, ws2],
        [(2 * G_WIDTH, F32)],
        accs=[(1, 2 * G_WIDTH), (1, G_WIDTH), (1, G_WIDTH), (CHUNK, LANES), (G_GROUPS * CHUNK, CHUNK)],
        tile=CHUNK)
    G["w_spatial"] = dws2.reshape(G_GROUPS, CHUNK, CHUNK)
    G["b_spatial"] = dbst[:, :G_GROUPS].T

    def dproj_cat(dzg, dzr, dpv, dgz):
        return jnp.concatenate([dzg, dzr + dpv, dgz], axis=1)

    (dproj,) = _rows("dproj_cat", dproj_cat, [dz_g, dz_r, _shift_up(dprev), dgates], [],
                     [(2 * G_WIDTH + R_COLS_PAD + 2 * D_MODEL, bf)])
    dh = _mm("proj_dx", dproj, w_in_p, "nt")
    G["w_in"] = _split_cols(_unpad_in_cols(_mm("proj_dw", h, dproj, "tn"), 1))
    G["b_in"] = _unpad_in_cols(jnp.concatenate([db_g, db_r, db_gate], axis=1), 1)

    def mod1_bwd(x_, dh_, dxr, sc):
        return dh_ * (1.0 + sc) + dxr, _colsum(dh_ * x_), _colsum(dh_)

    grad_x, dsc1, dsh1 = _rows("mod1_bwd", mod1_bwd, [x, dh, dx_res], [sc1], [(D_MODEL, F32)],
                               accs=[(1, D_MODEL)] * 2)

    dmod = jnp.concatenate([dsh1, dsc1, dgt1, dsh2, dsc2, dgt2], axis=1)
    G["b_ada"] = dmod
    G["w_ada"] = _ada_dw(c.reshape(D_MODEL, 1), dmod)
    return loss, grad_x, G


BIG = (("w_ada", (D_MODEL, 6 * D_MODEL), 1), ("w_in", (D_MODEL, 2 * G_WIDTH + R_COLS + 2 * D_MODEL), 1),
       ("w_branch_a", (G_WIDTH, D_MODEL), 1), ("w_branch_b", (R_WIDTH, D_MODEL), 1),
       ("w_out", (D_MODEL, D_MODEL), 0), ("w_ff1", (D_MODEL, D_FF), 1), ("w_ff2", (D_FF, D_MODEL), 0))
LORAS = (("w_decay_up", (DECAY_LORA, R_WIDTH), 1), ("w_aaa_up", (AAA_LORA, R_WIDTH), 1),
         ("w_gate_up", (GATE_LORA, R_WIDTH), 1))
SHARDED = BIG + LORAS
SMALL = (("b_ada", (1, 6 * D_MODEL)), ("b_in", (1, 2 * G_WIDTH + R_COLS + 2 * D_MODEL)),
         ("g_ln_v", (1, G_WIDTH)), ("b_ln_v", (1, G_WIDTH)), ("w_spatial", (G_GROUPS, CHUNK, CHUNK)),
         ("b_spatial", (G_GROUPS, CHUNK)), ("mu_shift", (1, R_COLS)), ("w0", (1, R_WIDTH)),
         ("a0", (1, R_WIDTH)), ("k_k", (1, R_WIDTH)), ("k_a", (1, R_WIDTH)), ("r_k", (R_HEADS, R_HEAD)),
         ("gn_gain", (1, R_WIDTH)), ("gn_bias", (1, R_WIDTH)), ("b_out", (1, D_MODEL)),
         ("ln1_g", (1, D_MODEL)), ("ln1_b", (1, D_MODEL)), ("b_ff1", (1, D_FF)), ("b_ff2", (1, D_MODEL)),
         ("ln2_g", (1, D_MODEL)), ("ln2_b", (1, D_MODEL)))
WEIGHT_ORDER = ("w_ada", "b_ada", "w_in", "b_in", "g_ln_v", "b_ln_v", "w_spatial", "b_spatial", "mu_shift",
                "w0", "w_decay_up", "a0", "w_aaa_up", "w_gate_up", "k_k", "k_a", "r_k", "gn_gain", "gn_bias",
                "w_branch_a", "w_branch_b", "w_out", "b_out", "ln1_g", "ln1_b", "w_ff1", "b_ff1", "w_ff2",
                "b_ff2", "ln2_g", "ln2_b")
N_CHIPS = 4


def _shard_shape(shape, axis):
    s = list(shape)
    s[axis] //= N_CHIPS
    return tuple(s)


def _numel(shape):
    return int(np.prod(shape))


def _round_up(n, q):
    return -(-n // q) * q


N_LORA = sum(_numel(_shard_shape(s, a)) for _, s, a in LORAS)
N_SMALL = sum(_numel(s) for _, s in SMALL)
ROWS_SW = _round_up(N_LORA, PACK_Q) // PACK_W
ROWS_SG = _round_up(N_LORA + N_SMALL, PACK_Q) // PACK_W


def _pack_small(loras, small, rows, dtype):
    parts = [loras[n] for n, _, _ in LORAS] + ([small[n] for n, _ in SMALL] if small is not None else [])
    flat = jnp.concatenate([p.reshape(-1).astype(dtype) for p in parts])
    flat = jnp.concatenate([flat, jnp.zeros((rows * PACK_W - flat.shape[0],), dtype)])
    return flat.reshape(rows, PACK_W)


def _unpack_small(pack, with_small):
    flat = pack.reshape(-1)
    out, off = {}, 0
    for n, s, a in LORAS:
        ss = _shard_shape(s, a)
        out[n] = flat[off:off + _numel(ss)].reshape(ss)
        off += _numel(ss)
    if with_small:
        for n, s in SMALL:
            out[n] = flat[off:off + _numel(s)].reshape(s)
            off += _numel(s)
    return out


def _pack_small_grads(G):
    segs = []
    for q in range(N_CHIPS):
        loras = {n: G[n][:, q * (s[1] // N_CHIPS):(q + 1) * (s[1] // N_CHIPS)] for n, s, _ in LORAS}
        segs.append(_pack_small(loras, G, ROWS_SG, WIRE_DT))
    return jnp.stack(segs)


ANY = pl.BlockSpec(memory_space=pl.ANY)
MESH = pl.DeviceIdType.MESH


def _place():
    x, y, c = lax.axis_index("x"), lax.axis_index("y"), lax.axis_index("c")
    chips = [(1 - x, y), (x, 1 - y), (1 - x, 1 - y)]
    return x, y, c, chips


def _remote(src, dst, send_sem, recv_sem, to):
    return pltpu.make_async_remote_copy(src_ref=src, dst_ref=dst, send_sem=send_sem, recv_sem=recv_sem,
                                        device_id=to, device_id_type=MESH)


def _all_gather(shards):
    n = len(shards)
    halves = [a.shape[0] // 2 for a in shards]

    def body(*refs):
        ins, outs = refs[:n], refs[n:2 * n]
        send_sems, recv_sems, local_sems = refs[2 * n:]
        x, y, c, chips = _place()
        s = 2 * x + y
        sibling = (x, y, 1 - c)
        slot = [2 * chip[0] + chip[1] for chip in chips]
        own, first, passed = [], [], []
        for a in range(n):
            H = halves[a]
            own += [pltpu.make_async_copy(ins[a].at[pl.ds(hf * H, H)], outs[a].at[2 * s + hf], local_sems.at[2 * a + hf])
                    for hf in range(2)]
            mine = ins[a].at[pl.ds(c * H, H)]
            first += [_remote(mine, outs[a].at[2 * s + c], send_sems.at[6 * a + j], recv_sems.at[6 * a + j], (*chip, c))
                      for j, chip in enumerate(chips)]
        for cp in own + first:
            cp.start()
        for a in range(n):
            for j in range(3):
                landed = outs[a].at[2 * slot[j] + c]
                _remote(landed, landed, send_sems.at[6 * a + j], recv_sems.at[6 * a + j], sibling).wait_recv()
                cp = _remote(landed, landed, send_sems.at[6 * a + 3 + j], recv_sems.at[6 * a + 3 + j], sibling)
                cp.start()
                passed.append(cp)
        for a in range(n):
            for j in range(3):
                got = outs[a].at[2 * slot[j] + 1 - c]
                _remote(got, got, send_sems.at[6 * a + 3 + j], recv_sems.at[6 * a + 3 + j], sibling).wait_recv()
        for cp in first + passed:
            cp.wait_send()
        for cp in own:
            cp.wait()

    return pl.pallas_call(
        body, name="ag_weights", in_specs=[ANY] * n, out_specs=[ANY] * n,
        out_shape=[jax.ShapeDtypeStruct((2 * N_CHIPS, h, a.shape[1]), a.dtype) for a, h in zip(shards, halves)],
        scratch_shapes=[pltpu.SemaphoreType.DMA((6 * n,)), pltpu.SemaphoreType.DMA((6 * n,)),
                        pltpu.SemaphoreType.DMA((2 * n,))],
    )(*shards)


def _rs_sibling_in(gps):
    n = len(gps)

    def body(*refs):
        ins, outs = refs[:n], refs[n:2 * n]
        send_sems, recv_sems = refs[2 * n:]
        x, y, c, _ = _place()
        cps = []
        for a in range(n):
            H = gps[a].shape[1] // 2
            cps += [_remote(ins[a].at[q, pl.ds((1 - c) * H, H)], outs[a].at[q], send_sems.at[N_CHIPS * a + q],
                            recv_sems.at[N_CHIPS * a + q], (x, y, 1 - c)) for q in range(N_CHIPS)]
        for cp in cps:
            cp.start()
        for cp in cps:
            cp.wait()

    return pl.pallas_call(
        body, name="rs_sibling_in", in_specs=[ANY] * n, out_specs=[ANY] * n,
        out_shape=[jax.ShapeDtypeStruct((N_CHIPS, g.shape[1] // 2, g.shape[2]), g.dtype) for g in gps],
        scratch_shapes=[pltpu.SemaphoreType.DMA((N_CHIPS * n,)), pltpu.SemaphoreType.DMA((N_CHIPS * n,))],
    )(*gps)


def _rs_add_own(name, gp, got, c_arr, tr=256):
    H, C = got.shape[1:]
    tr = _pick(H, tr, 8)
    nb = H // tr

    def body(c_ref, g_ref, r_ref, o_ref):
        o_ref[...] = g_ref[...].astype(F32) + r_ref[...].astype(F32)

    return pl.pallas_call(
        body, name="rs_add_own_" + name,
        grid_spec=pltpu.PrefetchScalarGridSpec(
            num_scalar_prefetch=1, grid=(N_CHIPS, nb),
            in_specs=[pl.BlockSpec((1, tr, C), lambda q, i, c_ref: (q, c_ref[0] * nb + i, 0)),
                      pl.BlockSpec((1, tr, C), lambda q, i, c_ref: (q, i, 0))],
            out_specs=pl.BlockSpec((1, tr, C), lambda q, i, c_ref: (q, i, 0))),
        out_shape=jax.ShapeDtypeStruct((N_CHIPS, H, C), WIRE_DT),
        compiler_params=pltpu.CompilerParams(dimension_semantics=("arbitrary", "arbitrary")),
    )(c_arr, gp, got)


def _rs_chips(parts):
    n = len(parts)

    def body(*refs):
        ins, outs = refs[:n], refs[n:2 * n]
        send_sems, recv_sems, local_sems = refs[2 * n:]
        x, y, c, chips = _place()
        s = 2 * x + y
        slot = [2 * chip[0] + chip[1] for chip in chips]
        own = [pltpu.make_async_copy(ins[a].at[s], outs[a].at[s], local_sems.at[a]) for a in range(n)]
        cps = [_remote(ins[a].at[slot[j]], outs[a].at[s], send_sems.at[3 * a + j], recv_sems.at[3 * a + j], (*chips[j], c))
               for a in range(n) for j in range(3)]
        for cp in own + cps:
            cp.start()
        for a in range(n):
            for j in range(3):
                cps[3 * a + j].wait_send()
                got = outs[a].at[slot[j]]
                _remote(got, got, send_sems.at[3 * a + j], recv_sems.at[3 * a + j], (*chips[j], c)).wait_recv()
        for cp in own:
            cp.wait()

    return pl.pallas_call(
        body, name="rs_chips", in_specs=[ANY] * n, out_specs=[ANY] * n,
        out_shape=[jax.ShapeDtypeStruct(p.shape, p.dtype) for p in parts],
        scratch_shapes=[pltpu.SemaphoreType.DMA((3 * n,)), pltpu.SemaphoreType.DMA((3 * n,)),
                        pltpu.SemaphoreType.DMA((n,))],
    )(*parts)


def _rs_add_chips(name, slots, tr=128):
    H, C = slots.shape[1:]
    tr = _pick(H, tr, 8)

    def body(s_ref, o_ref):
        o_ref[...] = ((s_ref[0].astype(F32) + s_ref[1].astype(F32)) + s_ref[2].astype(F32)) + s_ref[3].astype(F32)

    return pl.pallas_call(
        body, name="rs_add_chips_" + name, grid=(H // tr,),
        in_specs=[pl.BlockSpec((N_CHIPS, tr, C), lambda i: (0, i, 0))],
        out_specs=pl.BlockSpec((tr, C), lambda i: (i, 0)),
        out_shape=jax.ShapeDtypeStruct((H, C), F32),
        compiler_params=pltpu.CompilerParams(dimension_semantics=("arbitrary",)),
    )(slots)


def _rs_sibling_out(halves):
    n = len(halves)

    def body(*refs):
        ins, outs = refs[:n], refs[n:2 * n]
        send_sems, recv_sems, local_sems = refs[2 * n:]
        x, y, c, _ = _place()
        own, cps = [], []
        for a in range(n):
            H = halves[a].shape[0]
            own.append(pltpu.make_async_copy(ins[a], outs[a].at[pl.ds(c * H, H)], local_sems.at[a]))
            cps.append(_remote(ins[a], outs[a].at[pl.ds(c * H, H)], send_sems.at[a], recv_sems.at[a], (x, y, 1 - c)))
        for cp in own + cps:
            cp.start()
        for a in range(n):
            H = halves[a].shape[0]
            cps[a].wait_send()
            got = outs[a].at[pl.ds((1 - c) * H, H)]
            _remote(got, got, send_sems.at[a], recv_sems.at[a], (x, y, 1 - c)).wait_recv()
        for cp in own:
            cp.wait()

    return pl.pallas_call(
        body, name="rs_sibling_out", in_specs=[ANY] * n, out_specs=[ANY] * n,
        out_shape=[jax.ShapeDtypeStruct((2 * h.shape[0], h.shape[1]), h.dtype) for h in halves],
        scratch_shapes=[pltpu.SemaphoreType.DMA((n,)), pltpu.SemaphoreType.DMA((n,)), pltpu.SemaphoreType.DMA((n,))],
    )(*halves)


def _adamw(name, w, g, m, v):
    def fn(w_, g_, m_, v_):
        m2 = ADAM_B1 * m_ + (1.0 - ADAM_B1) * g_
        v2 = ADAM_B2 * v_ + (1.0 - ADAM_B2) * (g_ * g_)
        m_hat = m2 / (1.0 - ADAM_B1 ** ADAM_STEP)
        v_hat = v2 / (1.0 - ADAM_B2 ** ADAM_STEP)
        return -ADAM_LR * (m_hat / (jnp.sqrt(v_hat) + ADAM_EPS) + ADAM_WD * w_), m2, v2

    return _rows("adamw_" + name, fn, [w, g, m, v], [], [(w.shape[1], F32)] * 3, tile=_pick(w.shape[0], 256, 8))


def kernel(x, c, w_ada, b_ada, w_in, b_in, g_ln_v, b_ln_v, w_spatial, b_spatial, mu_shift, w0, w_decay_up, a0, w_aaa_up, w_gate_up, k_k, k_a, r_k, gn_gain, gn_bias, w_branch_a, w_branch_b, w_out, b_out, ln1_g, ln1_b, w_ff1, b_ff1, w_ff2, b_ff2, ln2_g, ln2_b, loss_target, m_w_ada, m_b_ada, m_w_in, m_b_in, m_g_ln_v, m_b_ln_v, m_w_spatial, m_b_spatial, m_mu_shift, m_w0, m_w_decay_up, m_a0, m_w_aaa_up, m_w_gate_up, m_k_k, m_k_a, m_r_k, m_gn_gain, m_gn_bias, m_w_branch_a, m_w_branch_b, m_w_out, m_b_out, m_ln1_g, m_ln1_b, m_w_ff1, m_b_ff1, m_w_ff2, m_b_ff2, m_ln2_g, m_ln2_b, v_w_ada, v_b_ada, v_w_in, v_b_in, v_g_ln_v, v_b_ln_v, v_w_spatial, v_b_spatial, v_mu_shift, v_w0, v_w_decay_up, v_a0, v_w_aaa_up, v_w_gate_up, v_k_k, v_k_a, v_r_k, v_gn_gain, v_gn_bias, v_w_branch_a, v_w_branch_b, v_w_out, v_b_out, v_ln1_g, v_ln1_b, v_w_ff1, v_b_ff1, v_w_ff2, v_b_ff2, v_ln2_g, v_ln2_b):
    args = dict(locals())
    local_shape = {n: _shard_shape(s, a) for n, s, a in SHARDED}
    local_shape.update(dict(SMALL))
    wts = {n: args[n].reshape(local_shape[n]) for n in WEIGHT_ORDER}
    mom = {n: args["m_" + n].reshape(local_shape[n]) for n in WEIGHT_ORDER}
    var = {n: args["v_" + n].reshape(local_shape[n]) for n in WEIGHT_ORDER}
    big = [n for n, _, _ in BIG]

    gath = _all_gather([wts[n].astype(MXU_DT) for n in big] + [_pack_small(wts, None, ROWS_SW, MXU_DT)])
    W = {n: wts[n] for n, _ in SMALL}
    for (n, s, a), g in zip(BIG, gath):
        r, cdim = _shard_shape(s, a)
        W[n] = g.reshape(N_CHIPS, r, cdim) if a == 1 else g.reshape(s)
    lora_q = [_unpack_small(gath[-1].reshape(N_CHIPS, ROWS_SW, PACK_W)[q], False) for q in range(N_CHIPS)]
    for n, _, _ in LORAS:
        W[n] = jnp.concatenate([lora_q[q][n] for q in range(N_CHIPS)], axis=1)

    loss, grad_x, G = _local_step(x[0], c, loss_target[0], W)
    loss = lax.psum(loss, MESH_AXES)

    names = big + ["small"]
    gps = [G[n].astype(WIRE_DT) for n in big] + [_pack_small_grads(G)]
    c_arr = lax.axis_index("c").astype(jnp.int32).reshape(1)
    parts = [_rs_add_own(n, g, r, c_arr) for n, g, r in zip(names, gps, _rs_sibling_in(gps))]
    halves = [_rs_add_chips(n, s) for n, s in zip(names, _rs_chips(parts))]
    segs = _rs_sibling_out(halves)

    out = {}
    for n, g in zip(big, segs[:-1]):
        out[n] = (g,) + tuple(_adamw(n, wts[n], g, mom[n], var[n]))
    packs = [_pack_small(t, t, ROWS_SG, F32) for t in (wts, mom, var)]
    small4 = [_unpack_small(t, True) for t in (segs[-1],) + tuple(_adamw("small", packs[0], segs[-1], packs[1], packs[2]))]
    res = [loss, grad_x[None]]
    for k in range(4):
        res += [(out[n][k] if n in out else small4[k][n]).reshape(args[n].shape) for n in WEIGHT_ORDER]
    return tuple(res)
```

```python
import functools

import numpy as np
import jax
import jax.numpy as jnp
from jax import lax
from jax.experimental import pallas as pl
from jax.experimental.pallas import tpu as pltpu

F32 = jnp.float32
MXU_DT = jnp.bfloat16
WIRE_DT = jnp.float32

D_MODEL = 1024
G_GROUPS = 8
G_WIDTH = 512
CHUNK = 128
R_WIDTH = 512
R_HEAD = 64
R_HEADS = 8
DECAY_LORA = 32
AAA_LORA = 32
GATE_LORA = 96
LORA = DECAY_LORA + AAA_LORA + GATE_LORA
LORA_PAD = 256
R_COLS = 3 * R_WIDTH + LORA
R_COLS_PAD = 3 * R_WIDTH + LORA_PAD
D_FF = 4 * D_MODEL
ALPHA = 2.0 ** 0.25
LN_EPS = 1e-5
GN_EPS = 64e-5
ADAM_LR = 0.001
ADAM_B1 = 0.9
ADAM_B2 = 0.999
ADAM_EPS = 1e-08
ADAM_WD = 0.01
ADAM_STEP = 10

LANES = 128
PACK_W = 512
PACK_Q = 2 * 16 * PACK_W
VMEM_LIMIT = 48 * 1024 * 1024
SCAN_T = 64

MESH_AXES = ("x", "y", "c")


def _dg(a, b, dims):
    return lax.dot_general(a.astype(MXU_DT), b.astype(MXU_DT), (dims, ((), ())),
                           preferred_element_type=F32)


@jax.custom_vjp
def _bdot(a, b):
    return _dg(a, b, ((1,), (0,)))


def _bdot_fwd(a, b):
    return _bdot(a, b), (a, b)


def _bdot_bwd(res, g):
    a, b = res
    return (_dg(g, b, ((1,), (1,))).astype(a.dtype), _dg(a, g, ((0,), (0,))).astype(b.dtype))


_bdot.defvjp(_bdot_fwd, _bdot_bwd)


def _split_dot(x, m, dims):
    hi = x.astype(jnp.bfloat16)
    lo = (x - hi.astype(F32)).astype(jnp.bfloat16)
    dn = (dims, ((), ()))
    return (lax.dot_general(hi, m, dn, preferred_element_type=F32)
            + lax.dot_general(lo, m, dn, preferred_element_type=F32))


@jax.custom_vjp
def _pdot(x, m):
    return _split_dot(x, m, ((1,), (0,)))


def _pdot_fwd(x, m):
    return _pdot(x, m), m


def _pdot_bwd(m, g):
    return _split_dot(g, m, ((1,), (1,))), None


_pdot.defvjp(_pdot_fwd, _pdot_bwd)


def _sigmoid(x):
    return 1.0 / (1.0 + jnp.exp(-x))


def _softplus(x):
    return jnp.maximum(x, 0.0) + jnp.log(1.0 + jnp.exp(-jnp.maximum(x, -x)))


def _gelu(x):
    return 0.5 * x * (1.0 + jnp.tanh(0.7978845608028654 * (x + 0.044715 * (x * x * x))))


def _ln(x, g, b, eps):
    mu = jnp.mean(x, axis=-1, keepdims=True)
    xc = x - mu
    var = jnp.mean(xc * xc, axis=-1, keepdims=True)
    return xc * lax.rsqrt(var + eps) * g + b


def _colsum(x):
    return jnp.sum(x, axis=0, keepdims=True)


def _pick(n, target, q=LANES):
    if n <= target:
        return n
    best = None
    for t in range(q, target + 1, q):
        if n % t == 0:
            best = t
    assert best is not None, (n, target)
    return best


def _mm(name, a, b, mode, out_dtype=F32, out_split=1, tm=1024, tn=1024, tk=512):
    bs = b.shape[0] if b.ndim == 3 else 1
    br, bc = b.shape[-2:]
    if mode == "nn":
        (M, K), K2, N = a.shape, br, bc * bs
    elif mode == "nt":
        (M, K), N, K2 = a.shape, br, bc * bs
    else:
        assert bs == 1
        (K, M), K2, N = a.shape, br, bc
    assert K == K2, (name, a.shape, b.shape, mode)
    n_piece = N // max(bs if mode == "nn" else 1, out_split)
    k_piece = K // (bs if mode == "nt" else 1)
    tm, tn, tk = _pick(M, tm, 8 if M < LANES else LANES), _pick(n_piece, tn), _pick(k_piece, tk)
    nk, npj, npk = K // tk, n_piece // tn, k_piece // tk
    dims = {"nn": ((1,), (0,)), "nt": ((1,), (1,)), "tn": ((0,), (0,))}[mode]

    def body(a_ref, b_ref, o_ref, acc_ref):
        k = pl.program_id(2)

        @pl.when(k == 0)
        def _():
            acc_ref[...] = jnp.zeros(acc_ref.shape, F32)

        acc_ref[...] += _dg(a_ref[...], b_ref[0] if bs > 1 else b_ref[...], dims)

        @pl.when(k == nk - 1)
        def _():
            if out_split > 1:
                o_ref[0] = acc_ref[...].astype(o_ref.dtype)
            else:
                o_ref[...] = acc_ref[...].astype(o_ref.dtype)

    if mode == "nn":
        a_spec = pl.BlockSpec((tm, tk), lambda i, j, k: (i, k))
        b_spec = (pl.BlockSpec((tk, tn), lambda i, j, k: (k, j)) if bs == 1 else
                  pl.BlockSpec((1, tk, tn), lambda i, j, k: (j // npj, k, j % npj)))
    elif mode == "nt":
        a_spec = pl.BlockSpec((tm, tk), lambda i, j, k: (i, k))
        b_spec = (pl.BlockSpec((tn, tk), lambda i, j, k: (j, k)) if bs == 1 else
                  pl.BlockSpec((1, tn, tk), lambda i, j, k: (k // npk, j, k % npk)))
    else:
        a_spec = pl.BlockSpec((tk, tm), lambda i, j, k: (k, i))
        b_spec = pl.BlockSpec((tk, tn), lambda i, j, k: (k, j))
    if out_split > 1:
        o_spec = pl.BlockSpec((1, tm, tn), lambda i, j, k: (j // npj, i, j % npj))
        o_shape = jax.ShapeDtypeStruct((out_split, M, n_piece), out_dtype)
    else:
        o_spec = pl.BlockSpec((tm, tn), lambda i, j, k: (i, j))
        o_shape = jax.ShapeDtypeStruct((M, N), out_dtype)
    return pl.pallas_call(
        body, name=name, grid=(M // tm, N // tn, nk),
        in_specs=[a_spec, b_spec], out_specs=o_spec, out_shape=o_shape,
        scratch_shapes=[pltpu.VMEM((tm, tn), F32)],
        compiler_params=pltpu.CompilerParams(
            dimension_semantics=("parallel", "parallel", "arbitrary"), vmem_limit_bytes=VMEM_LIMIT),
    )(a, b)


def _rows(name, fn, rows, params, outs, accs=(), tile=256):
    S = rows[0].shape[0]
    tile = min(tile, S)
    assert S % tile == 0, (name, S, tile)
    nr, npar, no, na = len(rows), len(params), len(outs), len(accs)

    def body(*refs):
        rin, pin = refs[:nr], refs[nr:nr + npar]
        oref, aref = refs[nr + npar:nr + npar + no], refs[nr + npar + no:]
        res = fn(*[r[...] for r in rin], *[p[...] for p in pin])
        if not isinstance(res, (tuple, list)):
            res = (res,)
        assert len(res) == no + na, (name, len(res), no, na)
        for ref, val in zip(oref, res[:no]):
            ref[...] = val.astype(ref.dtype)
        if na:
            @pl.when(pl.program_id(0) == 0)
            def _():
                for ref in aref:
                    ref[...] = jnp.zeros(ref.shape, ref.dtype)

            for ref, val in zip(aref, res[no:]):
                ref[...] += jnp.broadcast_to(val, ref.shape).astype(ref.dtype)

    def whole(shape):
        nd = len(shape)
        return pl.BlockSpec(tuple(shape), lambda i: (0,) * nd)

    in_specs = ([pl.BlockSpec((tile, r.shape[1]), lambda i: (i, 0)) for r in rows]
                + [whole(p.shape) for p in params])
    out_specs = ([pl.BlockSpec((tile, n), lambda i: (i, 0)) for n, _ in outs]
                 + [whole(s) for s in accs])
    out_shape = ([jax.ShapeDtypeStruct((S, n), dt) for n, dt in outs]
                 + [jax.ShapeDtypeStruct(tuple(s), F32) for s in accs])
    res = pl.pallas_call(
        body, name=name, grid=(S // tile,), in_specs=in_specs, out_specs=out_specs,
        out_shape=out_shape,
        compiler_params=pltpu.CompilerParams(
            dimension_semantics=("arbitrary",), vmem_limit_bytes=VMEM_LIMIT),
    )(*rows, *params)
    return res


def _modulate(x, sc, sh):
    return x * (1.0 + sc) + sh


def _gmlp_consts():
    lane = lax.broadcasted_iota(jnp.int32, (1, G_WIDTH), 1)
    gmask = [(lane // (G_WIDTH // G_GROUPS) == g).astype(F32) for g in range(G_GROUPS)]
    tril = (lax.broadcasted_iota(jnp.int32, (CHUNK, CHUNK), 0)
            >= lax.broadcasted_iota(jnp.int32, (CHUNK, CHUNK), 1))
    return gmask, tril


def _gmlp_core(gmask, tril, gsel, zu, zv, bu, bv, g, b, bst, *ws):
    u = _gelu(zu + bu)
    v = _ln(_gelu(zv + bv), g, b, LN_EPS)
    s = _pdot(bst, gsel)
    for gi in range(G_GROUPS):
        s = s + _bdot(jnp.where(tril, ws[gi], 0.0), v * gmask[gi])
    return u * s


def _pre_core(rowmask, bd, zr, zk, zv, zl, pr, pk, pv, pq, br, bk, bv, bl, mr, mk, mv, ml,
              w0, wd, a0, wa, wg, k_k, k_a):
    def mix(z, p, b, mu):
        zz = z + b
        return zz + ((p + b) * rowmask - zz) * mu

    r, k, v, l = mix(zr, pr, br, mr), mix(zk, pk, bk, mk), mix(zv, pv, bv, mv), mix(zl, pq, bl, ml)
    w_log = -_softplus(-(w0 + _bdot(jnp.tanh(l), wd))) - 0.5
    decay = jnp.exp(-jnp.exp(w_log))
    a = _sigmoid(a0 + _bdot(l, wa))
    g = _bdot(_sigmoid(l), wg)
    kk = k * k_k
    kkn = kk / jnp.maximum(jnp.sqrt(_pdot(kk * kk, bd)), 1e-12)
    k2 = k * (1.0 + (a - 1.0) * k_a)
    return r, decay, k2, v, -kkn, kkn * a, g


def _post_core(bd, y, r, k2, v, g, gain, bias, rk):
    inv = 1.0 / R_HEAD
    mu = _pdot(y, bd) * inv
    yc = y - mu
    var = _pdot(yc * yc, bd) * inv
    yn = yc * lax.rsqrt(var + GN_EPS) * gain + bias
    bonus = _pdot(r * k2 * rk, bd) * v
    return (yn + bonus) * g


def _merge_core(pa, pb, ga, gb, bga, bgb):
    return _sigmoid(ga + bga) * pa + _sigmoid(gb + bgb) * pb


def _ln1_core(x, mix, gt1, bout, g, b, sc2, sh2):
    h1 = _ln(ALPHA * x + gt1 * (mix + bout), g, b, LN_EPS)
    return h1, h1 * (1.0 + sc2) + sh2


def _ln2_loss_core(tgt, h1, ff, gt2, bff2, g, b):
    out = _ln(ALPHA * h1 + gt2 * (ff + bff2), g, b, LN_EPS)
    err = out - tgt
    return 0.5 * jnp.sum(err * err) * (1.0 / D_MODEL)


def _scan_consts():
    sub = lax.broadcasted_iota(jnp.int32, (R_HEAD, LANES), 0)
    lane = lax.broadcasted_iota(jnp.int32, (R_HEAD, LANES), 1)
    return lane < R_HEAD, sub == (lane & (R_HEAD - 1))


def _seg_sum(lo, xb):
    s_lo = jnp.sum(jnp.where(lo, xb, 0.0), axis=1, keepdims=True)
    s_hi = jnp.sum(jnp.where(lo, 0.0, xb), axis=1, keepdims=True)
    return jnp.where(lo, s_lo, s_hi)


def _col_of_row(lo, eye, row):
    return _seg_sum(lo, jnp.where(eye, jnp.broadcast_to(row, eye.shape), 0.0))


def _row_of_col(eye, colb):
    return jnp.sum(jnp.where(eye, colb, 0.0), axis=0, keepdims=True)


def _head_ones():
    i = lax.broadcasted_iota(jnp.int32, (LANES, LANES), 0) // R_HEAD
    j = lax.broadcasted_iota(jnp.int32, (LANES, LANES), 1) // R_HEAD
    return (i == j).astype(jnp.bfloat16)


def _split3(x):
    hi = x.astype(jnp.bfloat16)
    r1 = x - hi.astype(F32)
    mid = r1.astype(jnp.bfloat16)
    return hi, mid, (r1 - mid.astype(F32)).astype(jnp.bfloat16)


def _ones_dot(parts, ones):
    dn = (((1,), (0,)), ((), ()))
    out = lax.dot_general(parts[0], ones, dn, preferred_element_type=F32)
    for p in parts[1:]:
        out = out + lax.dot_general(p, ones, dn, preferred_element_type=F32)
    return out


def _seg_sums_mxu(ones, mats):
    parts = [p for m in mats for p in _split3(m)]
    res = lax.dot_general(jnp.concatenate(parts, axis=0), ones, (((1,), (0,)), ((), ())),
                          preferred_element_type=F32)
    out = []
    for n in range(len(mats)):
        t = [res[(3 * n + u) * R_HEAD:(3 * n + u + 1) * R_HEAD] for u in range(3)]
        out.append((t[0] + t[1]) + t[2])
    return out


def _cols_of_rows_mxu(eye, ones, rows8):
    return _seg_sums_mxu(ones, [jnp.where(eye, jnp.broadcast_to(rows8[i:i + 1, :], eye.shape), 0.0)
                                for i in range(rows8.shape[0])])


N_BLK = R_WIDTH // LANES
ROW_GROUP = 8


def _scan_fwd(r, w, k, v, a, b):
    S = r.shape[0]
    T = min(SCAN_T, S)
    nchunk = S // T

    def body(r_ref, w_ref, k_ref, v_ref, a_ref, b_ref, y_ref, ck_ref, st_ref):
        lo, eye = _scan_consts()
        ones = _head_ones()

        @pl.when(pl.program_id(0) == 0)
        def _():
            st_ref[...] = jnp.zeros(st_ref.shape, F32)

        ck_ref[0] = st_ref[...]
        sub8 = lax.broadcasted_iota(jnp.int32, (ROW_GROUP, LANES), 0)

        def group(gi, state):
            base = pl.multiple_of(gi * ROW_GROUP, ROW_GROUP)
            state = list(state)
            sls = [slice(q * LANES, (q + 1) * LANES) for q in range(N_BLK)]
            ld = lambda ref: [ref[pl.ds(base, ROW_GROUP), sl] for sl in sls]
            r8, w8, k8, v8, a8, b8 = ld(r_ref), ld(w_ref), ld(k_ref), ld(v_ref), ld(a_ref), ld(b_ref)
            vb = [_cols_of_rows_mxu(eye, ones, v8[q]) for q in range(N_BLK)]
            y8 = [jnp.zeros((ROW_GROUP, LANES), F32)] * N_BLK
            for i in range(ROW_GROUP):
                row = lambda t8: t8[i:i + 1, :]
                for q in range(N_BLK):
                    s = state[q]
                    sa = _seg_sum(lo, s * row(a8[q]))
                    state[q] = s * row(w8[q]) + sa * row(b8[q]) + vb[q][i] * row(k8[q])
                ycol = _seg_sums_mxu(ones, [state[q] * row(r8[q]) for q in range(N_BLK)])
                y8 = [jnp.where(sub8 == i, _row_of_col(eye, ycol[q]), y8[q]) for q in range(N_BLK)]
            for q in range(N_BLK):
                y_ref[pl.ds(base, ROW_GROUP), sls[q]] = y8[q]
            return tuple(state)

        init = tuple(st_ref[:, q * LANES:(q + 1) * LANES] for q in range(N_BLK))
        fin = lax.fori_loop(0, T // ROW_GROUP, group, init)
        for q in range(N_BLK):
            st_ref[:, q * LANES:(q + 1) * LANES] = fin[q]

    blk = pl.BlockSpec((T, R_WIDTH), lambda i: (i, 0))
    return pl.pallas_call(
        body, name="scan_fwd", grid=(nchunk,), in_specs=[blk] * 6,
        out_specs=[blk, pl.BlockSpec((1, R_HEAD, R_WIDTH), lambda i: (i, 0, 0))],
        out_shape=[jax.ShapeDtypeStruct((S, R_WIDTH), F32),
                   jax.ShapeDtypeStruct((nchunk, R_HEAD, R_WIDTH), F32)],
        scratch_shapes=[pltpu.VMEM((R_HEAD, R_WIDTH), F32)],
        compiler_params=pltpu.CompilerParams(
            dimension_semantics=("arbitrary",), vmem_limit_bytes=VMEM_LIMIT),
    )(r, w, k, v, a, b)


def _scan_bwd(r, w, k, v, a, b, ck, dy):
    S = r.shape[0]
    T = min(SCAN_T, S)
    nchunk = S // T

    def body(r_ref, w_ref, k_ref, v_ref, a_ref, b_ref, ck_ref, dy_ref,
             dr_ref, dw_ref, dk_ref, dv_ref, da_ref, db_ref, s_buf, sa_buf, vb_buf, ds_ref):
        lo, eye = _scan_consts()
        ones = _head_ones()

        @pl.when(pl.program_id(0) == 0)
        def _():
            ds_ref[...] = jnp.zeros(ds_ref.shape, F32)

        sub8 = lax.broadcasted_iota(jnp.int32, (ROW_GROUP, LANES), 0)

        def fgroup(gi, state):
            base = pl.multiple_of(gi * ROW_GROUP, ROW_GROUP)
            state = list(state)
            for q in range(N_BLK):
                sl = slice(q * LANES, (q + 1) * LANES)
                w8, k8, v8, a8, b8 = [ref[pl.ds(base, ROW_GROUP), sl]
                                      for ref in (w_ref, k_ref, v_ref, a_ref, b_ref)]
                vbs = _cols_of_rows_mxu(eye, ones, v8)
                s = state[q]
                for i in range(ROW_GROUP):
                    row = lambda t8: t8[i:i + 1, :]
                    sa = _seg_sum(lo, s * row(a8))
                    s_buf[base + i, :, sl] = s
                    sa_buf[base + i, :, sl] = sa
                    vb_buf[base + i, :, sl] = vbs[i]
                    s = s * row(w8) + sa * row(b8) + vbs[i] * row(k8)
                state[q] = s
            return tuple(state)

        lax.fori_loop(0, T // ROW_GROUP, fgroup,
                      tuple(ck_ref[0, :, q * LANES:(q + 1) * LANES] for q in range(N_BLK)))

        def bgroup(n, dstate):
            base = pl.multiple_of((T // ROW_GROUP - 1 - n) * ROW_GROUP, ROW_GROUP)
            dstate = list(dstate)
            sls = [slice(q * LANES, (q + 1) * LANES) for q in range(N_BLK)]
            ld = lambda ref: [ref[pl.ds(base, ROW_GROUP), sl] for sl in sls]
            r8, w8, k8, a8, b8, dy8 = ld(r_ref), ld(w_ref), ld(k_ref), ld(a_ref), ld(b_ref), ld(dy_ref)
            dycs = [_cols_of_rows_mxu(eye, ones, dy8[q]) for q in range(N_BLK)]
            acc = [{n_: jnp.zeros((ROW_GROUP, LANES), F32) for n_ in ("r", "w", "k", "v", "a", "b")}
                   for _ in range(N_BLK)]
            for i in reversed(range(ROW_GROUP)):
                row = lambda t8: t8[i:i + 1, :]
                put = lambda q_, n_, val: acc[q_].__setitem__(n_, jnp.where(sub8 == i, val, acc[q_][n_]))
                dks = []
                for q in range(N_BLK):
                    sl = sls[q]
                    sp, sa, vb = s_buf[base + i, :, sl], sa_buf[base + i, :, sl], vb_buf[base + i, :, sl]
                    wr, ar, br, kr, rr = row(w8[q]), row(a8[q]), row(b8[q]), row(k8[q]), row(r8[q])
                    st = sp * wr + sa * br + vb * kr
                    dyc = dycs[q][i]
                    ds = dstate[q] + dyc * rr
                    put(q, "r", _colsum(st * dyc))
                    put(q, "w", _colsum(ds * sp))
                    put(q, "b", _colsum(ds * sa))
                    put(q, "k", _colsum(ds * vb))
                    dsa = _seg_sum(lo, ds * br)
                    dks.append(ds * kr)
                    put(q, "a", _colsum(sp * dsa))
                    dstate[q] = ds * wr + dsa * ar
                dvc = _seg_sums_mxu(ones, dks)
                for q in range(N_BLK):
                    put(q, "v", _row_of_col(eye, dvc[q]))
            for q in range(N_BLK):
                for n_, ref in (("r", dr_ref), ("w", dw_ref), ("k", dk_ref), ("v", dv_ref), ("a", da_ref), ("b", db_ref)):
                    ref[pl.ds(base, ROW_GROUP), sls[q]] = acc[q][n_]
            return tuple(dstate)

        fin = lax.fori_loop(0, T // ROW_GROUP, bgroup,
                            tuple(ds_ref[:, q * LANES:(q + 1) * LANES] for q in range(N_BLK)))
        for q in range(N_BLK):
            ds_ref[:, q * LANES:(q + 1) * LANES] = fin[q]

    blk = pl.BlockSpec((T, R_WIDTH), lambda i: (nchunk - 1 - i, 0))
    ckb = pl.BlockSpec((1, R_HEAD, R_WIDTH), lambda i: (nchunk - 1 - i, 0, 0))
    return pl.pallas_call(
        body, name="scan_bwd", grid=(nchunk,), in_specs=[blk] * 6 + [ckb, blk],
        out_specs=[blk] * 6,
        out_shape=[jax.ShapeDtypeStruct((S, R_WIDTH), F32)] * 6,
        scratch_shapes=[pltpu.VMEM((T, R_HEAD, R_WIDTH), F32)] * 3 + [pltpu.VMEM((R_HEAD, R_WIDTH), F32)],
        compiler_params=pltpu.CompilerParams(
            dimension_semantics=("arbitrary",), vmem_limit_bytes=VMEM_LIMIT),
    )(r, w, k, v, a, b, ck, dy)


def _pad_in_cols(t, axis):
    cut = 2 * G_WIDTH + R_COLS
    lo, hi = lax.slice_in_dim(t, 0, cut, axis=axis), lax.slice_in_dim(t, cut, t.shape[axis], axis=axis)
    zshape = list(t.shape)
    zshape[axis] = LORA_PAD - LORA
    return jnp.concatenate([lo, jnp.zeros(zshape, t.dtype), hi], axis=axis)


def _unpad_in_cols(t, axis):
    cut = 2 * G_WIDTH + R_COLS
    return jnp.concatenate([lax.slice_in_dim(t, 0, cut, axis=axis),
                            lax.slice_in_dim(t, cut + LORA_PAD - LORA, t.shape[axis], axis=axis)], axis=axis)


def _pad_rows(t, lo, n):
    return jnp.zeros((n, t.shape[1]), t.dtype).at[lo:lo + t.shape[0]].set(t)


def _join_cols(w3):
    p, k, n = w3.shape
    return jnp.transpose(w3, (1, 0, 2)).reshape(k, p * n)


def _split_cols(w):
    k, n = w.shape
    return jnp.transpose(w.reshape(k, N_CHIPS, n // N_CHIPS), (1, 0, 2))


def _ada_dw(ccol, dmod):
    n = dmod.shape[1] // N_CHIPS
    tile = 256

    def body(c_ref, d_ref, o_ref):
        cc = c_ref[...]
        o_ref[0] = (cc * _sigmoid(cc)) * d_ref[...]

    return pl.pallas_call(
        body, name="ada_dw", grid=(N_CHIPS, D_MODEL // tile),
        in_specs=[pl.BlockSpec((tile, 1), lambda q, i: (i, 0)), pl.BlockSpec((1, n), lambda q, i: (0, q))],
        out_specs=pl.BlockSpec((1, tile, n), lambda q, i: (q, i, 0)),
        out_shape=jax.ShapeDtypeStruct((N_CHIPS, D_MODEL, n), F32),
        compiler_params=pltpu.CompilerParams(dimension_semantics=("parallel", "parallel")),
    )(ccol, dmod)


def _shift_down(t):
    return jnp.concatenate([jnp.zeros((1, t.shape[1]), t.dtype), t[:-1]], axis=0)


def _shift_up(t):
    return jnp.concatenate([t[1:], jnp.zeros((1, t.shape[1]), t.dtype)], axis=0)


def _local_step(x, c, tgt, W):
    S = x.shape[0]
    bf = MXU_DT
    G = {}

    hl = np.arange(R_WIDTH) // R_HEAD
    bd = jnp.asarray(hl[:, None] == hl[None, :], jnp.bfloat16)
    gsel = jnp.asarray(np.arange(LANES)[:, None] == (np.arange(G_WIDTH) // (G_WIDTH // G_GROUPS))[None, :],
                       jnp.bfloat16)
    w_in_p = _pad_in_cols(_join_cols(W["w_in"]), 1)
    b_in_p = _pad_in_cols(W["b_in"], 1)
    c_g, c_r = 2 * G_WIDTH, 2 * G_WIDTH + R_COLS_PAD
    w_g, w_r, w_gate = w_in_p[:, :c_g], w_in_p[:, c_g:c_r], w_in_p[:, c_r:]
    b_g, b_r, b_gate = b_in_p[:, :c_g], b_in_p[:, c_g:c_r], b_in_p[:, c_r:]
    mu_p = jnp.concatenate([W["mu_shift"], jnp.zeros((1, LORA_PAD - LORA), F32)], axis=1)
    wd_p = _pad_rows(W["w_decay_up"].astype(F32), 0, LORA_PAD)
    wa_p = _pad_rows(W["w_aaa_up"].astype(F32), DECAY_LORA, LORA_PAD)
    wg_p = _pad_rows(W["w_gate_up"].astype(F32), DECAY_LORA + AAA_LORA, LORA_PAD)
    ws2 = W["w_spatial"].reshape(G_GROUPS * CHUNK, CHUNK)
    bst = jnp.zeros((CHUNK, LANES), F32).at[:, :G_GROUPS].set(W["b_spatial"].T)
    rk = W["r_k"].reshape(1, R_WIDTH)

    c8 = jnp.broadcast_to(c, (8, D_MODEL))
    (ca8,) = _rows("ada_silu", lambda cc: cc * _sigmoid(cc), [c8], [], [(D_MODEL, bf)], tile=8)
    mod_raw = _mm("ada_mm", ca8, W["w_ada"], "nn")
    (mod8,) = _rows("ada_bias", lambda m, bb: m + bb, [mod_raw], [W["b_ada"]], [(6 * D_MODEL, F32)], tile=8)
    sh1, sc1, gt1, sh2, sc2, gt2 = [mod8[0:1, i * D_MODEL:(i + 1) * D_MODEL] for i in range(6)]

    (h,) = _rows("mod1", _modulate, [x], [sc1, sh1], [(D_MODEL, bf)])
    proj_g = _mm("proj_g", h, w_g, "nn")
    proj_r = _mm("proj_r", h, w_r, "nn")
    proj_gate = _mm("proj_gate", h, w_gate, "nn")

    def split2(t):
        return t[:, :G_WIDTH], t[:, G_WIDTH:]

    def gmlp_fwd(z, bz, g, b, bst_, gsel_, ws_):
        gmask, tril = _gmlp_consts()
        (zu, zv), (bu, bv) = split2(z), split2(bz)
        wsl = [ws_[i * CHUNK:(i + 1) * CHUNK] for i in range(G_GROUPS)]
        return _gmlp_core(gmask, tril, gsel_, zu, zv, bu, bv, g, b, bst_, *wsl)

    (y_a,) = _rows("gmlp_fwd", gmlp_fwd, [proj_g], [b_g, W["g_ln_v"], W["b_ln_v"], bst, gsel, ws2],
                   [(G_WIDTH, bf)], tile=CHUNK)

    r_cuts = (0, R_WIDTH, 2 * R_WIDTH, 3 * R_WIDTH, R_COLS_PAD)

    def split4(t):
        return [t[:, r_cuts[i]:r_cuts[i + 1]] for i in range(4)]

    tile_pre = min(256, S)

    def rowmask_of():
        grow = pl.program_id(0) * tile_pre + lax.broadcasted_iota(jnp.int32, (tile_pre, 1), 0)
        return (grow > 0).astype(F32)

    pre_params = [b_r, mu_p, W["w0"], wd_p, W["a0"], wa_p, wg_p, W["k_k"], W["k_a"], bd]

    def pre_fwd(z, p, bz, mu, w0, wd, a0, wa, wg, k_k, k_a, bd_):
        return _pre_core(rowmask_of(), bd_, *split4(z), *split4(p), *split4(bz), *split4(mu),
                         w0, wd, a0, wa, wg, k_k, k_a)

    proj_r_prev = _shift_down(proj_r)
    s_r, s_w, s_k, s_v, s_a, s_b, s_g = _rows(
        "rwkv_pre_fwd", pre_fwd, [proj_r, proj_r_prev], pre_params, [(R_WIDTH, F32)] * 7, tile=tile_pre)
    y_scan, ckpt = _scan_fwd(s_r, s_w, s_k, s_v, s_a, s_b)

    def post_fwd(y, r, k2, v, g, gain, bias, rk_, bd_):
        return _post_core(bd_, y, r, k2, v, g, gain, bias, rk_)

    post_params = [W["gn_gain"], W["gn_bias"], rk, bd]
    (y_b,) = _rows("rwkv_post_fwd", post_fwd, [y_scan, s_r, s_k, s_v, s_g], post_params, [(R_WIDTH, bf)])
    p_a = _mm("branch_a", y_a, W["w_branch_a"], "nn")
    p_b = _mm("branch_b", y_b, W["w_branch_b"], "nn")

    def merge_fwd(pa, pb, gz, bgz):
        return _merge_core(pa, pb, gz[:, :D_MODEL], gz[:, D_MODEL:], bgz[:, :D_MODEL], bgz[:, D_MODEL:])

    (merged,) = _rows("merge_fwd", merge_fwd, [p_a, p_b, proj_gate], [b_gate], [(D_MODEL, bf)])
    mix = _mm("out_proj", merged, W["w_out"], "nn")
    ln1_params = [gt1, W["b_out"], W["ln1_g"], W["ln1_b"], sc2, sh2]
    h1, h2 = _rows("ln1_fwd", _ln1_core, [x, mix], ln1_params, [(D_MODEL, F32), (D_MODEL, bf)])

    a1 = _mm("ff1", h2, W["w_ff1"], "nn")
    (act,) = _rows("ff_act", lambda z, bb: jnp.square(jnp.maximum(z + bb, 0.0)), [a1], [W["b_ff1"]], [(D_FF, bf)])
    ff = _mm("ff2", act, W["w_ff2"], "nn")

    def ln2_loss(h1_, ff_, tg, gt2_, bff2, g, b):
        loss, vjp = jax.vjp(functools.partial(_ln2_loss_core, tg), h1_, ff_, gt2_, bff2, g, b)
        return vjp(jnp.ones((), F32)) + (loss,)

    ln2_params = [gt2, W["b_ff2"], W["ln2_g"], W["ln2_b"]]
    dh1, dff, dgt2, G["b_ff2"], G["ln2_g"], G["ln2_b"], loss_acc = _rows(
        "ln2_loss", ln2_loss, [h1, ff, tgt], ln2_params, [(D_MODEL, F32), (D_MODEL, bf)],
        accs=[(1, D_MODEL)] * 4 + [(1, LANES)])
    loss = loss_acc[0, 0]

    dact = _mm("ff2_dx", dff, W["w_ff2"], "nt")
    G["w_ff2"] = _mm("ff2_dw", act, dff, "tn").reshape(N_CHIPS, D_FF // N_CHIPS, D_MODEL)

    def act_bwd(z, da, bb):
        d = da * 2.0 * jnp.maximum(z + bb, 0.0)
        return d, _colsum(d)

    da1, G["b_ff1"] = _rows("ff_act_bwd", act_bwd, [a1, dact], [W["b_ff1"]], [(D_FF, bf)], accs=[(1, D_FF)])
    dh2 = _mm("ff1_dx", da1, W["w_ff1"], "nt")
    G["w_ff1"] = _mm("ff1_dw", h2, da1, "tn", out_split=N_CHIPS)

    def ln1_bwd(x_, mix_, dh1_, dh2_, *ps):
        _, vjp = jax.vjp(_ln1_core, x_, mix_, *ps)
        return vjp((dh1_, dh2_))

    dx_res, dmix, dgt1, G["b_out"], G["ln1_g"], G["ln1_b"], dsc2, dsh2 = _rows(
        "ln1_bwd", ln1_bwd, [x, mix, dh1, dh2], ln1_params, [(D_MODEL, F32), (D_MODEL, bf)],
        accs=[(1, D_MODEL)] * 6)

    dmerged = _mm("out_proj_dx", dmix, W["w_out"], "nt")
    G["w_out"] = _mm("out_proj_dw", merged, dmix, "tn").reshape(N_CHIPS, D_MODEL // N_CHIPS, D_MODEL)

    def merge_bwd(pa, pb, gz, dm, bgz):
        args = (pa.astype(F32), pb.astype(F32), gz[:, :D_MODEL], gz[:, D_MODEL:], bgz[:, :D_MODEL], bgz[:, D_MODEL:])
        _, vjp = jax.vjp(_merge_core, *args)
        dpa, dpb, dga, dgb, dbga, dbgb = vjp(dm)
        return dpa, dpb, jnp.concatenate([dga, dgb], axis=1), jnp.concatenate([dbga, dbgb], axis=1)

    dp_a, dp_b, dgates, db_gate = _rows(
        "merge_bwd", merge_bwd, [p_a, p_b, proj_gate, dmerged], [b_gate],
        [(D_MODEL, bf), (D_MODEL, bf), (2 * D_MODEL, F32)], accs=[(1, 2 * D_MODEL)])
    dy_a = _mm("branch_a_dx", dp_a, W["w_branch_a"], "nt")
    G["w_branch_a"] = _mm("branch_a_dw", y_a, dp_a, "tn", out_split=N_CHIPS)
    dy_b = _mm("branch_b_dx", dp_b, W["w_branch_b"], "nt")
    G["w_branch_b"] = _mm("branch_b_dw", y_b, dp_b, "tn", out_split=N_CHIPS)

    def post_bwd(y, r, k2, v, g, dyb, gain, bias, rk_, bd_):
        _, vjp = jax.vjp(functools.partial(_post_core, bd_), y, r, k2, v, g, gain, bias, rk_)
        return vjp(dyb)

    dy_scan, dr_p, dk_p, dv_p, dg_p, G["gn_gain"], G["gn_bias"], drk = _rows(
        "rwkv_post_bwd", post_bwd, [y_scan, s_r, s_k, s_v, s_g, dy_b], post_params,
        [(R_WIDTH, F32)] * 5, accs=[(1, R_WIDTH)] * 3)
    G["r_k"] = drk.reshape(R_HEADS, R_HEAD)
    dr_s, dw_s, dk_s, dv_s, da_s, db_s = _scan_bwd(s_r, s_w, s_k, s_v, s_a, s_b, ckpt, dy_scan)

    def pre_bwd(z, p, dr1, dr2, dw, dk1, dk2, dv1, dv2, da, db, dg,
                bz, mu, w0, wd, a0, wa, wg, k_k, k_a, bd_):
        prim = (*split4(z), *split4(p), *split4(bz), *split4(mu), w0, wd, a0, wa, wg, k_k, k_a)
        _, vjp = jax.vjp(functools.partial(_pre_core, rowmask_of(), bd_), *prim)
        d = vjp((dr1 + dr2, dw, dk1 + dk2, dv1 + dv2, da, db, dg))
        cat = lambda parts: jnp.concatenate(parts, axis=1)
        return (cat(d[0:4]), cat(d[4:8]), cat(d[8:12]), cat(d[12:16])) + tuple(d[16:])

    dz_r, dprev, db_r, dmu_p, G["w0"], dwd_p, G["a0"], dwa_p, dwg_p, G["k_k"], G["k_a"] = _rows(
        "rwkv_pre_bwd", pre_bwd,
        [proj_r, proj_r_prev, dr_s, dr_p, dw_s, dk_s, dk_p, dv_s, dv_p, da_s, db_s, dg_p],
        pre_params, [(R_COLS_PAD, F32)] * 2,
        accs=[(1, R_COLS_PAD), (1, R_COLS_PAD), (1, R_WIDTH), (LORA_PAD, R_WIDTH), (1, R_WIDTH),
              (LORA_PAD, R_WIDTH), (LORA_PAD, R_WIDTH), (1, R_WIDTH), (1, R_WIDTH)],
        tile=tile_pre)
    G["mu_shift"] = dmu_p[:, :R_COLS]
    G["w_decay_up"] = dwd_p[:DECAY_LORA]
    G["w_aaa_up"] = dwa_p[DECAY_LORA:DECAY_LORA + AAA_LORA]
    G["w_gate_up"] = dwg_p[DECAY_LORA + AAA_LORA:LORA]

    def gmlp_bwd(z, dya, bz, g, b, bst_, gsel_, ws_):
        gmask, tril = _gmlp_consts()
        (zu, zv), (bu, bv) = split2(z), split2(bz)
        wsl = [ws_[i * CHUNK:(i + 1) * CHUNK] for i in range(G_GROUPS)]
        _, vjp = jax.vjp(functools.partial(_gmlp_core, gmask, tril, gsel_), zu, zv, bu, bv, g, b, bst_, *wsl)
        d = vjp(dya)
        return (jnp.concatenate(d[0:2], axis=1), jnp.concatenate(d[2:4], axis=1), d[4], d[5], d[6],
                jnp.concatenate(d[7:], axis=0))

    dz_g, db_g, G["g_ln_v"], G["b_ln_v"], dbst, dws2 = _rows(
        "gmlp_bwd", gmlp_bwd, [proj_g, dy_a], [b_g, W["g_ln_v"], W["b_ln_v"], bst, gsel, ws2],
        [(2 * G_WIDTH, F32)],
        accs=[(1, 2 * G_WIDTH), (1, G_WIDTH), (1, G_WIDTH), (CHUNK, LANES), (G_GROUPS * CHUNK, CHUNK)],
        tile=CHUNK)
    G["w_spatial"] = dws2.reshape(G_GROUPS, CHUNK, CHUNK)
    G["b_spatial"] = dbst[:, :G_GROUPS].T

    def dproj_cat(dzg, dzr, dpv, dgz):
        return jnp.concatenate([dzg, dzr + dpv, dgz], axis=1)

    (dproj,) = _rows("dproj_cat", dproj_cat, [dz_g, dz_r, _shift_up(dprev), dgates], [],
                     [(2 * G_WIDTH + R_COLS_PAD + 2 * D_MODEL, bf)])
    dh = _mm("proj_dx", dproj, w_in_p, "nt")
    G["w_in"] = _split_cols(_unpad_in_cols(_mm("proj_dw", h, dproj, "tn"), 1))
    G["b_in"] = _unpad_in_cols(jnp.concatenate([db_g, db_r, db_gate], axis=1), 1)

    def mod1_bwd(x_, dh_, dxr, sc):
        return dh_ * (1.0 + sc) + dxr, _colsum(dh_ * x_), _colsum(dh_)

    grad_x, dsc1, dsh1 = _rows("mod1_bwd", mod1_bwd, [x, dh, dx_res], [sc1], [(D_MODEL, F32)],
                               accs=[(1, D_MODEL)] * 2)

    dmod = jnp.concatenate([dsh1, dsc1, dgt1, dsh2, dsc2, dgt2], axis=1)
    G["b_ada"] = dmod
    G["w_ada"] = _ada_dw(c.reshape(D_MODEL, 1), dmod)
    return loss, grad_x, G


BIG = (("w_ada", (D_MODEL, 6 * D_MODEL), 1), ("w_in", (D_MODEL, 2 * G_WIDTH + R_COLS + 2 * D_MODEL), 1),
       ("w_branch_a", (G_WIDTH, D_MODEL), 1), ("w_branch_b", (R_WIDTH, D_MODEL), 1),
       ("w_out", (D_MODEL, D_MODEL), 0), ("w_ff1", (D_MODEL, D_FF), 1), ("w_ff2", (D_FF, D_MODEL), 0))
LORAS = (("w_decay_up", (DECAY_LORA, R_WIDTH), 1), ("w_aaa_up", (AAA_LORA, R_WIDTH), 1),
         ("w_gate_up", (GATE_LORA, R_WIDTH), 1))
SHARDED = BIG + LORAS
SMALL = (("b_ada", (1, 6 * D_MODEL)), ("b_in", (1, 2 * G_WIDTH + R_COLS + 2 * D_MODEL)),
         ("g_ln_v", (1, G_WIDTH)), ("b_ln_v", (1, G_WIDTH)), ("w_spatial", (G_GROUPS, CHUNK, CHUNK)),
         ("b_spatial", (G_GROUPS, CHUNK)), ("mu_shift", (1, R_COLS)), ("w0", (1, R_WIDTH)),
         ("a0", (1, R_WIDTH)), ("k_k", (1, R_WIDTH)), ("k_a", (1, R_WIDTH)), ("r_k", (R_HEADS, R_HEAD)),
         ("gn_gain", (1, R_WIDTH)), ("gn_bias", (1, R_WIDTH)), ("b_out", (1, D_MODEL)),
         ("ln1_g", (1, D_MODEL)), ("ln1_b", (1, D_MODEL)), ("b_ff1", (1, D_FF)), ("b_ff2", (1, D_MODEL)),
         ("ln2_g", (1, D_MODEL)), ("ln2_b", (1, D_MODEL)))
WEIGHT_ORDER = ("w_ada", "b_ada", "w_in", "b_in", "g_ln_v", "b_ln_v", "w_spatial", "b_spatial", "mu_shift",
                "w0", "w_decay_up", "a0", "w_aaa_up", "w_gate_up", "k_k", "k_a", "r_k", "gn_gain", "gn_bias",
                "w_branch_a", "w_branch_b", "w_out", "b_out", "ln1_g", "ln1_b", "w_ff1", "b_ff1", "w_ff2",
                "b_ff2", "ln2_g", "ln2_b")
N_CHIPS = 4


def _shard_shape(shape, axis):
    s = list(shape)
    s[axis] //= N_CHIPS
    return tuple(s)


def _numel(shape):
    return int(np.prod(shape))


def _round_up(n, q):
    return -(-n // q) * q


N_LORA = sum(_numel(_shard_shape(s, a)) for _, s, a in LORAS)
N_SMALL = sum(_numel(s) for _, s in SMALL)
ROWS_SW = _round_up(N_LORA, PACK_Q) // PACK_W
ROWS_SG = _round_up(N_LORA + N_SMALL, PACK_Q) // PACK_W


def _pack_small(loras, small, rows, dtype):
    parts = [loras[n] for n, _, _ in LORAS] + ([small[n] for n, _ in SMALL] if small is not None else [])
    flat = jnp.concatenate([p.reshape(-1).astype(dtype) for p in parts])
    flat = jnp.concatenate([flat, jnp.zeros((rows * PACK_W - flat.shape[0],), dtype)])
    return flat.reshape(rows, PACK_W)


def _unpack_small(pack, with_small):
    flat = pack.reshape(-1)
    out, off = {}, 0
    for n, s, a in LORAS:
        ss = _shard_shape(s, a)
        out[n] = flat[off:off + _numel(ss)].reshape(ss)
        off += _numel(ss)
    if with_small:
        for n, s in SMALL:
            out[n] = flat[off:off + _numel(s)].reshape(s)
            off += _numel(s)
    return out


def _pack_small_grads(G):
    segs = []
    for q in range(N_CHIPS):
        loras = {n: G[n][:, q * (s[1] // N_CHIPS):(q + 1) * (s[1] // N_CHIPS)] for n, s, _ in LORAS}
        segs.append(_pack_small(loras, G, ROWS_SG, WIRE_DT))
    return jnp.stack(segs)


ANY = pl.BlockSpec(memory_space=pl.ANY)
MESH = pl.DeviceIdType.MESH


def _place():
    x, y, c = lax.axis_index("x"), lax.axis_index("y"), lax.axis_index("c")
    chips = [(1 - x, y), (x, 1 - y), (1 - x, 1 - y)]
    return x, y, c, chips


def _remote(src, dst, send_sem, recv_sem, to):
    return pltpu.make_async_remote_copy(src_ref=src, dst_ref=dst, send_sem=send_sem, recv_sem=recv_sem,
                                        device_id=to, device_id_type=MESH)


def _all_gather(shards):
    n = len(shards)
    halves = [a.shape[0] // 2 for a in shards]

    def body(*refs):
        ins, outs = refs[:n], refs[n:2 * n]
        send_sems, recv_sems, local_sems = refs[2 * n:]
        x, y, c, chips = _place()
        s = 2 * x + y
        sibling = (x, y, 1 - c)
        slot = [2 * chip[0] + chip[1] for chip in chips]
        own, first, passed = [], [], []
        for a in range(n):
            H = halves[a]
            own += [pltpu.make_async_copy(ins[a].at[pl.ds(hf * H, H)], outs[a].at[2 * s + hf], local_sems.at[2 * a + hf])
                    for hf in range(2)]
            mine = ins[a].at[pl.ds(c * H, H)]
            first += [_remote(mine, outs[a].at[2 * s + c], send_sems.at[6 * a + j], recv_sems.at[6 * a + j], (*chip, c))
                      for j, chip in enumerate(chips)]
        for cp in own + first:
            cp.start()
        for a in range(n):
            for j in range(3):
                landed = outs[a].at[2 * slot[j] + c]
                _remote(landed, landed, send_sems.at[6 * a + j], recv_sems.at[6 * a + j], sibling).wait_recv()
                cp = _remote(landed, landed, send_sems.at[6 * a + 3 + j], recv_sems.at[6 * a + 3 + j], sibling)
                cp.start()
                passed.append(cp)
        for a in range(n):
            for j in range(3):
                got = outs[a].at[2 * slot[j] + 1 - c]
                _remote(got, got, send_sems.at[6 * a + 3 + j], recv_sems.at[6 * a + 3 + j], sibling).wait_recv()
        for cp in first + passed:
            cp.wait_send()
        for cp in own:
            cp.wait()

    return pl.pallas_call(
        body, name="ag_weights", in_specs=[ANY] * n, out_specs=[ANY] * n,
        out_shape=[jax.ShapeDtypeStruct((2 * N_CHIPS, h, a.shape[1]), a.dtype) for a, h in zip(shards, halves)],
        scratch_shapes=[pltpu.SemaphoreType.DMA((6 * n,)), pltpu.SemaphoreType.DMA((6 * n,)),
                        pltpu.SemaphoreType.DMA((2 * n,))],
    )(*shards)


def _rs_sibling_in(gps):
    n = len(gps)

    def body(*refs):
        ins, outs = refs[:n], refs[n:2 * n]
        send_sems, recv_sems = refs[2 * n:]
        x, y, c, _ = _place()
        cps = []
        for a in range(n):
            H = gps[a].shape[1] // 2
            cps += [_remote(ins[a].at[q, pl.ds((1 - c) * H, H)], outs[a].at[q], send_sems.at[N_CHIPS * a + q],
                            recv_sems.at[N_CHIPS * a + q], (x, y, 1 - c)) for q in range(N_CHIPS)]
        for cp in cps:
            cp.start()
        for cp in cps:
            cp.wait()

    return pl.pallas_call(
        body, name="rs_sibling_in", in_specs=[ANY] * n, out_specs=[ANY] * n,
        out_shape=[jax.ShapeDtypeStruct((N_CHIPS, g.shape[1] // 2, g.shape[2]), g.dtype) for g in gps],
        scratch_shapes=[pltpu.SemaphoreType.DMA((N_CHIPS * n,)), pltpu.SemaphoreType.DMA((N_CHIPS * n,))],
    )(*gps)


def _rs_add_own(name, gp, got, c_arr, tr=256):
    H, C = got.shape[1:]
    tr = _pick(H, tr, 8)
    nb = H // tr

    def body(c_ref, g_ref, r_ref, o_ref):
        o_ref[...] = g_ref[...].astype(F32) + r_ref[...].astype(F32)

    return pl.pallas_call(
        body, name="rs_add_own_" + name,
        grid_spec=pltpu.PrefetchScalarGridSpec(
            num_scalar_prefetch=1, grid=(N_CHIPS, nb),
            in_specs=[pl.BlockSpec((1, tr, C), lambda q, i, c_ref: (q, c_ref[0] * nb + i, 0)),
                      pl.BlockSpec((1, tr, C), lambda q, i, c_ref: (q, i, 0))],
            out_specs=pl.BlockSpec((1, tr, C), lambda q, i, c_ref: (q, i, 0))),
        out_shape=jax.ShapeDtypeStruct((N_CHIPS, H, C), WIRE_DT),
        compiler_params=pltpu.CompilerParams(dimension_semantics=("arbitrary", "arbitrary")),
    )(c_arr, gp, got)


def _rs_chips(parts):
    n = len(parts)

    def body(*refs):
        ins, outs = refs[:n], refs[n:2 * n]
        send_sems, recv_sems, local_sems = refs[2 * n:]
        x, y, c, chips = _place()
        s = 2 * x + y
        slot = [2 * chip[0] + chip[1] for chip in chips]
        own = [pltpu.make_async_copy(ins[a].at[s], outs[a].at[s], local_sems.at[a]) for a in range(n)]
        cps = [_remote(ins[a].at[slot[j]], outs[a].at[s], send_sems.at[3 * a + j], recv_sems.at[3 * a + j], (*chips[j], c))
               for a in range(n) for j in range(3)]
        for cp in own + cps:
            cp.start()
        for a in range(n):
            for j in range(3):
                cps[3 * a + j].wait_send()
                got = outs[a].at[slot[j]]
                _remote(got, got, send_sems.at[3 * a + j], recv_sems.at[3 * a + j], (*chips[j], c)).wait_recv()
        for cp in own:
            cp.wait()

    return pl.pallas_call(
        body, name="rs_chips", in_specs=[ANY] * n, out_specs=[ANY] * n,
        out_shape=[jax.ShapeDtypeStruct(p.shape, p.dtype) for p in parts],
        scratch_shapes=[pltpu.SemaphoreType.DMA((3 * n,)), pltpu.SemaphoreType.DMA((3 * n,)),
                        pltpu.SemaphoreType.DMA((n,))],
    )(*parts)


def _rs_add_chips(name, slots, tr=128):
    H, C = slots.shape[1:]
    tr = _pick(H, tr, 8)

    def body(s_ref, o_ref):
        o_ref[...] = ((s_ref[0].astype(F32) + s_ref[1].astype(F32)) + s_ref[2].astype(F32)) + s_ref[3].astype(F32)

    return pl.pallas_call(
        body, name="rs_add_chips_" + name, grid=(H // tr,),
        in_specs=[pl.BlockSpec((N_CHIPS, tr, C), lambda i: (0, i, 0))],
        out_specs=pl.BlockSpec((tr, C), lambda i: (i, 0)),
        out_shape=jax.ShapeDtypeStruct((H, C), F32),
        compiler_params=pltpu.CompilerParams(dimension_semantics=("arbitrary",)),
    )(slots)


def _rs_sibling_out(halves):
    n = len(halves)

    def body(*refs):
        ins, outs = refs[:n], refs[n:2 * n]
        send_sems, recv_sems, local_sems = refs[2 * n:]
        x, y, c, _ = _place()
        own, cps = [], []
        for a in range(n):
            H = halves[a].shape[0]
            own.append(pltpu.make_async_copy(ins[a], outs[a].at[pl.ds(c * H, H)], local_sems.at[a]))
            cps.append(_remote(ins[a], outs[a].at[pl.ds(c * H, H)], send_sems.at[a], recv_sems.at[a], (x, y, 1 - c)))
        for cp in own + cps:
            cp.start()
        for a in range(n):
            H = halves[a].shape[0]
            cps[a].wait_send()
            got = outs[a].at[pl.ds((1 - c) * H, H)]
            _remote(got, got, send_sems.at[a], recv_sems.at[a], (x, y, 1 - c)).wait_recv()
        for cp in own:
            cp.wait()

    return pl.pallas_call(
        body, name="rs_sibling_out", in_specs=[ANY] * n, out_specs=[ANY] * n,
        out_shape=[jax.ShapeDtypeStruct((2 * h.shape[0], h.shape[1]), h.dtype) for h in halves],
        scratch_shapes=[pltpu.SemaphoreType.DMA((n,)), pltpu.SemaphoreType.DMA((n,)), pltpu.SemaphoreType.DMA((n,))],
    )(*halves)


def _adamw(name, w, g, m, v):
    def fn(w_, g_, m_, v_):
        m2 = ADAM_B1 * m_ + (1.0 - ADAM_B1) * g_
        v2 = ADAM_B2 * v_ + (1.0 - ADAM_B2) * (g_ * g_)
        m_hat = m2 / (1.0 - ADAM_B1 ** ADAM_STEP)
        v_hat = v2 / (1.0 - ADAM_B2 ** ADAM_STEP)
        return -ADAM_LR * (m_hat / (jnp.sqrt(v_hat) + ADAM_EPS) + ADAM_WD * w_), m2, v2

    return _rows("adamw_" + name, fn, [w, g, m, v], [], [(w.shape[1], F32)] * 3, tile=_pick(w.shape[0], 256, 8))


def kernel(x, c, w_ada, b_ada, w_in, b_in, g_ln_v, b_ln_v, w_spatial, b_spatial, mu_shift, w0, w_decay_up, a0, w_aaa_up, w_gate_up, k_k, k_a, r_k, gn_gain, gn_bias, w_branch_a, w_branch_b, w_out, b_out, ln1_g, ln1_b, w_ff1, b_ff1, w_ff2, b_ff2, ln2_g, ln2_b, loss_target, m_w_ada, m_b_ada, m_w_in, m_b_in, m_g_ln_v, m_b_ln_v, m_w_spatial, m_b_spatial, m_mu_shift, m_w0, m_w_decay_up, m_a0, m_w_aaa_up, m_w_gate_up, m_k_k, m_k_a, m_r_k, m_gn_gain, m_gn_bias, m_w_branch_a, m_w_branch_b, m_w_out, m_b_out, m_ln1_g, m_ln1_b, m_w_ff1, m_b_ff1, m_w_ff2, m_b_ff2, m_ln2_g, m_ln2_b, v_w_ada, v_b_ada, v_w_in, v_b_in, v_g_ln_v, v_b_ln_v, v_w_spatial, v_b_spatial, v_mu_shift, v_w0, v_w_decay_up, v_a0, v_w_aaa_up, v_w_gate_up, v_k_k, v_k_a, v_r_k, v_gn_gain, v_gn_bias, v_w_branch_a, v_w_branch_b, v_w_out, v_b_out, v_ln1_g, v_ln1_b, v_w_ff1, v_b_ff1, v_w_ff2, v_b_ff2, v_ln2_g, v_ln2_b):
    args = dict(locals())
    local_shape = {n: _shard_shape(s, a) for n, s, a in SHARDED}
    local_shape.update(dict(SMALL))
    wts = {n: args[n].reshape(local_shape[n]) for n in WEIGHT_ORDER}
    mom = {n: args["m_" + n].reshape(local_shape[n]) for n in WEIGHT_ORDER}
    var = {n: args["v_" + n].reshape(local_shape[n]) for n in WEIGHT_ORDER}
    big = [n for n, _, _ in BIG]

    gath = _all_gather([wts[n].astype(MXU_DT) for n in big] + [_pack_small(wts, None, ROWS_SW, MXU_DT)])
    W = {n: wts[n] for n, _ in SMALL}
    for (n, s, a), g in zip(BIG, gath):
        r, cdim = _shard_shape(s, a)
        W[n] = g.reshape(N_CHIPS, r, cdim) if a == 1 else g.reshape(s)
    lora_q = [_unpack_small(gath[-1].reshape(N_CHIPS, ROWS_SW, PACK_W)[q], False) for q in range(N_CHIPS)]
    for n, _, _ in LORAS:
        W[n] = jnp.concatenate([lora_q[q][n] for q in range(N_CHIPS)], axis=1)

    loss, grad_x, G = _local_step(x[0], c, loss_target[0], W)
    loss = lax.psum(loss, MESH_AXES)

    names = big + ["small"]
    gps = [G[n].astype(WIRE_DT) for n in big] + [_pack_small_grads(G)]
    c_arr = lax.axis_index("c").astype(jnp.int32).reshape(1)
    parts = [_rs_add_own(n, g, r, c_arr) for n, g, r in zip(names, gps, _rs_sibling_in(gps))]
    halves = [_rs_add_chips(n, s) for n, s in zip(names, _rs_chips(parts))]
    segs = _rs_sibling_out(halves)

    out = {}
    for n, g in zip(big, segs[:-1]):
        out[n] = (g,) + tuple(_adamw(n, wts[n], g, mom[n], var[n]))
    packs = [_pack_small(t, t, ROWS_SG, F32) for t in (wts, mom, var)]
    small4 = [_unpack_small(t, True) for t in (segs[-1],) + tuple(_adamw("small", packs[0], segs[-1], packs[1], packs[2]))]
    res = [loss, grad_x[None]]
    for k in range(4):
        res += [(out[n][k] if n in out else small4[k][n]).reshape(args[n].shape) for n in WEIGHT_ORDER]
    return tuple(res)
```

```python
import functools

import numpy as np
import jax
import jax.numpy as jnp
from jax import lax
from jax.experimental import pallas as pl
from jax.experimental.pallas import tpu as pltpu

F32 = jnp.float32
MXU_DT = jnp.bfloat16
WIRE_DT = jnp.bfloat16

D_MODEL = 1024
G_GROUPS = 8
G_WIDTH = 512
CHUNK = 128
R_WIDTH = 512
R_HEAD = 64
R_HEADS = 8
DECAY_LORA = 32
AAA_LORA = 32
GATE_LORA = 96
LORA = DECAY_LORA + AAA_LORA + GATE_LORA
LORA_PAD = 256
R_COLS = 3 * R_WIDTH + LORA
R_COLS_PAD = 3 * R_WIDTH + LORA_PAD
D_FF = 4 * D_MODEL
ALPHA = 2.0 ** 0.25
LN_EPS = 1e-5
GN_EPS = 64e-5
ADAM_LR = 0.001
ADAM_B1 = 0.9
ADAM_B2 = 0.999
ADAM_EPS = 1e-08
ADAM_WD = 0.01
ADAM_STEP = 10

LANES = 128
PACK_W = 512
PACK_Q = 2 * 16 * PACK_W
VMEM_LIMIT = 48 * 1024 * 1024
SCAN_T = 64

MESH_AXES = ("x", "y", "c")


def _dg(a, b, dims):
    return lax.dot_general(a.astype(MXU_DT), b.astype(MXU_DT), (dims, ((), ())),
                           preferred_element_type=F32)


@jax.custom_vjp
def _bdot(a, b):
    return _dg(a, b, ((1,), (0,)))


def _bdot_fwd(a, b):
    return _bdot(a, b), (a, b)


def _bdot_bwd(res, g):
    a, b = res
    return (_dg(g, b, ((1,), (1,))).astype(a.dtype), _dg(a, g, ((0,), (0,))).astype(b.dtype))


_bdot.defvjp(_bdot_fwd, _bdot_bwd)


def _split_dot(x, m, dims):
    hi = x.astype(jnp.bfloat16)
    lo = (x - hi.astype(F32)).astype(jnp.bfloat16)
    dn = (dims, ((), ()))
    return (lax.dot_general(hi, m, dn, preferred_element_type=F32)
            + lax.dot_general(lo, m, dn, preferred_element_type=F32))


@jax.custom_vjp
def _pdot(x, m):
    return _split_dot(x, m, ((1,), (0,)))


def _pdot_fwd(x, m):
    return _pdot(x, m), m


def _pdot_bwd(m, g):
    return _split_dot(g, m, ((1,), (1,))), None


_pdot.defvjp(_pdot_fwd, _pdot_bwd)


def _sigmoid(x):
    return 1.0 / (1.0 + jnp.exp(-x))


def _softplus(x):
    return jnp.maximum(x, 0.0) + jnp.log(1.0 + jnp.exp(-jnp.maximum(x, -x)))


def _gelu(x):
    return 0.5 * x * (1.0 + jnp.tanh(0.7978845608028654 * (x + 0.044715 * (x * x * x))))


def _ln(x, g, b, eps):
    mu = jnp.mean(x, axis=-1, keepdims=True)
    xc = x - mu
    var = jnp.mean(xc * xc, axis=-1, keepdims=True)
    return xc * lax.rsqrt(var + eps) * g + b


def _colsum(x):
    return jnp.sum(x, axis=0, keepdims=True)


def _pick(n, target, q=LANES):
    if n <= target:
        return n
    best = None
    for t in range(q, target + 1, q):
        if n % t == 0:
            best = t
    assert best is not None, (n, target)
    return best


def _mm(name, a, b, mode, out_dtype=F32, out_split=1, tm=1024, tn=1024, tk=1024):
    bs = b.shape[0] if b.ndim == 3 else 1
    br, bc = b.shape[-2:]
    if mode == "nn":
        (M, K), K2, N = a.shape, br, bc * bs
    elif mode == "nt":
        (M, K), N, K2 = a.shape, br, bc * bs
    else:
        assert bs == 1
        (K, M), K2, N = a.shape, br, bc
    assert K == K2, (name, a.shape, b.shape, mode)
    n_piece = N // max(bs if mode == "nn" else 1, out_split)
    k_piece = K // (bs if mode == "nt" else 1)
    tm, tn, tk = _pick(M, tm, 8 if M < LANES else LANES), _pick(n_piece, tn), _pick(k_piece, tk)
    nk, npj, npk = K // tk, n_piece // tn, k_piece // tk
    dims = {"nn": ((1,), (0,)), "nt": ((1,), (1,)), "tn": ((0,), (0,))}[mode]

    def body(a_ref, b_ref, o_ref, acc_ref):
        k = pl.program_id(2)

        @pl.when(k == 0)
        def _():
            acc_ref[...] = jnp.zeros(acc_ref.shape, F32)

        acc_ref[...] += _dg(a_ref[...], b_ref[0] if bs > 1 else b_ref[...], dims)

        @pl.when(k == nk - 1)
        def _():
            if out_split > 1:
                o_ref[0] = acc_ref[...].astype(o_ref.dtype)
            else:
                o_ref[...] = acc_ref[...].astype(o_ref.dtype)

    if mode == "nn":
        a_spec = pl.BlockSpec((tm, tk), lambda i, j, k: (i, k))
        b_spec = (pl.BlockSpec((tk, tn), lambda i, j, k: (k, j)) if bs == 1 else
                  pl.BlockSpec((1, tk, tn), lambda i, j, k: (j // npj, k, j % npj)))
    elif mode == "nt":
        a_spec = pl.BlockSpec((tm, tk), lambda i, j, k: (i, k))
        b_spec = (pl.BlockSpec((tn, tk), lambda i, j, k: (j, k)) if bs == 1 else
                  pl.BlockSpec((1, tn, tk), lambda i, j, k: (k // npk, j, k % npk)))
    else:
        a_spec = pl.BlockSpec((tk, tm), lambda i, j, k: (k, i))
        b_spec = pl.BlockSpec((tk, tn), lambda i, j, k: (k, j))
    if out_split > 1:
        o_spec = pl.BlockSpec((1, tm, tn), lambda i, j, k: (j // npj, i, j % npj))
        o_shape = jax.ShapeDtypeStruct((out_split, M, n_piece), out_dtype)
    else:
        o_spec = pl.BlockSpec((tm, tn), lambda i, j, k: (i, j))
        o_shape = jax.ShapeDtypeStruct((M, N), out_dtype)
    return pl.pallas_call(
        body, name=name, grid=(M // tm, N // tn, nk),
        in_specs=[a_spec, b_spec], out_specs=o_spec, out_shape=o_shape,
        scratch_shapes=[pltpu.VMEM((tm, tn), F32)],
        compiler_params=pltpu.CompilerParams(
            dimension_semantics=("parallel", "parallel", "arbitrary"), vmem_limit_bytes=VMEM_LIMIT),
    )(a, b)


def _rows(name, fn, rows, params, outs, accs=(), tile=256):
    S = rows[0].shape[0]
    tile = min(tile, S)
    assert S % tile == 0, (name, S, tile)
    nr, npar, no, na = len(rows), len(params), len(outs), len(accs)

    def body(*refs):
        rin, pin = refs[:nr], refs[nr:nr + npar]
        oref, aref = refs[nr + npar:nr + npar + no], refs[nr + npar + no:]
        res = fn(*[r[...] for r in rin], *[p[...] for p in pin])
        if not isinstance(res, (tuple, list)):
            res = (res,)
        assert len(res) == no + na, (name, len(res), no, na)
        for ref, val in zip(oref, res[:no]):
            ref[...] = val.astype(ref.dtype)
        if na:
            @pl.when(pl.program_id(0) == 0)
            def _():
                for ref in aref:
                    ref[...] = jnp.zeros(ref.shape, ref.dtype)

            for ref, val in zip(aref, res[no:]):
                ref[...] += jnp.broadcast_to(val, ref.shape).astype(ref.dtype)

    def whole(shape):
        nd = len(shape)
        return pl.BlockSpec(tuple(shape), lambda i: (0,) * nd)

    in_specs = ([pl.BlockSpec((tile, r.shape[1]), lambda i: (i, 0)) for r in rows]
                + [whole(p.shape) for p in params])
    out_specs = ([pl.BlockSpec((tile, n), lambda i: (i, 0)) for n, _ in outs]
                 + [whole(s) for s in accs])
    out_shape = ([jax.ShapeDtypeStruct((S, n), dt) for n, dt in outs]
                 + [jax.ShapeDtypeStruct(tuple(s), F32) for s in accs])
    res = pl.pallas_call(
        body, name=name, grid=(S // tile,), in_specs=in_specs, out_specs=out_specs,
        out_shape=out_shape,
        compiler_params=pltpu.CompilerParams(
            dimension_semantics=("arbitrary",), vmem_limit_bytes=VMEM_LIMIT),
    )(*rows, *params)
    return res


def _modulate(x, sc, sh):
    return x * (1.0 + sc) + sh


def _gmlp_consts():
    lane = lax.broadcasted_iota(jnp.int32, (1, G_WIDTH), 1)
    gmask = [(lane // (G_WIDTH // G_GROUPS) == g).astype(F32) for g in range(G_GROUPS)]
    tril = (lax.broadcasted_iota(jnp.int32, (CHUNK, CHUNK), 0)
            >= lax.broadcasted_iota(jnp.int32, (CHUNK, CHUNK), 1))
    return gmask, tril


def _gmlp_core(gmask, tril, gsel, zu, zv, bu, bv, g, b, bst, *ws):
    u = _gelu(zu + bu)
    v = _ln(_gelu(zv + bv), g, b, LN_EPS)
    s = _pdot(bst, gsel)
    for gi in range(G_GROUPS):
        s = s + _bdot(jnp.where(tril, ws[gi], 0.0), v * gmask[gi])
    return u * s


def _pre_core(rowmask, bd, zr, zk, zv, zl, pr, pk, pv, pq, br, bk, bv, bl, mr, mk, mv, ml,
              w0, wd, a0, wa, wg, k_k, k_a):
    def mix(z, p, b, mu):
        zz = z + b
        return zz + ((p + b) * rowmask - zz) * mu

    r, k, v, l = mix(zr, pr, br, mr), mix(zk, pk, bk, mk), mix(zv, pv, bv, mv), mix(zl, pq, bl, ml)
    w_log = -_softplus(-(w0 + _bdot(jnp.tanh(l), wd))) - 0.5
    decay = jnp.exp(-jnp.exp(w_log))
    a = _sigmoid(a0 + _bdot(l, wa))
    g = _bdot(_sigmoid(l), wg)
    kk = k * k_k
    kkn = kk / jnp.maximum(jnp.sqrt(_pdot(kk * kk, bd)), 1e-12)
    k2 = k * (1.0 + (a - 1.0) * k_a)
    return r, decay, k2, v, -kkn, kkn * a, g


def _post_core(bd, y, r, k2, v, g, gain, bias, rk):
    inv = 1.0 / R_HEAD
    mu = _pdot(y, bd) * inv
    yc = y - mu
    var = _pdot(yc * yc, bd) * inv
    yn = yc * lax.rsqrt(var + GN_EPS) * gain + bias
    bonus = _pdot(r * k2 * rk, bd) * v
    return (yn + bonus) * g


def _merge_core(pa, pb, ga, gb, bga, bgb):
    return _sigmoid(ga + bga) * pa + _sigmoid(gb + bgb) * pb


def _ln1_core(x, mix, gt1, bout, g, b, sc2, sh2):
    h1 = _ln(ALPHA * x + gt1 * (mix + bout), g, b, LN_EPS)
    return h1, h1 * (1.0 + sc2) + sh2


def _ln2_loss_core(tgt, h1, ff, gt2, bff2, g, b):
    out = _ln(ALPHA * h1 + gt2 * (ff + bff2), g, b, LN_EPS)
    err = out - tgt
    return 0.5 * jnp.sum(err * err) * (1.0 / D_MODEL)


def _scan_consts():
    sub = lax.broadcasted_iota(jnp.int32, (R_HEAD, LANES), 0)
    lane = lax.broadcasted_iota(jnp.int32, (R_HEAD, LANES), 1)
    return lane < R_HEAD, sub == (lane & (R_HEAD - 1))


def _seg_sum(lo, xb):
    s_lo = jnp.sum(jnp.where(lo, xb, 0.0), axis=1, keepdims=True)
    s_hi = jnp.sum(jnp.where(lo, 0.0, xb), axis=1, keepdims=True)
    return jnp.where(lo, s_lo, s_hi)


def _seg_dot(lo, s, row):
    lo_row = lo[0:1, :]
    s_lo = jnp.sum(s * jnp.where(lo_row, row, 0.0), axis=1, keepdims=True)
    s_hi = jnp.sum(s * jnp.where(lo_row, 0.0, row), axis=1, keepdims=True)
    return jnp.where(lo, s_lo, s_hi)


def _row_of_col(eye, colb):
    return jnp.sum(jnp.where(eye, colb, 0.0), axis=0, keepdims=True)


def _head_ones():
    i = lax.broadcasted_iota(jnp.int32, (LANES, LANES), 0) // R_HEAD
    j = lax.broadcasted_iota(jnp.int32, (LANES, LANES), 1) // R_HEAD
    return (i == j).astype(jnp.bfloat16)


N_SPLIT = 2


def _split(x):
    parts, r = [], x
    for u in range(N_SPLIT):
        p = r.astype(jnp.bfloat16)
        parts.append(p)
        if u + 1 < N_SPLIT:
            r = r - p.astype(F32)
    return parts


def _seg_sums_mxu(ones, mats):
    parts = [p for m in mats for p in _split(m)]
    res = lax.dot_general(jnp.concatenate(parts, axis=0), ones, (((1,), (0,)), ((), ())),
                          preferred_element_type=F32)
    out = []
    for n in range(len(mats)):
        t = [res[(N_SPLIT * n + u) * R_HEAD:(N_SPLIT * n + u + 1) * R_HEAD] for u in range(N_SPLIT)]
        out.append(functools.reduce(lambda p, q: p + q, t))
    return out


def _cols_of_rows_mxu(eye, ones, rows8):
    return _seg_sums_mxu(ones, [jnp.where(eye, jnp.broadcast_to(rows8[i:i + 1, :], eye.shape), 0.0)
                                for i in range(rows8.shape[0])])


N_BLK = R_WIDTH // LANES
ROW_GROUP = 8


def _scan_fwd(r, w, k, v, a, b):
    S = r.shape[0]
    T = min(SCAN_T, S)
    nchunk = S // T

    def body(r_ref, w_ref, k_ref, v_ref, a_ref, b_ref, y_ref, ck_ref, st_ref):
        lo, eye = _scan_consts()
        ones = _head_ones()

        @pl.when(pl.program_id(0) == 0)
        def _():
            st_ref[...] = jnp.zeros(st_ref.shape, F32)

        ck_ref[0] = st_ref[...]
        sub8 = lax.broadcasted_iota(jnp.int32, (ROW_GROUP, LANES), 0)

        def group(gi, state):
            base = pl.multiple_of(gi * ROW_GROUP, ROW_GROUP)
            state = list(state)
            sls = [slice(q * LANES, (q + 1) * LANES) for q in range(N_BLK)]
            ld = lambda ref: [ref[pl.ds(base, ROW_GROUP), sl] for sl in sls]
            r8, w8, k8, v8, a8, b8 = ld(r_ref), ld(w_ref), ld(k_ref), ld(v_ref), ld(a_ref), ld(b_ref)
            vb = [_cols_of_rows_mxu(eye, ones, v8[q]) for q in range(N_BLK)]
            y8 = [jnp.zeros((ROW_GROUP, LANES), F32)] * N_BLK
            for i in range(ROW_GROUP):
                row = lambda t8: t8[i:i + 1, :]
                for q in range(N_BLK):
                    s = state[q]
                    sa = _seg_dot(lo, s, row(a8[q]))
                    state[q] = s * row(w8[q]) + sa * row(b8[q]) + vb[q][i] * row(k8[q])
                ycol = _seg_sums_mxu(ones, [state[q] * row(r8[q]) for q in range(N_BLK)])
                y8 = [jnp.where(sub8 == i, _row_of_col(eye, ycol[q]), y8[q]) for q in range(N_BLK)]
            for q in range(N_BLK):
                y_ref[pl.ds(base, ROW_GROUP), sls[q]] = y8[q]
            return tuple(state)

        init = tuple(st_ref[:, q * LANES:(q + 1) * LANES] for q in range(N_BLK))
        fin = lax.fori_loop(0, T // ROW_GROUP, group, init)
        for q in range(N_BLK):
            st_ref[:, q * LANES:(q + 1) * LANES] = fin[q]

    blk = pl.BlockSpec((T, R_WIDTH), lambda i: (i, 0))
    return pl.pallas_call(
        body, name="scan_fwd", grid=(nchunk,), in_specs=[blk] * 6,
        out_specs=[blk, pl.BlockSpec((1, R_HEAD, R_WIDTH), lambda i: (i, 0, 0))],
        out_shape=[jax.ShapeDtypeStruct((S, R_WIDTH), F32),
                   jax.ShapeDtypeStruct((nchunk, R_HEAD, R_WIDTH), F32)],
        scratch_shapes=[pltpu.VMEM((R_HEAD, R_WIDTH), F32)],
        compiler_params=pltpu.CompilerParams(
            dimension_semantics=("arbitrary",), vmem_limit_bytes=VMEM_LIMIT),
    )(r, w, k, v, a, b)


def _scan_bwd(r, w, k, v, a, b, ck, dy):
    S = r.shape[0]
    T = min(SCAN_T, S)
    nchunk = S // T

    def body(r_ref, w_ref, k_ref, v_ref, a_ref, b_ref, ck_ref, dy_ref,
             dr_ref, dw_ref, dk_ref, dv_ref, da_ref, db_ref, s_buf, sa_buf, vb_buf, ds_ref):
        lo, eye = _scan_consts()
        ones = _head_ones()

        @pl.when(pl.program_id(0) == 0)
        def _():
            ds_ref[...] = jnp.zeros(ds_ref.shape, F32)

        sub8 = lax.broadcasted_iota(jnp.int32, (ROW_GROUP, LANES), 0)

        def fgroup(gi, state):
            base = pl.multiple_of(gi * ROW_GROUP, ROW_GROUP)
            state = list(state)
            for q in range(N_BLK):
                sl = slice(q * LANES, (q + 1) * LANES)
                w8, k8, v8, a8, b8 = [ref[pl.ds(base, ROW_GROUP), sl]
                                      for ref in (w_ref, k_ref, v_ref, a_ref, b_ref)]
                vbs = _cols_of_rows_mxu(eye, ones, v8)
                s = state[q]
                for i in range(ROW_GROUP):
                    row = lambda t8: t8[i:i + 1, :]
                    sa = _seg_dot(lo, s, row(a8))
                    s_buf[base + i, :, sl] = s
                    sa_buf[base + i, :, sl] = sa
                    vb_buf[base + i, :, sl] = vbs[i]
                    s = s * row(w8) + sa * row(b8) + vbs[i] * row(k8)
                state[q] = s
            return tuple(state)

        lax.fori_loop(0, T // ROW_GROUP, fgroup,
                      tuple(ck_ref[0, :, q * LANES:(q + 1) * LANES] for q in range(N_BLK)))

        def bgroup(n, dstate):
            base = pl.multiple_of((T // ROW_GROUP - 1 - n) * ROW_GROUP, ROW_GROUP)
            dstate = list(dstate)
            sls = [slice(q * LANES, (q + 1) * LANES) for q in range(N_BLK)]
            ld = lambda ref: [ref[pl.ds(base, ROW_GROUP), sl] for sl in sls]
            r8, w8, k8, a8, b8, dy8 = ld(r_ref), ld(w_ref), ld(k_ref), ld(a_ref), ld(b_ref), ld(dy_ref)
            dycs = [_cols_of_rows_mxu(eye, ones, dy8[q]) for q in range(N_BLK)]
            acc = [{n_: jnp.zeros((ROW_GROUP, LANES), F32) for n_ in ("r", "w", "k", "v", "a", "b")}
                   for _ in range(N_BLK)]
            for i in reversed(range(ROW_GROUP)):
                row = lambda t8: t8[i:i + 1, :]
                put = lambda q_, n_, val: acc[q_].__setitem__(n_, jnp.where(sub8 == i, val, acc[q_][n_]))
                dks = []
                for q in range(N_BLK):
                    sl = sls[q]
                    sp, sa, vb = s_buf[base + i, :, sl], sa_buf[base + i, :, sl], vb_buf[base + i, :, sl]
                    wr, ar, br, kr, rr = row(w8[q]), row(a8[q]), row(b8[q]), row(k8[q]), row(r8[q])
                    st = sp * wr + sa * br + vb * kr
                    dyc = dycs[q][i]
                    ds = dstate[q] + dyc * rr
                    put(q, "r", _colsum(st * dyc))
                    put(q, "w", _colsum(ds * sp))
                    put(q, "b", _colsum(ds * sa))
                    put(q, "k", _colsum(ds * vb))
                    dsa = _seg_dot(lo, ds, br)
                    dks.append(ds * kr)
                    put(q, "a", _colsum(sp * dsa))
                    dstate[q] = ds * wr + dsa * ar
                dvc = _seg_sums_mxu(ones, dks)
                for q in range(N_BLK):
                    put(q, "v", _row_of_col(eye, dvc[q]))
            for q in range(N_BLK):
                for n_, ref in (("r", dr_ref), ("w", dw_ref), ("k", dk_ref), ("v", dv_ref), ("a", da_ref), ("b", db_ref)):
                    ref[pl.ds(base, ROW_GROUP), sls[q]] = acc[q][n_]
            return tuple(dstate)

        fin = lax.fori_loop(0, T // ROW_GROUP, bgroup,
                            tuple(ds_ref[:, q * LANES:(q + 1) * LANES] for q in range(N_BLK)))
        for q in range(N_BLK):
            ds_ref[:, q * LANES:(q + 1) * LANES] = fin[q]

    blk = pl.BlockSpec((T, R_WIDTH), lambda i: (nchunk - 1 - i, 0))
    ckb = pl.BlockSpec((1, R_HEAD, R_WIDTH), lambda i: (nchunk - 1 - i, 0, 0))
    return pl.pallas_call(
        body, name="scan_bwd", grid=(nchunk,), in_specs=[blk] * 6 + [ckb, blk],
        out_specs=[blk] * 6,
        out_shape=[jax.ShapeDtypeStruct((S, R_WIDTH), F32)] * 6,
        scratch_shapes=[pltpu.VMEM((T, R_HEAD, R_WIDTH), F32)] * 3 + [pltpu.VMEM((R_HEAD, R_WIDTH), F32)],
        compiler_params=pltpu.CompilerParams(
            dimension_semantics=("arbitrary",), vmem_limit_bytes=VMEM_LIMIT),
    )(r, w, k, v, a, b, ck, dy)


def _pad_in_cols(t, axis):
    cut = 2 * G_WIDTH + R_COLS
    lo, hi = lax.slice_in_dim(t, 0, cut, axis=axis), lax.slice_in_dim(t, cut, t.shape[axis], axis=axis)
    zshape = list(t.shape)
    zshape[axis] = LORA_PAD - LORA
    return jnp.concatenate([lo, jnp.zeros(zshape, t.dtype), hi], axis=axis)


def _unpad_in_cols(t, axis):
    cut = 2 * G_WIDTH + R_COLS
    return jnp.concatenate([lax.slice_in_dim(t, 0, cut, axis=axis),
                            lax.slice_in_dim(t, cut + LORA_PAD - LORA, t.shape[axis], axis=axis)], axis=axis)


def _pad_rows(t, lo, n):
    return jnp.zeros((n, t.shape[1]), t.dtype).at[lo:lo + t.shape[0]].set(t)


def _join_cols(w3):
    p, k, n = w3.shape
    return jnp.transpose(w3, (1, 0, 2)).reshape(k, p * n)


def _split_cols(w):
    k, n = w.shape
    return jnp.transpose(w.reshape(k, N_CHIPS, n // N_CHIPS), (1, 0, 2))


def _ada_dw(ccol, dmod):
    n = dmod.shape[1] // N_CHIPS
    tile = 256

    def body(c_ref, d_ref, o_ref):
        cc = c_ref[...]
        o_ref[0] = ((cc * _sigmoid(cc)) * d_ref[...]).astype(o_ref.dtype)

    return pl.pallas_call(
        body, name="ada_dw", grid=(N_CHIPS, D_MODEL // tile),
        in_specs=[pl.BlockSpec((tile, 1), lambda q, i: (i, 0)), pl.BlockSpec((1, n), lambda q, i: (0, q))],
        out_specs=pl.BlockSpec((1, tile, n), lambda q, i: (q, i, 0)),
        out_shape=jax.ShapeDtypeStruct((N_CHIPS, D_MODEL, n), WIRE_DT),
        compiler_params=pltpu.CompilerParams(dimension_semantics=("parallel", "parallel")),
    )(ccol, dmod)


def _shift_down(t):
    return jnp.concatenate([jnp.zeros((1, t.shape[1]), t.dtype), t[:-1]], axis=0)


def _shift_up(t):
    return jnp.concatenate([t[1:], jnp.zeros((1, t.shape[1]), t.dtype)], axis=0)


def _local_step(x, c, tgt, W):
    S = x.shape[0]
    bf = MXU_DT
    G = {}

    hl = np.arange(R_WIDTH) // R_HEAD
    bd = jnp.asarray(hl[:, None] == hl[None, :], jnp.bfloat16)
    gsel = jnp.asarray(np.arange(LANES)[:, None] == (np.arange(G_WIDTH) // (G_WIDTH // G_GROUPS))[None, :],
                       jnp.bfloat16)
    w_in_p = _pad_in_cols(_join_cols(W["w_in"]), 1)
    b_in_p = _pad_in_cols(W["b_in"], 1)
    c_g, c_r = 2 * G_WIDTH, 2 * G_WIDTH + R_COLS_PAD
    w_g, w_r, w_gate = w_in_p[:, :c_g], w_in_p[:, c_g:c_r], w_in_p[:, c_r:]
    b_g, b_r, b_gate = b_in_p[:, :c_g], b_in_p[:, c_g:c_r], b_in_p[:, c_r:]
    mu_p = jnp.concatenate([W["mu_shift"], jnp.zeros((1, LORA_PAD - LORA), F32)], axis=1)
    wd_p = _pad_rows(W["w_decay_up"].astype(F32), 0, LORA_PAD)
    wa_p = _pad_rows(W["w_aaa_up"].astype(F32), DECAY_LORA, LORA_PAD)
    wg_p = _pad_rows(W["w_gate_up"].astype(F32), DECAY_LORA + AAA_LORA, LORA_PAD)
    ws2 = W["w_spatial"].reshape(G_GROUPS * CHUNK, CHUNK)
    bst = jnp.zeros((CHUNK, LANES), F32).at[:, :G_GROUPS].set(W["b_spatial"].T)
    rk = W["r_k"].reshape(1, R_WIDTH)

    c8 = jnp.broadcast_to(c, (8, D_MODEL))
    (ca8,) = _rows("ada_silu", lambda cc: cc * _sigmoid(cc), [c8], [], [(D_MODEL, bf)], tile=8)
    mod_raw = _mm("ada_mm", ca8, W["w_ada"], "nn")
    (mod8,) = _rows("ada_bias", lambda m, bb: m + bb, [mod_raw], [W["b_ada"]], [(6 * D_MODEL, F32)], tile=8)
    sh1, sc1, gt1, sh2, sc2, gt2 = [mod8[0:1, i * D_MODEL:(i + 1) * D_MODEL] for i in range(6)]

    (h,) = _rows("mod1", _modulate, [x], [sc1, sh1], [(D_MODEL, bf)])
    proj_g = _mm("proj_g", h, w_g, "nn")
    proj_r = _mm("proj_r", h, w_r, "nn")
    proj_gate = _mm("proj_gate", h, w_gate, "nn")

    def split2(t):
        return t[:, :G_WIDTH], t[:, G_WIDTH:]

    def gmlp_fwd(z, bz, g, b, bst_, gsel_, ws_):
        gmask, tril = _gmlp_consts()
        (zu, zv), (bu, bv) = split2(z), split2(bz)
        wsl = [ws_[i * CHUNK:(i + 1) * CHUNK] for i in range(G_GROUPS)]
        return _gmlp_core(gmask, tril, gsel_, zu, zv, bu, bv, g, b, bst_, *wsl)

    (y_a,) = _rows("gmlp_fwd", gmlp_fwd, [proj_g], [b_g, W["g_ln_v"], W["b_ln_v"], bst, gsel, ws2],
                   [(G_WIDTH, bf)], tile=CHUNK)

    r_cuts = (0, R_WIDTH, 2 * R_WIDTH, 3 * R_WIDTH, R_COLS_PAD)

    def split4(t):
        return [t[:, r_cuts[i]:r_cuts[i + 1]] for i in range(4)]

    tile_pre = min(256, S)

    def rowmask_of():
        grow = pl.program_id(0) * tile_pre + lax.broadcasted_iota(jnp.int32, (tile_pre, 1), 0)
        return (grow > 0).astype(F32)

    pre_params = [b_r, mu_p, W["w0"], wd_p, W["a0"], wa_p, wg_p, W["k_k"], W["k_a"], bd]

    def pre_fwd(z, p, bz, mu, w0, wd, a0, wa, wg, k_k, k_a, bd_):
        return _pre_core(rowmask_of(), bd_, *split4(z), *split4(p), *split4(bz), *split4(mu),
                         w0, wd, a0, wa, wg, k_k, k_a)

    proj_r_prev = _shift_down(proj_r)
    s_r, s_w, s_k, s_v, s_a, s_b, s_g = _rows(
        "rwkv_pre_fwd", pre_fwd, [proj_r, proj_r_prev], pre_params, [(R_WIDTH, F32)] * 7, tile=tile_pre)
    y_scan, ckpt = _scan_fwd(s_r, s_w, s_k, s_v, s_a, s_b)

    def post_fwd(y, r, k2, v, g, gain, bias, rk_, bd_):
        return _post_core(bd_, y, r, k2, v, g, gain, bias, rk_)

    post_params = [W["gn_gain"], W["gn_bias"], rk, bd]
    (y_b,) = _rows("rwkv_post_fwd", post_fwd, [y_scan, s_r, s_k, s_v, s_g], post_params, [(R_WIDTH, bf)])
    p_a = _mm("branch_a", y_a, W["w_branch_a"], "nn")
    p_b = _mm("branch_b", y_b, W["w_branch_b"], "nn")

    def merge_fwd(pa, pb, gz, bgz):
        return _merge_core(pa, pb, gz[:, :D_MODEL], gz[:, D_MODEL:], bgz[:, :D_MODEL], bgz[:, D_MODEL:])

    (merged,) = _rows("merge_fwd", merge_fwd, [p_a, p_b, proj_gate], [b_gate], [(D_MODEL, bf)])
    mix = _mm("out_proj", merged, W["w_out"], "nn")
    ln1_params = [gt1, W["b_out"], W["ln1_g"], W["ln1_b"], sc2, sh2]
    h1, h2 = _rows("ln1_fwd", _ln1_core, [x, mix], ln1_params, [(D_MODEL, F32), (D_MODEL, bf)])

    a1 = _mm("ff1", h2, W["w_ff1"], "nn")
    (act,) = _rows("ff_act", lambda z, bb: jnp.square(jnp.maximum(z + bb, 0.0)), [a1], [W["b_ff1"]], [(D_FF, bf)])
    ff = _mm("ff2", act, W["w_ff2"], "nn")

    def ln2_loss(h1_, ff_, tg, gt2_, bff2, g, b):
        loss, vjp = jax.vjp(functools.partial(_ln2_loss_core, tg), h1_, ff_, gt2_, bff2, g, b)
        return vjp(jnp.ones((), F32)) + (loss,)

    ln2_params = [gt2, W["b_ff2"], W["ln2_g"], W["ln2_b"]]
    dh1, dff, dgt2, G["b_ff2"], G["ln2_g"], G["ln2_b"], loss_acc = _rows(
        "ln2_loss", ln2_loss, [h1, ff, tgt], ln2_params, [(D_MODEL, F32), (D_MODEL, bf)],
        accs=[(1, D_MODEL)] * 4 + [(1, LANES)])
    loss = loss_acc[0, 0]

    dact = _mm("ff2_dx", dff, W["w_ff2"], "nt")
    G["w_ff2"] = _mm("ff2_dw", act, dff, "tn", WIRE_DT).reshape(N_CHIPS, D_FF // N_CHIPS, D_MODEL)

    def act_bwd(z, da, bb):
        d = da * 2.0 * jnp.maximum(z + bb, 0.0)
        return d, _colsum(d)

    da1, G["b_ff1"] = _rows("ff_act_bwd", act_bwd, [a1, dact], [W["b_ff1"]], [(D_FF, bf)], accs=[(1, D_FF)])
    dh2 = _mm("ff1_dx", da1, W["w_ff1"], "nt")
    G["w_ff1"] = _mm("ff1_dw", h2, da1, "tn", WIRE_DT, out_split=N_CHIPS)

    def ln1_bwd(x_, mix_, dh1_, dh2_, *ps):
        _, vjp = jax.vjp(_ln1_core, x_, mix_, *ps)
        return vjp((dh1_, dh2_))

    dx_res, dmix, dgt1, G["b_out"], G["ln1_g"], G["ln1_b"], dsc2, dsh2 = _rows(
        "ln1_bwd", ln1_bwd, [x, mix, dh1, dh2], ln1_params, [(D_MODEL, F32), (D_MODEL, bf)],
        accs=[(1, D_MODEL)] * 6)

    dmerged = _mm("out_proj_dx", dmix, W["w_out"], "nt")
    G["w_out"] = _mm("out_proj_dw", merged, dmix, "tn", WIRE_DT).reshape(N_CHIPS, D_MODEL // N_CHIPS, D_MODEL)

    def merge_bwd(pa, pb, gz, dm, bgz):
        args = (pa.astype(F32), pb.astype(F32), gz[:, :D_MODEL], gz[:, D_MODEL:], bgz[:, :D_MODEL], bgz[:, D_MODEL:])
        _, vjp = jax.vjp(_merge_core, *args)
        dpa, dpb, dga, dgb, dbga, dbgb = vjp(dm)
        return dpa, dpb, jnp.concatenate([dga, dgb], axis=1), jnp.concatenate([dbga, dbgb], axis=1)

    dp_a, dp_b, dgates, db_gate = _rows(
        "merge_bwd", merge_bwd, [p_a, p_b, proj_gate, dmerged], [b_gate],
        [(D_MODEL, bf), (D_MODEL, bf), (2 * D_MODEL, F32)], accs=[(1, 2 * D_MODEL)])
    dy_a = _mm("branch_a_dx", dp_a, W["w_branch_a"], "nt")
    G["w_branch_a"] = _mm("branch_a_dw", y_a, dp_a, "tn", WIRE_DT, out_split=N_CHIPS)
    dy_b = _mm("branch_b_dx", dp_b, W["w_branch_b"], "nt")
    G["w_branch_b"] = _mm("branch_b_dw", y_b, dp_b, "tn", WIRE_DT, out_split=N_CHIPS)

    def post_bwd(y, r, k2, v, g, dyb, gain, bias, rk_, bd_):
        _, vjp = jax.vjp(functools.partial(_post_core, bd_), y, r, k2, v, g, gain, bias, rk_)
        return vjp(dyb)

    dy_scan, dr_p, dk_p, dv_p, dg_p, G["gn_gain"], G["gn_bias"], drk = _rows(
        "rwkv_post_bwd", post_bwd, [y_scan, s_r, s_k, s_v, s_g, dy_b], post_params,
        [(R_WIDTH, F32)] * 5, accs=[(1, R_WIDTH)] * 3)
    G["r_k"] = drk.reshape(R_HEADS, R_HEAD)
    dr_s, dw_s, dk_s, dv_s, da_s, db_s = _scan_bwd(s_r, s_w, s_k, s_v, s_a, s_b, ckpt, dy_scan)

    def pre_bwd(z, p, dr1, dr2, dw, dk1, dk2, dv1, dv2, da, db, dg,
                bz, mu, w0, wd, a0, wa, wg, k_k, k_a, bd_):
        prim = (*split4(z), *split4(p), *split4(bz), *split4(mu), w0, wd, a0, wa, wg, k_k, k_a)
        _, vjp = jax.vjp(functools.partial(_pre_core, rowmask_of(), bd_), *prim)
        d = vjp((dr1 + dr2, dw, dk1 + dk2, dv1 + dv2, da, db, dg))
        cat = lambda parts: jnp.concatenate(parts, axis=1)
        return (cat(d[0:4]), cat(d[4:8]), cat(d[8:12]), cat(d[12:16])) + tuple(d[16:])

    dz_r, dprev, db_r, dmu_p, G["w0"], dwd_p, G["a0"], dwa_p, dwg_p, G["k_k"], G["k_a"] = _rows(
        "rwkv_pre_bwd", pre_bwd,
        [proj_r, proj_r_prev, dr_s, dr_p, dw_s, dk_s, dk_p, dv_s, dv_p, da_s, db_s, dg_p],
        pre_params, [(R_COLS_PAD, F32)] * 2,
        accs=[(1, R_COLS_PAD), (1, R_COLS_PAD), (1, R_WIDTH), (LORA_PAD, R_WIDTH), (1, R_WIDTH),
              (LORA_PAD, R_WIDTH), (LORA_PAD, R_WIDTH), (1, R_WIDTH), (1, R_WIDTH)],
        tile=tile_pre)
    G["mu_shift"] = dmu_p[:, :R_COLS]
    G["w_decay_up"] = dwd_p[:DECAY_LORA]
    G["w_aaa_up"] = dwa_p[DECAY_LORA:DECAY_LORA + AAA_LORA]
    G["w_gate_up"] = dwg_p[DECAY_LORA + AAA_LORA:LORA]

    def gmlp_bwd(z, dya, bz, g, b, bst_, gsel_, ws_):
        gmask, tril = _gmlp_consts()
        (zu, zv), (bu, bv) = split2(z), split2(bz)
        wsl = [ws_[i * CHUNK:(i + 1) * CHUNK] for i in range(G_GROUPS)]
        _, vjp = jax.vjp(functools.partial(_gmlp_core, gmask, tril, gsel_), zu, zv, bu, bv, g, b, bst_, *wsl)
        d = vjp(dya)
        return (jnp.concatenate(d[0:2], axis=1), jnp.concatenate(d[2:4], axis=1), d[4], d[5], d[6],
                jnp.concatenate(d[7:], axis=0))

    dz_g, db_g, G["g_ln_v"], G["b_ln_v"], dbst, dws2 = _rows(
        "gmlp_bwd", gmlp_bwd, [proj_g, dy_a], [b_g, W["g_ln_v"], W["b_ln_v"], bst, gsel, ws2],
        [(2 * G_WIDTH, F32)],
        accs=[(1, 2 * G_WIDTH), (1, G_WIDTH), (1, G_WIDTH), (CHUNK, LANES), (G_GROUPS * CHUNK, CHUNK)],
        tile=CHUNK)
    G["w_spatial"] = dws2.reshape(G_GROUPS, CHUNK, CHUNK)
    G["b_spatial"] = dbst[:, :G_GROUPS].T

    def dproj_cat(dzg, dzr, dpv, dgz):
        return jnp.concatenate([dzg, dzr + dpv, dgz], axis=1)

    (dproj,) = _rows("dproj_cat", dproj_cat, [dz_g, dz_r, _shift_up(dprev), dgates], [],
                     [(2 * G_WIDTH + R_COLS_PAD + 2 * D_MODEL, bf)])
    dh = _mm("proj_dx", dproj, w_in_p, "nt", tk=2432)
    G["w_in"] = _split_cols(_unpad_in_cols(_mm("proj_dw", h, dproj, "tn", WIRE_DT, tm=512, tn=2432), 1))
    G["b_in"] = _unpad_in_cols(jnp.concatenate([db_g, db_r, db_gate], axis=1), 1)

    def mod1_bwd(x_, dh_, dxr, sc):
        return dh_ * (1.0 + sc) + dxr, _colsum(dh_ * x_), _colsum(dh_)

    grad_x, dsc1, dsh1 = _rows("mod1_bwd", mod1_bwd, [x, dh, dx_res], [sc1], [(D_MODEL, F32)],
                               accs=[(1, D_MODEL)] * 2)

    dmod = jnp.concatenate([dsh1, dsc1, dgt1, dsh2, dsc2, dgt2], axis=1)
    G["b_ada"] = dmod
    G["w_ada"] = _ada_dw(c.reshape(D_MODEL, 1), dmod)
    return loss, grad_x, G


BIG = (("w_ada", (D_MODEL, 6 * D_MODEL), 1), ("w_in", (D_MODEL, 2 * G_WIDTH + R_COLS + 2 * D_MODEL), 1),
       ("w_branch_a", (G_WIDTH, D_MODEL), 1), ("w_branch_b", (R_WIDTH, D_MODEL), 1),
       ("w_out", (D_MODEL, D_MODEL), 0), ("w_ff1", (D_MODEL, D_FF), 1), ("w_ff2", (D_FF, D_MODEL), 0))
LORAS = (("w_decay_up", (DECAY_LORA, R_WIDTH), 1), ("w_aaa_up", (AAA_LORA, R_WIDTH), 1),
         ("w_gate_up", (GATE_LORA, R_WIDTH), 1))
SHARDED = BIG + LORAS
SMALL = (("b_ada", (1, 6 * D_MODEL)), ("b_in", (1, 2 * G_WIDTH + R_COLS + 2 * D_MODEL)),
         ("g_ln_v", (1, G_WIDTH)), ("b_ln_v", (1, G_WIDTH)), ("w_spatial", (G_GROUPS, CHUNK, CHUNK)),
         ("b_spatial", (G_GROUPS, CHUNK)), ("mu_shift", (1, R_COLS)), ("w0", (1, R_WIDTH)),
         ("a0", (1, R_WIDTH)), ("k_k", (1, R_WIDTH)), ("k_a", (1, R_WIDTH)), ("r_k", (R_HEADS, R_HEAD)),
         ("gn_gain", (1, R_WIDTH)), ("gn_bias", (1, R_WIDTH)), ("b_out", (1, D_MODEL)),
         ("ln1_g", (1, D_MODEL)), ("ln1_b", (1, D_MODEL)), ("b_ff1", (1, D_FF)), ("b_ff2", (1, D_MODEL)),
         ("ln2_g", (1, D_MODEL)), ("ln2_b", (1, D_MODEL)))
WEIGHT_ORDER = ("w_ada", "b_ada", "w_in", "b_in", "g_ln_v", "b_ln_v", "w_spatial", "b_spatial", "mu_shift",
                "w0", "w_decay_up", "a0", "w_aaa_up", "w_gate_up", "k_k", "k_a", "r_k", "gn_gain", "gn_bias",
                "w_branch_a", "w_branch_b", "w_out", "b_out", "ln1_g", "ln1_b", "w_ff1", "b_ff1", "w_ff2",
                "b_ff2", "ln2_g", "ln2_b")
N_CHIPS = 4


def _shard_shape(shape, axis):
    s = list(shape)
    s[axis] //= N_CHIPS
    return tuple(s)


def _numel(shape):
    return int(np.prod(shape))


def _round_up(n, q):
    return -(-n // q) * q


N_LORA = sum(_numel(_shard_shape(s, a)) for _, s, a in LORAS)
N_SMALL = sum(_numel(s) for _, s in SMALL)
ROWS_SW = _round_up(N_LORA, PACK_Q) // PACK_W
ROWS_SG = _round_up(N_LORA + N_SMALL, PACK_Q) // PACK_W


def _pack_small(loras, small, rows, dtype):
    parts = [loras[n] for n, _, _ in LORAS] + ([small[n] for n, _ in SMALL] if small is not None else [])
    flat = jnp.concatenate([p.reshape(-1).astype(dtype) for p in parts])
    flat = jnp.concatenate([flat, jnp.zeros((rows * PACK_W - flat.shape[0],), dtype)])
    return flat.reshape(rows, PACK_W)


def _unpack_small(pack, with_small):
    flat = pack.reshape(-1)
    out, off = {}, 0
    for n, s, a in LORAS:
        ss = _shard_shape(s, a)
        out[n] = flat[off:off + _numel(ss)].reshape(ss)
        off += _numel(ss)
    if with_small:
        for n, s in SMALL:
            out[n] = flat[off:off + _numel(s)].reshape(s)
            off += _numel(s)
    return out


def _pack_small_grads(G):
    segs = []
    for q in range(N_CHIPS):
        loras = {n: G[n][:, q * (s[1] // N_CHIPS):(q + 1) * (s[1] // N_CHIPS)] for n, s, _ in LORAS}
        segs.append(_pack_small(loras, G, ROWS_SG, F32))
    return jnp.stack(segs)


ANY = pl.BlockSpec(memory_space=pl.ANY)
MESH = pl.DeviceIdType.MESH


def _place():
    x, y, c = lax.axis_index("x"), lax.axis_index("y"), lax.axis_index("c")
    chips = [(1 - x, y), (x, 1 - y), (1 - x, 1 - y)]
    return x, y, c, chips


def _remote(src, dst, send_sem, recv_sem, to):
    return pltpu.make_async_remote_copy(src_ref=src, dst_ref=dst, send_sem=send_sem, recv_sem=recv_sem,
                                        device_id=to, device_id_type=MESH)


def _all_gather(shards):
    n = len(shards)
    halves = [a.shape[0] // 2 for a in shards]

    def body(*refs):
        ins, outs = refs[:n], refs[n:2 * n]
        send_sems, recv_sems, local_sems = refs[2 * n:]
        x, y, c, chips = _place()
        s = 2 * x + y
        sibling = (x, y, 1 - c)
        slot = [2 * chip[0] + chip[1] for chip in chips]
        own, first, passed = [], [], []
        for a in range(n):
            H = halves[a]
            own += [pltpu.make_async_copy(ins[a].at[pl.ds(hf * H, H)], outs[a].at[2 * s + hf], local_sems.at[2 * a + hf])
                    for hf in range(2)]
            mine = ins[a].at[pl.ds(c * H, H)]
            first += [_remote(mine, outs[a].at[2 * s + c], send_sems.at[6 * a + j], recv_sems.at[6 * a + j], (*chip, c))
                      for j, chip in enumerate(chips)]
        for cp in own + first:
            cp.start()
        for a in range(n):
            for j in range(3):
                landed = outs[a].at[2 * slot[j] + c]
                _remote(landed, landed, send_sems.at[6 * a + j], recv_sems.at[6 * a + j], sibling).wait_recv()
                cp = _remote(landed, landed, send_sems.at[6 * a + 3 + j], recv_sems.at[6 * a + 3 + j], sibling)
                cp.start()
                passed.append(cp)
        for a in range(n):
            for j in range(3):
                got = outs[a].at[2 * slot[j] + 1 - c]
                _remote(got, got, send_sems.at[6 * a + 3 + j], recv_sems.at[6 * a + 3 + j], sibling).wait_recv()
        for cp in first + passed:
            cp.wait_send()
        for cp in own:
            cp.wait()

    return pl.pallas_call(
        body, name="ag_weights", in_specs=[ANY] * n, out_specs=[ANY] * n,
        out_shape=[jax.ShapeDtypeStruct((2 * N_CHIPS, h, a.shape[1]), a.dtype) for a, h in zip(shards, halves)],
        scratch_shapes=[pltpu.SemaphoreType.DMA((6 * n,)), pltpu.SemaphoreType.DMA((6 * n,)),
                        pltpu.SemaphoreType.DMA((2 * n,))],
    )(*shards)


def _rs_sibling_in(gps):
    n = len(gps)

    def body(*refs):
        ins, outs = refs[:n], refs[n:2 * n]
        send_sems, recv_sems = refs[2 * n:]
        x, y, c, _ = _place()
        cps = []
        for a in range(n):
            H = gps[a].shape[1] // 2
            cps += [_remote(ins[a].at[q, pl.ds((1 - c) * H, H)], outs[a].at[q], send_sems.at[N_CHIPS * a + q],
                            recv_sems.at[N_CHIPS * a + q], (x, y, 1 - c)) for q in range(N_CHIPS)]
        for cp in cps:
            cp.start()
        for cp in cps:
            cp.wait()

    return pl.pallas_call(
        body, name="rs_sibling_in", in_specs=[ANY] * n, out_specs=[ANY] * n,
        out_shape=[jax.ShapeDtypeStruct((N_CHIPS, g.shape[1] // 2, g.shape[2]), g.dtype) for g in gps],
        scratch_shapes=[pltpu.SemaphoreType.DMA((N_CHIPS * n,)), pltpu.SemaphoreType.DMA((N_CHIPS * n,))],
    )(*gps)


def _rs_add_own(name, gp, got, c_arr, tr=256):
    H, C = got.shape[1:]
    tr = _pick(H, tr, 8)
    nb = H // tr

    def body(c_ref, g_ref, r_ref, o_ref):
        o_ref[...] = (g_ref[...].astype(F32) + r_ref[...].astype(F32)).astype(o_ref.dtype)

    return pl.pallas_call(
        body, name="rs_add_own_" + name,
        grid_spec=pltpu.PrefetchScalarGridSpec(
            num_scalar_prefetch=1, grid=(N_CHIPS, nb),
            in_specs=[pl.BlockSpec((1, tr, C), lambda q, i, c_ref: (q, c_ref[0] * nb + i, 0)),
                      pl.BlockSpec((1, tr, C), lambda q, i, c_ref: (q, i, 0))],
            out_specs=pl.BlockSpec((1, tr, C), lambda q, i, c_ref: (q, i, 0))),
        out_shape=jax.ShapeDtypeStruct((N_CHIPS, H, C), gp.dtype),
        compiler_params=pltpu.CompilerParams(dimension_semantics=("arbitrary", "arbitrary")),
    )(c_arr, gp, got)


def _rs_chips(parts):
    n = len(parts)

    def body(*refs):
        ins, outs = refs[:n], refs[n:2 * n]
        send_sems, recv_sems, local_sems = refs[2 * n:]
        x, y, c, chips = _place()
        s = 2 * x + y
        slot = [2 * chip[0] + chip[1] for chip in chips]
        own = [pltpu.make_async_copy(ins[a].at[s], outs[a].at[s], local_sems.at[a]) for a in range(n)]
        cps = [_remote(ins[a].at[slot[j]], outs[a].at[s], send_sems.at[3 * a + j], recv_sems.at[3 * a + j], (*chips[j], c))
               for a in range(n) for j in range(3)]
        for cp in own + cps:
            cp.start()
        for a in range(n):
            for j in range(3):
                cps[3 * a + j].wait_send()
                got = outs[a].at[slot[j]]
                _remote(got, got, send_sems.at[3 * a + j], recv_sems.at[3 * a + j], (*chips[j], c)).wait_recv()
        for cp in own:
            cp.wait()

    return pl.pallas_call(
        body, name="rs_chips", in_specs=[ANY] * n, out_specs=[ANY] * n,
        out_shape=[jax.ShapeDtypeStruct(p.shape, p.dtype) for p in parts],
        scratch_shapes=[pltpu.SemaphoreType.DMA((3 * n,)), pltpu.SemaphoreType.DMA((3 * n,)),
                        pltpu.SemaphoreType.DMA((n,))],
    )(*parts)


def _rs_add_chips(name, slots, tr=128):
    H, C = slots.shape[1:]
    tr = _pick(H, tr, 8)

    def body(s_ref, o_ref):
        o_ref[...] = ((s_ref[0].astype(F32) + s_ref[1].astype(F32)) + s_ref[2].astype(F32)) + s_ref[3].astype(F32)

    return pl.pallas_call(
        body, name="rs_add_chips_" + name, grid=(H // tr,),
        in_specs=[pl.BlockSpec((N_CHIPS, tr, C), lambda i: (0, i, 0))],
        out_specs=pl.BlockSpec((tr, C), lambda i: (i, 0)),
        out_shape=jax.ShapeDtypeStruct((H, C), F32),
        compiler_params=pltpu.CompilerParams(dimension_semantics=("arbitrary",)),
    )(slots)


def _rs_sibling_out(halves):
    n = len(halves)

    def body(*refs):
        ins, outs = refs[:n], refs[n:2 * n]
        send_sems, recv_sems, local_sems = refs[2 * n:]
        x, y, c, _ = _place()
        own, cps = [], []
        for a in range(n):
            H = halves[a].shape[0]
            own.append(pltpu.make_async_copy(ins[a], outs[a].at[pl.ds(c * H, H)], local_sems.at[a]))
            cps.append(_remote(ins[a], outs[a].at[pl.ds(c * H, H)], send_sems.at[a], recv_sems.at[a], (x, y, 1 - c)))
        for cp in own + cps:
            cp.start()
        for a in range(n):
            H = halves[a].shape[0]
            cps[a].wait_send()
            got = outs[a].at[pl.ds((1 - c) * H, H)]
            _remote(got, got, send_sems.at[a], recv_sems.at[a], (x, y, 1 - c)).wait_recv()
        for cp in own:
            cp.wait()

    return pl.pallas_call(
        body, name="rs_sibling_out", in_specs=[ANY] * n, out_specs=[ANY] * n,
        out_shape=[jax.ShapeDtypeStruct((2 * h.shape[0], h.shape[1]), h.dtype) for h in halves],
        scratch_shapes=[pltpu.SemaphoreType.DMA((n,)), pltpu.SemaphoreType.DMA((n,)), pltpu.SemaphoreType.DMA((n,))],
    )(*halves)


def _adamw(name, w, g, m, v):
    def fn(w_, g_, m_, v_):
        m2 = ADAM_B1 * m_ + (1.0 - ADAM_B1) * g_
        v2 = ADAM_B2 * v_ + (1.0 - ADAM_B2) * (g_ * g_)
        m_hat = m2 / (1.0 - ADAM_B1 ** ADAM_STEP)
        v_hat = v2 / (1.0 - ADAM_B2 ** ADAM_STEP)
        return -ADAM_LR * (m_hat / (jnp.sqrt(v_hat) + ADAM_EPS) + ADAM_WD * w_), m2, v2

    return _rows("adamw_" + name, fn, [w, g, m, v], [], [(w.shape[1], F32)] * 3, tile=_pick(w.shape[0], 256, 8))


def kernel(x, c, w_ada, b_ada, w_in, b_in, g_ln_v, b_ln_v, w_spatial, b_spatial, mu_shift, w0, w_decay_up, a0, w_aaa_up, w_gate_up, k_k, k_a, r_k, gn_gain, gn_bias, w_branch_a, w_branch_b, w_out, b_out, ln1_g, ln1_b, w_ff1, b_ff1, w_ff2, b_ff2, ln2_g, ln2_b, loss_target, m_w_ada, m_b_ada, m_w_in, m_b_in, m_g_ln_v, m_b_ln_v, m_w_spatial, m_b_spatial, m_mu_shift, m_w0, m_w_decay_up, m_a0, m_w_aaa_up, m_w_gate_up, m_k_k, m_k_a, m_r_k, m_gn_gain, m_gn_bias, m_w_branch_a, m_w_branch_b, m_w_out, m_b_out, m_ln1_g, m_ln1_b, m_w_ff1, m_b_ff1, m_w_ff2, m_b_ff2, m_ln2_g, m_ln2_b, v_w_ada, v_b_ada, v_w_in, v_b_in, v_g_ln_v, v_b_ln_v, v_w_spatial, v_b_spatial, v_mu_shift, v_w0, v_w_decay_up, v_a0, v_w_aaa_up, v_w_gate_up, v_k_k, v_k_a, v_r_k, v_gn_gain, v_gn_bias, v_w_branch_a, v_w_branch_b, v_w_out, v_b_out, v_ln1_g, v_ln1_b, v_w_ff1, v_b_ff1, v_w_ff2, v_b_ff2, v_ln2_g, v_ln2_b):
    args = dict(locals())
    local_shape = {n: _shard_shape(s, a) for n, s, a in SHARDED}
    local_shape.update(dict(SMALL))
    wts = {n: args[n].reshape(local_shape[n]) for n in WEIGHT_ORDER}
    mom = {n: args["m_" + n].reshape(local_shape[n]) for n in WEIGHT_ORDER}
    var = {n: args["v_" + n].reshape(local_shape[n]) for n in WEIGHT_ORDER}
    big = [n for n, _, _ in BIG]

    gath = _all_gather([wts[n].astype(MXU_DT) for n in big] + [_pack_small(wts, None, ROWS_SW, MXU_DT)])
    W = {n: wts[n] for n, _ in SMALL}
    for (n, s, a), g in zip(BIG, gath):
        r, cdim = _shard_shape(s, a)
        W[n] = g.reshape(N_CHIPS, r, cdim) if a == 1 else g.reshape(s)
    lora_q = [_unpack_small(gath[-1].reshape(N_CHIPS, ROWS_SW, PACK_W)[q], False) for q in range(N_CHIPS)]
    for n, _, _ in LORAS:
        W[n] = jnp.concatenate([lora_q[q][n] for q in range(N_CHIPS)], axis=1)

    loss, grad_x, G = _local_step(x[0], c, loss_target[0], W)
    loss = lax.psum(loss, MESH_AXES)

    names = big + ["small"]
    gps = [G[n] for n in big] + [_pack_small_grads(G)]
    c_arr = lax.axis_index("c").astype(jnp.int32).reshape(1)
    parts = [_rs_add_own(n, g, r, c_arr) for n, g, r in zip(names, gps, _rs_sibling_in(gps))]
    halves = [_rs_add_chips(n, s) for n, s in zip(names, _rs_chips(parts))]
    segs = _rs_sibling_out(halves)

    out = {}
    for n, g in zip(big, segs[:-1]):
        out[n] = (g,) + tuple(_adamw(n, wts[n], g, mom[n], var[n]))
    packs = [_pack_small(t, t, ROWS_SG, F32) for t in (wts, mom, var)]
    small4 = [_unpack_small(t, True) for t in (segs[-1],) + tuple(_adamw("small", packs[0], segs[-1], packs[1], packs[2]))]
    res = [loss, grad_x[None]]
    for k in range(4):
        res += [(out[n][k] if n in out else small4[k][n]).reshape(args[n].shape) for n in WEIGHT_ORDER]
    return tuple(res)
```

```python
import functools

import numpy as np
import jax
import jax.numpy as jnp
from jax import lax
from jax.experimental import pallas as pl
from jax.experimental.pallas import tpu as pltpu

F32 = jnp.float32
MXU_DT = jnp.bfloat16
WIRE_DT = jnp.bfloat16

D_MODEL = 1024
G_GROUPS = 8
G_WIDTH = 512
CHUNK = 128
R_WIDTH = 512
R_HEAD = 64
R_HEADS = 8
DECAY_LORA = 32
AAA_LORA = 32
GATE_LORA = 96
LORA = DECAY_LORA + AAA_LORA + GATE_LORA
LORA_PAD = 256
R_COLS = 3 * R_WIDTH + LORA
R_COLS_PAD = 3 * R_WIDTH + LORA_PAD
D_FF = 4 * D_MODEL
ALPHA = 2.0 ** 0.25
LN_EPS = 1e-5
GN_EPS = 64e-5
ADAM_LR = 0.001
ADAM_B1 = 0.9
ADAM_B2 = 0.999
ADAM_EPS = 1e-08
ADAM_WD = 0.01
ADAM_STEP = 10

LANES = 128
PACK_W = 512
PACK_Q = 2 * 16 * PACK_W
VMEM_LIMIT = 48 * 1024 * 1024
SCAN_T = 64

MESH_AXES = ("x", "y", "c")


def _dg(a, b, dims):
    return lax.dot_general(a.astype(MXU_DT), b.astype(MXU_DT), (dims, ((), ())),
                           preferred_element_type=F32)


@jax.custom_vjp
def _bdot(a, b):
    return _dg(a, b, ((1,), (0,)))


def _bdot_fwd(a, b):
    return _bdot(a, b), (a, b)


def _bdot_bwd(res, g):
    a, b = res
    return (_dg(g, b, ((1,), (1,))).astype(a.dtype), _dg(a, g, ((0,), (0,))).astype(b.dtype))


_bdot.defvjp(_bdot_fwd, _bdot_bwd)


def _split_dot(x, m, dims):
    hi = x.astype(jnp.bfloat16)
    lo = (x - hi.astype(F32)).astype(jnp.bfloat16)
    dn = (dims, ((), ()))
    return (lax.dot_general(hi, m, dn, preferred_element_type=F32)
            + lax.dot_general(lo, m, dn, preferred_element_type=F32))


@jax.custom_vjp
def _pdot(x, m):
    return _split_dot(x, m, ((1,), (0,)))


def _pdot_fwd(x, m):
    return _pdot(x, m), m


def _pdot_bwd(m, g):
    return _split_dot(g, m, ((1,), (1,))), None


_pdot.defvjp(_pdot_fwd, _pdot_bwd)


def _sigmoid(x):
    return 1.0 / (1.0 + jnp.exp(-x))


def _softplus(x):
    return jnp.maximum(x, 0.0) + jnp.log(1.0 + jnp.exp(-jnp.maximum(x, -x)))


def _gelu(x):
    return 0.5 * x * (1.0 + jnp.tanh(0.7978845608028654 * (x + 0.044715 * (x * x * x))))


def _ln(x, g, b, eps):
    mu = jnp.mean(x, axis=-1, keepdims=True)
    xc = x - mu
    var = jnp.mean(xc * xc, axis=-1, keepdims=True)
    return xc * lax.rsqrt(var + eps) * g + b


def _colsum(x):
    return jnp.sum(x, axis=0, keepdims=True)


def _pick(n, target, q=LANES):
    if n <= target:
        return n
    best = None
    for t in range(q, target + 1, q):
        if n % t == 0:
            best = t
    assert best is not None, (n, target)
    return best


def _mm(name, a, b, mode, out_dtype=F32, out_split=1, tm=1024, tn=1024, tk=1024):
    bs = b.shape[0] if b.ndim == 3 else 1
    br, bc = b.shape[-2:]
    if mode == "nn":
        (M, K), K2, N = a.shape, br, bc * bs
    elif mode == "nt":
        (M, K), N, K2 = a.shape, br, bc * bs
    else:
        assert bs == 1
        (K, M), K2, N = a.shape, br, bc
    assert K == K2, (name, a.shape, b.shape, mode)
    n_piece = N // max(bs if mode == "nn" else 1, out_split)
    k_piece = K // (bs if mode == "nt" else 1)
    tm, tn, tk = _pick(M, tm, 8 if M < LANES else LANES), _pick(n_piece, tn), _pick(k_piece, tk)
    nk, npj, npk = K // tk, n_piece // tn, k_piece // tk
    dims = {"nn": ((1,), (0,)), "nt": ((1,), (1,)), "tn": ((0,), (0,))}[mode]

    def body(a_ref, b_ref, o_ref, acc_ref):
        k = pl.program_id(2)

        @pl.when(k == 0)
        def _():
            acc_ref[...] = jnp.zeros(acc_ref.shape, F32)

        acc_ref[...] += _dg(a_ref[...], b_ref[0] if bs > 1 else b_ref[...], dims)

        @pl.when(k == nk - 1)
        def _():
            if out_split > 1:
                o_ref[0] = acc_ref[...].astype(o_ref.dtype)
            else:
                o_ref[...] = acc_ref[...].astype(o_ref.dtype)

    if mode == "nn":
        a_spec = pl.BlockSpec((tm, tk), lambda i, j, k: (i, k))
        b_spec = (pl.BlockSpec((tk, tn), lambda i, j, k: (k, j)) if bs == 1 else
                  pl.BlockSpec((1, tk, tn), lambda i, j, k: (j // npj, k, j % npj)))
    elif mode == "nt":
        a_spec = pl.BlockSpec((tm, tk), lambda i, j, k: (i, k))
        b_spec = (pl.BlockSpec((tn, tk), lambda i, j, k: (j, k)) if bs == 1 else
                  pl.BlockSpec((1, tn, tk), lambda i, j, k: (k // npk, j, k % npk)))
    else:
        a_spec = pl.BlockSpec((tk, tm), lambda i, j, k: (k, i))
        b_spec = pl.BlockSpec((tk, tn), lambda i, j, k: (k, j))
    if out_split > 1:
        o_spec = pl.BlockSpec((1, tm, tn), lambda i, j, k: (j // npj, i, j % npj))
        o_shape = jax.ShapeDtypeStruct((out_split, M, n_piece), out_dtype)
    else:
        o_spec = pl.BlockSpec((tm, tn), lambda i, j, k: (i, j))
        o_shape = jax.ShapeDtypeStruct((M, N), out_dtype)
    return pl.pallas_call(
        body, name=name, grid=(M // tm, N // tn, nk),
        in_specs=[a_spec, b_spec], out_specs=o_spec, out_shape=o_shape,
        scratch_shapes=[pltpu.VMEM((tm, tn), F32)],
        compiler_params=pltpu.CompilerParams(
            dimension_semantics=("parallel", "parallel", "arbitrary"), vmem_limit_bytes=VMEM_LIMIT),
    )(a, b)


def _rows(name, fn, rows, params, outs, accs=(), tile=256):
    S = rows[0].shape[0]
    tile = min(tile, S)
    assert S % tile == 0, (name, S, tile)
    nr, npar, no, na = len(rows), len(params), len(outs), len(accs)

    def body(*refs):
        rin, pin = refs[:nr], refs[nr:nr + npar]
        oref, aref = refs[nr + npar:nr + npar + no], refs[nr + npar + no:]
        res = fn(*[r[...] for r in rin], *[p[...] for p in pin])
        if not isinstance(res, (tuple, list)):
            res = (res,)
        assert len(res) == no + na, (name, len(res), no, na)
        for ref, val in zip(oref, res[:no]):
            ref[...] = val.astype(ref.dtype)
        if na:
            @pl.when(pl.program_id(0) == 0)
            def _():
                for ref in aref:
                    ref[...] = jnp.zeros(ref.shape, ref.dtype)

            for ref, val in zip(aref, res[no:]):
                ref[...] += jnp.broadcast_to(val, ref.shape).astype(ref.dtype)

    def whole(shape):
        nd = len(shape)
        return pl.BlockSpec(tuple(shape), lambda i: (0,) * nd)

    in_specs = ([pl.BlockSpec((tile, r.shape[1]), lambda i: (i, 0)) for r in rows]
                + [whole(p.shape) for p in params])
    out_specs = ([pl.BlockSpec((tile, n), lambda i: (i, 0)) for n, _ in outs]
                 + [whole(s) for s in accs])
    out_shape = ([jax.ShapeDtypeStruct((S, n), dt) for n, dt in outs]
                 + [jax.ShapeDtypeStruct(tuple(s), F32) for s in accs])
    res = pl.pallas_call(
        body, name=name, grid=(S // tile,), in_specs=in_specs, out_specs=out_specs,
        out_shape=out_shape,
        compiler_params=pltpu.CompilerParams(
            dimension_semantics=("arbitrary",), vmem_limit_bytes=VMEM_LIMIT),
    )(*rows, *params)
    return res


def _modulate(x, sc, sh):
    return x * (1.0 + sc) + sh


def _gmlp_consts():
    lane = lax.broadcasted_iota(jnp.int32, (1, G_WIDTH), 1)
    gmask = [(lane // (G_WIDTH // G_GROUPS) == g).astype(F32) for g in range(G_GROUPS)]
    tril = (lax.broadcasted_iota(jnp.int32, (CHUNK, CHUNK), 0)
            >= lax.broadcasted_iota(jnp.int32, (CHUNK, CHUNK), 1))
    return gmask, tril


def _gmlp_core(gmask, tril, gsel, zu, zv, bu, bv, g, b, bst, *ws):
    u = _gelu(zu + bu)
    v = _ln(_gelu(zv + bv), g, b, LN_EPS)
    s = _pdot(bst, gsel)
    for gi in range(G_GROUPS):
        s = s + _bdot(jnp.where(tril, ws[gi], 0.0), v * gmask[gi])
    return u * s


def _pre_core(rowmask, bd, zr, zk, zv, zl, pr, pk, pv, pq, br, bk, bv, bl, mr, mk, mv, ml,
              w0, wd, a0, wa, wg, k_k, k_a):
    def mix(z, p, b, mu):
        zz = z + b
        return zz + ((p + b) * rowmask - zz) * mu

    r, k, v, l = mix(zr, pr, br, mr), mix(zk, pk, bk, mk), mix(zv, pv, bv, mv), mix(zl, pq, bl, ml)
    w_log = -_softplus(-(w0 + _bdot(jnp.tanh(l), wd))) - 0.5
    decay = jnp.exp(-jnp.exp(w_log))
    a = _sigmoid(a0 + _bdot(l, wa))
    g = _bdot(_sigmoid(l), wg)
    kk = k * k_k
    kkn = kk / jnp.maximum(jnp.sqrt(_pdot(kk * kk, bd)), 1e-12)
    k2 = k * (1.0 + (a - 1.0) * k_a)
    return r, decay, k2, v, -kkn, kkn * a, g


def _post_core(bd, y, r, k2, v, g, gain, bias, rk):
    inv = 1.0 / R_HEAD
    mu = _pdot(y, bd) * inv
    yc = y - mu
    var = _pdot(yc * yc, bd) * inv
    yn = yc * lax.rsqrt(var + GN_EPS) * gain + bias
    bonus = _pdot(r * k2 * rk, bd) * v
    return (yn + bonus) * g


def _merge_core(pa, pb, ga, gb, bga, bgb):
    return _sigmoid(ga + bga) * pa + _sigmoid(gb + bgb) * pb


def _ln1_core(x, mix, gt1, bout, g, b, sc2, sh2):
    h1 = _ln(ALPHA * x + gt1 * (mix + bout), g, b, LN_EPS)
    return h1, h1 * (1.0 + sc2) + sh2


def _ln2_loss_core(tgt, h1, ff, gt2, bff2, g, b):
    out = _ln(ALPHA * h1 + gt2 * (ff + bff2), g, b, LN_EPS)
    err = out - tgt
    return 0.5 * jnp.sum(err * err) * (1.0 / D_MODEL)


def _scan_consts():
    sub = lax.broadcasted_iota(jnp.int32, (R_HEAD, LANES), 0)
    lane = lax.broadcasted_iota(jnp.int32, (R_HEAD, LANES), 1)
    return lane < R_HEAD, sub == (lane & (R_HEAD - 1))


def _seg_sum(lo, xb):
    s_lo = jnp.sum(jnp.where(lo, xb, 0.0), axis=1, keepdims=True)
    s_hi = jnp.sum(jnp.where(lo, 0.0, xb), axis=1, keepdims=True)
    return jnp.where(lo, s_lo, s_hi)


def _seg_dot(lo, s, row):
    lo_row = lo[0:1, :]
    s_lo = jnp.sum(s * jnp.where(lo_row, row, 0.0), axis=1, keepdims=True)
    s_hi = jnp.sum(s * jnp.where(lo_row, 0.0, row), axis=1, keepdims=True)
    return jnp.where(lo, s_lo, s_hi)


def _row_of_col(eye, colb):
    return jnp.sum(jnp.where(eye, colb, 0.0), axis=0, keepdims=True)


def _head_ones():
    i = lax.broadcasted_iota(jnp.int32, (LANES, LANES), 0) // R_HEAD
    j = lax.broadcasted_iota(jnp.int32, (LANES, LANES), 1) // R_HEAD
    return (i == j).astype(jnp.bfloat16)


N_SPLIT = 2


def _split(x):
    parts, r = [], x
    for u in range(N_SPLIT):
        p = r.astype(jnp.bfloat16)
        parts.append(p)
        if u + 1 < N_SPLIT:
            r = r - p.astype(F32)
    return parts


def _seg_sums_mxu(ones, mats):
    parts = [p for m in mats for p in _split(m)]
    res = lax.dot_general(jnp.concatenate(parts, axis=0), ones, (((1,), (0,)), ((), ())),
                          preferred_element_type=F32)
    out = []
    for n in range(len(mats)):
        t = [res[(N_SPLIT * n + u) * R_HEAD:(N_SPLIT * n + u + 1) * R_HEAD] for u in range(N_SPLIT)]
        out.append(functools.reduce(lambda p, q: p + q, t))
    return out


def _cols_of_rows_mxu(eye, ones, rows8):
    return _seg_sums_mxu(ones, [jnp.where(eye, jnp.broadcast_to(rows8[i:i + 1, :], eye.shape), 0.0)
                                for i in range(rows8.shape[0])])


N_BLK = R_WIDTH // LANES
ROW_GROUP = 8


def _scan_fwd(r, w, k, v, a, b):
    S = r.shape[0]
    T = min(SCAN_T, S)
    nchunk = S // T

    def body(r_ref, w_ref, k_ref, v_ref, a_ref, b_ref, y_ref, ck_ref, st_ref):
        lo, eye = _scan_consts()
        ones = _head_ones()

        @pl.when(pl.program_id(0) == 0)
        def _():
            st_ref[...] = jnp.zeros(st_ref.shape, F32)

        ck_ref[0] = st_ref[...]
        sub8 = lax.broadcasted_iota(jnp.int32, (ROW_GROUP, LANES), 0)

        def group(gi, state):
            base = pl.multiple_of(gi * ROW_GROUP, ROW_GROUP)
            state = list(state)
            sls = [slice(q * LANES, (q + 1) * LANES) for q in range(N_BLK)]
            ld = lambda ref: [ref[pl.ds(base, ROW_GROUP), sl] for sl in sls]
            r8, w8, k8, v8, a8, b8 = ld(r_ref), ld(w_ref), ld(k_ref), ld(v_ref), ld(a_ref), ld(b_ref)
            vb = [_cols_of_rows_mxu(eye, ones, v8[q]) for q in range(N_BLK)]
            y8 = [jnp.zeros((ROW_GROUP, LANES), F32)] * N_BLK
            for i in range(ROW_GROUP):
                row = lambda t8: t8[i:i + 1, :]
                for q in range(N_BLK):
                    s = state[q]
                    sa = _seg_dot(lo, s, row(a8[q]))
                    state[q] = s * row(w8[q]) + sa * row(b8[q]) + vb[q][i] * row(k8[q])
                ycol = _seg_sums_mxu(ones, [state[q] * row(r8[q]) for q in range(N_BLK)])
                y8 = [jnp.where(sub8 == i, _row_of_col(eye, ycol[q]), y8[q]) for q in range(N_BLK)]
            for q in range(N_BLK):
                y_ref[pl.ds(base, ROW_GROUP), sls[q]] = y8[q]
            return tuple(state)

        init = tuple(st_ref[:, q * LANES:(q + 1) * LANES] for q in range(N_BLK))
        fin = lax.fori_loop(0, T // ROW_GROUP, group, init)
        for q in range(N_BLK):
            st_ref[:, q * LANES:(q + 1) * LANES] = fin[q]

    blk = pl.BlockSpec((T, R_WIDTH), lambda i: (i, 0))
    return pl.pallas_call(
        body, name="scan_fwd", grid=(nchunk,), in_specs=[blk] * 6,
        out_specs=[blk, pl.BlockSpec((1, R_HEAD, R_WIDTH), lambda i: (i, 0, 0))],
        out_shape=[jax.ShapeDtypeStruct((S, R_WIDTH), F32),
                   jax.ShapeDtypeStruct((nchunk, R_HEAD, R_WIDTH), F32)],
        scratch_shapes=[pltpu.VMEM((R_HEAD, R_WIDTH), F32)],
        compiler_params=pltpu.CompilerParams(
            dimension_semantics=("arbitrary",), vmem_limit_bytes=VMEM_LIMIT),
    )(r, w, k, v, a, b)


def _scan_bwd(r, w, k, v, a, b, ck, dy):
    S = r.shape[0]
    T = min(SCAN_T, S)
    nchunk = S // T

    def body(r_ref, w_ref, k_ref, v_ref, a_ref, b_ref, ck_ref, dy_ref,
             dr_ref, dw_ref, dk_ref, dv_ref, da_ref, db_ref, s_buf, sa_buf, vb_buf, ds_ref):
        lo, eye = _scan_consts()
        ones = _head_ones()

        @pl.when(pl.program_id(0) == 0)
        def _():
            ds_ref[...] = jnp.zeros(ds_ref.shape, F32)

        sub8 = lax.broadcasted_iota(jnp.int32, (ROW_GROUP, LANES), 0)

        def fgroup(gi, state):
            base = pl.multiple_of(gi * ROW_GROUP, ROW_GROUP)
            state = list(state)
            for q in range(N_BLK):
                sl = slice(q * LANES, (q + 1) * LANES)
                w8, k8, v8, a8, b8 = [ref[pl.ds(base, ROW_GROUP), sl]
                                      for ref in (w_ref, k_ref, v_ref, a_ref, b_ref)]
                vbs = _cols_of_rows_mxu(eye, ones, v8)
                s = state[q]
                for i in range(ROW_GROUP):
                    row = lambda t8: t8[i:i + 1, :]
                    sa = _seg_dot(lo, s, row(a8))
                    s_buf[base + i, :, sl] = s
                    sa_buf[base + i, :, sl] = sa
                    vb_buf[base + i, :, sl] = vbs[i]
                    s = s * row(w8) + sa * row(b8) + vbs[i] * row(k8)
                state[q] = s
            return tuple(state)

        lax.fori_loop(0, T // ROW_GROUP, fgroup,
                      tuple(ck_ref[0, :, q * LANES:(q + 1) * LANES] for q in range(N_BLK)))

        def bgroup(n, dstate):
            base = pl.multiple_of((T // ROW_GROUP - 1 - n) * ROW_GROUP, ROW_GROUP)
            dstate = list(dstate)
            sls = [slice(q * LANES, (q + 1) * LANES) for q in range(N_BLK)]
            ld = lambda ref: [ref[pl.ds(base, ROW_GROUP), sl] for sl in sls]
            r8, w8, k8, a8, b8, dy8 = ld(r_ref), ld(w_ref), ld(k_ref), ld(a_ref), ld(b_ref), ld(dy_ref)
            dycs = [_cols_of_rows_mxu(eye, ones, dy8[q]) for q in range(N_BLK)]
            acc = [{n_: jnp.zeros((ROW_GROUP, LANES), F32) for n_ in ("r", "w", "k", "v", "a", "b")}
                   for _ in range(N_BLK)]
            for i in reversed(range(ROW_GROUP)):
                row = lambda t8: t8[i:i + 1, :]
                put = lambda q_, n_, val: acc[q_].__setitem__(n_, jnp.where(sub8 == i, val, acc[q_][n_]))
                dks = []
                for q in range(N_BLK):
                    sl = sls[q]
                    sp, sa, vb = s_buf[base + i, :, sl], sa_buf[base + i, :, sl], vb_buf[base + i, :, sl]
                    wr, ar, br, kr, rr = row(w8[q]), row(a8[q]), row(b8[q]), row(k8[q]), row(r8[q])
                    st = sp * wr + sa * br + vb * kr
                    dyc = dycs[q][i]
                    ds = dstate[q] + dyc * rr
                    put(q, "r", _colsum(st * dyc))
                    put(q, "w", _colsum(ds * sp))
                    put(q, "b", _colsum(ds * sa))
                    put(q, "k", _colsum(ds * vb))
                    dsa = _seg_dot(lo, ds, br)
                    dks.append(ds * kr)
                    put(q, "a", _colsum(sp * dsa))
                    dstate[q] = ds * wr + dsa * ar
                dvc = _seg_sums_mxu(ones, dks)
                for q in range(N_BLK):
                    put(q, "v", _row_of_col(eye, dvc[q]))
            for q in range(N_BLK):
                for n_, ref in (("r", dr_ref), ("w", dw_ref), ("k", dk_ref), ("v", dv_ref), ("a", da_ref), ("b", db_ref)):
                    ref[pl.ds(base, ROW_GROUP), sls[q]] = acc[q][n_]
            return tuple(dstate)

        fin = lax.fori_loop(0, T // ROW_GROUP, bgroup,
                            tuple(ds_ref[:, q * LANES:(q + 1) * LANES] for q in range(N_BLK)))
        for q in range(N_BLK):
            ds_ref[:, q * LANES:(q + 1) * LANES] = fin[q]

    blk = pl.BlockSpec((T, R_WIDTH), lambda i: (nchunk - 1 - i, 0))
    ckb = pl.BlockSpec((1, R_HEAD, R_WIDTH), lambda i: (nchunk - 1 - i, 0, 0))
    return pl.pallas_call(
        body, name="scan_bwd", grid=(nchunk,), in_specs=[blk] * 6 + [ckb, blk],
        out_specs=[blk] * 6,
        out_shape=[jax.ShapeDtypeStruct((S, R_WIDTH), F32)] * 6,
        scratch_shapes=[pltpu.VMEM((T, R_HEAD, R_WIDTH), F32)] * 3 + [pltpu.VMEM((R_HEAD, R_WIDTH), F32)],
        compiler_params=pltpu.CompilerParams(
            dimension_semantics=("arbitrary",), vmem_limit_bytes=VMEM_LIMIT),
    )(r, w, k, v, a, b, ck, dy)


def _pad_in_cols(t, axis):
    cut = 2 * G_WIDTH + R_COLS
    lo, hi = lax.slice_in_dim(t, 0, cut, axis=axis), lax.slice_in_dim(t, cut, t.shape[axis], axis=axis)
    zshape = list(t.shape)
    zshape[axis] = LORA_PAD - LORA
    return jnp.concatenate([lo, jnp.zeros(zshape, t.dtype), hi], axis=axis)


def _unpad_in_cols(t, axis):
    cut = 2 * G_WIDTH + R_COLS
    return jnp.concatenate([lax.slice_in_dim(t, 0, cut, axis=axis),
                            lax.slice_in_dim(t, cut + LORA_PAD - LORA, t.shape[axis], axis=axis)], axis=axis)


def _pad_rows(t, lo, n):
    return jnp.zeros((n, t.shape[1]), t.dtype).at[lo:lo + t.shape[0]].set(t)


def _join_cols(w3):
    p, k, n = w3.shape
    return jnp.transpose(w3, (1, 0, 2)).reshape(k, p * n)


def _split_cols(w):
    k, n = w.shape
    return jnp.transpose(w.reshape(k, N_CHIPS, n // N_CHIPS), (1, 0, 2))


def _ada_dw(ccol, dmod):
    n = dmod.shape[1] // N_CHIPS
    tile = 256

    def body(c_ref, d_ref, o_ref):
        cc = c_ref[...]
        o_ref[0] = ((cc * _sigmoid(cc)) * d_ref[...]).astype(o_ref.dtype)

    return pl.pallas_call(
        body, name="ada_dw", grid=(N_CHIPS, D_MODEL // tile),
        in_specs=[pl.BlockSpec((tile, 1), lambda q, i: (i, 0)), pl.BlockSpec((1, n), lambda q, i: (0, q))],
        out_specs=pl.BlockSpec((1, tile, n), lambda q, i: (q, i, 0)),
        out_shape=jax.ShapeDtypeStruct((N_CHIPS, D_MODEL, n), WIRE_DT),
        compiler_params=pltpu.CompilerParams(dimension_semantics=("parallel", "parallel")),
    )(ccol, dmod)


def _shift_down(t):
    return jnp.concatenate([jnp.zeros((1, t.shape[1]), t.dtype), t[:-1]], axis=0)


def _shift_up(t):
    return jnp.concatenate([t[1:], jnp.zeros((1, t.shape[1]), t.dtype)], axis=0)


def _local_step(x, c, tgt, W):
    S = x.shape[0]
    bf = MXU_DT
    G = {}

    hl = np.arange(R_WIDTH) // R_HEAD
    bd = jnp.asarray(hl[:, None] == hl[None, :], jnp.bfloat16)
    gsel = jnp.asarray(np.arange(LANES)[:, None] == (np.arange(G_WIDTH) // (G_WIDTH // G_GROUPS))[None, :],
                       jnp.bfloat16)
    w_in_p = _pad_in_cols(_join_cols(W["w_in"]), 1)
    b_in_p = _pad_in_cols(W["b_in"], 1)
    c_g, c_r = 2 * G_WIDTH, 2 * G_WIDTH + R_COLS_PAD
    w_g, w_r, w_gate = w_in_p[:, :c_g], w_in_p[:, c_g:c_r], w_in_p[:, c_r:]
    b_g, b_r, b_gate = b_in_p[:, :c_g], b_in_p[:, c_g:c_r], b_in_p[:, c_r:]
    mu_p = jnp.concatenate([W["mu_shift"], jnp.zeros((1, LORA_PAD - LORA), F32)], axis=1)
    wd_p = _pad_rows(W["w_decay_up"].astype(F32), 0, LORA_PAD)
    wa_p = _pad_rows(W["w_aaa_up"].astype(F32), DECAY_LORA, LORA_PAD)
    wg_p = _pad_rows(W["w_gate_up"].astype(F32), DECAY_LORA + AAA_LORA, LORA_PAD)
    ws2 = W["w_spatial"].reshape(G_GROUPS * CHUNK, CHUNK)
    bst = jnp.zeros((CHUNK, LANES), F32).at[:, :G_GROUPS].set(W["b_spatial"].T)
    rk = W["r_k"].reshape(1, R_WIDTH)

    c8 = jnp.broadcast_to(c, (8, D_MODEL))
    (ca8,) = _rows("ada_silu", lambda cc: cc * _sigmoid(cc), [c8], [], [(D_MODEL, bf)], tile=8)
    mod_raw = _mm("ada_mm", ca8, W["w_ada"], "nn")
    (mod8,) = _rows("ada_bias", lambda m, bb: m + bb, [mod_raw], [W["b_ada"]], [(6 * D_MODEL, F32)], tile=8)
    sh1, sc1, gt1, sh2, sc2, gt2 = [mod8[0:1, i * D_MODEL:(i + 1) * D_MODEL] for i in range(6)]

    (h,) = _rows("mod1", _modulate, [x], [sc1, sh1], [(D_MODEL, bf)])
    proj_g = _mm("proj_g", h, w_g, "nn")
    proj_r = _mm("proj_r", h, w_r, "nn")
    proj_gate = _mm("proj_gate", h, w_gate, "nn")

    def split2(t):
        return t[:, :G_WIDTH], t[:, G_WIDTH:]

    def gmlp_fwd(z, bz, g, b, bst_, gsel_, ws_):
        gmask, tril = _gmlp_consts()
        (zu, zv), (bu, bv) = split2(z), split2(bz)
        wsl = [ws_[i * CHUNK:(i + 1) * CHUNK] for i in range(G_GROUPS)]
        return _gmlp_core(gmask, tril, gsel_, zu, zv, bu, bv, g, b, bst_, *wsl)

    (y_a,) = _rows("gmlp_fwd", gmlp_fwd, [proj_g], [b_g, W["g_ln_v"], W["b_ln_v"], bst, gsel, ws2],
                   [(G_WIDTH, bf)], tile=CHUNK)

    r_cuts = (0, R_WIDTH, 2 * R_WIDTH, 3 * R_WIDTH, R_COLS_PAD)

    def split4(t):
        return [t[:, r_cuts[i]:r_cuts[i + 1]] for i in range(4)]

    tile_pre = min(256, S)

    def rowmask_of():
        grow = pl.program_id(0) * tile_pre + lax.broadcasted_iota(jnp.int32, (tile_pre, 1), 0)
        return (grow > 0).astype(F32)

    pre_params = [b_r, mu_p, W["w0"], wd_p, W["a0"], wa_p, wg_p, W["k_k"], W["k_a"], bd]

    def pre_fwd(z, p, bz, mu, w0, wd, a0, wa, wg, k_k, k_a, bd_):
        return _pre_core(rowmask_of(), bd_, *split4(z), *split4(p), *split4(bz), *split4(mu),
                         w0, wd, a0, wa, wg, k_k, k_a)

    proj_r_prev = _shift_down(proj_r)
    s_r, s_w, s_k, s_v, s_a, s_b, s_g = _rows(
        "rwkv_pre_fwd", pre_fwd, [proj_r, proj_r_prev], pre_params, [(R_WIDTH, F32)] * 7, tile=tile_pre)
    y_scan, ckpt = _scan_fwd(s_r, s_w, s_k, s_v, s_a, s_b)

    def post_fwd(y, r, k2, v, g, gain, bias, rk_, bd_):
        return _post_core(bd_, y, r, k2, v, g, gain, bias, rk_)

    post_params = [W["gn_gain"], W["gn_bias"], rk, bd]
    (y_b,) = _rows("rwkv_post_fwd", post_fwd, [y_scan, s_r, s_k, s_v, s_g], post_params, [(R_WIDTH, bf)])
    p_a = _mm("branch_a", y_a, W["w_branch_a"], "nn")
    p_b = _mm("branch_b", y_b, W["w_branch_b"], "nn")

    def merge_fwd(pa, pb, gz, bgz):
        return _merge_core(pa, pb, gz[:, :D_MODEL], gz[:, D_MODEL:], bgz[:, :D_MODEL], bgz[:, D_MODEL:])

    (merged,) = _rows("merge_fwd", merge_fwd, [p_a, p_b, proj_gate], [b_gate], [(D_MODEL, bf)])
    mix = _mm("out_proj", merged, W["w_out"], "nn")
    ln1_params = [gt1, W["b_out"], W["ln1_g"], W["ln1_b"], sc2, sh2]
    h1, h2 = _rows("ln1_fwd", _ln1_core, [x, mix], ln1_params, [(D_MODEL, F32), (D_MODEL, bf)])

    a1 = _mm("ff1", h2, W["w_ff1"], "nn")
    (act,) = _rows("ff_act", lambda z, bb: jnp.square(jnp.maximum(z + bb, 0.0)), [a1], [W["b_ff1"]], [(D_FF, bf)])
    ff = _mm("ff2", act, W["w_ff2"], "nn")

    def ln2_loss(h1_, ff_, tg, gt2_, bff2, g, b):
        loss, vjp = jax.vjp(functools.partial(_ln2_loss_core, tg), h1_, ff_, gt2_, bff2, g, b)
        return vjp(jnp.ones((), F32)) + (loss,)

    ln2_params = [gt2, W["b_ff2"], W["ln2_g"], W["ln2_b"]]
    dh1, dff, dgt2, G["b_ff2"], G["ln2_g"], G["ln2_b"], loss_acc = _rows(
        "ln2_loss", ln2_loss, [h1, ff, tgt], ln2_params, [(D_MODEL, F32), (D_MODEL, bf)],
        accs=[(1, D_MODEL)] * 4 + [(1, LANES)])
    loss = loss_acc[0, 0]

    dact = _mm("ff2_dx", dff, W["w_ff2"], "nt")
    G["w_ff2"] = _mm("ff2_dw", act, dff, "tn", WIRE_DT).reshape(N_CHIPS, D_FF // N_CHIPS, D_MODEL)

    def act_bwd(z, da, bb):
        d = da * 2.0 * jnp.maximum(z + bb, 0.0)
        return d, _colsum(d)

    da1, G["b_ff1"] = _rows("ff_act_bwd", act_bwd, [a1, dact], [W["b_ff1"]], [(D_FF, bf)], accs=[(1, D_FF)])
    dh2 = _mm("ff1_dx", da1, W["w_ff1"], "nt")
    G["w_ff1"] = _mm("ff1_dw", h2, da1, "tn", WIRE_DT, out_split=N_CHIPS)

    def ln1_bwd(x_, mix_, dh1_, dh2_, *ps):
        _, vjp = jax.vjp(_ln1_core, x_, mix_, *ps)
        return vjp((dh1_, dh2_))

    dx_res, dmix, dgt1, G["b_out"], G["ln1_g"], G["ln1_b"], dsc2, dsh2 = _rows(
        "ln1_bwd", ln1_bwd, [x, mix, dh1, dh2], ln1_params, [(D_MODEL, F32), (D_MODEL, bf)],
        accs=[(1, D_MODEL)] * 6)

    dmerged = _mm("out_proj_dx", dmix, W["w_out"], "nt")
    G["w_out"] = _mm("out_proj_dw", merged, dmix, "tn", WIRE_DT).reshape(N_CHIPS, D_MODEL // N_CHIPS, D_MODEL)

    def merge_bwd(pa, pb, gz, dm, bgz):
        args = (pa.astype(F32), pb.astype(F32), gz[:, :D_MODEL], gz[:, D_MODEL:], bgz[:, :D_MODEL], bgz[:, D_MODEL:])
        _, vjp = jax.vjp(_merge_core, *args)
        dpa, dpb, dga, dgb, dbga, dbgb = vjp(dm)
        return dpa, dpb, jnp.concatenate([dga, dgb], axis=1), jnp.concatenate([dbga, dbgb], axis=1)

    dp_a, dp_b, dgates, db_gate = _rows(
        "merge_bwd", merge_bwd, [p_a, p_b, proj_gate, dmerged], [b_gate],
        [(D_MODEL, bf), (D_MODEL, bf), (2 * D_MODEL, F32)], accs=[(1, 2 * D_MODEL)])
    dy_a = _mm("branch_a_dx", dp_a, W["w_branch_a"], "nt")
    G["w_branch_a"] = _mm("branch_a_dw", y_a, dp_a, "tn", WIRE_DT, out_split=N_CHIPS)
    dy_b = _mm("branch_b_dx", dp_b, W["w_branch_b"], "nt")
    G["w_branch_b"] = _mm("branch_b_dw", y_b, dp_b, "tn", WIRE_DT, out_split=N_CHIPS)

    def post_bwd(y, r, k2, v, g, dyb, gain, bias, rk_, bd_):
        _, vjp = jax.vjp(functools.partial(_post_core, bd_), y, r, k2, v, g, gain, bias, rk_)
        return vjp(dyb)

    dy_scan, dr_p, dk_p, dv_p, dg_p, G["gn_gain"], G["gn_bias"], drk = _rows(
        "rwkv_post_bwd", post_bwd, [y_scan, s_r, s_k, s_v, s_g, dy_b], post_params,
        [(R_WIDTH, F32)] * 5, accs=[(1, R_WIDTH)] * 3)
    G["r_k"] = drk.reshape(R_HEADS, R_HEAD)
    dr_s, dw_s, dk_s, dv_s, da_s, db_s = _scan_bwd(s_r, s_w, s_k, s_v, s_a, s_b, ckpt, dy_scan)

    def pre_bwd(z, p, dr1, dr2, dw, dk1, dk2, dv1, dv2, da, db, dg,
                bz, mu, w0, wd, a0, wa, wg, k_k, k_a, bd_):
        prim = (*split4(z), *split4(p), *split4(bz), *split4(mu), w0, wd, a0, wa, wg, k_k, k_a)
        _, vjp = jax.vjp(functools.partial(_pre_core, rowmask_of(), bd_), *prim)
        d = vjp((dr1 + dr2, dw, dk1 + dk2, dv1 + dv2, da, db, dg))
        cat = lambda parts: jnp.concatenate(parts, axis=1)
        return (cat(d[0:4]), cat(d[4:8]), cat(d[8:12]), cat(d[12:16])) + tuple(d[16:])

    dz_r, dprev, db_r, dmu_p, G["w0"], dwd_p, G["a0"], dwa_p, dwg_p, G["k_k"], G["k_a"] = _rows(
        "rwkv_pre_bwd", pre_bwd,
        [proj_r, proj_r_prev, dr_s, dr_p, dw_s, dk_s, dk_p, dv_s, dv_p, da_s, db_s, dg_p],
        pre_params, [(R_COLS_PAD, F32)] * 2,
        accs=[(1, R_COLS_PAD), (1, R_COLS_PAD), (1, R_WIDTH), (LORA_PAD, R_WIDTH), (1, R_WIDTH),
              (LORA_PAD, R_WIDTH), (LORA_PAD, R_WIDTH), (1, R_WIDTH), (1, R_WIDTH)],
        tile=tile_pre)
    G["mu_shift"] = dmu_p[:, :R_COLS]
    G["w_decay_up"] = dwd_p[:DECAY_LORA]
    G["w_aaa_up"] = dwa_p[DECAY_LORA:DECAY_LORA + AAA_LORA]
    G["w_gate_up"] = dwg_p[DECAY_LORA + AAA_LORA:LORA]

    def gmlp_bwd(z, dya, bz, g, b, bst_, gsel_, ws_):
        gmask, tril = _gmlp_consts()
        (zu, zv), (bu, bv) = split2(z), split2(bz)
        wsl = [ws_[i * CHUNK:(i + 1) * CHUNK] for i in range(G_GROUPS)]
        _, vjp = jax.vjp(functools.partial(_gmlp_core, gmask, tril, gsel_), zu, zv, bu, bv, g, b, bst_, *wsl)
        d = vjp(dya)
        return (jnp.concatenate(d[0:2], axis=1), jnp.concatenate(d[2:4], axis=1), d[4], d[5], d[6],
                jnp.concatenate(d[7:], axis=0))

    dz_g, db_g, G["g_ln_v"], G["b_ln_v"], dbst, dws2 = _rows(
        "gmlp_bwd", gmlp_bwd, [proj_g, dy_a], [b_g, W["g_ln_v"], W["b_ln_v"], bst, gsel, ws2],
        [(2 * G_WIDTH, F32)],
        accs=[(1, 2 * G_WIDTH), (1, G_WIDTH), (1, G_WIDTH), (CHUNK, LANES), (G_GROUPS * CHUNK, CHUNK)],
        tile=CHUNK)
    G["w_spatial"] = dws2.reshape(G_GROUPS, CHUNK, CHUNK)
    G["b_spatial"] = dbst[:, :G_GROUPS].T

    def dproj_cat(dzg, dzr, dpv, dgz):
        return jnp.concatenate([dzg, dzr + dpv, dgz], axis=1)

    (dproj,) = _rows("dproj_cat", dproj_cat, [dz_g, dz_r, _shift_up(dprev), dgates], [],
                     [(2 * G_WIDTH + R_COLS_PAD + 2 * D_MODEL, bf)])
    dh = _mm("proj_dx", dproj, w_in_p, "nt", tk=2432)
    G["w_in"] = _split_cols(_unpad_in_cols(_mm("proj_dw", h, dproj, "tn", WIRE_DT, tm=512, tn=2432), 1))
    G["b_in"] = _unpad_in_cols(jnp.concatenate([db_g, db_r, db_gate], axis=1), 1)

    def mod1_bwd(x_, dh_, dxr, sc):
        return dh_ * (1.0 + sc) + dxr, _colsum(dh_ * x_), _colsum(dh_)

    grad_x, dsc1, dsh1 = _rows("mod1_bwd", mod1_bwd, [x, dh, dx_res], [sc1], [(D_MODEL, F32)],
                               accs=[(1, D_MODEL)] * 2)

    dmod = jnp.concatenate([dsh1, dsc1, dgt1, dsh2, dsc2, dgt2], axis=1)
    G["b_ada"] = dmod
    G["w_ada"] = _ada_dw(c.reshape(D_MODEL, 1), dmod)
    return loss, grad_x, G


BIG = (("w_ada", (D_MODEL, 6 * D_MODEL), 1), ("w_in", (D_MODEL, 2 * G_WIDTH + R_COLS + 2 * D_MODEL), 1),
       ("w_branch_a", (G_WIDTH, D_MODEL), 1), ("w_branch_b", (R_WIDTH, D_MODEL), 1),
       ("w_out", (D_MODEL, D_MODEL), 0), ("w_ff1", (D_MODEL, D_FF), 1), ("w_ff2", (D_FF, D_MODEL), 0))
LORAS = (("w_decay_up", (DECAY_LORA, R_WIDTH), 1), ("w_aaa_up", (AAA_LORA, R_WIDTH), 1),
         ("w_gate_up", (GATE_LORA, R_WIDTH), 1))
SHARDED = BIG + LORAS
SMALL = (("b_ada", (1, 6 * D_MODEL)), ("b_in", (1, 2 * G_WIDTH + R_COLS + 2 * D_MODEL)),
         ("g_ln_v", (1, G_WIDTH)), ("b_ln_v", (1, G_WIDTH)), ("w_spatial", (G_GROUPS, CHUNK, CHUNK)),
         ("b_spatial", (G_GROUPS, CHUNK)), ("mu_shift", (1, R_COLS)), ("w0", (1, R_WIDTH)),
         ("a0", (1, R_WIDTH)), ("k_k", (1, R_WIDTH)), ("k_a", (1, R_WIDTH)), ("r_k", (R_HEADS, R_HEAD)),
         ("gn_gain", (1, R_WIDTH)), ("gn_bias", (1, R_WIDTH)), ("b_out", (1, D_MODEL)),
         ("ln1_g", (1, D_MODEL)), ("ln1_b", (1, D_MODEL)), ("b_ff1", (1, D_FF)), ("b_ff2", (1, D_MODEL)),
         ("ln2_g", (1, D_MODEL)), ("ln2_b", (1, D_MODEL)))
WEIGHT_ORDER = ("w_ada", "b_ada", "w_in", "b_in", "g_ln_v", "b_ln_v", "w_spatial", "b_spatial", "mu_shift",
                "w0", "w_decay_up", "a0", "w_aaa_up", "w_gate_up", "k_k", "k_a", "r_k", "gn_gain", "gn_bias",
                "w_branch_a", "w_branch_b", "w_out", "b_out", "ln1_g", "ln1_b", "w_ff1", "b_ff1", "w_ff2",
                "b_ff2", "ln2_g", "ln2_b")
N_CHIPS = 4


def _shard_shape(shape, axis):
    s = list(shape)
    s[axis] //= N_CHIPS
    return tuple(s)


def _numel(shape):
    return int(np.prod(shape))


def _round_up(n, q):
    return -(-n // q) * q


N_LORA = sum(_numel(_shard_shape(s, a)) for _, s, a in LORAS)
N_SMALL = sum(_numel(s) for _, s in SMALL)
ROWS_SW = _round_up(N_LORA, PACK_Q) // PACK_W
ROWS_SG = _round_up(N_LORA + N_SMALL, PACK_Q) // PACK_W


def _pack_small(loras, small, rows, dtype):
    parts = [loras[n] for n, _, _ in LORAS] + ([small[n] for n, _ in SMALL] if small is not None else [])
    flat = jnp.concatenate([p.reshape(-1).astype(dtype) for p in parts])
    flat = jnp.concatenate([flat, jnp.zeros((rows * PACK_W - flat.shape[0],), dtype)])
    return flat.reshape(rows, PACK_W)


def _unpack_small(pack, with_small):
    flat = pack.reshape(-1)
    out, off = {}, 0
    for n, s, a in LORAS:
        ss = _shard_shape(s, a)
        out[n] = flat[off:off + _numel(ss)].reshape(ss)
        off += _numel(ss)
    if with_small:
        for n, s in SMALL:
            out[n] = flat[off:off + _numel(s)].reshape(s)
            off += _numel(s)
    return out


def _pack_small_grads(G):
    segs = []
    for q in range(N_CHIPS):
        loras = {n: G[n][:, q * (s[1] // N_CHIPS):(q + 1) * (s[1] // N_CHIPS)] for n, s, _ in LORAS}
        segs.append(_pack_small(loras, G, ROWS_SG, F32))
    return jnp.stack(segs)


ANY = pl.BlockSpec(memory_space=pl.ANY)
MESH = pl.DeviceIdType.MESH


def _place():
    x, y, c = lax.axis_index("x"), lax.axis_index("y"), lax.axis_index("c")
    chips = [(1 - x, y), (x, 1 - y), (1 - x, 1 - y)]
    return x, y, c, chips


def _remote(src, dst, send_sem, recv_sem, to):
    return pltpu.make_async_remote_copy(src_ref=src, dst_ref=dst, send_sem=send_sem, recv_sem=recv_sem,
                                        device_id=to, device_id_type=MESH)


def _all_gather(shards):
    n = len(shards)
    halves = [a.shape[0] // 2 for a in shards]

    def body(*refs):
        ins, outs = refs[:n], refs[n:2 * n]
        send_sems, recv_sems = refs[2 * n:]
        x, y, c, chips = _place()
        s = 2 * x + y
        sibling = (x, y, 1 - c)
        slot = [2 * chip[0] + chip[1] for chip in chips]
        first, passed = [], []
        for a in range(n):
            H = halves[a]
            mine = ins[a].at[pl.ds(c * H, H)]
            first += [_remote(mine, outs[a].at[2 * s + c], send_sems.at[6 * a + j], recv_sems.at[6 * a + j], (*chip, c))
                      for j, chip in enumerate(chips)]
        for cp in first:
            cp.start()
        for a in range(n):
            for j in range(3):
                landed = outs[a].at[2 * slot[j] + c]
                _remote(landed, landed, send_sems.at[6 * a + j], recv_sems.at[6 * a + j], sibling).wait_recv()
                cp = _remote(landed, landed, send_sems.at[6 * a + 3 + j], recv_sems.at[6 * a + 3 + j], sibling)
                cp.start()
                passed.append(cp)
        for a in range(n):
            for j in range(3):
                got = outs[a].at[2 * slot[j] + 1 - c]
                _remote(got, got, send_sems.at[6 * a + 3 + j], recv_sems.at[6 * a + 3 + j], sibling).wait_recv()
        for cp in first + passed:
            cp.wait_send()

    return pl.pallas_call(
        body, name="ag_weights", in_specs=[ANY] * n, out_specs=[ANY] * n,
        out_shape=[jax.ShapeDtypeStruct((2 * N_CHIPS, h, a.shape[1]), a.dtype) for a, h in zip(shards, halves)],
        scratch_shapes=[pltpu.SemaphoreType.DMA((6 * n,)), pltpu.SemaphoreType.DMA((6 * n,))],
    )(*shards)


def _rs_sibling_in(gps):
    n = len(gps)

    def body(*refs):
        ins, outs = refs[:n], refs[n:2 * n]
        send_sems, recv_sems = refs[2 * n:]
        x, y, c, _ = _place()
        cps = []
        for a in range(n):
            H = gps[a].shape[1] // 2
            cps += [_remote(ins[a].at[q, pl.ds((1 - c) * H, H)], outs[a].at[q], send_sems.at[N_CHIPS * a + q],
                            recv_sems.at[N_CHIPS * a + q], (x, y, 1 - c)) for q in range(N_CHIPS)]
        for cp in cps:
            cp.start()
        for cp in cps:
            cp.wait()

    return pl.pallas_call(
        body, name="rs_sibling_in", in_specs=[ANY] * n, out_specs=[ANY] * n,
        out_shape=[jax.ShapeDtypeStruct((N_CHIPS, g.shape[1] // 2, g.shape[2]), g.dtype) for g in gps],
        scratch_shapes=[pltpu.SemaphoreType.DMA((N_CHIPS * n,)), pltpu.SemaphoreType.DMA((N_CHIPS * n,))],
    )(*gps)


def _rs_add_own(name, gp, got, c_arr, tr=256):
    H, C = got.shape[1:]
    tr = _pick(H, tr, 8)
    nb = H // tr

    def body(c_ref, g_ref, r_ref, o_ref):
        o_ref[...] = (g_ref[...].astype(F32) + r_ref[...].astype(F32)).astype(o_ref.dtype)

    return pl.pallas_call(
        body, name="rs_add_own_" + name,
        grid_spec=pltpu.PrefetchScalarGridSpec(
            num_scalar_prefetch=1, grid=(N_CHIPS, nb),
            in_specs=[pl.BlockSpec((1, tr, C), lambda q, i, c_ref: (q, c_ref[0] * nb + i, 0)),
                      pl.BlockSpec((1, tr, C), lambda q, i, c_ref: (q, i, 0))],
            out_specs=pl.BlockSpec((1, tr, C), lambda q, i, c_ref: (q, i, 0))),
        out_shape=jax.ShapeDtypeStruct((N_CHIPS, H, C), gp.dtype),
        compiler_params=pltpu.CompilerParams(dimension_semantics=("arbitrary", "arbitrary")),
    )(c_arr, gp, got)


def _rs_chips(parts):
    n = len(parts)

    def body(*refs):
        ins, outs = refs[:n], refs[n:2 * n]
        send_sems, recv_sems = refs[2 * n:]
        x, y, c, chips = _place()
        s = 2 * x + y
        slot = [2 * chip[0] + chip[1] for chip in chips]
        cps = [_remote(ins[a].at[slot[j]], outs[a].at[s], send_sems.at[3 * a + j], recv_sems.at[3 * a + j], (*chips[j], c))
               for a in range(n) for j in range(3)]
        for cp in cps:
            cp.start()
        for a in range(n):
            for j in range(3):
                cps[3 * a + j].wait_send()
                got = outs[a].at[slot[j]]
                _remote(got, got, send_sems.at[3 * a + j], recv_sems.at[3 * a + j], (*chips[j], c)).wait_recv()

    return pl.pallas_call(
        body, name="rs_chips", in_specs=[ANY] * n, out_specs=[ANY] * n,
        out_shape=[jax.ShapeDtypeStruct(p.shape, p.dtype) for p in parts],
        scratch_shapes=[pltpu.SemaphoreType.DMA((3 * n,)), pltpu.SemaphoreType.DMA((3 * n,))],
    )(*parts)


def _rs_add_chips(name, part, slots, sc_arr, tr=128):
    H, C = slots.shape[1:]
    tr = _pick(H, tr, 8)
    nb = H // tr

    def body(sc_ref, p_ref, s_ref, o_ref):
        acc = None
        for q in range(N_CHIPS):
            term = lax.cond(sc_ref[0] == q, lambda: p_ref[0].astype(F32), lambda q=q: s_ref[q].astype(F32))
            acc = term if acc is None else acc + term
        o_ref[...] = acc

    return pl.pallas_call(
        body, name="rs_add_chips_" + name,
        grid_spec=pltpu.PrefetchScalarGridSpec(
            num_scalar_prefetch=1, grid=(nb,),
            in_specs=[pl.BlockSpec((1, tr, C), lambda i, sc: (sc[0], i, 0)),
                      pl.BlockSpec((N_CHIPS, tr, C), lambda i, sc: (0, i, 0))],
            out_specs=pl.BlockSpec((tr, C), lambda i, sc: (sc[1] * nb + i, 0))),
        out_shape=jax.ShapeDtypeStruct((2 * H, C), F32),
        compiler_params=pltpu.CompilerParams(dimension_semantics=("arbitrary",)),
    )(sc_arr, part, slots)


def _rs_sibling_out(wholes):
    n = len(wholes)

    def body(*refs):
        ins, outs = refs[:n], refs[n:2 * n]
        send_sems, recv_sems = refs[2 * n:]
        x, y, c, _ = _place()
        cps = []
        for a in range(n):
            H = wholes[a].shape[0] // 2
            cps.append(_remote(ins[a].at[pl.ds(c * H, H)], outs[a].at[pl.ds(c * H, H)], send_sems.at[a], recv_sems.at[a],
                               (x, y, 1 - c)))
        for cp in cps:
            cp.start()
        for a in range(n):
            H = wholes[a].shape[0] // 2
            cps[a].wait_send()
            got = outs[a].at[pl.ds((1 - c) * H, H)]
            _remote(got, got, send_sems.at[a], recv_sems.at[a], (x, y, 1 - c)).wait_recv()

    return pl.pallas_call(
        body, name="rs_sibling_out", in_specs=[ANY] * n, out_specs=[ANY] * n,
        out_shape=[jax.ShapeDtypeStruct(w.shape, w.dtype) for w in wholes],
        input_output_aliases={a: a for a in range(n)},
        scratch_shapes=[pltpu.SemaphoreType.DMA((n,)), pltpu.SemaphoreType.DMA((n,))],
    )(*wholes)


def _adamw(name, w, g, m, v):
    def fn(w_, g_, m_, v_):
        m2 = ADAM_B1 * m_ + (1.0 - ADAM_B1) * g_
        v2 = ADAM_B2 * v_ + (1.0 - ADAM_B2) * (g_ * g_)
        m_hat = m2 / (1.0 - ADAM_B1 ** ADAM_STEP)
        v_hat = v2 / (1.0 - ADAM_B2 ** ADAM_STEP)
        return -ADAM_LR * (m_hat / (jnp.sqrt(v_hat) + ADAM_EPS) + ADAM_WD * w_), m2, v2

    return _rows("adamw_" + name, fn, [w, g, m, v], [], [(w.shape[1], F32)] * 3, tile=_pick(w.shape[0], 256, 8))


def kernel(x, c, w_ada, b_ada, w_in, b_in, g_ln_v, b_ln_v, w_spatial, b_spatial, mu_shift, w0, w_decay_up, a0, w_aaa_up, w_gate_up, k_k, k_a, r_k, gn_gain, gn_bias, w_branch_a, w_branch_b, w_out, b_out, ln1_g, ln1_b, w_ff1, b_ff1, w_ff2, b_ff2, ln2_g, ln2_b, loss_target, m_w_ada, m_b_ada, m_w_in, m_b_in, m_g_ln_v, m_b_ln_v, m_w_spatial, m_b_spatial, m_mu_shift, m_w0, m_w_decay_up, m_a0, m_w_aaa_up, m_w_gate_up, m_k_k, m_k_a, m_r_k, m_gn_gain, m_gn_bias, m_w_branch_a, m_w_branch_b, m_w_out, m_b_out, m_ln1_g, m_ln1_b, m_w_ff1, m_b_ff1, m_w_ff2, m_b_ff2, m_ln2_g, m_ln2_b, v_w_ada, v_b_ada, v_w_in, v_b_in, v_g_ln_v, v_b_ln_v, v_w_spatial, v_b_spatial, v_mu_shift, v_w0, v_w_decay_up, v_a0, v_w_aaa_up, v_w_gate_up, v_k_k, v_k_a, v_r_k, v_gn_gain, v_gn_bias, v_w_branch_a, v_w_branch_b, v_w_out, v_b_out, v_ln1_g, v_ln1_b, v_w_ff1, v_b_ff1, v_w_ff2, v_b_ff2, v_ln2_g, v_ln2_b):
    args = dict(locals())
    local_shape = {n: _shard_shape(s, a) for n, s, a in SHARDED}
    local_shape.update(dict(SMALL))
    wts = {n: args[n].reshape(local_shape[n]) for n in WEIGHT_ORDER}
    mom = {n: args["m_" + n].reshape(local_shape[n]) for n in WEIGHT_ORDER}
    var = {n: args["v_" + n].reshape(local_shape[n]) for n in WEIGHT_ORDER}
    big = [n for n, _, _ in BIG]

    chip = 2 * lax.axis_index("x") + lax.axis_index("y")
    mine = [wts[n].astype(MXU_DT) for n in big] + [_pack_small(wts, None, ROWS_SW, MXU_DT)]
    gath = [lax.dynamic_update_slice(g.reshape((N_CHIPS,) + m.shape), m[None], (chip, 0, 0))
            for g, m in zip(_all_gather(mine), mine)]
    W = {n: wts[n] for n, _ in SMALL}
    for (n, s, a), g in zip(BIG, gath):
        W[n] = g if a == 1 else g.reshape(s)
    lora_q = [_unpack_small(gath[-1][q], False) for q in range(N_CHIPS)]
    for n, _, _ in LORAS:
        W[n] = jnp.concatenate([lora_q[q][n] for q in range(N_CHIPS)], axis=1)

    loss, grad_x, G = _local_step(x[0], c, loss_target[0], W)
    loss = lax.psum(loss, MESH_AXES)

    names = big + ["small"]
    gps = [G[n] for n in big] + [_pack_small_grads(G)]
    c_arr = lax.axis_index("c").astype(jnp.int32).reshape(1)
    parts = [_rs_add_own(n, g, r, c_arr) for n, g, r in zip(names, gps, _rs_sibling_in(gps))]
    sc_arr = jnp.stack([chip, lax.axis_index("c")]).astype(jnp.int32)
    segs = _rs_sibling_out([_rs_add_chips(n, p, s, sc_arr) for n, p, s in zip(names, parts, _rs_chips(parts))])

    out = {}
    for n, g in zip(big, segs[:-1]):
        out[n] = (g,) + tuple(_adamw(n, wts[n], g, mom[n], var[n]))
    packs = [_pack_small(t, t, ROWS_SG, F32) for t in (wts, mom, var)]
    small4 = [_unpack_small(t, True) for t in (segs[-1],) + tuple(_adamw("small", packs[0], segs[-1], packs[1], packs[2]))]
    res = [loss, grad_x[None]]
    for k in range(4):
        res += [(out[n][k] if n in out else small4[k][n]).reshape(args[n].shape) for n in WEIGHT_ORDER]
    return tuple(res)
```

```python
import functools

import numpy as np
import jax
import jax.numpy as jnp
from jax import lax
from jax.experimental import pallas as pl
from jax.experimental.pallas import tpu as pltpu

F32 = jnp.float32
MXU_DT = jnp.bfloat16
WIRE_DT = jnp.bfloat16

D_MODEL = 1024
G_GROUPS = 8
G_WIDTH = 512
CHUNK = 128
R_WIDTH = 512
R_HEAD = 64
R_HEADS = 8
DECAY_LORA = 32
AAA_LORA = 32
GATE_LORA = 96
LORA = DECAY_LORA + AAA_LORA + GATE_LORA
LORA_PAD = 256
R_COLS = 3 * R_WIDTH + LORA
R_COLS_PAD = 3 * R_WIDTH + LORA_PAD
D_FF = 4 * D_MODEL
ALPHA = 2.0 ** 0.25
LN_EPS = 1e-5
GN_EPS = 64e-5
ADAM_LR = 0.001
ADAM_B1 = 0.9
ADAM_B2 = 0.999
ADAM_EPS = 1e-08
ADAM_WD = 0.01
ADAM_STEP = 10

LANES = 128
PACK_W = 512
PACK_Q = 2 * 16 * PACK_W
VMEM_LIMIT = 48 * 1024 * 1024
SCAN_T = 64

MESH_AXES = ("x", "y", "c")


def _dg(a, b, dims):
    return lax.dot_general(a.astype(MXU_DT), b.astype(MXU_DT), (dims, ((), ())),
                           preferred_element_type=F32)


@jax.custom_vjp
def _bdot(a, b):
    return _dg(a, b, ((1,), (0,)))


def _bdot_fwd(a, b):
    return _bdot(a, b), (a, b)


def _bdot_bwd(res, g):
    a, b = res
    return (_dg(g, b, ((1,), (1,))).astype(a.dtype), _dg(a, g, ((0,), (0,))).astype(b.dtype))


_bdot.defvjp(_bdot_fwd, _bdot_bwd)


def _split_dot(x, m, dims):
    hi = x.astype(jnp.bfloat16)
    lo = (x - hi.astype(F32)).astype(jnp.bfloat16)
    dn = (dims, ((), ()))
    return (lax.dot_general(hi, m, dn, preferred_element_type=F32)
            + lax.dot_general(lo, m, dn, preferred_element_type=F32))


@jax.custom_vjp
def _pdot(x, m):
    return _split_dot(x, m, ((1,), (0,)))


def _pdot_fwd(x, m):
    return _pdot(x, m), m


def _pdot_bwd(m, g):
    return _split_dot(g, m, ((1,), (1,))), None


_pdot.defvjp(_pdot_fwd, _pdot_bwd)


def _sigmoid(x):
    return 1.0 / (1.0 + jnp.exp(-x))


def _softplus(x):
    return jnp.maximum(x, 0.0) + jnp.log(1.0 + jnp.exp(-jnp.maximum(x, -x)))


def _gelu(x):
    return 0.5 * x * (1.0 + jnp.tanh(0.7978845608028654 * (x + 0.044715 * (x * x * x))))


def _ln(x, g, b, eps):
    mu = jnp.mean(x, axis=-1, keepdims=True)
    xc = x - mu
    var = jnp.mean(xc * xc, axis=-1, keepdims=True)
    return xc * lax.rsqrt(var + eps) * g + b


def _colsum(x):
    return jnp.sum(x, axis=0, keepdims=True)


def _pick(n, target, q=LANES):
    if n <= target:
        return n
    best = None
    for t in range(q, target + 1, q):
        if n % t == 0:
            best = t
    assert best is not None, (n, target)
    return best


def _mm(name, a, b, mode, out_dtype=F32, out_split=1, tm=1024, tn=1024, tk=1024):
    bs = b.shape[0] if b.ndim == 3 else 1
    br, bc = b.shape[-2:]
    if mode == "nn":
        (M, K), K2, N = a.shape, br, bc * bs
    elif mode == "nt":
        (M, K), N, K2 = a.shape, br, bc * bs
    else:
        assert bs == 1
        (K, M), K2, N = a.shape, br, bc
    assert K == K2, (name, a.shape, b.shape, mode)
    n_piece = N // max(bs if mode == "nn" else 1, out_split)
    k_piece = K // (bs if mode == "nt" else 1)
    tm, tn, tk = _pick(M, tm, 8 if M < LANES else LANES), _pick(n_piece, tn), _pick(k_piece, tk)
    nk, npj, npk = K // tk, n_piece // tn, k_piece // tk
    dims = {"nn": ((1,), (0,)), "nt": ((1,), (1,)), "tn": ((0,), (0,))}[mode]

    def body(a_ref, b_ref, o_ref, acc_ref):
        k = pl.program_id(2)

        @pl.when(k == 0)
        def _():
            acc_ref[...] = jnp.zeros(acc_ref.shape, F32)

        acc_ref[...] += _dg(a_ref[...], b_ref[0] if bs > 1 else b_ref[...], dims)

        @pl.when(k == nk - 1)
        def _():
            if out_split > 1:
                o_ref[0] = acc_ref[...].astype(o_ref.dtype)
            else:
                o_ref[...] = acc_ref[...].astype(o_ref.dtype)

    if mode == "nn":
        a_spec = pl.BlockSpec((tm, tk), lambda i, j, k: (i, k))
        b_spec = (pl.BlockSpec((tk, tn), lambda i, j, k: (k, j)) if bs == 1 else
                  pl.BlockSpec((1, tk, tn), lambda i, j, k: (j // npj, k, j % npj)))
    elif mode == "nt":
        a_spec = pl.BlockSpec((tm, tk), lambda i, j, k: (i, k))
        b_spec = (pl.BlockSpec((tn, tk), lambda i, j, k: (j, k)) if bs == 1 else
                  pl.BlockSpec((1, tn, tk), lambda i, j, k: (k // npk, j, k % npk)))
    else:
        a_spec = pl.BlockSpec((tk, tm), lambda i, j, k: (k, i))
        b_spec = pl.BlockSpec((tk, tn), lambda i, j, k: (k, j))
    if out_split > 1:
        o_spec = pl.BlockSpec((1, tm, tn), lambda i, j, k: (j // npj, i, j % npj))
        o_shape = jax.ShapeDtypeStruct((out_split, M, n_piece), out_dtype)
    else:
        o_spec = pl.BlockSpec((tm, tn), lambda i, j, k: (i, j))
        o_shape = jax.ShapeDtypeStruct((M, N), out_dtype)
    return pl.pallas_call(
        body, name=name, grid=(M // tm, N // tn, nk),
        in_specs=[a_spec, b_spec], out_specs=o_spec, out_shape=o_shape,
        scratch_shapes=[pltpu.VMEM((tm, tn), F32)],
        compiler_params=pltpu.CompilerParams(
            dimension_semantics=("parallel", "parallel", "arbitrary"), vmem_limit_bytes=VMEM_LIMIT),
    )(a, b)


def _rows(name, fn, rows, params, outs, accs=(), tile=256):
    S = rows[0].shape[0]
    tile = min(tile, S)
    assert S % tile == 0, (name, S, tile)
    nr, npar, no, na = len(rows), len(params), len(outs), len(accs)

    def body(*refs):
        rin, pin = refs[:nr], refs[nr:nr + npar]
        oref, aref = refs[nr + npar:nr + npar + no], refs[nr + npar + no:]
        res = fn(*[r[...] for r in rin], *[p[...] for p in pin])
        if not isinstance(res, (tuple, list)):
            res = (res,)
        assert len(res) == no + na, (name, len(res), no, na)
        for ref, val in zip(oref, res[:no]):
            ref[...] = val.astype(ref.dtype)
        if na:
            @pl.when(pl.program_id(0) == 0)
            def _():
                for ref in aref:
                    ref[...] = jnp.zeros(ref.shape, ref.dtype)

            for ref, val in zip(aref, res[no:]):
                ref[...] += jnp.broadcast_to(val, ref.shape).astype(ref.dtype)

    def whole(shape):
        nd = len(shape)
        return pl.BlockSpec(tuple(shape), lambda i: (0,) * nd)

    in_specs = ([pl.BlockSpec((tile, r.shape[1]), lambda i: (i, 0)) for r in rows]
                + [whole(p.shape) for p in params])
    out_specs = ([pl.BlockSpec((tile, n), lambda i: (i, 0)) for n, _ in outs]
                 + [whole(s) for s in accs])
    out_shape = ([jax.ShapeDtypeStruct((S, n), dt) for n, dt in outs]
                 + [jax.ShapeDtypeStruct(tuple(s), F32) for s in accs])
    res = pl.pallas_call(
        body, name=name, grid=(S // tile,), in_specs=in_specs, out_specs=out_specs,
        out_shape=out_shape,
        compiler_params=pltpu.CompilerParams(
            dimension_semantics=("arbitrary",), vmem_limit_bytes=VMEM_LIMIT),
    )(*rows, *params)
    return res


def _modulate(x, sc, sh):
    return x * (1.0 + sc) + sh


def _gmlp_consts():
    lane = lax.broadcasted_iota(jnp.int32, (1, G_WIDTH), 1)
    gmask = [(lane // (G_WIDTH // G_GROUPS) == g).astype(F32) for g in range(G_GROUPS)]
    tril = (lax.broadcasted_iota(jnp.int32, (CHUNK, CHUNK), 0)
            >= lax.broadcasted_iota(jnp.int32, (CHUNK, CHUNK), 1))
    return gmask, tril


def _gmlp_core(gmask, tril, gsel, zu, zv, bu, bv, g, b, bst, *ws):
    u = _gelu(zu + bu)
    v = _ln(_gelu(zv + bv), g, b, LN_EPS)
    s = _pdot(bst, gsel)
    for gi in range(G_GROUPS):
        s = s + _bdot(jnp.where(tril, ws[gi], 0.0), v * gmask[gi])
    return u * s


def _pre_core(rowmask, bd, zr, zk, zv, zl, pr, pk, pv, pq, br, bk, bv, bl, mr, mk, mv, ml,
              w0, wd, a0, wa, wg, k_k, k_a):
    def mix(z, p, b, mu):
        zz = z + b
        return zz + ((p + b) * rowmask - zz) * mu

    r, k, v, l = mix(zr, pr, br, mr), mix(zk, pk, bk, mk), mix(zv, pv, bv, mv), mix(zl, pq, bl, ml)
    w_log = -_softplus(-(w0 + _bdot(jnp.tanh(l), wd))) - 0.5
    decay = jnp.exp(-jnp.exp(w_log))
    a = _sigmoid(a0 + _bdot(l, wa))
    g = _bdot(_sigmoid(l), wg)
    kk = k * k_k
    kkn = kk / jnp.maximum(jnp.sqrt(_pdot(kk * kk, bd)), 1e-12)
    k2 = k * (1.0 + (a - 1.0) * k_a)
    return r, decay, k2, v, -kkn, kkn * a, g


def _post_core(bd, y, r, k2, v, g, gain, bias, rk):
    inv = 1.0 / R_HEAD
    mu = _pdot(y, bd) * inv
    yc = y - mu
    var = _pdot(yc * yc, bd) * inv
    yn = yc * lax.rsqrt(var + GN_EPS) * gain + bias
    bonus = _pdot(r * k2 * rk, bd) * v
    return (yn + bonus) * g


def _merge_core(pa, pb, ga, gb, bga, bgb):
    return _sigmoid(ga + bga) * pa + _sigmoid(gb + bgb) * pb


def _ln1_core(x, mix, gt1, bout, g, b, sc2, sh2):
    h1 = _ln(ALPHA * x + gt1 * (mix + bout), g, b, LN_EPS)
    return h1, h1 * (1.0 + sc2) + sh2


def _ln2_loss_core(tgt, h1, ff, gt2, bff2, g, b):
    out = _ln(ALPHA * h1 + gt2 * (ff + bff2), g, b, LN_EPS)
    err = out - tgt
    return 0.5 * jnp.sum(err * err) * (1.0 / D_MODEL)


def _scan_consts():
    sub = lax.broadcasted_iota(jnp.int32, (R_HEAD, LANES), 0)
    lane = lax.broadcasted_iota(jnp.int32, (R_HEAD, LANES), 1)
    return lane < R_HEAD, sub == (lane & (R_HEAD - 1))


def _seg_sum(lo, xb):
    s_lo = jnp.sum(jnp.where(lo, xb, 0.0), axis=1, keepdims=True)
    s_hi = jnp.sum(jnp.where(lo, 0.0, xb), axis=1, keepdims=True)
    return jnp.where(lo, s_lo, s_hi)


def _seg_dot(lo, s, row):
    lo_row = lo[0:1, :]
    s_lo = jnp.sum(s * jnp.where(lo_row, row, 0.0), axis=1, keepdims=True)
    s_hi = jnp.sum(s * jnp.where(lo_row, 0.0, row), axis=1, keepdims=True)
    return jnp.where(lo, s_lo, s_hi)


def _row_of_col(eye, colb):
    return jnp.sum(jnp.where(eye, colb, 0.0), axis=0, keepdims=True)


def _head_ones():
    i = lax.broadcasted_iota(jnp.int32, (LANES, LANES), 0) // R_HEAD
    j = lax.broadcasted_iota(jnp.int32, (LANES, LANES), 1) // R_HEAD
    return (i == j).astype(jnp.bfloat16)


N_SPLIT = 2


def _split(x):
    parts, r = [], x
    for u in range(N_SPLIT):
        p = r.astype(jnp.bfloat16)
        parts.append(p)
        if u + 1 < N_SPLIT:
            r = r - p.astype(F32)
    return parts


def _seg_sums_mxu(ones, mats):
    parts = [p for m in mats for p in _split(m)]
    res = lax.dot_general(jnp.concatenate(parts, axis=0), ones, (((1,), (0,)), ((), ())),
                          preferred_element_type=F32)
    out = []
    for n in range(len(mats)):
        t = [res[(N_SPLIT * n + u) * R_HEAD:(N_SPLIT * n + u + 1) * R_HEAD] for u in range(N_SPLIT)]
        out.append(functools.reduce(lambda p, q: p + q, t))
    return out


def _cols_of_rows_mxu(eye, ones, rows8):
    return _seg_sums_mxu(ones, [jnp.where(eye, jnp.broadcast_to(rows8[i:i + 1, :], eye.shape), 0.0)
                                for i in range(rows8.shape[0])])


N_BLK = R_WIDTH // LANES
ROW_GROUP = 8


def _scan_fwd(r, w, k, v, a, b):
    S = r.shape[0]
    T = min(SCAN_T, S)
    nchunk = S // T

    def body(r_ref, w_ref, k_ref, v_ref, a_ref, b_ref, y_ref, sv_ref, st_ref):
        lo, eye = _scan_consts()
        ones = _head_ones()

        @pl.when(pl.program_id(0) == 0)
        def _():
            st_ref[...] = jnp.zeros(st_ref.shape, F32)

        sub8 = lax.broadcasted_iota(jnp.int32, (ROW_GROUP, LANES), 0)

        def group(gi, state):
            base = pl.multiple_of(gi * ROW_GROUP, ROW_GROUP)
            state = list(state)
            sls = [slice(q * LANES, (q + 1) * LANES) for q in range(N_BLK)]
            ld = lambda ref: [ref[pl.ds(base, ROW_GROUP), sl] for sl in sls]
            r8, w8, k8, v8, a8, b8 = ld(r_ref), ld(w_ref), ld(k_ref), ld(v_ref), ld(a_ref), ld(b_ref)
            vb = [_cols_of_rows_mxu(eye, ones, v8[q]) for q in range(N_BLK)]
            y8 = [jnp.zeros((ROW_GROUP, LANES), F32)] * N_BLK
            for i in range(ROW_GROUP):
                row = lambda t8: t8[i:i + 1, :]
                for q in range(N_BLK):
                    s = state[q]
                    sv_ref[base + i, :, sls[q]] = s
                    sa = _seg_dot(lo, s, row(a8[q]))
                    state[q] = s * row(w8[q]) + sa * row(b8[q]) + vb[q][i] * row(k8[q])
                ycol = _seg_sums_mxu(ones, [state[q] * row(r8[q]) for q in range(N_BLK)])
                y8 = [jnp.where(sub8 == i, _row_of_col(eye, ycol[q]), y8[q]) for q in range(N_BLK)]
            for q in range(N_BLK):
                y_ref[pl.ds(base, ROW_GROUP), sls[q]] = y8[q]
            return tuple(state)

        init = tuple(st_ref[:, q * LANES:(q + 1) * LANES] for q in range(N_BLK))
        fin = lax.fori_loop(0, T // ROW_GROUP, group, init)
        for q in range(N_BLK):
            st_ref[:, q * LANES:(q + 1) * LANES] = fin[q]

    blk = pl.BlockSpec((T, R_WIDTH), lambda i: (i, 0))
    return pl.pallas_call(
        body, name="scan_fwd", grid=(nchunk,), in_specs=[blk] * 6,
        out_specs=[blk, pl.BlockSpec((T, R_HEAD, R_WIDTH), lambda i: (i, 0, 0))],
        out_shape=[jax.ShapeDtypeStruct((S, R_WIDTH), F32),
                   jax.ShapeDtypeStruct((S, R_HEAD, R_WIDTH), F32)],
        scratch_shapes=[pltpu.VMEM((R_HEAD, R_WIDTH), F32)],
        compiler_params=pltpu.CompilerParams(
            dimension_semantics=("arbitrary",), vmem_limit_bytes=VMEM_LIMIT),
    )(r, w, k, v, a, b)


def _scan_bwd(r, w, k, v, a, b, states, dy):
    S = r.shape[0]
    T = min(SCAN_T, S)
    nchunk = S // T

    def body(r_ref, w_ref, k_ref, v_ref, a_ref, b_ref, sv_ref, dy_ref,
             dr_ref, dw_ref, dk_ref, dv_ref, da_ref, db_ref, ds_ref):
        lo, eye = _scan_consts()
        ones = _head_ones()

        @pl.when(pl.program_id(0) == 0)
        def _():
            ds_ref[...] = jnp.zeros(ds_ref.shape, F32)

        sub8 = lax.broadcasted_iota(jnp.int32, (ROW_GROUP, LANES), 0)

        def bgroup(n, dstate):
            base = pl.multiple_of((T // ROW_GROUP - 1 - n) * ROW_GROUP, ROW_GROUP)
            dstate = list(dstate)
            sls = [slice(q * LANES, (q + 1) * LANES) for q in range(N_BLK)]
            ld = lambda ref: [ref[pl.ds(base, ROW_GROUP), sl] for sl in sls]
            r8, w8, k8, v8, a8, b8, dy8 = (ld(r_ref), ld(w_ref), ld(k_ref), ld(v_ref), ld(a_ref), ld(b_ref),
                                           ld(dy_ref))
            vbs = [_cols_of_rows_mxu(eye, ones, v8[q]) for q in range(N_BLK)]
            dycs = [_cols_of_rows_mxu(eye, ones, dy8[q]) for q in range(N_BLK)]
            acc = [{n_: jnp.zeros((ROW_GROUP, LANES), F32) for n_ in ("r", "w", "k", "v", "a", "b")}
                   for _ in range(N_BLK)]
            for i in reversed(range(ROW_GROUP)):
                row = lambda t8: t8[i:i + 1, :]
                put = lambda q_, n_, val: acc[q_].__setitem__(n_, jnp.where(sub8 == i, val, acc[q_][n_]))
                dks = []
                for q in range(N_BLK):
                    sp, vb = sv_ref[base + i, :, sls[q]], vbs[q][i]
                    wr, ar, br, kr, rr = row(w8[q]), row(a8[q]), row(b8[q]), row(k8[q]), row(r8[q])
                    sa = _seg_dot(lo, sp, ar)
                    st = sp * wr + sa * br + vb * kr
                    dyc = dycs[q][i]
                    ds = dstate[q] + dyc * rr
                    put(q, "r", _colsum(st * dyc))
                    put(q, "w", _colsum(ds * sp))
                    put(q, "b", _colsum(ds * sa))
                    put(q, "k", _colsum(ds * vb))
                    dsa = _seg_dot(lo, ds, br)
                    dks.append(ds * kr)
                    put(q, "a", _colsum(sp * dsa))
                    dstate[q] = ds * wr + dsa * ar
                dvc = _seg_sums_mxu(ones, dks)
                for q in range(N_BLK):
                    put(q, "v", _row_of_col(eye, dvc[q]))
            for q in range(N_BLK):
                for n_, ref in (("r", dr_ref), ("w", dw_ref), ("k", dk_ref), ("v", dv_ref), ("a", da_ref), ("b", db_ref)):
                    ref[pl.ds(base, ROW_GROUP), sls[q]] = acc[q][n_]
            return tuple(dstate)

        fin = lax.fori_loop(0, T // ROW_GROUP, bgroup,
                            tuple(ds_ref[:, q * LANES:(q + 1) * LANES] for q in range(N_BLK)))
        for q in range(N_BLK):
            ds_ref[:, q * LANES:(q + 1) * LANES] = fin[q]

    blk = pl.BlockSpec((T, R_WIDTH), lambda i: (nchunk - 1 - i, 0))
    svb = pl.BlockSpec((T, R_HEAD, R_WIDTH), lambda i: (nchunk - 1 - i, 0, 0))
    return pl.pallas_call(
        body, name="scan_bwd", grid=(nchunk,), in_specs=[blk] * 6 + [svb, blk],
        out_specs=[blk] * 6,
        out_shape=[jax.ShapeDtypeStruct((S, R_WIDTH), F32)] * 6,
        scratch_shapes=[pltpu.VMEM((R_HEAD, R_WIDTH), F32)],
        compiler_params=pltpu.CompilerParams(
            dimension_semantics=("arbitrary",), vmem_limit_bytes=VMEM_LIMIT),
    )(r, w, k, v, a, b, states, dy)


def _pad_in_cols(t, axis):
    cut = 2 * G_WIDTH + R_COLS
    lo, hi = lax.slice_in_dim(t, 0, cut, axis=axis), lax.slice_in_dim(t, cut, t.shape[axis], axis=axis)
    zshape = list(t.shape)
    zshape[axis] = LORA_PAD - LORA
    return jnp.concatenate([lo, jnp.zeros(zshape, t.dtype), hi], axis=axis)


def _unpad_in_cols(t, axis):
    cut = 2 * G_WIDTH + R_COLS
    return jnp.concatenate([lax.slice_in_dim(t, 0, cut, axis=axis),
                            lax.slice_in_dim(t, cut + LORA_PAD - LORA, t.shape[axis], axis=axis)], axis=axis)


def _pad_rows(t, lo, n):
    return jnp.zeros((n, t.shape[1]), t.dtype).at[lo:lo + t.shape[0]].set(t)


def _join_cols(w3):
    p, k, n = w3.shape
    return jnp.transpose(w3, (1, 0, 2)).reshape(k, p * n)


def _split_cols(w):
    k, n = w.shape
    return jnp.transpose(w.reshape(k, N_CHIPS, n // N_CHIPS), (1, 0, 2))


def _ada_dw(ccol, dmod):
    n = dmod.shape[1] // N_CHIPS
    tile = 256

    def body(c_ref, d_ref, o_ref):
        cc = c_ref[...]
        o_ref[0] = ((cc * _sigmoid(cc)) * d_ref[...]).astype(o_ref.dtype)

    return pl.pallas_call(
        body, name="ada_dw", grid=(N_CHIPS, D_MODEL // tile),
        in_specs=[pl.BlockSpec((tile, 1), lambda q, i: (i, 0)), pl.BlockSpec((1, n), lambda q, i: (0, q))],
        out_specs=pl.BlockSpec((1, tile, n), lambda q, i: (q, i, 0)),
        out_shape=jax.ShapeDtypeStruct((N_CHIPS, D_MODEL, n), WIRE_DT),
        compiler_params=pltpu.CompilerParams(dimension_semantics=("parallel", "parallel")),
    )(ccol, dmod)


def _shift_down(t):
    return jnp.concatenate([jnp.zeros((1, t.shape[1]), t.dtype), t[:-1]], axis=0)


def _shift_up(t):
    return jnp.concatenate([t[1:], jnp.zeros((1, t.shape[1]), t.dtype)], axis=0)


def _local_step(x, c, tgt, W):
    S = x.shape[0]
    bf = MXU_DT
    G = {}

    hl = np.arange(R_WIDTH) // R_HEAD
    bd = jnp.asarray(hl[:, None] == hl[None, :], jnp.bfloat16)
    gsel = jnp.asarray(np.arange(LANES)[:, None] == (np.arange(G_WIDTH) // (G_WIDTH // G_GROUPS))[None, :],
                       jnp.bfloat16)
    w_in_p = _pad_in_cols(_join_cols(W["w_in"]), 1)
    b_in_p = _pad_in_cols(W["b_in"], 1)
    c_g, c_r = 2 * G_WIDTH, 2 * G_WIDTH + R_COLS_PAD
    w_g, w_r, w_gate = w_in_p[:, :c_g], w_in_p[:, c_g:c_r], w_in_p[:, c_r:]
    b_g, b_r, b_gate = b_in_p[:, :c_g], b_in_p[:, c_g:c_r], b_in_p[:, c_r:]
    mu_p = jnp.concatenate([W["mu_shift"], jnp.zeros((1, LORA_PAD - LORA), F32)], axis=1)
    wd_p = _pad_rows(W["w_decay_up"].astype(F32), 0, LORA_PAD)
    wa_p = _pad_rows(W["w_aaa_up"].astype(F32), DECAY_LORA, LORA_PAD)
    wg_p = _pad_rows(W["w_gate_up"].astype(F32), DECAY_LORA + AAA_LORA, LORA_PAD)
    ws2 = W["w_spatial"].reshape(G_GROUPS * CHUNK, CHUNK)
    bst = jnp.zeros((CHUNK, LANES), F32).at[:, :G_GROUPS].set(W["b_spatial"].T)
    rk = W["r_k"].reshape(1, R_WIDTH)

    c8 = jnp.broadcast_to(c, (8, D_MODEL))
    (ca8,) = _rows("ada_silu", lambda cc: cc * _sigmoid(cc), [c8], [], [(D_MODEL, bf)], tile=8)
    mod_raw = _mm("ada_mm", ca8, W["w_ada"], "nn")
    (mod8,) = _rows("ada_bias", lambda m, bb: m + bb, [mod_raw], [W["b_ada"]], [(6 * D_MODEL, F32)], tile=8)
    sh1, sc1, gt1, sh2, sc2, gt2 = [mod8[0:1, i * D_MODEL:(i + 1) * D_MODEL] for i in range(6)]

    (h,) = _rows("mod1", _modulate, [x], [sc1, sh1], [(D_MODEL, bf)])
    proj_g = _mm("proj_g", h, w_g, "nn")
    proj_r = _mm("proj_r", h, w_r, "nn")
    proj_gate = _mm("proj_gate", h, w_gate, "nn")

    def split2(t):
        return t[:, :G_WIDTH], t[:, G_WIDTH:]

    def gmlp_fwd(z, bz, g, b, bst_, gsel_, ws_):
        gmask, tril = _gmlp_consts()
        (zu, zv), (bu, bv) = split2(z), split2(bz)
        wsl = [ws_[i * CHUNK:(i + 1) * CHUNK] for i in range(G_GROUPS)]
        return _gmlp_core(gmask, tril, gsel_, zu, zv, bu, bv, g, b, bst_, *wsl)

    (y_a,) = _rows("gmlp_fwd", gmlp_fwd, [proj_g], [b_g, W["g_ln_v"], W["b_ln_v"], bst, gsel, ws2],
                   [(G_WIDTH, bf)], tile=CHUNK)

    r_cuts = (0, R_WIDTH, 2 * R_WIDTH, 3 * R_WIDTH, R_COLS_PAD)

    def split4(t):
        return [t[:, r_cuts[i]:r_cuts[i + 1]] for i in range(4)]

    tile_pre = min(256, S)

    def rowmask_of():
        grow = pl.program_id(0) * tile_pre + lax.broadcasted_iota(jnp.int32, (tile_pre, 1), 0)
        return (grow > 0).astype(F32)

    pre_params = [b_r, mu_p, W["w0"], wd_p, W["a0"], wa_p, wg_p, W["k_k"], W["k_a"], bd]

    def pre_fwd(z, p, bz, mu, w0, wd, a0, wa, wg, k_k, k_a, bd_):
        return _pre_core(rowmask_of(), bd_, *split4(z), *split4(p), *split4(bz), *split4(mu),
                         w0, wd, a0, wa, wg, k_k, k_a)

    proj_r_prev = _shift_down(proj_r)
    s_r, s_w, s_k, s_v, s_a, s_b, s_g = _rows(
        "rwkv_pre_fwd", pre_fwd, [proj_r, proj_r_prev], pre_params, [(R_WIDTH, F32)] * 7, tile=tile_pre)
    y_scan, states = _scan_fwd(s_r, s_w, s_k, s_v, s_a, s_b)

    def post_fwd(y, r, k2, v, g, gain, bias, rk_, bd_):
        return _post_core(bd_, y, r, k2, v, g, gain, bias, rk_)

    post_params = [W["gn_gain"], W["gn_bias"], rk, bd]
    (y_b,) = _rows("rwkv_post_fwd", post_fwd, [y_scan, s_r, s_k, s_v, s_g], post_params, [(R_WIDTH, bf)])
    p_a = _mm("branch_a", y_a, W["w_branch_a"], "nn")
    p_b = _mm("branch_b", y_b, W["w_branch_b"], "nn")

    def merge_fwd(pa, pb, gz, bgz):
        return _merge_core(pa, pb, gz[:, :D_MODEL], gz[:, D_MODEL:], bgz[:, :D_MODEL], bgz[:, D_MODEL:])

    (merged,) = _rows("merge_fwd", merge_fwd, [p_a, p_b, proj_gate], [b_gate], [(D_MODEL, bf)])
    mix = _mm("out_proj", merged, W["w_out"], "nn")
    ln1_params = [gt1, W["b_out"], W["ln1_g"], W["ln1_b"], sc2, sh2]
    h1, h2 = _rows("ln1_fwd", _ln1_core, [x, mix], ln1_params, [(D_MODEL, F32), (D_MODEL, bf)])

    a1 = _mm("ff1", h2, W["w_ff1"], "nn")
    (act,) = _rows("ff_act", lambda z, bb: jnp.square(jnp.maximum(z + bb, 0.0)), [a1], [W["b_ff1"]], [(D_FF, bf)])
    ff = _mm("ff2", act, W["w_ff2"], "nn")

    def ln2_loss(h1_, ff_, tg, gt2_, bff2, g, b):
        loss, vjp = jax.vjp(functools.partial(_ln2_loss_core, tg), h1_, ff_, gt2_, bff2, g, b)
        return vjp(jnp.ones((), F32)) + (loss,)

    ln2_params = [gt2, W["b_ff2"], W["ln2_g"], W["ln2_b"]]
    dh1, dff, dgt2, G["b_ff2"], G["ln2_g"], G["ln2_b"], loss_acc = _rows(
        "ln2_loss", ln2_loss, [h1, ff, tgt], ln2_params, [(D_MODEL, F32), (D_MODEL, bf)],
        accs=[(1, D_MODEL)] * 4 + [(1, LANES)])
    loss = loss_acc[0, 0]

    dact = _mm("ff2_dx", dff, W["w_ff2"], "nt")
    G["w_ff2"] = _mm("ff2_dw", act, dff, "tn", WIRE_DT).reshape(N_CHIPS, D_FF // N_CHIPS, D_MODEL)

    def act_bwd(z, da, bb):
        d = da * 2.0 * jnp.maximum(z + bb, 0.0)
        return d, _colsum(d)

    da1, G["b_ff1"] = _rows("ff_act_bwd", act_bwd, [a1, dact], [W["b_ff1"]], [(D_FF, bf)], accs=[(1, D_FF)])
    dh2 = _mm("ff1_dx", da1, W["w_ff1"], "nt")
    G["w_ff1"] = _mm("ff1_dw", h2, da1, "tn", WIRE_DT, out_split=N_CHIPS)

    def ln1_bwd(x_, mix_, dh1_, dh2_, *ps):
        _, vjp = jax.vjp(_ln1_core, x_, mix_, *ps)
        return vjp((dh1_, dh2_))

    dx_res, dmix, dgt1, G["b_out"], G["ln1_g"], G["ln1_b"], dsc2, dsh2 = _rows(
        "ln1_bwd", ln1_bwd, [x, mix, dh1, dh2], ln1_params, [(D_MODEL, F32), (D_MODEL, bf)],
        accs=[(1, D_MODEL)] * 6)

    dmerged = _mm("out_proj_dx", dmix, W["w_out"], "nt")
    G["w_out"] = _mm("out_proj_dw", merged, dmix, "tn", WIRE_DT).reshape(N_CHIPS, D_MODEL // N_CHIPS, D_MODEL)

    def merge_bwd(pa, pb, gz, dm, bgz):
        args = (pa.astype(F32), pb.astype(F32), gz[:, :D_MODEL], gz[:, D_MODEL:], bgz[:, :D_MODEL], bgz[:, D_MODEL:])
        _, vjp = jax.vjp(_merge_core, *args)
        dpa, dpb, dga, dgb, dbga, dbgb = vjp(dm)
        return dpa, dpb, jnp.concatenate([dga, dgb], axis=1), jnp.concatenate([dbga, dbgb], axis=1)

    dp_a, dp_b, dgates, db_gate = _rows(
        "merge_bwd", merge_bwd, [p_a, p_b, proj_gate, dmerged], [b_gate],
        [(D_MODEL, bf), (D_MODEL, bf), (2 * D_MODEL, F32)], accs=[(1, 2 * D_MODEL)])
    dy_a = _mm("branch_a_dx", dp_a, W["w_branch_a"], "nt")
    G["w_branch_a"] = _mm("branch_a_dw", y_a, dp_a, "tn", WIRE_DT, out_split=N_CHIPS)
    dy_b = _mm("branch_b_dx", dp_b, W["w_branch_b"], "nt")
    G["w_branch_b"] = _mm("branch_b_dw", y_b, dp_b, "tn", WIRE_DT, out_split=N_CHIPS)

    def post_bwd(y, r, k2, v, g, dyb, gain, bias, rk_, bd_):
        _, vjp = jax.vjp(functools.partial(_post_core, bd_), y, r, k2, v, g, gain, bias, rk_)
        return vjp(dyb)

    dy_scan, dr_p, dk_p, dv_p, dg_p, G["gn_gain"], G["gn_bias"], drk = _rows(
        "rwkv_post_bwd", post_bwd, [y_scan, s_r, s_k, s_v, s_g, dy_b], post_params,
        [(R_WIDTH, F32)] * 5, accs=[(1, R_WIDTH)] * 3)
    G["r_k"] = drk.reshape(R_HEADS, R_HEAD)
    dr_s, dw_s, dk_s, dv_s, da_s, db_s = _scan_bwd(s_r, s_w, s_k, s_v, s_a, s_b, states, dy_scan)

    def pre_bwd(z, p, dr1, dr2, dw, dk1, dk2, dv1, dv2, da, db, dg,
                bz, mu, w0, wd, a0, wa, wg, k_k, k_a, bd_):
        prim = (*split4(z), *split4(p), *split4(bz), *split4(mu), w0, wd, a0, wa, wg, k_k, k_a)
        _, vjp = jax.vjp(functools.partial(_pre_core, rowmask_of(), bd_), *prim)
        d = vjp((dr1 + dr2, dw, dk1 + dk2, dv1 + dv2, da, db, dg))
        cat = lambda parts: jnp.concatenate(parts, axis=1)
        return (cat(d[0:4]), cat(d[4:8]), cat(d[8:12]), cat(d[12:16])) + tuple(d[16:])

    dz_r, dprev, db_r, dmu_p, G["w0"], dwd_p, G["a0"], dwa_p, dwg_p, G["k_k"], G["k_a"] = _rows(
        "rwkv_pre_bwd", pre_bwd,
        [proj_r, proj_r_prev, dr_s, dr_p, dw_s, dk_s, dk_p, dv_s, dv_p, da_s, db_s, dg_p],
        pre_params, [(R_COLS_PAD, F32)] * 2,
        accs=[(1, R_COLS_PAD), (1, R_COLS_PAD), (1, R_WIDTH), (LORA_PAD, R_WIDTH), (1, R_WIDTH),
              (LORA_PAD, R_WIDTH), (LORA_PAD, R_WIDTH), (1, R_WIDTH), (1, R_WIDTH)],
        tile=tile_pre)
    G["mu_shift"] = dmu_p[:, :R_COLS]
    G["w_decay_up"] = dwd_p[:DECAY_LORA]
    G["w_aaa_up"] = dwa_p[DECAY_LORA:DECAY_LORA + AAA_LORA]
    G["w_gate_up"] = dwg_p[DECAY_LORA + AAA_LORA:LORA]

    def gmlp_bwd(z, dya, bz, g, b, bst_, gsel_, ws_):
        gmask, tril = _gmlp_consts()
        (zu, zv), (bu, bv) = split2(z), split2(bz)
        wsl = [ws_[i * CHUNK:(i + 1) * CHUNK] for i in range(G_GROUPS)]
        _, vjp = jax.vjp(functools.partial(_gmlp_core, gmask, tril, gsel_), zu, zv, bu, bv, g, b, bst_, *wsl)
        d = vjp(dya)
        return (jnp.concatenate(d[0:2], axis=1), jnp.concatenate(d[2:4], axis=1), d[4], d[5], d[6],
                jnp.concatenate(d[7:], axis=0))

    dz_g, db_g, G["g_ln_v"], G["b_ln_v"], dbst, dws2 = _rows(
        "gmlp_bwd", gmlp_bwd, [proj_g, dy_a], [b_g, W["g_ln_v"], W["b_ln_v"], bst, gsel, ws2],
        [(2 * G_WIDTH, F32)],
        accs=[(1, 2 * G_WIDTH), (1, G_WIDTH), (1, G_WIDTH), (CHUNK, LANES), (G_GROUPS * CHUNK, CHUNK)],
        tile=CHUNK)
    G["w_spatial"] = dws2.reshape(G_GROUPS, CHUNK, CHUNK)
    G["b_spatial"] = dbst[:, :G_GROUPS].T

    def dproj_cat(dzg, dzr, dpv, dgz):
        return jnp.concatenate([dzg, dzr + dpv, dgz], axis=1)

    (dproj,) = _rows("dproj_cat", dproj_cat, [dz_g, dz_r, _shift_up(dprev), dgates], [],
                     [(2 * G_WIDTH + R_COLS_PAD + 2 * D_MODEL, bf)])
    dh = _mm("proj_dx", dproj, w_in_p, "nt", tk=2432)
    G["w_in"] = _split_cols(_unpad_in_cols(_mm("proj_dw", h, dproj, "tn", WIRE_DT, tm=512, tn=2432), 1))
    G["b_in"] = _unpad_in_cols(jnp.concatenate([db_g, db_r, db_gate], axis=1), 1)

    def mod1_bwd(x_, dh_, dxr, sc):
        return dh_ * (1.0 + sc) + dxr, _colsum(dh_ * x_), _colsum(dh_)

    grad_x, dsc1, dsh1 = _rows("mod1_bwd", mod1_bwd, [x, dh, dx_res], [sc1], [(D_MODEL, F32)],
                               accs=[(1, D_MODEL)] * 2)

    dmod = jnp.concatenate([dsh1, dsc1, dgt1, dsh2, dsc2, dgt2], axis=1)
    G["b_ada"] = dmod
    G["w_ada"] = _ada_dw(c.reshape(D_MODEL, 1), dmod)
    return loss, grad_x, G


BIG = (("w_ada", (D_MODEL, 6 * D_MODEL), 1), ("w_in", (D_MODEL, 2 * G_WIDTH + R_COLS + 2 * D_MODEL), 1),
       ("w_branch_a", (G_WIDTH, D_MODEL), 1), ("w_branch_b", (R_WIDTH, D_MODEL), 1),
       ("w_out", (D_MODEL, D_MODEL), 0), ("w_ff1", (D_MODEL, D_FF), 1), ("w_ff2", (D_FF, D_MODEL), 0))
LORAS = (("w_decay_up", (DECAY_LORA, R_WIDTH), 1), ("w_aaa_up", (AAA_LORA, R_WIDTH), 1),
         ("w_gate_up", (GATE_LORA, R_WIDTH), 1))
SHARDED = BIG + LORAS
SMALL = (("b_ada", (1, 6 * D_MODEL)), ("b_in", (1, 2 * G_WIDTH + R_COLS + 2 * D_MODEL)),
         ("g_ln_v", (1, G_WIDTH)), ("b_ln_v", (1, G_WIDTH)), ("w_spatial", (G_GROUPS, CHUNK, CHUNK)),
         ("b_spatial", (G_GROUPS, CHUNK)), ("mu_shift", (1, R_COLS)), ("w0", (1, R_WIDTH)),
         ("a0", (1, R_WIDTH)), ("k_k", (1, R_WIDTH)), ("k_a", (1, R_WIDTH)), ("r_k", (R_HEADS, R_HEAD)),
         ("gn_gain", (1, R_WIDTH)), ("gn_bias", (1, R_WIDTH)), ("b_out", (1, D_MODEL)),
         ("ln1_g", (1, D_MODEL)), ("ln1_b", (1, D_MODEL)), ("b_ff1", (1, D_FF)), ("b_ff2", (1, D_MODEL)),
         ("ln2_g", (1, D_MODEL)), ("ln2_b", (1, D_MODEL)))
WEIGHT_ORDER = ("w_ada", "b_ada", "w_in", "b_in", "g_ln_v", "b_ln_v", "w_spatial", "b_spatial", "mu_shift",
                "w0", "w_decay_up", "a0", "w_aaa_up", "w_gate_up", "k_k", "k_a", "r_k", "gn_gain", "gn_bias",
                "w_branch_a", "w_branch_b", "w_out", "b_out", "ln1_g", "ln1_b", "w_ff1", "b_ff1", "w_ff2",
                "b_ff2", "ln2_g", "ln2_b")
N_CHIPS = 4


def _shard_shape(shape, axis):
    s = list(shape)
    s[axis] //= N_CHIPS
    return tuple(s)


def _numel(shape):
    return int(np.prod(shape))


def _round_up(n, q):
    return -(-n // q) * q


N_LORA = sum(_numel(_shard_shape(s, a)) for _, s, a in LORAS)
N_SMALL = sum(_numel(s) for _, s in SMALL)
ROWS_SW = _round_up(N_LORA, PACK_Q) // PACK_W
ROWS_SG = _round_up(N_LORA + N_SMALL, PACK_Q) // PACK_W


def _pack_small(loras, small, rows, dtype):
    parts = [loras[n] for n, _, _ in LORAS] + ([small[n] for n, _ in SMALL] if small is not None else [])
    flat = jnp.concatenate([p.reshape(-1).astype(dtype) for p in parts])
    flat = jnp.concatenate([flat, jnp.zeros((rows * PACK_W - flat.shape[0],), dtype)])
    return flat.reshape(rows, PACK_W)


def _unpack_small(pack, with_small):
    flat = pack.reshape(-1)
    out, off = {}, 0
    for n, s, a in LORAS:
        ss = _shard_shape(s, a)
        out[n] = flat[off:off + _numel(ss)].reshape(ss)
        off += _numel(ss)
    if with_small:
        for n, s in SMALL:
            out[n] = flat[off:off + _numel(s)].reshape(s)
            off += _numel(s)
    return out


def _pack_small_grads(G):
    segs = []
    for q in range(N_CHIPS):
        loras = {n: G[n][:, q * (s[1] // N_CHIPS):(q + 1) * (s[1] // N_CHIPS)] for n, s, _ in LORAS}
        segs.append(_pack_small(loras, G, ROWS_SG, F32))
    return jnp.stack(segs)


ANY = pl.BlockSpec(memory_space=pl.ANY)
MESH = pl.DeviceIdType.MESH


def _place():
    x, y, c = lax.axis_index("x"), lax.axis_index("y"), lax.axis_index("c")
    chips = [(1 - x, y), (x, 1 - y), (1 - x, 1 - y)]
    return x, y, c, chips


def _remote(src, dst, send_sem, recv_sem, to):
    return pltpu.make_async_remote_copy(src_ref=src, dst_ref=dst, send_sem=send_sem, recv_sem=recv_sem,
                                        device_id=to, device_id_type=MESH)


def _all_gather(shards):
    n = len(shards)
    halves = [a.shape[0] // 2 for a in shards]

    def body(*refs):
        ins, outs = refs[:n], refs[n:2 * n]
        send_sems, recv_sems = refs[2 * n:]
        x, y, c, chips = _place()
        s = 2 * x + y
        sibling = (x, y, 1 - c)
        slot = [2 * chip[0] + chip[1] for chip in chips]
        first, passed = [], []
        for a in range(n):
            H = halves[a]
            mine = ins[a].at[pl.ds(c * H, H)]
            first += [_remote(mine, outs[a].at[2 * s + c], send_sems.at[6 * a + j], recv_sems.at[6 * a + j], (*chip, c))
                      for j, chip in enumerate(chips)]
        for cp in first:
            cp.start()
        for a in range(n):
            for j in range(3):
                landed = outs[a].at[2 * slot[j] + c]
                _remote(landed, landed, send_sems.at[6 * a + j], recv_sems.at[6 * a + j], sibling).wait_recv()
                cp = _remote(landed, landed, send_sems.at[6 * a + 3 + j], recv_sems.at[6 * a + 3 + j], sibling)
                cp.start()
                passed.append(cp)
        for a in range(n):
            for j in range(3):
                got = outs[a].at[2 * slot[j] + 1 - c]
                _remote(got, got, send_sems.at[6 * a + 3 + j], recv_sems.at[6 * a + 3 + j], sibling).wait_recv()
        for cp in first + passed:
            cp.wait_send()

    return pl.pallas_call(
        body, name="ag_weights", in_specs=[ANY] * n, out_specs=[ANY] * n,
        out_shape=[jax.ShapeDtypeStruct((2 * N_CHIPS, h, a.shape[1]), a.dtype) for a, h in zip(shards, halves)],
        scratch_shapes=[pltpu.SemaphoreType.DMA((6 * n,)), pltpu.SemaphoreType.DMA((6 * n,))],
    )(*shards)


def _rs_sibling_in(gps):
    n = len(gps)

    def body(*refs):
        ins, outs = refs[:n], refs[n:2 * n]
        send_sems, recv_sems = refs[2 * n:]
        x, y, c, _ = _place()
        cps = []
        for a in range(n):
            H = gps[a].shape[1] // 2
            cps += [_remote(ins[a].at[q, pl.ds((1 - c) * H, H)], outs[a].at[q], send_sems.at[N_CHIPS * a + q],
                            recv_sems.at[N_CHIPS * a + q], (x, y, 1 - c)) for q in range(N_CHIPS)]
        for cp in cps:
            cp.start()
        for cp in cps:
            cp.wait()

    return pl.pallas_call(
        body, name="rs_sibling_in", in_specs=[ANY] * n, out_specs=[ANY] * n,
        out_shape=[jax.ShapeDtypeStruct((N_CHIPS, g.shape[1] // 2, g.shape[2]), g.dtype) for g in gps],
        scratch_shapes=[pltpu.SemaphoreType.DMA((N_CHIPS * n,)), pltpu.SemaphoreType.DMA((N_CHIPS * n,))],
    )(*gps)


def _rs_add_own(name, gp, got, c_arr, tr=256):
    H, C = got.shape[1:]
    tr = _pick(H, tr, 8)
    nb = H // tr

    def body(c_ref, g_ref, r_ref, o_ref):
        o_ref[...] = (g_ref[...].astype(F32) + r_ref[...].astype(F32)).astype(o_ref.dtype)

    return pl.pallas_call(
        body, name="rs_add_own_" + name,
        grid_spec=pltpu.PrefetchScalarGridSpec(
            num_scalar_prefetch=1, grid=(N_CHIPS, nb),
            in_specs=[pl.BlockSpec((1, tr, C), lambda q, i, c_ref: (q, c_ref[0] * nb + i, 0)),
                      pl.BlockSpec((1, tr, C), lambda q, i, c_ref: (q, i, 0))],
            out_specs=pl.BlockSpec((1, tr, C), lambda q, i, c_ref: (q, i, 0))),
        out_shape=jax.ShapeDtypeStruct((N_CHIPS, H, C), gp.dtype),
        compiler_params=pltpu.CompilerParams(dimension_semantics=("arbitrary", "arbitrary")),
    )(c_arr, gp, got)


def _rs_chips(parts):
    n = len(parts)

    def body(*refs):
        ins, outs = refs[:n], refs[n:2 * n]
        send_sems, recv_sems = refs[2 * n:]
        x, y, c, chips = _place()
        s = 2 * x + y
        slot = [2 * chip[0] + chip[1] for chip in chips]
        cps = [_remote(ins[a].at[slot[j]], outs[a].at[s], send_sems.at[3 * a + j], recv_sems.at[3 * a + j], (*chips[j], c))
               for a in range(n) for j in range(3)]
        for cp in cps:
            cp.start()
        for a in range(n):
            for j in range(3):
                cps[3 * a + j].wait_send()
                got = outs[a].at[slot[j]]
                _remote(got, got, send_sems.at[3 * a + j], recv_sems.at[3 * a + j], (*chips[j], c)).wait_recv()

    return pl.pallas_call(
        body, name="rs_chips", in_specs=[ANY] * n, out_specs=[ANY] * n,
        out_shape=[jax.ShapeDtypeStruct(p.shape, p.dtype) for p in parts],
        scratch_shapes=[pltpu.SemaphoreType.DMA((3 * n,)), pltpu.SemaphoreType.DMA((3 * n,))],
    )(*parts)


def _rs_add_chips(name, part, slots, sc_arr, tr=128):
    H, C = slots.shape[1:]
    tr = _pick(H, tr, 8)
    nb = H // tr

    def body(sc_ref, p_ref, s_ref, o_ref):
        acc = None
        for q in range(N_CHIPS):
            term = lax.cond(sc_ref[0] == q, lambda: p_ref[0].astype(F32), lambda q=q: s_ref[q].astype(F32))
            acc = term if acc is None else acc + term
        o_ref[...] = acc

    return pl.pallas_call(
        body, name="rs_add_chips_" + name,
        grid_spec=pltpu.PrefetchScalarGridSpec(
            num_scalar_prefetch=1, grid=(nb,),
            in_specs=[pl.BlockSpec((1, tr, C), lambda i, sc: (sc[0], i, 0)),
                      pl.BlockSpec((N_CHIPS, tr, C), lambda i, sc: (0, i, 0))],
            out_specs=pl.BlockSpec((tr, C), lambda i, sc: (sc[1] * nb + i, 0))),
        out_shape=jax.ShapeDtypeStruct((2 * H, C), F32),
        compiler_params=pltpu.CompilerParams(dimension_semantics=("arbitrary",)),
    )(sc_arr, part, slots)


def _rs_sibling_out(wholes):
    n = len(wholes)

    def body(*refs):
        ins, outs = refs[:n], refs[n:2 * n]
        send_sems, recv_sems = refs[2 * n:]
        x, y, c, _ = _place()
        cps = []
        for a in range(n):
            H = wholes[a].shape[0] // 2
            cps.append(_remote(ins[a].at[pl.ds(c * H, H)], outs[a].at[pl.ds(c * H, H)], send_sems.at[a], recv_sems.at[a],
                               (x, y, 1 - c)))
        for cp in cps:
            cp.start()
        for a in range(n):
            H = wholes[a].shape[0] // 2
            cps[a].wait_send()
            got = outs[a].at[pl.ds((1 - c) * H, H)]
            _remote(got, got, send_sems.at[a], recv_sems.at[a], (x, y, 1 - c)).wait_recv()

    return pl.pallas_call(
        body, name="rs_sibling_out", in_specs=[ANY] * n, out_specs=[ANY] * n,
        out_shape=[jax.ShapeDtypeStruct(w.shape, w.dtype) for w in wholes],
        input_output_aliases={a: a for a in range(n)},
        scratch_shapes=[pltpu.SemaphoreType.DMA((n,)), pltpu.SemaphoreType.DMA((n,))],
    )(*wholes)


def _adamw(name, w, g, m, v):
    def fn(w_, g_, m_, v_):
        m2 = ADAM_B1 * m_ + (1.0 - ADAM_B1) * g_
        v2 = ADAM_B2 * v_ + (1.0 - ADAM_B2) * (g_ * g_)
        m_hat = m2 / (1.0 - ADAM_B1 ** ADAM_STEP)
        v_hat = v2 / (1.0 - ADAM_B2 ** ADAM_STEP)
        return -ADAM_LR * (m_hat / (jnp.sqrt(v_hat) + ADAM_EPS) + ADAM_WD * w_), m2, v2

    return _rows("adamw_" + name, fn, [w, g, m, v], [], [(w.shape[1], F32)] * 3, tile=_pick(w.shape[0], 256, 8))


def kernel(x, c, w_ada, b_ada, w_in, b_in, g_ln_v, b_ln_v, w_spatial, b_spatial, mu_shift, w0, w_decay_up, a0, w_aaa_up, w_gate_up, k_k, k_a, r_k, gn_gain, gn_bias, w_branch_a, w_branch_b, w_out, b_out, ln1_g, ln1_b, w_ff1, b_ff1, w_ff2, b_ff2, ln2_g, ln2_b, loss_target, m_w_ada, m_b_ada, m_w_in, m_b_in, m_g_ln_v, m_b_ln_v, m_w_spatial, m_b_spatial, m_mu_shift, m_w0, m_w_decay_up, m_a0, m_w_aaa_up, m_w_gate_up, m_k_k, m_k_a, m_r_k, m_gn_gain, m_gn_bias, m_w_branch_a, m_w_branch_b, m_w_out, m_b_out, m_ln1_g, m_ln1_b, m_w_ff1, m_b_ff1, m_w_ff2, m_b_ff2, m_ln2_g, m_ln2_b, v_w_ada, v_b_ada, v_w_in, v_b_in, v_g_ln_v, v_b_ln_v, v_w_spatial, v_b_spatial, v_mu_shift, v_w0, v_w_decay_up, v_a0, v_w_aaa_up, v_w_gate_up, v_k_k, v_k_a, v_r_k, v_gn_gain, v_gn_bias, v_w_branch_a, v_w_branch_b, v_w_out, v_b_out, v_ln1_g, v_ln1_b, v_w_ff1, v_b_ff1, v_w_ff2, v_b_ff2, v_ln2_g, v_ln2_b):
    args = dict(locals())
    local_shape = {n: _shard_shape(s, a) for n, s, a in SHARDED}
    local_shape.update(dict(SMALL))
    wts = {n: args[n].reshape(local_shape[n]) for n in WEIGHT_ORDER}
    mom = {n: args["m_" + n].reshape(local_shape[n]) for n in WEIGHT_ORDER}
    var = {n: args["v_" + n].reshape(local_shape[n]) for n in WEIGHT_ORDER}
    big = [n for n, _, _ in BIG]

    chip = 2 * lax.axis_index("x") + lax.axis_index("y")
    mine = [wts[n].astype(MXU_DT) for n in big] + [_pack_small(wts, None, ROWS_SW, MXU_DT)]
    gath = [lax.dynamic_update_slice(g.reshape((N_CHIPS,) + m.shape), m[None], (chip, 0, 0))
            for g, m in zip(_all_gather(mine), mine)]
    W = {n: wts[n] for n, _ in SMALL}
    for (n, s, a), g in zip(BIG, gath):
        W[n] = g if a == 1 else g.reshape(s)
    lora_q = [_unpack_small(gath[-1][q], False) for q in range(N_CHIPS)]
    for n, _, _ in LORAS:
        W[n] = jnp.concatenate([lora_q[q][n] for q in range(N_CHIPS)], axis=1)

    loss, grad_x, G = _local_step(x[0], c, loss_target[0], W)
    loss = lax.psum(loss, MESH_AXES)

    names = big + ["small"]
    gps = [G[n] for n in big] + [_pack_small_grads(G)]
    c_arr = lax.axis_index("c").astype(jnp.int32).reshape(1)
    parts = [_rs_add_own(n, g, r, c_arr) for n, g, r in zip(names, gps, _rs_sibling_in(gps))]
    sc_arr = jnp.stack([chip, lax.axis_index("c")]).astype(jnp.int32)
    segs = _rs_sibling_out([_rs_add_chips(n, p, s, sc_arr) for n, p, s in zip(names, parts, _rs_chips(parts))])

    out = {}
    for n, g in zip(big, segs[:-1]):
        out[n] = (g,) + tuple(_adamw(n, wts[n], g, mom[n], var[n]))
    packs = [_pack_small(t, t, ROWS_SG, F32) for t in (wts, mom, var)]
    small4 = [_unpack_small(t, True) for t in (segs[-1],) + tuple(_adamw("small", packs[0], segs[-1], packs[1], packs[2]))]
    res = [loss, grad_x[None]]
    for k in range(4):
        res += [(out[n][k] if n in out else small4[k][n]).reshape(args[n].shape) for n in WEIGHT_ORDER]
    return tuple(res)
```

```python
import functools

import numpy as np
import jax
import jax.numpy as jnp
from jax import lax
from jax.experimental import pallas as pl
from jax.experimental.pallas import tpu as pltpu

F32 = jnp.float32
MXU_DT = jnp.bfloat16
WIRE_DT = jnp.bfloat16

D_MODEL = 1024
G_GROUPS = 8
G_WIDTH = 512
CHUNK = 128
R_WIDTH = 512
R_HEAD = 64
R_HEADS = 8
DECAY_LORA = 32
AAA_LORA = 32
GATE_LORA = 96
LORA = DECAY_LORA + AAA_LORA + GATE_LORA
LORA_PAD = 256
R_COLS = 3 * R_WIDTH + LORA
R_COLS_PAD = 3 * R_WIDTH + LORA_PAD
D_FF = 4 * D_MODEL
ALPHA = 2.0 ** 0.25
LN_EPS = 1e-5
GN_EPS = 64e-5
ADAM_LR = 0.001
ADAM_B1 = 0.9
ADAM_B2 = 0.999
ADAM_EPS = 1e-08
ADAM_WD = 0.01
ADAM_STEP = 10

LANES = 128
PACK_W = 512
PACK_Q = 2 * 16 * PACK_W
VMEM_LIMIT = 48 * 1024 * 1024
SCAN_T = 64

MESH_AXES = ("x", "y", "c")


def _dg(a, b, dims):
    return lax.dot_general(a.astype(MXU_DT), b.astype(MXU_DT), (dims, ((), ())),
                           preferred_element_type=F32)


@jax.custom_vjp
def _bdot(a, b):
    return _dg(a, b, ((1,), (0,)))


def _bdot_fwd(a, b):
    return _bdot(a, b), (a, b)


def _bdot_bwd(res, g):
    a, b = res
    return (_dg(g, b, ((1,), (1,))).astype(a.dtype), _dg(a, g, ((0,), (0,))).astype(b.dtype))


_bdot.defvjp(_bdot_fwd, _bdot_bwd)


def _split_dot(x, m, dims):
    hi = x.astype(jnp.bfloat16)
    lo = (x - hi.astype(F32)).astype(jnp.bfloat16)
    dn = (dims, ((), ()))
    return (lax.dot_general(hi, m, dn, preferred_element_type=F32)
            + lax.dot_general(lo, m, dn, preferred_element_type=F32))


@jax.custom_vjp
def _pdot(x, m):
    return _split_dot(x, m, ((1,), (0,)))


def _pdot_fwd(x, m):
    return _pdot(x, m), m


def _pdot_bwd(m, g):
    return _split_dot(g, m, ((1,), (1,))), None


_pdot.defvjp(_pdot_fwd, _pdot_bwd)


def _sigmoid(x):
    return 1.0 / (1.0 + jnp.exp(-x))


def _softplus(x):
    return jnp.maximum(x, 0.0) + jnp.log(1.0 + jnp.exp(-jnp.maximum(x, -x)))


def _gelu(x):
    return 0.5 * x * (1.0 + jnp.tanh(0.7978845608028654 * (x + 0.044715 * (x * x * x))))


def _ln(x, g, b, eps):
    mu = jnp.mean(x, axis=-1, keepdims=True)
    xc = x - mu
    var = jnp.mean(xc * xc, axis=-1, keepdims=True)
    return xc * lax.rsqrt(var + eps) * g + b


def _colsum(x):
    return jnp.sum(x, axis=0, keepdims=True)


def _pick(n, target, q=LANES):
    if n <= target:
        return n
    best = None
    for t in range(q, target + 1, q):
        if n % t == 0:
            best = t
    assert best is not None, (n, target)
    return best


def _mm(name, a, b, mode, out_dtype=F32, out_split=1, tm=1024, tn=1024, tk=1024):
    bs = b.shape[0] if b.ndim == 3 else 1
    br, bc = b.shape[-2:]
    if mode == "nn":
        (M, K), K2, N = a.shape, br, bc * bs
    elif mode == "nt":
        (M, K), N, K2 = a.shape, br, bc * bs
    else:
        assert bs == 1
        (K, M), K2, N = a.shape, br, bc
    assert K == K2, (name, a.shape, b.shape, mode)
    n_piece = N // max(bs if mode == "nn" else 1, out_split)
    k_piece = K // (bs if mode == "nt" else 1)
    tm, tn, tk = _pick(M, tm, 8 if M < LANES else LANES), _pick(n_piece, tn), _pick(k_piece, tk)
    nk, npj, npk = K // tk, n_piece // tn, k_piece // tk
    dims = {"nn": ((1,), (0,)), "nt": ((1,), (1,)), "tn": ((0,), (0,))}[mode]

    def body(a_ref, b_ref, o_ref, acc_ref):
        k = pl.program_id(2)

        @pl.when(k == 0)
        def _():
            acc_ref[...] = jnp.zeros(acc_ref.shape, F32)

        acc_ref[...] += _dg(a_ref[...], b_ref[0] if bs > 1 else b_ref[...], dims)

        @pl.when(k == nk - 1)
        def _():
            if out_split > 1:
                o_ref[0] = acc_ref[...].astype(o_ref.dtype)
            else:
                o_ref[...] = acc_ref[...].astype(o_ref.dtype)

    if mode == "nn":
        a_spec = pl.BlockSpec((tm, tk), lambda i, j, k: (i, k))
        b_spec = (pl.BlockSpec((tk, tn), lambda i, j, k: (k, j)) if bs == 1 else
                  pl.BlockSpec((1, tk, tn), lambda i, j, k: (j // npj, k, j % npj)))
    elif mode == "nt":
        a_spec = pl.BlockSpec((tm, tk), lambda i, j, k: (i, k))
        b_spec = (pl.BlockSpec((tn, tk), lambda i, j, k: (j, k)) if bs == 1 else
                  pl.BlockSpec((1, tn, tk), lambda i, j, k: (k // npk, j, k % npk)))
    else:
        a_spec = pl.BlockSpec((tk, tm), lambda i, j, k: (k, i))
        b_spec = pl.BlockSpec((tk, tn), lambda i, j, k: (k, j))
    if out_split > 1:
        o_spec = pl.BlockSpec((1, tm, tn), lambda i, j, k: (j // npj, i, j % npj))
        o_shape = jax.ShapeDtypeStruct((out_split, M, n_piece), out_dtype)
    else:
        o_spec = pl.BlockSpec((tm, tn), lambda i, j, k: (i, j))
        o_shape = jax.ShapeDtypeStruct((M, N), out_dtype)
    return pl.pallas_call(
        body, name=name, grid=(M // tm, N // tn, nk),
        in_specs=[a_spec, b_spec], out_specs=o_spec, out_shape=o_shape,
        scratch_shapes=[pltpu.VMEM((tm, tn), F32)],
        compiler_params=pltpu.CompilerParams(
            dimension_semantics=("parallel", "parallel", "arbitrary"), vmem_limit_bytes=VMEM_LIMIT),
    )(a, b)


def _rows(name, fn, rows, params, outs, accs=(), tile=256):
    S = rows[0].shape[0]
    tile = min(tile, S)
    assert S % tile == 0, (name, S, tile)
    nr, npar, no, na = len(rows), len(params), len(outs), len(accs)

    def body(*refs):
        rin, pin = refs[:nr], refs[nr:nr + npar]
        oref, aref = refs[nr + npar:nr + npar + no], refs[nr + npar + no:]
        res = fn(*[r[...] for r in rin], *[p[...] for p in pin])
        if not isinstance(res, (tuple, list)):
            res = (res,)
        assert len(res) == no + na, (name, len(res), no, na)
        for ref, val in zip(oref, res[:no]):
            ref[...] = val.astype(ref.dtype)
        if na:
            @pl.when(pl.program_id(0) == 0)
            def _():
                for ref in aref:
                    ref[...] = jnp.zeros(ref.shape, ref.dtype)

            for ref, val in zip(aref, res[no:]):
                ref[...] += jnp.broadcast_to(val, ref.shape).astype(ref.dtype)

    def whole(shape):
        nd = len(shape)
        return pl.BlockSpec(tuple(shape), lambda i: (0,) * nd)

    in_specs = ([pl.BlockSpec((tile, r.shape[1]), lambda i: (i, 0)) for r in rows]
                + [whole(p.shape) for p in params])
    out_specs = ([pl.BlockSpec((tile, n), lambda i: (i, 0)) for n, _ in outs]
                 + [whole(s) for s in accs])
    out_shape = ([jax.ShapeDtypeStruct((S, n), dt) for n, dt in outs]
                 + [jax.ShapeDtypeStruct(tuple(s), F32) for s in accs])
    res = pl.pallas_call(
        body, name=name, grid=(S // tile,), in_specs=in_specs, out_specs=out_specs,
        out_shape=out_shape,
        compiler_params=pltpu.CompilerParams(
            dimension_semantics=("arbitrary",), vmem_limit_bytes=VMEM_LIMIT),
    )(*rows, *params)
    return res


def _modulate(x, sc, sh):
    return x * (1.0 + sc) + sh


def _gmlp_consts():
    lane = lax.broadcasted_iota(jnp.int32, (1, G_WIDTH), 1)
    gmask = [(lane // (G_WIDTH // G_GROUPS) == g).astype(F32) for g in range(G_GROUPS)]
    tril = (lax.broadcasted_iota(jnp.int32, (CHUNK, CHUNK), 0)
            >= lax.broadcasted_iota(jnp.int32, (CHUNK, CHUNK), 1))
    return gmask, tril


def _gmlp_core(gmask, tril, gsel, zu, zv, bu, bv, g, b, bst, *ws):
    u = _gelu(zu + bu)
    v = _ln(_gelu(zv + bv), g, b, LN_EPS)
    s = _pdot(bst, gsel)
    for gi in range(G_GROUPS):
        s = s + _bdot(jnp.where(tril, ws[gi], 0.0), v * gmask[gi])
    return u * s


def _pre_core(rowmask, bd, zr, zk, zv, zl, pr, pk, pv, pq, br, bk, bv, bl, mr, mk, mv, ml,
              w0, wd, a0, wa, wg, k_k, k_a):
    def mix(z, p, b, mu):
        zz = z + b
        return zz + ((p + b) * rowmask - zz) * mu

    r, k, v, l = mix(zr, pr, br, mr), mix(zk, pk, bk, mk), mix(zv, pv, bv, mv), mix(zl, pq, bl, ml)
    w_log = -_softplus(-(w0 + _bdot(jnp.tanh(l), wd))) - 0.5
    decay = jnp.exp(-jnp.exp(w_log))
    a = _sigmoid(a0 + _bdot(l, wa))
    g = _bdot(_sigmoid(l), wg)
    kk = k * k_k
    kkn = kk / jnp.maximum(jnp.sqrt(_pdot(kk * kk, bd)), 1e-12)
    k2 = k * (1.0 + (a - 1.0) * k_a)
    return r, decay, k2, v, -kkn, kkn * a, g


def _post_core(bd, y, r, k2, v, g, gain, bias, rk):
    inv = 1.0 / R_HEAD
    mu = _pdot(y, bd) * inv
    yc = y - mu
    var = _pdot(yc * yc, bd) * inv
    yn = yc * lax.rsqrt(var + GN_EPS) * gain + bias
    bonus = _pdot(r * k2 * rk, bd) * v
    return (yn + bonus) * g


def _merge_core(pa, pb, ga, gb, bga, bgb):
    return _sigmoid(ga + bga) * pa + _sigmoid(gb + bgb) * pb


def _ln1_core(x, mix, gt1, bout, g, b, sc2, sh2):
    h1 = _ln(ALPHA * x + gt1 * (mix + bout), g, b, LN_EPS)
    return h1, h1 * (1.0 + sc2) + sh2


def _ln2_loss_core(tgt, h1, ff, gt2, bff2, g, b):
    out = _ln(ALPHA * h1 + gt2 * (ff + bff2), g, b, LN_EPS)
    err = out - tgt
    return 0.5 * jnp.sum(err * err) * (1.0 / D_MODEL)


def _scan_consts():
    sub = lax.broadcasted_iota(jnp.int32, (R_HEAD, LANES), 0)
    lane = lax.broadcasted_iota(jnp.int32, (R_HEAD, LANES), 1)
    return lane < R_HEAD, sub == (lane & (R_HEAD - 1))


def _seg_sum(lo, xb):
    s_lo = jnp.sum(jnp.where(lo, xb, 0.0), axis=1, keepdims=True)
    s_hi = jnp.sum(jnp.where(lo, 0.0, xb), axis=1, keepdims=True)
    return jnp.where(lo, s_lo, s_hi)


def _seg_dot(lo, s, row):
    lo_row = lo[0:1, :]
    s_lo = jnp.sum(s * jnp.where(lo_row, row, 0.0), axis=1, keepdims=True)
    s_hi = jnp.sum(s * jnp.where(lo_row, 0.0, row), axis=1, keepdims=True)
    return jnp.where(lo, s_lo, s_hi)


def _row_of_col(eye, colb):
    return jnp.sum(jnp.where(eye, colb, 0.0), axis=0, keepdims=True)


def _head_ones():
    i = lax.broadcasted_iota(jnp.int32, (LANES, LANES), 0) // R_HEAD
    j = lax.broadcasted_iota(jnp.int32, (LANES, LANES), 1) // R_HEAD
    return (i == j).astype(jnp.bfloat16)


N_SPLIT = 2


def _split(x):
    parts, r = [], x
    for u in range(N_SPLIT):
        p = r.astype(jnp.bfloat16)
        parts.append(p)
        if u + 1 < N_SPLIT:
            r = r - p.astype(F32)
    return parts


def _ones_dot(parts, ones, n):
    res = lax.dot_general(jnp.concatenate(parts, axis=0), ones, (((1,), (0,)), ((), ())),
                          preferred_element_type=F32)
    out = []
    for m in range(n):
        t = [res[(N_SPLIT * m + u) * R_HEAD:(N_SPLIT * m + u + 1) * R_HEAD] for u in range(N_SPLIT)]
        out.append(functools.reduce(lambda p, q: p + q, t))
    return out


def _seg_sums_mxu(ones, mats):
    return _ones_dot([p for m in mats for p in _split(m)], ones, len(mats))


def _cols_of_rows_mxu(eye, ones, rows8):
    terms = [p.astype(F32) for p in _split(rows8)]
    parts = [jnp.where(eye, jnp.broadcast_to(t[i:i + 1, :], eye.shape), 0.0).astype(jnp.bfloat16)
             for i in range(rows8.shape[0]) for t in terms]
    return _ones_dot(parts, ones, rows8.shape[0])


N_BLK = R_WIDTH // LANES
ROW_GROUP = 8


def _scan_fwd(r, w, k, v, a, b):
    S = r.shape[0]
    T = min(SCAN_T, S)
    nchunk = S // T

    def body(r_ref, w_ref, k_ref, v_ref, a_ref, b_ref, y_ref, sv_ref, st_ref):
        lo, eye = _scan_consts()
        ones = _head_ones()

        @pl.when(pl.program_id(0) == 0)
        def _():
            st_ref[...] = jnp.zeros(st_ref.shape, F32)

        sub8 = lax.broadcasted_iota(jnp.int32, (ROW_GROUP, LANES), 0)

        def group(gi, state):
            base = pl.multiple_of(gi * ROW_GROUP, ROW_GROUP)
            state = list(state)
            sls = [slice(q * LANES, (q + 1) * LANES) for q in range(N_BLK)]
            ld = lambda ref: [ref[pl.ds(base, ROW_GROUP), sl] for sl in sls]
            r8, w8, k8, v8, a8, b8 = ld(r_ref), ld(w_ref), ld(k_ref), ld(v_ref), ld(a_ref), ld(b_ref)
            vb = [_cols_of_rows_mxu(eye, ones, v8[q]) for q in range(N_BLK)]
            an = [pltpu.roll(a8[q], ROW_GROUP - 1, 0) for q in range(N_BLK)]
            ap = [w8[q] * an[q] for q in range(N_BLK)]
            beta = [_seg_sum(lo[:ROW_GROUP], b8[q] * an[q]) for q in range(N_BLK)]
            kappa = [_seg_sum(lo[:ROW_GROUP], k8[q] * an[q]) for q in range(N_BLK)]
            y8 = [jnp.zeros((ROW_GROUP, LANES), F32)] * N_BLK

            def emit_y(i, y8_):
                ycol = _seg_sums_mxu(ones, [state[q] * r8[q][i:i + 1, :] for q in range(N_BLK)])
                return [jnp.where(sub8 == i, _row_of_col(eye, ycol[q]), y8_[q]) for q in range(N_BLK)]

            for i in range(0, ROW_GROUP, 2):
                row = lambda t8, d=0: t8[i + d:i + d + 1, :]
                sa1 = [None] * N_BLK
                for q in range(N_BLK):
                    s = state[q]
                    sv_ref[base + i, :, sls[q]] = s
                    sa0 = _seg_dot(lo, s, row(a8[q]))
                    nxt = _seg_dot(lo, s, row(ap[q]))
                    sa1[q] = nxt + row(beta[q]) * sa0 + row(kappa[q]) * vb[q][i]
                    state[q] = s * row(w8[q]) + sa0 * row(b8[q]) + vb[q][i] * row(k8[q])
                y8 = emit_y(i, y8)
                for q in range(N_BLK):
                    s = state[q]
                    sv_ref[base + i + 1, :, sls[q]] = s
                    state[q] = s * row(w8[q], 1) + sa1[q] * row(b8[q], 1) + vb[q][i + 1] * row(k8[q], 1)
                y8 = emit_y(i + 1, y8)
            for q in range(N_BLK):
                y_ref[pl.ds(base, ROW_GROUP), sls[q]] = y8[q]
            return tuple(state)

        init = tuple(st_ref[:, q * LANES:(q + 1) * LANES] for q in range(N_BLK))
        fin = lax.fori_loop(0, T // ROW_GROUP, group, init)
        for q in range(N_BLK):
            st_ref[:, q * LANES:(q + 1) * LANES] = fin[q]

    blk = pl.BlockSpec((T, R_WIDTH), lambda i: (i, 0))
    return pl.pallas_call(
        body, name="scan_fwd", grid=(nchunk,), in_specs=[blk] * 6,
        out_specs=[blk, pl.BlockSpec((T, R_HEAD, R_WIDTH), lambda i: (i, 0, 0))],
        out_shape=[jax.ShapeDtypeStruct((S, R_WIDTH), F32),
                   jax.ShapeDtypeStruct((S, R_HEAD, R_WIDTH), F32)],
        scratch_shapes=[pltpu.VMEM((R_HEAD, R_WIDTH), F32)],
        compiler_params=pltpu.CompilerParams(
            dimension_semantics=("arbitrary",), vmem_limit_bytes=VMEM_LIMIT),
    )(r, w, k, v, a, b)


def _scan_bwd(r, w, k, v, a, b, states, dy):
    S = r.shape[0]
    T = min(SCAN_T, S)
    nchunk = S // T

    def body(r_ref, w_ref, k_ref, v_ref, a_ref, b_ref, sv_ref, dy_ref,
             dr_ref, dw_ref, dk_ref, dv_ref, da_ref, db_ref, ds_ref):
        lo, eye = _scan_consts()
        ones = _head_ones()

        @pl.when(pl.program_id(0) == 0)
        def _():
            ds_ref[...] = jnp.zeros(ds_ref.shape, F32)

        sub8 = lax.broadcasted_iota(jnp.int32, (ROW_GROUP, LANES), 0)

        def bgroup(n, dstate, chunk_end):
            base = (T // ROW_GROUP - 1 - n) * ROW_GROUP
            if not isinstance(n, int):
                base = pl.multiple_of(base, ROW_GROUP)
            dstate = list(dstate)
            sls = [slice(q * LANES, (q + 1) * LANES) for q in range(N_BLK)]
            ld = lambda ref: [ref[pl.ds(base, ROW_GROUP), sl] for sl in sls]
            r8, w8, k8, v8, a8, b8, dy8 = (ld(r_ref), ld(w_ref), ld(k_ref), ld(v_ref), ld(a_ref), ld(b_ref),
                                           ld(dy_ref))
            vbs = [_cols_of_rows_mxu(eye, ones, v8[q]) for q in range(N_BLK)]
            dycs = [_cols_of_rows_mxu(eye, ones, dy8[q]) for q in range(N_BLK)]
            acc = [{n_: jnp.zeros((ROW_GROUP, LANES), F32) for n_ in ("r", "w", "k", "v", "a", "b")}
                   for _ in range(N_BLK)]
            for i in reversed(range(ROW_GROUP)):
                row = lambda t8: t8[i:i + 1, :]
                put = lambda q_, n_, val: acc[q_].__setitem__(n_, jnp.where(sub8 == i, val, acc[q_][n_]))
                dks = []
                for q in range(N_BLK):
                    sp, vb = sv_ref[base + i, :, sls[q]], vbs[q][i]
                    wr, ar, br, kr, rr = row(w8[q]), row(a8[q]), row(b8[q]), row(k8[q]), row(r8[q])
                    sa = _seg_dot(lo, sp, ar)
                    if chunk_end and i == ROW_GROUP - 1:
                        st = sp * wr + sa * br + vb * kr
                    else:
                        st = sv_ref[base + i + 1, :, sls[q]]
                    dyc = dycs[q][i]
                    ds = dstate[q] + dyc * rr
                    put(q, "r", _colsum(st * dyc))
                    put(q, "w", _colsum(ds * sp))
                    put(q, "b", _colsum(ds * sa))
                    put(q, "k", _colsum(ds * vb))
                    dsa = _seg_dot(lo, ds, br)
                    dks.append(ds * kr)
                    put(q, "a", _colsum(sp * dsa))
                    dstate[q] = ds * wr + dsa * ar
                dvc = _seg_sums_mxu(ones, dks)
                for q in range(N_BLK):
                    put(q, "v", _row_of_col(eye, dvc[q]))
            for q in range(N_BLK):
                for n_, ref in (("r", dr_ref), ("w", dw_ref), ("k", dk_ref), ("v", dv_ref), ("a", da_ref), ("b", db_ref)):
                    ref[pl.ds(base, ROW_GROUP), sls[q]] = acc[q][n_]
            return tuple(dstate)

        fin = bgroup(0, tuple(ds_ref[:, q * LANES:(q + 1) * LANES] for q in range(N_BLK)), True)
        fin = lax.fori_loop(1, T // ROW_GROUP, lambda n, d: bgroup(n, d, False), fin)
        for q in range(N_BLK):
            ds_ref[:, q * LANES:(q + 1) * LANES] = fin[q]

    blk = pl.BlockSpec((T, R_WIDTH), lambda i: (nchunk - 1 - i, 0))
    svb = pl.BlockSpec((T, R_HEAD, R_WIDTH), lambda i: (nchunk - 1 - i, 0, 0))
    return pl.pallas_call(
        body, name="scan_bwd", grid=(nchunk,), in_specs=[blk] * 6 + [svb, blk],
        out_specs=[blk] * 6,
        out_shape=[jax.ShapeDtypeStruct((S, R_WIDTH), F32)] * 6,
        scratch_shapes=[pltpu.VMEM((R_HEAD, R_WIDTH), F32)],
        compiler_params=pltpu.CompilerParams(
            dimension_semantics=("arbitrary",), vmem_limit_bytes=VMEM_LIMIT),
    )(r, w, k, v, a, b, states, dy)


def _pad_in_cols(t, axis):
    cut = 2 * G_WIDTH + R_COLS
    lo, hi = lax.slice_in_dim(t, 0, cut, axis=axis), lax.slice_in_dim(t, cut, t.shape[axis], axis=axis)
    zshape = list(t.shape)
    zshape[axis] = LORA_PAD - LORA
    return jnp.concatenate([lo, jnp.zeros(zshape, t.dtype), hi], axis=axis)


def _unpad_in_cols(t, axis):
    cut = 2 * G_WIDTH + R_COLS
    return jnp.concatenate([lax.slice_in_dim(t, 0, cut, axis=axis),
                            lax.slice_in_dim(t, cut + LORA_PAD - LORA, t.shape[axis], axis=axis)], axis=axis)


def _pad_rows(t, lo, n):
    return jnp.zeros((n, t.shape[1]), t.dtype).at[lo:lo + t.shape[0]].set(t)


def _join_cols(w3):
    p, k, n = w3.shape
    return jnp.transpose(w3, (1, 0, 2)).reshape(k, p * n)


def _split_cols(w):
    k, n = w.shape
    return jnp.transpose(w.reshape(k, N_CHIPS, n // N_CHIPS), (1, 0, 2))


def _ada_dw(ccol, dmod):
    n = dmod.shape[1] // N_CHIPS
    tile = 256

    def body(c_ref, d_ref, o_ref):
        cc = c_ref[...]
        o_ref[0] = ((cc * _sigmoid(cc)) * d_ref[...]).astype(o_ref.dtype)

    return pl.pallas_call(
        body, name="ada_dw", grid=(N_CHIPS, D_MODEL // tile),
        in_specs=[pl.BlockSpec((tile, 1), lambda q, i: (i, 0)), pl.BlockSpec((1, n), lambda q, i: (0, q))],
        out_specs=pl.BlockSpec((1, tile, n), lambda q, i: (q, i, 0)),
        out_shape=jax.ShapeDtypeStruct((N_CHIPS, D_MODEL, n), WIRE_DT),
        compiler_params=pltpu.CompilerParams(dimension_semantics=("parallel", "parallel")),
    )(ccol, dmod)


def _shift_down(t):
    return jnp.concatenate([jnp.zeros((1, t.shape[1]), t.dtype), t[:-1]], axis=0)


def _shift_up(t):
    return jnp.concatenate([t[1:], jnp.zeros((1, t.shape[1]), t.dtype)], axis=0)


def _local_step(x, c, tgt, W):
    S = x.shape[0]
    bf = MXU_DT
    G = {}

    hl = np.arange(R_WIDTH) // R_HEAD
    bd = jnp.asarray(hl[:, None] == hl[None, :], jnp.bfloat16)
    gsel = jnp.asarray(np.arange(LANES)[:, None] == (np.arange(G_WIDTH) // (G_WIDTH // G_GROUPS))[None, :],
                       jnp.bfloat16)
    w_in_p = _pad_in_cols(_join_cols(W["w_in"]), 1)
    b_in_p = _pad_in_cols(W["b_in"], 1)
    c_g, c_r = 2 * G_WIDTH, 2 * G_WIDTH + R_COLS_PAD
    w_g, w_r, w_gate = w_in_p[:, :c_g], w_in_p[:, c_g:c_r], w_in_p[:, c_r:]
    b_g, b_r, b_gate = b_in_p[:, :c_g], b_in_p[:, c_g:c_r], b_in_p[:, c_r:]
    mu_p = jnp.concatenate([W["mu_shift"], jnp.zeros((1, LORA_PAD - LORA), F32)], axis=1)
    wd_p = _pad_rows(W["w_decay_up"].astype(F32), 0, LORA_PAD)
    wa_p = _pad_rows(W["w_aaa_up"].astype(F32), DECAY_LORA, LORA_PAD)
    wg_p = _pad_rows(W["w_gate_up"].astype(F32), DECAY_LORA + AAA_LORA, LORA_PAD)
    ws2 = W["w_spatial"].reshape(G_GROUPS * CHUNK, CHUNK)
    bst = jnp.zeros((CHUNK, LANES), F32).at[:, :G_GROUPS].set(W["b_spatial"].T)
    rk = W["r_k"].reshape(1, R_WIDTH)

    c8 = jnp.broadcast_to(c, (8, D_MODEL))
    (ca8,) = _rows("ada_silu", lambda cc: cc * _sigmoid(cc), [c8], [], [(D_MODEL, bf)], tile=8)
    mod_raw = _mm("ada_mm", ca8, W["w_ada"], "nn")
    (mod8,) = _rows("ada_bias", lambda m, bb: m + bb, [mod_raw], [W["b_ada"]], [(6 * D_MODEL, F32)], tile=8)
    sh1, sc1, gt1, sh2, sc2, gt2 = [mod8[0:1, i * D_MODEL:(i + 1) * D_MODEL] for i in range(6)]

    (h,) = _rows("mod1", _modulate, [x], [sc1, sh1], [(D_MODEL, bf)])
    proj_g = _mm("proj_g", h, w_g, "nn")
    proj_r = _mm("proj_r", h, w_r, "nn")
    proj_gate = _mm("proj_gate", h, w_gate, "nn")

    def split2(t):
        return t[:, :G_WIDTH], t[:, G_WIDTH:]

    def gmlp_fwd(z, bz, g, b, bst_, gsel_, ws_):
        gmask, tril = _gmlp_consts()
        (zu, zv), (bu, bv) = split2(z), split2(bz)
        wsl = [ws_[i * CHUNK:(i + 1) * CHUNK] for i in range(G_GROUPS)]
        return _gmlp_core(gmask, tril, gsel_, zu, zv, bu, bv, g, b, bst_, *wsl)

    (y_a,) = _rows("gmlp_fwd", gmlp_fwd, [proj_g], [b_g, W["g_ln_v"], W["b_ln_v"], bst, gsel, ws2],
                   [(G_WIDTH, bf)], tile=CHUNK)

    r_cuts = (0, R_WIDTH, 2 * R_WIDTH, 3 * R_WIDTH, R_COLS_PAD)

    def split4(t):
        return [t[:, r_cuts[i]:r_cuts[i + 1]] for i in range(4)]

    tile_pre = min(256, S)

    def rowmask_of():
        grow = pl.program_id(0) * tile_pre + lax.broadcasted_iota(jnp.int32, (tile_pre, 1), 0)
        return (grow > 0).astype(F32)

    pre_params = [b_r, mu_p, W["w0"], wd_p, W["a0"], wa_p, wg_p, W["k_k"], W["k_a"], bd]

    def pre_fwd(z, p, bz, mu, w0, wd, a0, wa, wg, k_k, k_a, bd_):
        return _pre_core(rowmask_of(), bd_, *split4(z), *split4(p), *split4(bz), *split4(mu),
                         w0, wd, a0, wa, wg, k_k, k_a)

    proj_r_prev = _shift_down(proj_r)
    s_r, s_w, s_k, s_v, s_a, s_b, s_g = _rows(
        "rwkv_pre_fwd", pre_fwd, [proj_r, proj_r_prev], pre_params, [(R_WIDTH, F32)] * 7, tile=tile_pre)
    y_scan, states = _scan_fwd(s_r, s_w, s_k, s_v, s_a, s_b)

    def post_fwd(y, r, k2, v, g, gain, bias, rk_, bd_):
        return _post_core(bd_, y, r, k2, v, g, gain, bias, rk_)

    post_params = [W["gn_gain"], W["gn_bias"], rk, bd]
    (y_b,) = _rows("rwkv_post_fwd", post_fwd, [y_scan, s_r, s_k, s_v, s_g], post_params, [(R_WIDTH, bf)])
    p_a = _mm("branch_a", y_a, W["w_branch_a"], "nn")
    p_b = _mm("branch_b", y_b, W["w_branch_b"], "nn")

    def merge_fwd(pa, pb, gz, bgz):
        return _merge_core(pa, pb, gz[:, :D_MODEL], gz[:, D_MODEL:], bgz[:, :D_MODEL], bgz[:, D_MODEL:])

    (merged,) = _rows("merge_fwd", merge_fwd, [p_a, p_b, proj_gate], [b_gate], [(D_MODEL, bf)])
    mix = _mm("out_proj", merged, W["w_out"], "nn")
    ln1_params = [gt1, W["b_out"], W["ln1_g"], W["ln1_b"], sc2, sh2]
    h1, h2 = _rows("ln1_fwd", _ln1_core, [x, mix], ln1_params, [(D_MODEL, F32), (D_MODEL, bf)])

    a1 = _mm("ff1", h2, W["w_ff1"], "nn")
    (act,) = _rows("ff_act", lambda z, bb: jnp.square(jnp.maximum(z + bb, 0.0)), [a1], [W["b_ff1"]], [(D_FF, bf)])
    ff = _mm("ff2", act, W["w_ff2"], "nn")

    def ln2_loss(h1_, ff_, tg, gt2_, bff2, g, b):
        loss, vjp = jax.vjp(functools.partial(_ln2_loss_core, tg), h1_, ff_, gt2_, bff2, g, b)
        return vjp(jnp.ones((), F32)) + (loss,)

    ln2_params = [gt2, W["b_ff2"], W["ln2_g"], W["ln2_b"]]
    dh1, dff, dgt2, G["b_ff2"], G["ln2_g"], G["ln2_b"], loss_acc = _rows(
        "ln2_loss", ln2_loss, [h1, ff, tgt], ln2_params, [(D_MODEL, F32), (D_MODEL, bf)],
        accs=[(1, D_MODEL)] * 4 + [(1, LANES)])
    loss = loss_acc[0, 0]

    dact = _mm("ff2_dx", dff, W["w_ff2"], "nt")
    G["w_ff2"] = _mm("ff2_dw", act, dff, "tn", WIRE_DT).reshape(N_CHIPS, D_FF // N_CHIPS, D_MODEL)

    def act_bwd(z, da, bb):
        d = da * 2.0 * jnp.maximum(z + bb, 0.0)
        return d, _colsum(d)

    da1, G["b_ff1"] = _rows("ff_act_bwd", act_bwd, [a1, dact], [W["b_ff1"]], [(D_FF, bf)], accs=[(1, D_FF)])
    dh2 = _mm("ff1_dx", da1, W["w_ff1"], "nt")
    G["w_ff1"] = _mm("ff1_dw", h2, da1, "tn", WIRE_DT, out_split=N_CHIPS)

    def ln1_bwd(x_, mix_, dh1_, dh2_, *ps):
        _, vjp = jax.vjp(_ln1_core, x_, mix_, *ps)
        return vjp((dh1_, dh2_))

    dx_res, dmix, dgt1, G["b_out"], G["ln1_g"], G["ln1_b"], dsc2, dsh2 = _rows(
        "ln1_bwd", ln1_bwd, [x, mix, dh1, dh2], ln1_params, [(D_MODEL, F32), (D_MODEL, bf)],
        accs=[(1, D_MODEL)] * 6)

    dmerged = _mm("out_proj_dx", dmix, W["w_out"], "nt")
    G["w_out"] = _mm("out_proj_dw", merged, dmix, "tn", WIRE_DT).reshape(N_CHIPS, D_MODEL // N_CHIPS, D_MODEL)

    def merge_bwd(pa, pb, gz, dm, bgz):
        args = (pa.astype(F32), pb.astype(F32), gz[:, :D_MODEL], gz[:, D_MODEL:], bgz[:, :D_MODEL], bgz[:, D_MODEL:])
        _, vjp = jax.vjp(_merge_core, *args)
        dpa, dpb, dga, dgb, dbga, dbgb = vjp(dm)
        return dpa, dpb, jnp.concatenate([dga, dgb], axis=1), jnp.concatenate([dbga, dbgb], axis=1)

    dp_a, dp_b, dgates, db_gate = _rows(
        "merge_bwd", merge_bwd, [p_a, p_b, proj_gate, dmerged], [b_gate],
        [(D_MODEL, bf), (D_MODEL, bf), (2 * D_MODEL, F32)], accs=[(1, 2 * D_MODEL)])
    dy_a = _mm("branch_a_dx", dp_a, W["w_branch_a"], "nt")
    G["w_branch_a"] = _mm("branch_a_dw", y_a, dp_a, "tn", WIRE_DT, out_split=N_CHIPS)
    dy_b = _mm("branch_b_dx", dp_b, W["w_branch_b"], "nt")
    G["w_branch_b"] = _mm("branch_b_dw", y_b, dp_b, "tn", WIRE_DT, out_split=N_CHIPS)

    def post_bwd(y, r, k2, v, g, dyb, gain, bias, rk_, bd_):
        _, vjp = jax.vjp(functools.partial(_post_core, bd_), y, r, k2, v, g, gain, bias, rk_)
        return vjp(dyb)

    dy_scan, dr_p, dk_p, dv_p, dg_p, G["gn_gain"], G["gn_bias"], drk = _rows(
        "rwkv_post_bwd", post_bwd, [y_scan, s_r, s_k, s_v, s_g, dy_b], post_params,
        [(R_WIDTH, F32)] * 5, accs=[(1, R_WIDTH)] * 3)
    G["r_k"] = drk.reshape(R_HEADS, R_HEAD)
    dr_s, dw_s, dk_s, dv_s, da_s, db_s = _scan_bwd(s_r, s_w, s_k, s_v, s_a, s_b, states, dy_scan)

    def pre_bwd(z, p, dr1, dr2, dw, dk1, dk2, dv1, dv2, da, db, dg,
                bz, mu, w0, wd, a0, wa, wg, k_k, k_a, bd_):
        prim = (*split4(z), *split4(p), *split4(bz), *split4(mu), w0, wd, a0, wa, wg, k_k, k_a)
        _, vjp = jax.vjp(functools.partial(_pre_core, rowmask_of(), bd_), *prim)
        d = vjp((dr1 + dr2, dw, dk1 + dk2, dv1 + dv2, da, db, dg))
        cat = lambda parts: jnp.concatenate(parts, axis=1)
        return (cat(d[0:4]), cat(d[4:8]), cat(d[8:12]), cat(d[12:16])) + tuple(d[16:])

    dz_r, dprev, db_r, dmu_p, G["w0"], dwd_p, G["a0"], dwa_p, dwg_p, G["k_k"], G["k_a"] = _rows(
        "rwkv_pre_bwd", pre_bwd,
        [proj_r, proj_r_prev, dr_s, dr_p, dw_s, dk_s, dk_p, dv_s, dv_p, da_s, db_s, dg_p],
        pre_params, [(R_COLS_PAD, F32)] * 2,
        accs=[(1, R_COLS_PAD), (1, R_COLS_PAD), (1, R_WIDTH), (LORA_PAD, R_WIDTH), (1, R_WIDTH),
              (LORA_PAD, R_WIDTH), (LORA_PAD, R_WIDTH), (1, R_WIDTH), (1, R_WIDTH)],
        tile=tile_pre)
    G["mu_shift"] = dmu_p[:, :R_COLS]
    G["w_decay_up"] = dwd_p[:DECAY_LORA]
    G["w_aaa_up"] = dwa_p[DECAY_LORA:DECAY_LORA + AAA_LORA]
    G["w_gate_up"] = dwg_p[DECAY_LORA + AAA_LORA:LORA]

    def gmlp_bwd(z, dya, bz, g, b, bst_, gsel_, ws_):
        gmask, tril = _gmlp_consts()
        (zu, zv), (bu, bv) = split2(z), split2(bz)
        wsl = [ws_[i * CHUNK:(i + 1) * CHUNK] for i in range(G_GROUPS)]
        _, vjp = jax.vjp(functools.partial(_gmlp_core, gmask, tril, gsel_), zu, zv, bu, bv, g, b, bst_, *wsl)
        d = vjp(dya)
        return (jnp.concatenate(d[0:2], axis=1), jnp.concatenate(d[2:4], axis=1), d[4], d[5], d[6],
                jnp.concatenate(d[7:], axis=0))

    dz_g, db_g, G["g_ln_v"], G["b_ln_v"], dbst, dws2 = _rows(
        "gmlp_bwd", gmlp_bwd, [proj_g, dy_a], [b_g, W["g_ln_v"], W["b_ln_v"], bst, gsel, ws2],
        [(2 * G_WIDTH, F32)],
        accs=[(1, 2 * G_WIDTH), (1, G_WIDTH), (1, G_WIDTH), (CHUNK, LANES), (G_GROUPS * CHUNK, CHUNK)],
        tile=CHUNK)
    G["w_spatial"] = dws2.reshape(G_GROUPS, CHUNK, CHUNK)
    G["b_spatial"] = dbst[:, :G_GROUPS].T

    def dproj_cat(dzg, dzr, dpv, dgz):
        return jnp.concatenate([dzg, dzr + dpv, dgz], axis=1)

    (dproj,) = _rows("dproj_cat", dproj_cat, [dz_g, dz_r, _shift_up(dprev), dgates], [],
                     [(2 * G_WIDTH + R_COLS_PAD + 2 * D_MODEL, bf)])
    dh = _mm("proj_dx", dproj, w_in_p, "nt", tk=2432)
    G["w_in"] = _split_cols(_unpad_in_cols(_mm("proj_dw", h, dproj, "tn", WIRE_DT, tm=512, tn=2432), 1))
    G["b_in"] = _unpad_in_cols(jnp.concatenate([db_g, db_r, db_gate], axis=1), 1)

    def mod1_bwd(x_, dh_, dxr, sc):
        return dh_ * (1.0 + sc) + dxr, _colsum(dh_ * x_), _colsum(dh_)

    grad_x, dsc1, dsh1 = _rows("mod1_bwd", mod1_bwd, [x, dh, dx_res], [sc1], [(D_MODEL, F32)],
                               accs=[(1, D_MODEL)] * 2)

    dmod = jnp.concatenate([dsh1, dsc1, dgt1, dsh2, dsc2, dgt2], axis=1)
    G["b_ada"] = dmod
    G["w_ada"] = _ada_dw(c.reshape(D_MODEL, 1), dmod)
    return loss, grad_x, G


BIG = (("w_ada", (D_MODEL, 6 * D_MODEL), 1), ("w_in", (D_MODEL, 2 * G_WIDTH + R_COLS + 2 * D_MODEL), 1),
       ("w_branch_a", (G_WIDTH, D_MODEL), 1), ("w_branch_b", (R_WIDTH, D_MODEL), 1),
       ("w_out", (D_MODEL, D_MODEL), 0), ("w_ff1", (D_MODEL, D_FF), 1), ("w_ff2", (D_FF, D_MODEL), 0))
LORAS = (("w_decay_up", (DECAY_LORA, R_WIDTH), 1), ("w_aaa_up", (AAA_LORA, R_WIDTH), 1),
         ("w_gate_up", (GATE_LORA, R_WIDTH), 1))
SHARDED = BIG + LORAS
SMALL = (("b_ada", (1, 6 * D_MODEL)), ("b_in", (1, 2 * G_WIDTH + R_COLS + 2 * D_MODEL)),
         ("g_ln_v", (1, G_WIDTH)), ("b_ln_v", (1, G_WIDTH)), ("w_spatial", (G_GROUPS, CHUNK, CHUNK)),
         ("b_spatial", (G_GROUPS, CHUNK)), ("mu_shift", (1, R_COLS)), ("w0", (1, R_WIDTH)),
         ("a0", (1, R_WIDTH)), ("k_k", (1, R_WIDTH)), ("k_a", (1, R_WIDTH)), ("r_k", (R_HEADS, R_HEAD)),
         ("gn_gain", (1, R_WIDTH)), ("gn_bias", (1, R_WIDTH)), ("b_out", (1, D_MODEL)),
         ("ln1_g", (1, D_MODEL)), ("ln1_b", (1, D_MODEL)), ("b_ff1", (1, D_FF)), ("b_ff2", (1, D_MODEL)),
         ("ln2_g", (1, D_MODEL)), ("ln2_b", (1, D_MODEL)))
WEIGHT_ORDER = ("w_ada", "b_ada", "w_in", "b_in", "g_ln_v", "b_ln_v", "w_spatial", "b_spatial", "mu_shift",
                "w0", "w_decay_up", "a0", "w_aaa_up", "w_gate_up", "k_k", "k_a", "r_k", "gn_gain", "gn_bias",
                "w_branch_a", "w_branch_b", "w_out", "b_out", "ln1_g", "ln1_b", "w_ff1", "b_ff1", "w_ff2",
                "b_ff2", "ln2_g", "ln2_b")
N_CHIPS = 4


def _shard_shape(shape, axis):
    s = list(shape)
    s[axis] //= N_CHIPS
    return tuple(s)


def _numel(shape):
    return int(np.prod(shape))


def _round_up(n, q):
    return -(-n // q) * q


N_LORA = sum(_numel(_shard_shape(s, a)) for _, s, a in LORAS)
N_SMALL = sum(_numel(s) for _, s in SMALL)
ROWS_SW = _round_up(N_LORA, PACK_Q) // PACK_W
ROWS_SG = _round_up(N_LORA + N_SMALL, PACK_Q) // PACK_W


def _pack_small(loras, small, rows, dtype):
    parts = [loras[n] for n, _, _ in LORAS] + ([small[n] for n, _ in SMALL] if small is not None else [])
    flat = jnp.concatenate([p.reshape(-1).astype(dtype) for p in parts])
    flat = jnp.concatenate([flat, jnp.zeros((rows * PACK_W - flat.shape[0],), dtype)])
    return flat.reshape(rows, PACK_W)


def _unpack_small(pack, with_small):
    flat = pack.reshape(-1)
    out, off = {}, 0
    for n, s, a in LORAS:
        ss = _shard_shape(s, a)
        out[n] = flat[off:off + _numel(ss)].reshape(ss)
        off += _numel(ss)
    if with_small:
        for n, s in SMALL:
            out[n] = flat[off:off + _numel(s)].reshape(s)
            off += _numel(s)
    return out


def _pack_small_grads(G):
    segs = []
    for q in range(N_CHIPS):
        loras = {n: G[n][:, q * (s[1] // N_CHIPS):(q + 1) * (s[1] // N_CHIPS)] for n, s, _ in LORAS}
        segs.append(_pack_small(loras, G, ROWS_SG, F32))
    return jnp.stack(segs)


ANY = pl.BlockSpec(memory_space=pl.ANY)
MESH = pl.DeviceIdType.MESH


def _place():
    x, y, c = lax.axis_index("x"), lax.axis_index("y"), lax.axis_index("c")
    chips = [(1 - x, y), (x, 1 - y), (1 - x, 1 - y)]
    return x, y, c, chips


def _remote(src, dst, send_sem, recv_sem, to):
    return pltpu.make_async_remote_copy(src_ref=src, dst_ref=dst, send_sem=send_sem, recv_sem=recv_sem,
                                        device_id=to, device_id_type=MESH)


def _all_gather(shards):
    n = len(shards)
    halves = [a.shape[0] // 2 for a in shards]

    def body(*refs):
        ins, outs = refs[:n], refs[n:2 * n]
        send_sems, recv_sems = refs[2 * n:]
        x, y, c, chips = _place()
        s = 2 * x + y
        sibling = (x, y, 1 - c)
        slot = [2 * chip[0] + chip[1] for chip in chips]
        first, passed = [], []
        for a in range(n):
            H = halves[a]
            mine = ins[a].at[pl.ds(c * H, H)]
            first += [_remote(mine, outs[a].at[2 * s + c], send_sems.at[6 * a + j], recv_sems.at[6 * a + j], (*chip, c))
                      for j, chip in enumerate(chips)]
        for cp in first:
            cp.start()
        for a in range(n):
            for j in range(3):
                landed = outs[a].at[2 * slot[j] + c]
                _remote(landed, landed, send_sems.at[6 * a + j], recv_sems.at[6 * a + j], sibling).wait_recv()
                cp = _remote(landed, landed, send_sems.at[6 * a + 3 + j], recv_sems.at[6 * a + 3 + j], sibling)
                cp.start()
                passed.append(cp)
        for a in range(n):
            for j in range(3):
                got = outs[a].at[2 * slot[j] + 1 - c]
                _remote(got, got, send_sems.at[6 * a + 3 + j], recv_sems.at[6 * a + 3 + j], sibling).wait_recv()
        for cp in first + passed:
            cp.wait_send()

    return pl.pallas_call(
        body, name="ag_weights", in_specs=[ANY] * n, out_specs=[ANY] * n,
        out_shape=[jax.ShapeDtypeStruct((2 * N_CHIPS, h, a.shape[1]), a.dtype) for a, h in zip(shards, halves)],
        scratch_shapes=[pltpu.SemaphoreType.DMA((6 * n,)), pltpu.SemaphoreType.DMA((6 * n,))],
    )(*shards)


def _rs_sibling_in(gps):
    n = len(gps)

    def body(*refs):
        ins, outs = refs[:n], refs[n:2 * n]
        send_sems, recv_sems = refs[2 * n:]
        x, y, c, _ = _place()
        cps = []
        for a in range(n):
            H = gps[a].shape[1] // 2
            cps += [_remote(ins[a].at[q, pl.ds((1 - c) * H, H)], outs[a].at[q], send_sems.at[N_CHIPS * a + q],
                            recv_sems.at[N_CHIPS * a + q], (x, y, 1 - c)) for q in range(N_CHIPS)]
        for cp in cps:
            cp.start()
        for cp in cps:
            cp.wait()

    return pl.pallas_call(
        body, name="rs_sibling_in", in_specs=[ANY] * n, out_specs=[ANY] * n,
        out_shape=[jax.ShapeDtypeStruct((N_CHIPS, g.shape[1] // 2, g.shape[2]), g.dtype) for g in gps],
        scratch_shapes=[pltpu.SemaphoreType.DMA((N_CHIPS * n,)), pltpu.SemaphoreType.DMA((N_CHIPS * n,))],
    )(*gps)


def _rs_add_own(name, gp, got, c_arr, tr=256):
    H, C = got.shape[1:]
    tr = _pick(H, tr, 8)
    nb = H // tr

    def body(c_ref, g_ref, r_ref, o_ref):
        o_ref[...] = (g_ref[...].astype(F32) + r_ref[...].astype(F32)).astype(o_ref.dtype)

    return pl.pallas_call(
        body, name="rs_add_own_" + name,
        grid_spec=pltpu.PrefetchScalarGridSpec(
            num_scalar_prefetch=1, grid=(N_CHIPS, nb),
            in_specs=[pl.BlockSpec((1, tr, C), lambda q, i, c_ref: (q, c_ref[0] * nb + i, 0)),
                      pl.BlockSpec((1, tr, C), lambda q, i, c_ref: (q, i, 0))],
            out_specs=pl.BlockSpec((1, tr, C), lambda q, i, c_ref: (q, i, 0))),
        out_shape=jax.ShapeDtypeStruct((N_CHIPS, H, C), gp.dtype),
        compiler_params=pltpu.CompilerParams(dimension_semantics=("arbitrary", "arbitrary")),
    )(c_arr, gp, got)


def _rs_chips(parts):
    n = len(parts)

    def body(*refs):
        ins, outs = refs[:n], refs[n:2 * n]
        send_sems, recv_sems = refs[2 * n:]
        x, y, c, chips = _place()
        s = 2 * x + y
        slot = [2 * chip[0] + chip[1] for chip in chips]
        cps = [_remote(ins[a].at[slot[j]], outs[a].at[s], send_sems.at[3 * a + j], recv_sems.at[3 * a + j], (*chips[j], c))
               for a in range(n) for j in range(3)]
        for cp in cps:
            cp.start()
        for a in range(n):
            for j in range(3):
                cps[3 * a + j].wait_send()
                got = outs[a].at[slot[j]]
                _remote(got, got, send_sems.at[3 * a + j], recv_sems.at[3 * a + j], (*chips[j], c)).wait_recv()

    return pl.pallas_call(
        body, name="rs_chips", in_specs=[ANY] * n, out_specs=[ANY] * n,
        out_shape=[jax.ShapeDtypeStruct(p.shape, p.dtype) for p in parts],
        scratch_shapes=[pltpu.SemaphoreType.DMA((3 * n,)), pltpu.SemaphoreType.DMA((3 * n,))],
    )(*parts)


def _rs_add_chips(name, part, slots, sc_arr, tr=128):
    H, C = slots.shape[1:]
    tr = _pick(H, tr, 8)
    nb = H // tr

    def body(sc_ref, p_ref, s_ref, o_ref):
        acc = None
        for q in range(N_CHIPS):
            term = lax.cond(sc_ref[0] == q, lambda: p_ref[0].astype(F32), lambda q=q: s_ref[q].astype(F32))
            acc = term if acc is None else acc + term
        o_ref[...] = acc

    return pl.pallas_call(
        body, name="rs_add_chips_" + name,
        grid_spec=pltpu.PrefetchScalarGridSpec(
            num_scalar_prefetch=1, grid=(nb,),
            in_specs=[pl.BlockSpec((1, tr, C), lambda i, sc: (sc[0], i, 0)),
                      pl.BlockSpec((N_CHIPS, tr, C), lambda i, sc: (0, i, 0))],
            out_specs=pl.BlockSpec((tr, C), lambda i, sc: (sc[1] * nb + i, 0))),
        out_shape=jax.ShapeDtypeStruct((2 * H, C), F32),
        compiler_params=pltpu.CompilerParams(dimension_semantics=("arbitrary",)),
    )(sc_arr, part, slots)


def _rs_sibling_out(wholes):
    n = len(wholes)

    def body(*refs):
        ins, outs = refs[:n], refs[n:2 * n]
        send_sems, recv_sems = refs[2 * n:]
        x, y, c, _ = _place()
        cps = []
        for a in range(n):
            H = wholes[a].shape[0] // 2
            cps.append(_remote(ins[a].at[pl.ds(c * H, H)], outs[a].at[pl.ds(c * H, H)], send_sems.at[a], recv_sems.at[a],
                               (x, y, 1 - c)))
        for cp in cps:
            cp.start()
        for a in range(n):
            H = wholes[a].shape[0] // 2
            cps[a].wait_send()
            got = outs[a].at[pl.ds((1 - c) * H, H)]
            _remote(got, got, send_sems.at[a], recv_sems.at[a], (x, y, 1 - c)).wait_recv()

    return pl.pallas_call(
        body, name="rs_sibling_out", in_specs=[ANY] * n, out_specs=[ANY] * n,
        out_shape=[jax.ShapeDtypeStruct(w.shape, w.dtype) for w in wholes],
        input_output_aliases={a: a for a in range(n)},
        scratch_shapes=[pltpu.SemaphoreType.DMA((n,)), pltpu.SemaphoreType.DMA((n,))],
    )(*wholes)


def _adamw(name, w, g, m, v):
    def fn(w_, g_, m_, v_):
        m2 = ADAM_B1 * m_ + (1.0 - ADAM_B1) * g_
        v2 = ADAM_B2 * v_ + (1.0 - ADAM_B2) * (g_ * g_)
        m_hat = m2 / (1.0 - ADAM_B1 ** ADAM_STEP)
        v_hat = v2 / (1.0 - ADAM_B2 ** ADAM_STEP)
        return -ADAM_LR * (m_hat / (jnp.sqrt(v_hat) + ADAM_EPS) + ADAM_WD * w_), m2, v2

    return _rows("adamw_" + name, fn, [w, g, m, v], [], [(w.shape[1], F32)] * 3, tile=_pick(w.shape[0], 256, 8))


def kernel(x, c, w_ada, b_ada, w_in, b_in, g_ln_v, b_ln_v, w_spatial, b_spatial, mu_shift, w0, w_decay_up, a0, w_aaa_up, w_gate_up, k_k, k_a, r_k, gn_gain, gn_bias, w_branch_a, w_branch_b, w_out, b_out, ln1_g, ln1_b, w_ff1, b_ff1, w_ff2, b_ff2, ln2_g, ln2_b, loss_target, m_w_ada, m_b_ada, m_w_in, m_b_in, m_g_ln_v, m_b_ln_v, m_w_spatial, m_b_spatial, m_mu_shift, m_w0, m_w_decay_up, m_a0, m_w_aaa_up, m_w_gate_up, m_k_k, m_k_a, m_r_k, m_gn_gain, m_gn_bias, m_w_branch_a, m_w_branch_b, m_w_out, m_b_out, m_ln1_g, m_ln1_b, m_w_ff1, m_b_ff1, m_w_ff2, m_b_ff2, m_ln2_g, m_ln2_b, v_w_ada, v_b_ada, v_w_in, v_b_in, v_g_ln_v, v_b_ln_v, v_w_spatial, v_b_spatial, v_mu_shift, v_w0, v_w_decay_up, v_a0, v_w_aaa_up, v_w_gate_up, v_k_k, v_k_a, v_r_k, v_gn_gain, v_gn_bias, v_w_branch_a, v_w_branch_b, v_w_out, v_b_out, v_ln1_g, v_ln1_b, v_w_ff1, v_b_ff1, v_w_ff2, v_b_ff2, v_ln2_g, v_ln2_b):
    args = dict(locals())
    local_shape = {n: _shard_shape(s, a) for n, s, a in SHARDED}
    local_shape.update(dict(SMALL))
    wts = {n: args[n].reshape(local_shape[n]) for n in WEIGHT_ORDER}
    mom = {n: args["m_" + n].reshape(local_shape[n]) for n in WEIGHT_ORDER}
    var = {n: args["v_" + n].reshape(local_shape[n]) for n in WEIGHT_ORDER}
    big = [n for n, _, _ in BIG]

    chip = 2 * lax.axis_index("x") + lax.axis_index("y")
    mine = [wts[n].astype(MXU_DT) for n in big] + [_pack_small(wts, None, ROWS_SW, MXU_DT)]
    gath = [lax.dynamic_update_slice(g.reshape((N_CHIPS,) + m.shape), m[None], (chip, 0, 0))
            for g, m in zip(_all_gather(mine), mine)]
    W = {n: wts[n] for n, _ in SMALL}
    for (n, s, a), g in zip(BIG, gath):
        W[n] = g if a == 1 else g.reshape(s)
    lora_q = [_unpack_small(gath[-1][q], False) for q in range(N_CHIPS)]
    for n, _, _ in LORAS:
        W[n] = jnp.concatenate([lora_q[q][n] for q in range(N_CHIPS)], axis=1)

    loss, grad_x, G = _local_step(x[0], c, loss_target[0], W)
    loss = lax.psum(loss, MESH_AXES)

    names = big + ["small"]
    gps = [G[n] for n in big] + [_pack_small_grads(G)]
    c_arr = lax.axis_index("c").astype(jnp.int32).reshape(1)
    parts = [_rs_add_own(n, g, r, c_arr) for n, g, r in zip(names, gps, _rs_sibling_in(gps))]
    sc_arr = jnp.stack([chip, lax.axis_index("c")]).astype(jnp.int32)
    segs = _rs_sibling_out([_rs_add_chips(n, p, s, sc_arr) for n, p, s in zip(names, parts, _rs_chips(parts))])

    out = {}
    for n, g in zip(big, segs[:-1]):
        out[n] = (g,) + tuple(_adamw(n, wts[n], g, mom[n], var[n]))
    packs = [_pack_small(t, t, ROWS_SG, F32) for t in (wts, mom, var)]
    small4 = [_unpack_small(t, True) for t in (segs[-1],) + tuple(_adamw("small", packs[0], segs[-1], packs[1], packs[2]))]
    res = [loss, grad_x[None]]
    for k in range(4):
        res += [(out[n][k] if n in out else small4[k][n]).reshape(args[n].shape) for n in WEIGHT_ORDER]
    return tuple(res)
```

```python
import functools

import numpy as np
import jax
import jax.numpy as jnp
from jax import lax
from jax.experimental import pallas as pl
from jax.experimental.pallas import tpu as pltpu

F32 = jnp.float32
MXU_DT = jnp.bfloat16
WIRE_DT = jnp.bfloat16

D_MODEL = 1024
G_GROUPS = 8
G_WIDTH = 512
CHUNK = 128
R_WIDTH = 512
R_HEAD = 64
R_HEADS = 8
DECAY_LORA = 32
AAA_LORA = 32
GATE_LORA = 96
LORA = DECAY_LORA + AAA_LORA + GATE_LORA
LORA_PAD = 256
R_COLS = 3 * R_WIDTH + LORA
R_COLS_PAD = 3 * R_WIDTH + LORA_PAD
D_FF = 4 * D_MODEL
ALPHA = 2.0 ** 0.25
LN_EPS = 1e-5
GN_EPS = 64e-5
ADAM_LR = 0.001
ADAM_B1 = 0.9
ADAM_B2 = 0.999
ADAM_EPS = 1e-08
ADAM_WD = 0.01
ADAM_STEP = 10

LANES = 128
PACK_W = 512
PACK_Q = 2 * 16 * PACK_W
VMEM_LIMIT = 48 * 1024 * 1024
SCAN_T = 64

MESH_AXES = ("x", "y", "c")


def _dg(a, b, dims):
    return lax.dot_general(a.astype(MXU_DT), b.astype(MXU_DT), (dims, ((), ())),
                           preferred_element_type=F32)


@jax.custom_vjp
def _bdot(a, b):
    return _dg(a, b, ((1,), (0,)))


def _bdot_fwd(a, b):
    return _bdot(a, b), (a, b)


def _bdot_bwd(res, g):
    a, b = res
    return (_dg(g, b, ((1,), (1,))).astype(a.dtype), _dg(a, g, ((0,), (0,))).astype(b.dtype))


_bdot.defvjp(_bdot_fwd, _bdot_bwd)


def _split_dot(x, m, dims):
    hi = x.astype(jnp.bfloat16)
    lo = (x - hi.astype(F32)).astype(jnp.bfloat16)
    dn = (dims, ((), ()))
    return (lax.dot_general(hi, m, dn, preferred_element_type=F32)
            + lax.dot_general(lo, m, dn, preferred_element_type=F32))


@jax.custom_vjp
def _pdot(x, m):
    return _split_dot(x, m, ((1,), (0,)))


def _pdot_fwd(x, m):
    return _pdot(x, m), m


def _pdot_bwd(m, g):
    return _split_dot(g, m, ((1,), (1,))), None


_pdot.defvjp(_pdot_fwd, _pdot_bwd)


def _sigmoid(x):
    return 1.0 / (1.0 + jnp.exp(-x))


def _softplus(x):
    return jnp.maximum(x, 0.0) + jnp.log(1.0 + jnp.exp(-jnp.maximum(x, -x)))


def _gelu(x):
    return 0.5 * x * (1.0 + jnp.tanh(0.7978845608028654 * (x + 0.044715 * (x * x * x))))


def _ln(x, g, b, eps):
    mu = jnp.mean(x, axis=-1, keepdims=True)
    xc = x - mu
    var = jnp.mean(xc * xc, axis=-1, keepdims=True)
    return xc * lax.rsqrt(var + eps) * g + b


def _colsum(x):
    return jnp.sum(x, axis=0, keepdims=True)


def _pick(n, target, q=LANES):
    if n <= target:
        return n
    best = None
    for t in range(q, target + 1, q):
        if n % t == 0:
            best = t
    assert best is not None, (n, target)
    return best


def _mm(name, a, b, mode, out_dtype=F32, out_split=1, tm=1024, tn=1024, tk=1024):
    bs = b.shape[0] if b.ndim == 3 else 1
    br, bc = b.shape[-2:]
    if mode == "nn":
        (M, K), K2, N = a.shape, br, bc * bs
    elif mode == "nt":
        (M, K), N, K2 = a.shape, br, bc * bs
    else:
        assert bs == 1
        (K, M), K2, N = a.shape, br, bc
    assert K == K2, (name, a.shape, b.shape, mode)
    n_piece = N // max(bs if mode == "nn" else 1, out_split)
    k_piece = K // (bs if mode == "nt" else 1)
    tm, tn, tk = _pick(M, tm, 8 if M < LANES else LANES), _pick(n_piece, tn), _pick(k_piece, tk)
    nk, npj, npk = K // tk, n_piece // tn, k_piece // tk
    dims = {"nn": ((1,), (0,)), "nt": ((1,), (1,)), "tn": ((0,), (0,))}[mode]

    def body(a_ref, b_ref, o_ref, acc_ref):
        k = pl.program_id(2)

        @pl.when(k == 0)
        def _():
            acc_ref[...] = jnp.zeros(acc_ref.shape, F32)

        acc_ref[...] += _dg(a_ref[...], b_ref[0] if bs > 1 else b_ref[...], dims)

        @pl.when(k == nk - 1)
        def _():
            if out_split > 1:
                o_ref[0] = acc_ref[...].astype(o_ref.dtype)
            else:
                o_ref[...] = acc_ref[...].astype(o_ref.dtype)

    if mode == "nn":
        a_spec = pl.BlockSpec((tm, tk), lambda i, j, k: (i, k))
        b_spec = (pl.BlockSpec((tk, tn), lambda i, j, k: (k, j)) if bs == 1 else
                  pl.BlockSpec((1, tk, tn), lambda i, j, k: (j // npj, k, j % npj)))
    elif mode == "nt":
        a_spec = pl.BlockSpec((tm, tk), lambda i, j, k: (i, k))
        b_spec = (pl.BlockSpec((tn, tk), lambda i, j, k: (j, k)) if bs == 1 else
                  pl.BlockSpec((1, tn, tk), lambda i, j, k: (k // npk, j, k % npk)))
    else:
        a_spec = pl.BlockSpec((tk, tm), lambda i, j, k: (k, i))
        b_spec = pl.BlockSpec((tk, tn), lambda i, j, k: (k, j))
    if out_split > 1:
        o_spec = pl.BlockSpec((1, tm, tn), lambda i, j, k: (j // npj, i, j % npj))
        o_shape = jax.ShapeDtypeStruct((out_split, M, n_piece), out_dtype)
    else:
        o_spec = pl.BlockSpec((tm, tn), lambda i, j, k: (i, j))
        o_shape = jax.ShapeDtypeStruct((M, N), out_dtype)
    return pl.pallas_call(
        body, name=name, grid=(M // tm, N // tn, nk),
        in_specs=[a_spec, b_spec], out_specs=o_spec, out_shape=o_shape,
        scratch_shapes=[pltpu.VMEM((tm, tn), F32)],
        compiler_params=pltpu.CompilerParams(
            dimension_semantics=("parallel", "parallel", "arbitrary"), vmem_limit_bytes=VMEM_LIMIT),
    )(a, b)


def _rows(name, fn, rows, params, outs, accs=(), tile=256):
    S = rows[0].shape[0]
    tile = min(tile, S)
    assert S % tile == 0, (name, S, tile)
    nr, npar, no, na = len(rows), len(params), len(outs), len(accs)

    def body(*refs):
        rin, pin = refs[:nr], refs[nr:nr + npar]
        oref, aref = refs[nr + npar:nr + npar + no], refs[nr + npar + no:]
        res = fn(*[r[...] for r in rin], *[p[...] for p in pin])
        if not isinstance(res, (tuple, list)):
            res = (res,)
        assert len(res) == no + na, (name, len(res), no, na)
        for ref, val in zip(oref, res[:no]):
            ref[...] = val.astype(ref.dtype)
        if na:
            @pl.when(pl.program_id(0) == 0)
            def _():
                for ref in aref:
                    ref[...] = jnp.zeros(ref.shape, ref.dtype)

            for ref, val in zip(aref, res[no:]):
                ref[...] += jnp.broadcast_to(val, ref.shape).astype(ref.dtype)

    def whole(shape):
        nd = len(shape)
        return pl.BlockSpec(tuple(shape), lambda i: (0,) * nd)

    in_specs = ([pl.BlockSpec((tile, r.shape[1]), lambda i: (i, 0)) for r in rows]
                + [whole(p.shape) for p in params])
    out_specs = ([pl.BlockSpec((tile, n), lambda i: (i, 0)) for n, _ in outs]
                 + [whole(s) for s in accs])
    out_shape = ([jax.ShapeDtypeStruct((S, n), dt) for n, dt in outs]
                 + [jax.ShapeDtypeStruct(tuple(s), F32) for s in accs])
    res = pl.pallas_call(
        body, name=name, grid=(S // tile,), in_specs=in_specs, out_specs=out_specs,
        out_shape=out_shape,
        compiler_params=pltpu.CompilerParams(
            dimension_semantics=("arbitrary",), vmem_limit_bytes=VMEM_LIMIT),
    )(*rows, *params)
    return res


def _modulate(x, sc, sh):
    return x * (1.0 + sc) + sh


def _gmlp_consts():
    lane = lax.broadcasted_iota(jnp.int32, (1, G_WIDTH), 1)
    gmask = [(lane // (G_WIDTH // G_GROUPS) == g).astype(F32) for g in range(G_GROUPS)]
    tril = (lax.broadcasted_iota(jnp.int32, (CHUNK, CHUNK), 0)
            >= lax.broadcasted_iota(jnp.int32, (CHUNK, CHUNK), 1))
    return gmask, tril


def _gmlp_core(gmask, tril, gsel, zu, zv, bu, bv, g, b, bst, *ws):
    u = _gelu(zu + bu)
    v = _ln(_gelu(zv + bv), g, b, LN_EPS)
    s = _pdot(bst, gsel)
    for gi in range(G_GROUPS):
        s = s + _bdot(jnp.where(tril, ws[gi], 0.0), v * gmask[gi])
    return u * s


def _pre_core(rowmask, bd, zr, zk, zv, zl, pr, pk, pv, pq, br, bk, bv, bl, mr, mk, mv, ml,
              w0, wd, a0, wa, wg, k_k, k_a):
    def mix(z, p, b, mu):
        zz = z + b
        return zz + ((p + b) * rowmask - zz) * mu

    r, k, v, l = mix(zr, pr, br, mr), mix(zk, pk, bk, mk), mix(zv, pv, bv, mv), mix(zl, pq, bl, ml)
    w_log = -_softplus(-(w0 + _bdot(jnp.tanh(l), wd))) - 0.5
    decay = jnp.exp(-jnp.exp(w_log))
    a = _sigmoid(a0 + _bdot(l, wa))
    g = _bdot(_sigmoid(l), wg)
    kk = k * k_k
    kkn = kk / jnp.maximum(jnp.sqrt(_pdot(kk * kk, bd)), 1e-12)
    k2 = k * (1.0 + (a - 1.0) * k_a)
    return r, decay, k2, v, -kkn, kkn * a, g


def _post_core(bd, y, r, k2, v, g, gain, bias, rk):
    inv = 1.0 / R_HEAD
    mu = _pdot(y, bd) * inv
    yc = y - mu
    var = _pdot(yc * yc, bd) * inv
    yn = yc * lax.rsqrt(var + GN_EPS) * gain + bias
    bonus = _pdot(r * k2 * rk, bd) * v
    return (yn + bonus) * g


def _merge_core(pa, pb, ga, gb, bga, bgb):
    return _sigmoid(ga + bga) * pa + _sigmoid(gb + bgb) * pb


def _ln1_core(x, mix, gt1, bout, g, b, sc2, sh2):
    h1 = _ln(ALPHA * x + gt1 * (mix + bout), g, b, LN_EPS)
    return h1, h1 * (1.0 + sc2) + sh2


def _ln2_loss_core(tgt, h1, ff, gt2, bff2, g, b):
    out = _ln(ALPHA * h1 + gt2 * (ff + bff2), g, b, LN_EPS)
    err = out - tgt
    return 0.5 * jnp.sum(err * err) * (1.0 / D_MODEL)


def _scan_consts():
    sub = lax.broadcasted_iota(jnp.int32, (R_HEAD, LANES), 0)
    lane = lax.broadcasted_iota(jnp.int32, (R_HEAD, LANES), 1)
    return lane < R_HEAD, sub == (lane & (R_HEAD - 1))


def _seg_sum(lo, xb):
    s_lo = jnp.sum(jnp.where(lo, xb, 0.0), axis=1, keepdims=True)
    s_hi = jnp.sum(jnp.where(lo, 0.0, xb), axis=1, keepdims=True)
    return jnp.where(lo, s_lo, s_hi)


def _seg_dot(lo, s, row):
    lo_row = lo[0:1, :]
    s_lo = jnp.sum(s * jnp.where(lo_row, row, 0.0), axis=1, keepdims=True)
    s_hi = jnp.sum(s * jnp.where(lo_row, 0.0, row), axis=1, keepdims=True)
    return jnp.where(lo, s_lo, s_hi)


def _row_of_col(eye, colb):
    return jnp.sum(jnp.where(eye, colb, 0.0), axis=0, keepdims=True)


def _head_ones():
    i = lax.broadcasted_iota(jnp.int32, (LANES, LANES), 0) // R_HEAD
    j = lax.broadcasted_iota(jnp.int32, (LANES, LANES), 1) // R_HEAD
    return (i == j).astype(jnp.bfloat16)


N_SPLIT = 1


def _split(x):
    parts, r = [], x
    for u in range(N_SPLIT):
        p = r.astype(jnp.bfloat16)
        parts.append(p)
        if u + 1 < N_SPLIT:
            r = r - p.astype(F32)
    return parts


def _ones_dot(parts, ones, n):
    res = lax.dot_general(jnp.concatenate(parts, axis=0), ones, (((1,), (0,)), ((), ())),
                          preferred_element_type=F32)
    out = []
    for m in range(n):
        t = [res[(N_SPLIT * m + u) * R_HEAD:(N_SPLIT * m + u + 1) * R_HEAD] for u in range(N_SPLIT)]
        out.append(functools.reduce(lambda p, q: p + q, t))
    return out


def _seg_sums_mxu(ones, mats):
    return _ones_dot([p for m in mats for p in _split(m)], ones, len(mats))


def _cols_of_rows_mxu(eye, ones, rows8):
    terms = [p.astype(F32) for p in _split(rows8)]
    parts = [jnp.where(eye, jnp.broadcast_to(t[i:i + 1, :], eye.shape), 0.0).astype(jnp.bfloat16)
             for i in range(rows8.shape[0]) for t in terms]
    return _ones_dot(parts, ones, rows8.shape[0])


N_BLK = R_WIDTH // LANES
ROW_GROUP = 8


def _scan_fwd(r, w, k, v, a, b):
    S = r.shape[0]
    T = min(SCAN_T, S)
    nchunk = S // T

    def body(r_ref, w_ref, k_ref, v_ref, a_ref, b_ref, y_ref, sv_ref, st_ref):
        lo, eye = _scan_consts()
        ones = _head_ones()

        @pl.when(pl.program_id(0) == 0)
        def _():
            st_ref[...] = jnp.zeros(st_ref.shape, F32)

        sub8 = lax.broadcasted_iota(jnp.int32, (ROW_GROUP, LANES), 0)

        def group(gi, state):
            base = pl.multiple_of(gi * ROW_GROUP, ROW_GROUP)
            state = list(state)
            sls = [slice(q * LANES, (q + 1) * LANES) for q in range(N_BLK)]
            ld = lambda ref: [ref[pl.ds(base, ROW_GROUP), sl] for sl in sls]
            r8, w8, k8, v8, a8, b8 = ld(r_ref), ld(w_ref), ld(k_ref), ld(v_ref), ld(a_ref), ld(b_ref)
            vb = [_cols_of_rows_mxu(eye, ones, v8[q]) for q in range(N_BLK)]
            an = [pltpu.roll(a8[q], ROW_GROUP - 1, 0) for q in range(N_BLK)]
            ap = [w8[q] * an[q] for q in range(N_BLK)]
            beta = [_seg_sum(lo[:ROW_GROUP], b8[q] * an[q]) for q in range(N_BLK)]
            kappa = [_seg_sum(lo[:ROW_GROUP], k8[q] * an[q]) for q in range(N_BLK)]
            y8 = [jnp.zeros((ROW_GROUP, LANES), F32)] * N_BLK

            def emit_y(i, y8_):
                ycol = _seg_sums_mxu(ones, [state[q] * r8[q][i:i + 1, :] for q in range(N_BLK)])
                return [jnp.where(sub8 == i, _row_of_col(eye, ycol[q]), y8_[q]) for q in range(N_BLK)]

            for i in range(0, ROW_GROUP, 2):
                row = lambda t8, d=0: t8[i + d:i + d + 1, :]
                sa1 = [None] * N_BLK
                for q in range(N_BLK):
                    s = state[q]
                    sv_ref[base + i, :, sls[q]] = s
                    sa0 = _seg_dot(lo, s, row(a8[q]))
                    nxt = _seg_dot(lo, s, row(ap[q]))
                    sa1[q] = nxt + row(beta[q]) * sa0 + row(kappa[q]) * vb[q][i]
                    state[q] = s * row(w8[q]) + sa0 * row(b8[q]) + vb[q][i] * row(k8[q])
                y8 = emit_y(i, y8)
                for q in range(N_BLK):
                    s = state[q]
                    sv_ref[base + i + 1, :, sls[q]] = s
                    state[q] = s * row(w8[q], 1) + sa1[q] * row(b8[q], 1) + vb[q][i + 1] * row(k8[q], 1)
                y8 = emit_y(i + 1, y8)
            for q in range(N_BLK):
                y_ref[pl.ds(base, ROW_GROUP), sls[q]] = y8[q]
            return tuple(state)

        init = tuple(st_ref[:, q * LANES:(q + 1) * LANES] for q in range(N_BLK))
        fin = lax.fori_loop(0, T // ROW_GROUP, group, init)
        for q in range(N_BLK):
            st_ref[:, q * LANES:(q + 1) * LANES] = fin[q]

    blk = pl.BlockSpec((T, R_WIDTH), lambda i: (i, 0))
    return pl.pallas_call(
        body, name="scan_fwd", grid=(nchunk,), in_specs=[blk] * 6,
        out_specs=[blk, pl.BlockSpec((T, R_HEAD, R_WIDTH), lambda i: (i, 0, 0))],
        out_shape=[jax.ShapeDtypeStruct((S, R_WIDTH), F32),
                   jax.ShapeDtypeStruct((S, R_HEAD, R_WIDTH), F32)],
        scratch_shapes=[pltpu.VMEM((R_HEAD, R_WIDTH), F32)],
        compiler_params=pltpu.CompilerParams(
            dimension_semantics=("arbitrary",), vmem_limit_bytes=VMEM_LIMIT),
    )(r, w, k, v, a, b)


def _scan_bwd(r, w, k, v, a, b, states, dy):
    S = r.shape[0]
    T = min(SCAN_T, S)
    nchunk = S // T

    def body(r_ref, w_ref, k_ref, v_ref, a_ref, b_ref, sv_ref, dy_ref,
             dr_ref, dw_ref, dk_ref, dv_ref, da_ref, db_ref, ds_ref):
        lo, eye = _scan_consts()
        ones = _head_ones()

        @pl.when(pl.program_id(0) == 0)
        def _():
            ds_ref[...] = jnp.zeros(ds_ref.shape, F32)

        sub8 = lax.broadcasted_iota(jnp.int32, (ROW_GROUP, LANES), 0)

        def bgroup(n, dstate, chunk_end):
            base = (T // ROW_GROUP - 1 - n) * ROW_GROUP
            if not isinstance(n, int):
                base = pl.multiple_of(base, ROW_GROUP)
            dstate = list(dstate)
            sls = [slice(q * LANES, (q + 1) * LANES) for q in range(N_BLK)]
            ld = lambda ref: [ref[pl.ds(base, ROW_GROUP), sl] for sl in sls]
            r8, w8, k8, v8, a8, b8, dy8 = (ld(r_ref), ld(w_ref), ld(k_ref), ld(v_ref), ld(a_ref), ld(b_ref),
                                           ld(dy_ref))
            vbs = [_cols_of_rows_mxu(eye, ones, v8[q]) for q in range(N_BLK)]
            dycs = [_cols_of_rows_mxu(eye, ones, dy8[q]) for q in range(N_BLK)]
            acc = [{n_: jnp.zeros((ROW_GROUP, LANES), F32) for n_ in ("r", "w", "k", "v", "a", "b")}
                   for _ in range(N_BLK)]
            for i in reversed(range(ROW_GROUP)):
                row = lambda t8: t8[i:i + 1, :]
                put = lambda q_, n_, val: acc[q_].__setitem__(n_, jnp.where(sub8 == i, val, acc[q_][n_]))
                dks = []
                for q in range(N_BLK):
                    sp, vb = sv_ref[base + i, :, sls[q]], vbs[q][i]
                    wr, ar, br, kr, rr = row(w8[q]), row(a8[q]), row(b8[q]), row(k8[q]), row(r8[q])
                    sa = _seg_dot(lo, sp, ar)
                    if chunk_end and i == ROW_GROUP - 1:
                        st = sp * wr + sa * br + vb * kr
                    else:
                        st = sv_ref[base + i + 1, :, sls[q]]
                    dyc = dycs[q][i]
                    ds = dstate[q] + dyc * rr
                    put(q, "r", _colsum(st * dyc))
                    put(q, "w", _colsum(ds * sp))
                    put(q, "b", _colsum(ds * sa))
                    put(q, "k", _colsum(ds * vb))
                    dsa = _seg_dot(lo, ds, br)
                    dks.append(ds * kr)
                    put(q, "a", _colsum(sp * dsa))
                    dstate[q] = ds * wr + dsa * ar
                dvc = _seg_sums_mxu(ones, dks)
                for q in range(N_BLK):
                    put(q, "v", _row_of_col(eye, dvc[q]))
            for q in range(N_BLK):
                for n_, ref in (("r", dr_ref), ("w", dw_ref), ("k", dk_ref), ("v", dv_ref), ("a", da_ref), ("b", db_ref)):
                    ref[pl.ds(base, ROW_GROUP), sls[q]] = acc[q][n_]
            return tuple(dstate)

        fin = bgroup(0, tuple(ds_ref[:, q * LANES:(q + 1) * LANES] for q in range(N_BLK)), True)
        fin = lax.fori_loop(1, T // ROW_GROUP, lambda n, d: bgroup(n, d, False), fin)
        for q in range(N_BLK):
            ds_ref[:, q * LANES:(q + 1) * LANES] = fin[q]

    blk = pl.BlockSpec((T, R_WIDTH), lambda i: (nchunk - 1 - i, 0))
    svb = pl.BlockSpec((T, R_HEAD, R_WIDTH), lambda i: (nchunk - 1 - i, 0, 0))
    return pl.pallas_call(
        body, name="scan_bwd", grid=(nchunk,), in_specs=[blk] * 6 + [svb, blk],
        out_specs=[blk] * 6,
        out_shape=[jax.ShapeDtypeStruct((S, R_WIDTH), F32)] * 6,
        scratch_shapes=[pltpu.VMEM((R_HEAD, R_WIDTH), F32)],
        compiler_params=pltpu.CompilerParams(
            dimension_semantics=("arbitrary",), vmem_limit_bytes=VMEM_LIMIT),
    )(r, w, k, v, a, b, states, dy)


def _pad_in_cols(t, axis):
    cut = 2 * G_WIDTH + R_COLS
    lo, hi = lax.slice_in_dim(t, 0, cut, axis=axis), lax.slice_in_dim(t, cut, t.shape[axis], axis=axis)
    zshape = list(t.shape)
    zshape[axis] = LORA_PAD - LORA
    return jnp.concatenate([lo, jnp.zeros(zshape, t.dtype), hi], axis=axis)


def _unpad_in_cols(t, axis):
    cut = 2 * G_WIDTH + R_COLS
    return jnp.concatenate([lax.slice_in_dim(t, 0, cut, axis=axis),
                            lax.slice_in_dim(t, cut + LORA_PAD - LORA, t.shape[axis], axis=axis)], axis=axis)


def _pad_rows(t, lo, n):
    return jnp.zeros((n, t.shape[1]), t.dtype).at[lo:lo + t.shape[0]].set(t)


def _join_cols(w3):
    p, k, n = w3.shape
    return jnp.transpose(w3, (1, 0, 2)).reshape(k, p * n)


def _split_cols(w):
    k, n = w.shape
    return jnp.transpose(w.reshape(k, N_CHIPS, n // N_CHIPS), (1, 0, 2))


def _ada_dw(ccol, dmod):
    n = dmod.shape[1] // N_CHIPS
    tile = 256

    def body(c_ref, d_ref, o_ref):
        cc = c_ref[...]
        o_ref[0] = ((cc * _sigmoid(cc)) * d_ref[...]).astype(o_ref.dtype)

    return pl.pallas_call(
        body, name="ada_dw", grid=(N_CHIPS, D_MODEL // tile),
        in_specs=[pl.BlockSpec((tile, 1), lambda q, i: (i, 0)), pl.BlockSpec((1, n), lambda q, i: (0, q))],
        out_specs=pl.BlockSpec((1, tile, n), lambda q, i: (q, i, 0)),
        out_shape=jax.ShapeDtypeStruct((N_CHIPS, D_MODEL, n), WIRE_DT),
        compiler_params=pltpu.CompilerParams(dimension_semantics=("parallel", "parallel")),
    )(ccol, dmod)


def _shift_down(t):
    return jnp.concatenate([jnp.zeros((1, t.shape[1]), t.dtype), t[:-1]], axis=0)


def _shift_up(t):
    return jnp.concatenate([t[1:], jnp.zeros((1, t.shape[1]), t.dtype)], axis=0)


def _local_step(x, c, tgt, W):
    S = x.shape[0]
    bf = MXU_DT
    G = {}

    hl = np.arange(R_WIDTH) // R_HEAD
    bd = jnp.asarray(hl[:, None] == hl[None, :], jnp.bfloat16)
    gsel = jnp.asarray(np.arange(LANES)[:, None] == (np.arange(G_WIDTH) // (G_WIDTH // G_GROUPS))[None, :],
                       jnp.bfloat16)
    w_in_p = _pad_in_cols(_join_cols(W["w_in"]), 1)
    b_in_p = _pad_in_cols(W["b_in"], 1)
    c_g, c_r = 2 * G_WIDTH, 2 * G_WIDTH + R_COLS_PAD
    w_g, w_r, w_gate = w_in_p[:, :c_g], w_in_p[:, c_g:c_r], w_in_p[:, c_r:]
    b_g, b_r, b_gate = b_in_p[:, :c_g], b_in_p[:, c_g:c_r], b_in_p[:, c_r:]
    mu_p = jnp.concatenate([W["mu_shift"], jnp.zeros((1, LORA_PAD - LORA), F32)], axis=1)
    wd_p = _pad_rows(W["w_decay_up"].astype(F32), 0, LORA_PAD)
    wa_p = _pad_rows(W["w_aaa_up"].astype(F32), DECAY_LORA, LORA_PAD)
    wg_p = _pad_rows(W["w_gate_up"].astype(F32), DECAY_LORA + AAA_LORA, LORA_PAD)
    ws2 = W["w_spatial"].reshape(G_GROUPS * CHUNK, CHUNK)
    bst = jnp.zeros((CHUNK, LANES), F32).at[:, :G_GROUPS].set(W["b_spatial"].T)
    rk = W["r_k"].reshape(1, R_WIDTH)

    c8 = jnp.broadcast_to(c, (8, D_MODEL))
    (ca8,) = _rows("ada_silu", lambda cc: cc * _sigmoid(cc), [c8], [], [(D_MODEL, bf)], tile=8)
    mod_raw = _mm("ada_mm", ca8, W["w_ada"], "nn")
    (mod8,) = _rows("ada_bias", lambda m, bb: m + bb, [mod_raw], [W["b_ada"]], [(6 * D_MODEL, F32)], tile=8)
    sh1, sc1, gt1, sh2, sc2, gt2 = [mod8[0:1, i * D_MODEL:(i + 1) * D_MODEL] for i in range(6)]

    (h,) = _rows("mod1", _modulate, [x], [sc1, sh1], [(D_MODEL, bf)])
    proj_g = _mm("proj_g", h, w_g, "nn")
    proj_r = _mm("proj_r", h, w_r, "nn")
    proj_gate = _mm("proj_gate", h, w_gate, "nn")

    def split2(t):
        return t[:, :G_WIDTH], t[:, G_WIDTH:]

    def gmlp_fwd(z, bz, g, b, bst_, gsel_, ws_):
        gmask, tril = _gmlp_consts()
        (zu, zv), (bu, bv) = split2(z), split2(bz)
        wsl = [ws_[i * CHUNK:(i + 1) * CHUNK] for i in range(G_GROUPS)]
        return _gmlp_core(gmask, tril, gsel_, zu, zv, bu, bv, g, b, bst_, *wsl)

    (y_a,) = _rows("gmlp_fwd", gmlp_fwd, [proj_g], [b_g, W["g_ln_v"], W["b_ln_v"], bst, gsel, ws2],
                   [(G_WIDTH, bf)], tile=CHUNK)

    r_cuts = (0, R_WIDTH, 2 * R_WIDTH, 3 * R_WIDTH, R_COLS_PAD)

    def split4(t):
        return [t[:, r_cuts[i]:r_cuts[i + 1]] for i in range(4)]

    tile_pre = min(256, S)

    def rowmask_of():
        grow = pl.program_id(0) * tile_pre + lax.broadcasted_iota(jnp.int32, (tile_pre, 1), 0)
        return (grow > 0).astype(F32)

    pre_params = [b_r, mu_p, W["w0"], wd_p, W["a0"], wa_p, wg_p, W["k_k"], W["k_a"], bd]

    def pre_fwd(z, p, bz, mu, w0, wd, a0, wa, wg, k_k, k_a, bd_):
        return _pre_core(rowmask_of(), bd_, *split4(z), *split4(p), *split4(bz), *split4(mu),
                         w0, wd, a0, wa, wg, k_k, k_a)

    proj_r_prev = _shift_down(proj_r)
    s_r, s_w, s_k, s_v, s_a, s_b, s_g = _rows(
        "rwkv_pre_fwd", pre_fwd, [proj_r, proj_r_prev], pre_params, [(R_WIDTH, F32)] * 7, tile=tile_pre)
    y_scan, states = _scan_fwd(s_r, s_w, s_k, s_v, s_a, s_b)

    def post_fwd(y, r, k2, v, g, gain, bias, rk_, bd_):
        return _post_core(bd_, y, r, k2, v, g, gain, bias, rk_)

    post_params = [W["gn_gain"], W["gn_bias"], rk, bd]
    (y_b,) = _rows("rwkv_post_fwd", post_fwd, [y_scan, s_r, s_k, s_v, s_g], post_params, [(R_WIDTH, bf)])
    p_a = _mm("branch_a", y_a, W["w_branch_a"], "nn")
    p_b = _mm("branch_b", y_b, W["w_branch_b"], "nn")

    def merge_fwd(pa, pb, gz, bgz):
        return _merge_core(pa, pb, gz[:, :D_MODEL], gz[:, D_MODEL:], bgz[:, :D_MODEL], bgz[:, D_MODEL:])

    (merged,) = _rows("merge_fwd", merge_fwd, [p_a, p_b, proj_gate], [b_gate], [(D_MODEL, bf)])
    mix = _mm("out_proj", merged, W["w_out"], "nn")
    ln1_params = [gt1, W["b_out"], W["ln1_g"], W["ln1_b"], sc2, sh2]
    h1, h2 = _rows("ln1_fwd", _ln1_core, [x, mix], ln1_params, [(D_MODEL, F32), (D_MODEL, bf)])

    a1 = _mm("ff1", h2, W["w_ff1"], "nn")
    (act,) = _rows("ff_act", lambda z, bb: jnp.square(jnp.maximum(z + bb, 0.0)), [a1], [W["b_ff1"]], [(D_FF, bf)])
    ff = _mm("ff2", act, W["w_ff2"], "nn")

    def ln2_loss(h1_, ff_, tg, gt2_, bff2, g, b):
        loss, vjp = jax.vjp(functools.partial(_ln2_loss_core, tg), h1_, ff_, gt2_, bff2, g, b)
        return vjp(jnp.ones((), F32)) + (loss,)

    ln2_params = [gt2, W["b_ff2"], W["ln2_g"], W["ln2_b"]]
    dh1, dff, dgt2, G["b_ff2"], G["ln2_g"], G["ln2_b"], loss_acc = _rows(
        "ln2_loss", ln2_loss, [h1, ff, tgt], ln2_params, [(D_MODEL, F32), (D_MODEL, bf)],
        accs=[(1, D_MODEL)] * 4 + [(1, LANES)])
    loss = loss_acc[0, 0]

    dact = _mm("ff2_dx", dff, W["w_ff2"], "nt")
    G["w_ff2"] = _mm("ff2_dw", act, dff, "tn", WIRE_DT).reshape(N_CHIPS, D_FF // N_CHIPS, D_MODEL)

    def act_bwd(z, da, bb):
        d = da * 2.0 * jnp.maximum(z + bb, 0.0)
        return d, _colsum(d)

    da1, G["b_ff1"] = _rows("ff_act_bwd", act_bwd, [a1, dact], [W["b_ff1"]], [(D_FF, bf)], accs=[(1, D_FF)])
    dh2 = _mm("ff1_dx", da1, W["w_ff1"], "nt")
    G["w_ff1"] = _mm("ff1_dw", h2, da1, "tn", WIRE_DT, out_split=N_CHIPS)

    def ln1_bwd(x_, mix_, dh1_, dh2_, *ps):
        _, vjp = jax.vjp(_ln1_core, x_, mix_, *ps)
        return vjp((dh1_, dh2_))

    dx_res, dmix, dgt1, G["b_out"], G["ln1_g"], G["ln1_b"], dsc2, dsh2 = _rows(
        "ln1_bwd", ln1_bwd, [x, mix, dh1, dh2], ln1_params, [(D_MODEL, F32), (D_MODEL, bf)],
        accs=[(1, D_MODEL)] * 6)

    dmerged = _mm("out_proj_dx", dmix, W["w_out"], "nt")
    G["w_out"] = _mm("out_proj_dw", merged, dmix, "tn", WIRE_DT).reshape(N_CHIPS, D_MODEL // N_CHIPS, D_MODEL)

    def merge_bwd(pa, pb, gz, dm, bgz):
        args = (pa.astype(F32), pb.astype(F32), gz[:, :D_MODEL], gz[:, D_MODEL:], bgz[:, :D_MODEL], bgz[:, D_MODEL:])
        _, vjp = jax.vjp(_merge_core, *args)
        dpa, dpb, dga, dgb, dbga, dbgb = vjp(dm)
        return dpa, dpb, jnp.concatenate([dga, dgb], axis=1), jnp.concatenate([dbga, dbgb], axis=1)

    dp_a, dp_b, dgates, db_gate = _rows(
        "merge_bwd", merge_bwd, [p_a, p_b, proj_gate, dmerged], [b_gate],
        [(D_MODEL, bf), (D_MODEL, bf), (2 * D_MODEL, bf)], accs=[(1, 2 * D_MODEL)])
    dy_a = _mm("branch_a_dx", dp_a, W["w_branch_a"], "nt")
    G["w_branch_a"] = _mm("branch_a_dw", y_a, dp_a, "tn", WIRE_DT, out_split=N_CHIPS)
    dy_b = _mm("branch_b_dx", dp_b, W["w_branch_b"], "nt")
    G["w_branch_b"] = _mm("branch_b_dw", y_b, dp_b, "tn", WIRE_DT, out_split=N_CHIPS)

    def post_bwd(y, r, k2, v, g, dyb, gain, bias, rk_, bd_):
        _, vjp = jax.vjp(functools.partial(_post_core, bd_), y, r, k2, v, g, gain, bias, rk_)
        return vjp(dyb)

    dy_scan, dr_p, dk_p, dv_p, dg_p, G["gn_gain"], G["gn_bias"], drk = _rows(
        "rwkv_post_bwd", post_bwd, [y_scan, s_r, s_k, s_v, s_g, dy_b], post_params,
        [(R_WIDTH, F32)] * 5, accs=[(1, R_WIDTH)] * 3)
    G["r_k"] = drk.reshape(R_HEADS, R_HEAD)
    dr_s, dw_s, dk_s, dv_s, da_s, db_s = _scan_bwd(s_r, s_w, s_k, s_v, s_a, s_b, states, dy_scan)

    def pre_bwd(z, p, dr1, dr2, dw, dk1, dk2, dv1, dv2, da, db, dg,
                bz, mu, w0, wd, a0, wa, wg, k_k, k_a, bd_):
        prim = (*split4(z), *split4(p), *split4(bz), *split4(mu), w0, wd, a0, wa, wg, k_k, k_a)
        _, vjp = jax.vjp(functools.partial(_pre_core, rowmask_of(), bd_), *prim)
        d = vjp((dr1 + dr2, dw, dk1 + dk2, dv1 + dv2, da, db, dg))
        cat = lambda parts: jnp.concatenate(parts, axis=1)
        return (cat(d[0:4]), cat(d[4:8]), cat(d[8:12]), cat(d[12:16])) + tuple(d[16:])

    dz_r, dprev, db_r, dmu_p, G["w0"], dwd_p, G["a0"], dwa_p, dwg_p, G["k_k"], G["k_a"] = _rows(
        "rwkv_pre_bwd", pre_bwd,
        [proj_r, proj_r_prev, dr_s, dr_p, dw_s, dk_s, dk_p, dv_s, dv_p, da_s, db_s, dg_p],
        pre_params, [(R_COLS_PAD, bf)] * 2,
        accs=[(1, R_COLS_PAD), (1, R_COLS_PAD), (1, R_WIDTH), (LORA_PAD, R_WIDTH), (1, R_WIDTH),
              (LORA_PAD, R_WIDTH), (LORA_PAD, R_WIDTH), (1, R_WIDTH), (1, R_WIDTH)],
        tile=tile_pre)
    G["mu_shift"] = dmu_p[:, :R_COLS]
    G["w_decay_up"] = dwd_p[:DECAY_LORA]
    G["w_aaa_up"] = dwa_p[DECAY_LORA:DECAY_LORA + AAA_LORA]
    G["w_gate_up"] = dwg_p[DECAY_LORA + AAA_LORA:LORA]

    def gmlp_bwd(z, dya, bz, g, b, bst_, gsel_, ws_):
        gmask, tril = _gmlp_consts()
        (zu, zv), (bu, bv) = split2(z), split2(bz)
        wsl = [ws_[i * CHUNK:(i + 1) * CHUNK] for i in range(G_GROUPS)]
        _, vjp = jax.vjp(functools.partial(_gmlp_core, gmask, tril, gsel_), zu, zv, bu, bv, g, b, bst_, *wsl)
        d = vjp(dya)
        return (jnp.concatenate(d[0:2], axis=1), jnp.concatenate(d[2:4], axis=1), d[4], d[5], d[6],
                jnp.concatenate(d[7:], axis=0))

    dz_g, db_g, G["g_ln_v"], G["b_ln_v"], dbst, dws2 = _rows(
        "gmlp_bwd", gmlp_bwd, [proj_g, dy_a], [b_g, W["g_ln_v"], W["b_ln_v"], bst, gsel, ws2],
        [(2 * G_WIDTH, bf)],
        accs=[(1, 2 * G_WIDTH), (1, G_WIDTH), (1, G_WIDTH), (CHUNK, LANES), (G_GROUPS * CHUNK, CHUNK)],
        tile=CHUNK)
    G["w_spatial"] = dws2.reshape(G_GROUPS, CHUNK, CHUNK)
    G["b_spatial"] = dbst[:, :G_GROUPS].T

    def dproj_cat(dzg, dzr, dpv, dgz):
        return jnp.concatenate([dzg, (dzr.astype(F32) + dpv.astype(F32)).astype(dzg.dtype), dgz], axis=1)

    (dproj,) = _rows("dproj_cat", dproj_cat, [dz_g, dz_r, _shift_up(dprev), dgates], [],
                     [(2 * G_WIDTH + R_COLS_PAD + 2 * D_MODEL, bf)])
    dh = _mm("proj_dx", dproj, w_in_p, "nt", tk=2432)
    G["w_in"] = _split_cols(_unpad_in_cols(_mm("proj_dw", h, dproj, "tn", WIRE_DT, tm=512, tn=2432), 1))
    G["b_in"] = _unpad_in_cols(jnp.concatenate([db_g, db_r, db_gate], axis=1), 1)

    def mod1_bwd(x_, dh_, dxr, sc):
        return dh_ * (1.0 + sc) + dxr, _colsum(dh_ * x_), _colsum(dh_)

    grad_x, dsc1, dsh1 = _rows("mod1_bwd", mod1_bwd, [x, dh, dx_res], [sc1], [(D_MODEL, F32)],
                               accs=[(1, D_MODEL)] * 2)

    dmod = jnp.concatenate([dsh1, dsc1, dgt1, dsh2, dsc2, dgt2], axis=1)
    G["b_ada"] = dmod
    G["w_ada"] = _ada_dw(c.reshape(D_MODEL, 1), dmod)
    return loss, grad_x, G


BIG = (("w_ada", (D_MODEL, 6 * D_MODEL), 1), ("w_in", (D_MODEL, 2 * G_WIDTH + R_COLS + 2 * D_MODEL), 1),
       ("w_branch_a", (G_WIDTH, D_MODEL), 1), ("w_branch_b", (R_WIDTH, D_MODEL), 1),
       ("w_out", (D_MODEL, D_MODEL), 0), ("w_ff1", (D_MODEL, D_FF), 1), ("w_ff2", (D_FF, D_MODEL), 0))
LORAS = (("w_decay_up", (DECAY_LORA, R_WIDTH), 1), ("w_aaa_up", (AAA_LORA, R_WIDTH), 1),
         ("w_gate_up", (GATE_LORA, R_WIDTH), 1))
SHARDED = BIG + LORAS
SMALL = (("b_ada", (1, 6 * D_MODEL)), ("b_in", (1, 2 * G_WIDTH + R_COLS + 2 * D_MODEL)),
         ("g_ln_v", (1, G_WIDTH)), ("b_ln_v", (1, G_WIDTH)), ("w_spatial", (G_GROUPS, CHUNK, CHUNK)),
         ("b_spatial", (G_GROUPS, CHUNK)), ("mu_shift", (1, R_COLS)), ("w0", (1, R_WIDTH)),
         ("a0", (1, R_WIDTH)), ("k_k", (1, R_WIDTH)), ("k_a", (1, R_WIDTH)), ("r_k", (R_HEADS, R_HEAD)),
         ("gn_gain", (1, R_WIDTH)), ("gn_bias", (1, R_WIDTH)), ("b_out", (1, D_MODEL)),
         ("ln1_g", (1, D_MODEL)), ("ln1_b", (1, D_MODEL)), ("b_ff1", (1, D_FF)), ("b_ff2", (1, D_MODEL)),
         ("ln2_g", (1, D_MODEL)), ("ln2_b", (1, D_MODEL)))
WEIGHT_ORDER = ("w_ada", "b_ada", "w_in", "b_in", "g_ln_v", "b_ln_v", "w_spatial", "b_spatial", "mu_shift",
                "w0", "w_decay_up", "a0", "w_aaa_up", "w_gate_up", "k_k", "k_a", "r_k", "gn_gain", "gn_bias",
                "w_branch_a", "w_branch_b", "w_out", "b_out", "ln1_g", "ln1_b", "w_ff1", "b_ff1", "w_ff2",
                "b_ff2", "ln2_g", "ln2_b")
N_CHIPS = 4


def _shard_shape(shape, axis):
    s = list(shape)
    s[axis] //= N_CHIPS
    return tuple(s)


def _numel(shape):
    return int(np.prod(shape))


def _round_up(n, q):
    return -(-n // q) * q


N_LORA = sum(_numel(_shard_shape(s, a)) for _, s, a in LORAS)
N_SMALL = sum(_numel(s) for _, s in SMALL)
ROWS_SW = _round_up(N_LORA, PACK_Q) // PACK_W
ROWS_SG = _round_up(N_LORA + N_SMALL, PACK_Q) // PACK_W


def _pack_small(loras, small, rows, dtype):
    parts = [loras[n] for n, _, _ in LORAS] + ([small[n] for n, _ in SMALL] if small is not None else [])
    flat = jnp.concatenate([p.reshape(-1).astype(dtype) for p in parts])
    flat = jnp.concatenate([flat, jnp.zeros((rows * PACK_W - flat.shape[0],), dtype)])
    return flat.reshape(rows, PACK_W)


def _unpack_small(pack, with_small):
    flat = pack.reshape(-1)
    out, off = {}, 0
    for n, s, a in LORAS:
        ss = _shard_shape(s, a)
        out[n] = flat[off:off + _numel(ss)].reshape(ss)
        off += _numel(ss)
    if with_small:
        for n, s in SMALL:
            out[n] = flat[off:off + _numel(s)].reshape(s)
            off += _numel(s)
    return out


def _pack_small_grads(G):
    segs = []
    for q in range(N_CHIPS):
        loras = {n: G[n][:, q * (s[1] // N_CHIPS):(q + 1) * (s[1] // N_CHIPS)] for n, s, _ in LORAS}
        segs.append(_pack_small(loras, G, ROWS_SG, F32))
    return jnp.stack(segs)


ANY = pl.BlockSpec(memory_space=pl.ANY)
MESH = pl.DeviceIdType.MESH


def _place():
    x, y, c = lax.axis_index("x"), lax.axis_index("y"), lax.axis_index("c")
    chips = [(1 - x, y), (x, 1 - y), (1 - x, 1 - y)]
    return x, y, c, chips


def _remote(src, dst, send_sem, recv_sem, to):
    return pltpu.make_async_remote_copy(src_ref=src, dst_ref=dst, send_sem=send_sem, recv_sem=recv_sem,
                                        device_id=to, device_id_type=MESH)


def _all_gather(shards):
    n = len(shards)
    halves = [a.shape[0] // 2 for a in shards]

    def body(*refs):
        ins, outs = refs[:n], refs[n:2 * n]
        send_sems, recv_sems = refs[2 * n:]
        x, y, c, chips = _place()
        s = 2 * x + y
        sibling = (x, y, 1 - c)
        slot = [2 * chip[0] + chip[1] for chip in chips]
        first, passed = [], []
        for a in range(n):
            H = halves[a]
            mine = ins[a].at[pl.ds(c * H, H)]
            first += [_remote(mine, outs[a].at[2 * s + c], send_sems.at[6 * a + j], recv_sems.at[6 * a + j], (*chip, c))
                      for j, chip in enumerate(chips)]
        for cp in first:
            cp.start()
        for a in range(n):
            for j in range(3):
                landed = outs[a].at[2 * slot[j] + c]
                _remote(landed, landed, send_sems.at[6 * a + j], recv_sems.at[6 * a + j], sibling).wait_recv()
                cp = _remote(landed, landed, send_sems.at[6 * a + 3 + j], recv_sems.at[6 * a + 3 + j], sibling)
                cp.start()
                passed.append(cp)
        for a in range(n):
            for j in range(3):
                got = outs[a].at[2 * slot[j] + 1 - c]
                _remote(got, got, send_sems.at[6 * a + 3 + j], recv_sems.at[6 * a + 3 + j], sibling).wait_recv()
        for cp in first + passed:
            cp.wait_send()

    return pl.pallas_call(
        body, name="ag_weights", in_specs=[ANY] * n, out_specs=[ANY] * n,
        out_shape=[jax.ShapeDtypeStruct((2 * N_CHIPS, h, a.shape[1]), a.dtype) for a, h in zip(shards, halves)],
        scratch_shapes=[pltpu.SemaphoreType.DMA((6 * n,)), pltpu.SemaphoreType.DMA((6 * n,))],
    )(*shards)


def _rs_sibling_in(gps):
    n = len(gps)

    def body(*refs):
        ins, outs = refs[:n], refs[n:2 * n]
        send_sems, recv_sems = refs[2 * n:]
        x, y, c, _ = _place()
        cps = []
        for a in range(n):
            H = gps[a].shape[1] // 2
            cps += [_remote(ins[a].at[q, pl.ds((1 - c) * H, H)], outs[a].at[q], send_sems.at[N_CHIPS * a + q],
                            recv_sems.at[N_CHIPS * a + q], (x, y, 1 - c)) for q in range(N_CHIPS)]
        for cp in cps:
            cp.start()
        for cp in cps:
            cp.wait()

    return pl.pallas_call(
        body, name="rs_sibling_in", in_specs=[ANY] * n, out_specs=[ANY] * n,
        out_shape=[jax.ShapeDtypeStruct((N_CHIPS, g.shape[1] // 2, g.shape[2]), g.dtype) for g in gps],
        scratch_shapes=[pltpu.SemaphoreType.DMA((N_CHIPS * n,)), pltpu.SemaphoreType.DMA((N_CHIPS * n,))],
    )(*gps)


def _rs_add_own(name, gp, got, c_arr, tr=256):
    H, C = got.shape[1:]
    tr = _pick(H, tr, 8)
    nb = H // tr

    def body(c_ref, g_ref, r_ref, o_ref):
        o_ref[...] = (g_ref[...].astype(F32) + r_ref[...].astype(F32)).astype(o_ref.dtype)

    return pl.pallas_call(
        body, name="rs_add_own_" + name,
        grid_spec=pltpu.PrefetchScalarGridSpec(
            num_scalar_prefetch=1, grid=(N_CHIPS, nb),
            in_specs=[pl.BlockSpec((1, tr, C), lambda q, i, c_ref: (q, c_ref[0] * nb + i, 0)),
                      pl.BlockSpec((1, tr, C), lambda q, i, c_ref: (q, i, 0))],
            out_specs=pl.BlockSpec((1, tr, C), lambda q, i, c_ref: (q, i, 0))),
        out_shape=jax.ShapeDtypeStruct((N_CHIPS, H, C), gp.dtype),
        compiler_params=pltpu.CompilerParams(dimension_semantics=("arbitrary", "arbitrary")),
    )(c_arr, gp, got)


def _rs_chips(parts):
    n = len(parts)

    def body(*refs):
        ins, outs = refs[:n], refs[n:2 * n]
        send_sems, recv_sems = refs[2 * n:]
        x, y, c, chips = _place()
        s = 2 * x + y
        slot = [2 * chip[0] + chip[1] for chip in chips]
        cps = [_remote(ins[a].at[slot[j]], outs[a].at[s], send_sems.at[3 * a + j], recv_sems.at[3 * a + j], (*chips[j], c))
               for a in range(n) for j in range(3)]
        for cp in cps:
            cp.start()
        for a in range(n):
            for j in range(3):
                cps[3 * a + j].wait_send()
                got = outs[a].at[slot[j]]
                _remote(got, got, send_sems.at[3 * a + j], recv_sems.at[3 * a + j], (*chips[j], c)).wait_recv()

    return pl.pallas_call(
        body, name="rs_chips", in_specs=[ANY] * n, out_specs=[ANY] * n,
        out_shape=[jax.ShapeDtypeStruct(p.shape, p.dtype) for p in parts],
        scratch_shapes=[pltpu.SemaphoreType.DMA((3 * n,)), pltpu.SemaphoreType.DMA((3 * n,))],
    )(*parts)


def _rs_add_chips(name, part, slots, sc_arr, tr=128):
    H, C = slots.shape[1:]
    tr = _pick(H, tr, 8)
    nb = H // tr

    def body(sc_ref, p_ref, s_ref, o_ref):
        acc = None
        for q in range(N_CHIPS):
            term = lax.cond(sc_ref[0] == q, lambda: p_ref[0].astype(F32), lambda q=q: s_ref[q].astype(F32))
            acc = term if acc is None else acc + term
        o_ref[...] = acc

    return pl.pallas_call(
        body, name="rs_add_chips_" + name,
        grid_spec=pltpu.PrefetchScalarGridSpec(
            num_scalar_prefetch=1, grid=(nb,),
            in_specs=[pl.BlockSpec((1, tr, C), lambda i, sc: (sc[0], i, 0)),
                      pl.BlockSpec((N_CHIPS, tr, C), lambda i, sc: (0, i, 0))],
            out_specs=pl.BlockSpec((tr, C), lambda i, sc: (sc[1] * nb + i, 0))),
        out_shape=jax.ShapeDtypeStruct((2 * H, C), F32),
        compiler_params=pltpu.CompilerParams(dimension_semantics=("arbitrary",)),
    )(sc_arr, part, slots)


def _rs_sibling_out(wholes):
    n = len(wholes)

    def body(*refs):
        ins, outs = refs[:n], refs[n:2 * n]
        send_sems, recv_sems = refs[2 * n:]
        x, y, c, _ = _place()
        cps = []
        for a in range(n):
            H = wholes[a].shape[0] // 2
            cps.append(_remote(ins[a].at[pl.ds(c * H, H)], outs[a].at[pl.ds(c * H, H)], send_sems.at[a], recv_sems.at[a],
                               (x, y, 1 - c)))
        for cp in cps:
            cp.start()
        for a in range(n):
            H = wholes[a].shape[0] // 2
            cps[a].wait_send()
            got = outs[a].at[pl.ds((1 - c) * H, H)]
            _remote(got, got, send_sems.at[a], recv_sems.at[a], (x, y, 1 - c)).wait_recv()

    return pl.pallas_call(
        body, name="rs_sibling_out", in_specs=[ANY] * n, out_specs=[ANY] * n,
        out_shape=[jax.ShapeDtypeStruct(w.shape, w.dtype) for w in wholes],
        input_output_aliases={a: a for a in range(n)},
        scratch_shapes=[pltpu.SemaphoreType.DMA((n,)), pltpu.SemaphoreType.DMA((n,))],
    )(*wholes)


def _adamw(name, w, g, m, v):
    def fn(w_, g_, m_, v_):
        m2 = ADAM_B1 * m_ + (1.0 - ADAM_B1) * g_
        v2 = ADAM_B2 * v_ + (1.0 - ADAM_B2) * (g_ * g_)
        m_hat = m2 / (1.0 - ADAM_B1 ** ADAM_STEP)
        v_hat = v2 / (1.0 - ADAM_B2 ** ADAM_STEP)
        return -ADAM_LR * (m_hat / (jnp.sqrt(v_hat) + ADAM_EPS) + ADAM_WD * w_), m2, v2

    return _rows("adamw_" + name, fn, [w, g, m, v], [], [(w.shape[1], F32)] * 3, tile=_pick(w.shape[0], 256, 8))


def kernel(x, c, w_ada, b_ada, w_in, b_in, g_ln_v, b_ln_v, w_spatial, b_spatial, mu_shift, w0, w_decay_up, a0, w_aaa_up, w_gate_up, k_k, k_a, r_k, gn_gain, gn_bias, w_branch_a, w_branch_b, w_out, b_out, ln1_g, ln1_b, w_ff1, b_ff1, w_ff2, b_ff2, ln2_g, ln2_b, loss_target, m_w_ada, m_b_ada, m_w_in, m_b_in, m_g_ln_v, m_b_ln_v, m_w_spatial, m_b_spatial, m_mu_shift, m_w0, m_w_decay_up, m_a0, m_w_aaa_up, m_w_gate_up, m_k_k, m_k_a, m_r_k, m_gn_gain, m_gn_bias, m_w_branch_a, m_w_branch_b, m_w_out, m_b_out, m_ln1_g, m_ln1_b, m_w_ff1, m_b_ff1, m_w_ff2, m_b_ff2, m_ln2_g, m_ln2_b, v_w_ada, v_b_ada, v_w_in, v_b_in, v_g_ln_v, v_b_ln_v, v_w_spatial, v_b_spatial, v_mu_shift, v_w0, v_w_decay_up, v_a0, v_w_aaa_up, v_w_gate_up, v_k_k, v_k_a, v_r_k, v_gn_gain, v_gn_bias, v_w_branch_a, v_w_branch_b, v_w_out, v_b_out, v_ln1_g, v_ln1_b, v_w_ff1, v_b_ff1, v_w_ff2, v_b_ff2, v_ln2_g, v_ln2_b):
    args = dict(locals())
    local_shape = {n: _shard_shape(s, a) for n, s, a in SHARDED}
    local_shape.update(dict(SMALL))
    wts = {n: args[n].reshape(local_shape[n]) for n in WEIGHT_ORDER}
    mom = {n: args["m_" + n].reshape(local_shape[n]) for n in WEIGHT_ORDER}
    var = {n: args["v_" + n].reshape(local_shape[n]) for n in WEIGHT_ORDER}
    big = [n for n, _, _ in BIG]

    chip = 2 * lax.axis_index("x") + lax.axis_index("y")
    mine = [wts[n].astype(MXU_DT) for n in big] + [_pack_small(wts, None, ROWS_SW, MXU_DT)]
    gath = [lax.dynamic_update_slice(g.reshape((N_CHIPS,) + m.shape), m[None], (chip, 0, 0))
            for g, m in zip(_all_gather(mine), mine)]
    W = {n: wts[n] for n, _ in SMALL}
    for (n, s, a), g in zip(BIG, gath):
        W[n] = g if a == 1 else g.reshape(s)
    lora_q = [_unpack_small(gath[-1][q], False) for q in range(N_CHIPS)]
    for n, _, _ in LORAS:
        W[n] = jnp.concatenate([lora_q[q][n] for q in range(N_CHIPS)], axis=1)

    loss, grad_x, G = _local_step(x[0], c, loss_target[0], W)
    loss = lax.psum(loss, MESH_AXES)

    names = big + ["small"]
    gps = [G[n] for n in big] + [_pack_small_grads(G)]
    c_arr = lax.axis_index("c").astype(jnp.int32).reshape(1)
    parts = [_rs_add_own(n, g, r, c_arr) for n, g, r in zip(names, gps, _rs_sibling_in(gps))]
    sc_arr = jnp.stack([chip, lax.axis_index("c")]).astype(jnp.int32)
    segs = _rs_sibling_out([_rs_add_chips(n, p, s, sc_arr) for n, p, s in zip(names, parts, _rs_chips(parts))])

    out = {}
    for n, g in zip(big, segs[:-1]):
        out[n] = (g,) + tuple(_adamw(n, wts[n], g, mom[n], var[n]))
    packs = [_pack_small(t, t, ROWS_SG, F32) for t in (wts, mom, var)]
    small4 = [_unpack_small(t, True) for t in (segs[-1],) + tuple(_adamw("small", packs[0], segs[-1], packs[1], packs[2]))]
    res = [loss, grad_x[None]]
    for k in range(4):
        res += [(out[n][k] if n in out else small4[k][n]).reshape(args[n].shape) for n in WEIGHT_ORDER]
    return tuple(res)
```

```python
import functools

import numpy as np
import jax
import jax.numpy as jnp
from jax import lax
from jax.experimental import pallas as pl
from jax.experimental.pallas import tpu as pltpu

F32 = jnp.float32
MXU_DT = jnp.bfloat16
WIRE_DT = jnp.bfloat16

D_MODEL = 1024
G_GROUPS = 8
G_WIDTH = 512
CHUNK = 128
R_WIDTH = 512
R_HEAD = 64
R_HEADS = 8
DECAY_LORA = 32
AAA_LORA = 32
GATE_LORA = 96
LORA = DECAY_LORA + AAA_LORA + GATE_LORA
LORA_PAD = 256
R_COLS = 3 * R_WIDTH + LORA
R_COLS_PAD = 3 * R_WIDTH + LORA_PAD
D_FF = 4 * D_MODEL
ALPHA = 2.0 ** 0.25
LN_EPS = 1e-5
GN_EPS = 64e-5
ADAM_LR = 0.001
ADAM_B1 = 0.9
ADAM_B2 = 0.999
ADAM_EPS = 1e-08
ADAM_WD = 0.01
ADAM_STEP = 10

LANES = 128
PACK_W = 512
PACK_Q = 2 * 16 * PACK_W
VMEM_LIMIT = 48 * 1024 * 1024
SCAN_T = 64

MESH_AXES = ("x", "y", "c")


def _dg(a, b, dims):
    return lax.dot_general(a.astype(MXU_DT), b.astype(MXU_DT), (dims, ((), ())),
                           preferred_element_type=F32)


@jax.custom_vjp
def _bdot(a, b):
    return _dg(a, b, ((1,), (0,)))


def _bdot_fwd(a, b):
    return _bdot(a, b), (a, b)


def _bdot_bwd(res, g):
    a, b = res
    return (_dg(g, b, ((1,), (1,))).astype(a.dtype), _dg(a, g, ((0,), (0,))).astype(b.dtype))


_bdot.defvjp(_bdot_fwd, _bdot_bwd)


def _split_dot(x, m, dims):
    hi = x.astype(jnp.bfloat16)
    lo = (x - hi.astype(F32)).astype(jnp.bfloat16)
    dn = (dims, ((), ()))
    return (lax.dot_general(hi, m, dn, preferred_element_type=F32)
            + lax.dot_general(lo, m, dn, preferred_element_type=F32))


@jax.custom_vjp
def _pdot(x, m):
    return _split_dot(x, m, ((1,), (0,)))


def _pdot_fwd(x, m):
    return _pdot(x, m), m


def _pdot_bwd(m, g):
    return _split_dot(g, m, ((1,), (1,))), None


_pdot.defvjp(_pdot_fwd, _pdot_bwd)


def _sigmoid(x):
    return 1.0 / (1.0 + jnp.exp(-x))


def _softplus(x):
    return jnp.maximum(x, 0.0) + jnp.log(1.0 + jnp.exp(-jnp.maximum(x, -x)))


def _gelu(x):
    return 0.5 * x * (1.0 + jnp.tanh(0.7978845608028654 * (x + 0.044715 * (x * x * x))))


def _ln(x, g, b, eps):
    mu = jnp.mean(x, axis=-1, keepdims=True)
    xc = x - mu
    var = jnp.mean(xc * xc, axis=-1, keepdims=True)
    return xc * lax.rsqrt(var + eps) * g + b


def _colsum(x):
    return jnp.sum(x, axis=0, keepdims=True)


def _pick(n, target, q=LANES):
    if n <= target:
        return n
    best = None
    for t in range(q, target + 1, q):
        if n % t == 0:
            best = t
    assert best is not None, (n, target)
    return best


def _mm(name, a, b, mode, out_dtype=F32, out_split=1, tm=1024, tn=1024, tk=1024):
    bs = b.shape[0] if b.ndim == 3 else 1
    br, bc = b.shape[-2:]
    if mode == "nn":
        (M, K), K2, N = a.shape, br, bc * bs
    elif mode == "nt":
        (M, K), N, K2 = a.shape, br, bc * bs
    else:
        assert bs == 1
        (K, M), K2, N = a.shape, br, bc
    assert K == K2, (name, a.shape, b.shape, mode)
    n_piece = N // max(bs if mode == "nn" else 1, out_split)
    k_piece = K // (bs if mode == "nt" else 1)
    tm, tn, tk = _pick(M, tm, 8 if M < LANES else LANES), _pick(n_piece, tn), _pick(k_piece, tk)
    nk, npj, npk = K // tk, n_piece // tn, k_piece // tk
    dims = {"nn": ((1,), (0,)), "nt": ((1,), (1,)), "tn": ((0,), (0,))}[mode]

    def body(a_ref, b_ref, o_ref, acc_ref):
        k = pl.program_id(2)

        @pl.when(k == 0)
        def _():
            acc_ref[...] = jnp.zeros(acc_ref.shape, F32)

        acc_ref[...] += _dg(a_ref[...], b_ref[0] if bs > 1 else b_ref[...], dims)

        @pl.when(k == nk - 1)
        def _():
            if out_split > 1:
                o_ref[0] = acc_ref[...].astype(o_ref.dtype)
            else:
                o_ref[...] = acc_ref[...].astype(o_ref.dtype)

    if mode == "nn":
        a_spec = pl.BlockSpec((tm, tk), lambda i, j, k: (i, k))
        b_spec = (pl.BlockSpec((tk, tn), lambda i, j, k: (k, j)) if bs == 1 else
                  pl.BlockSpec((1, tk, tn), lambda i, j, k: (j // npj, k, j % npj)))
    elif mode == "nt":
        a_spec = pl.BlockSpec((tm, tk), lambda i, j, k: (i, k))
        b_spec = (pl.BlockSpec((tn, tk), lambda i, j, k: (j, k)) if bs == 1 else
                  pl.BlockSpec((1, tn, tk), lambda i, j, k: (k // npk, j, k % npk)))
    else:
        a_spec = pl.BlockSpec((tk, tm), lambda i, j, k: (k, i))
        b_spec = pl.BlockSpec((tk, tn), lambda i, j, k: (k, j))
    if out_split > 1:
        o_spec = pl.BlockSpec((1, tm, tn), lambda i, j, k: (j // npj, i, j % npj))
        o_shape = jax.ShapeDtypeStruct((out_split, M, n_piece), out_dtype)
    else:
        o_spec = pl.BlockSpec((tm, tn), lambda i, j, k: (i, j))
        o_shape = jax.ShapeDtypeStruct((M, N), out_dtype)
    return pl.pallas_call(
        body, name=name, grid=(M // tm, N // tn, nk),
        in_specs=[a_spec, b_spec], out_specs=o_spec, out_shape=o_shape,
        scratch_shapes=[pltpu.VMEM((tm, tn), F32)],
        compiler_params=pltpu.CompilerParams(
            dimension_semantics=("parallel", "parallel", "arbitrary"), vmem_limit_bytes=VMEM_LIMIT),
    )(a, b)


def _rows(name, fn, rows, params, outs, accs=(), tile=256):
    S = rows[0].shape[0]
    tile = min(tile, S)
    assert S % tile == 0, (name, S, tile)
    nr, npar, no, na = len(rows), len(params), len(outs), len(accs)

    def body(*refs):
        rin, pin = refs[:nr], refs[nr:nr + npar]
        oref, aref = refs[nr + npar:nr + npar + no], refs[nr + npar + no:]
        res = fn(*[r[...] for r in rin], *[p[...] for p in pin])
        if not isinstance(res, (tuple, list)):
            res = (res,)
        assert len(res) == no + na, (name, len(res), no, na)
        for ref, val in zip(oref, res[:no]):
            ref[...] = val.astype(ref.dtype)
        if na:
            @pl.when(pl.program_id(0) == 0)
            def _():
                for ref in aref:
                    ref[...] = jnp.zeros(ref.shape, ref.dtype)

            for ref, val in zip(aref, res[no:]):
                ref[...] += jnp.broadcast_to(val, ref.shape).astype(ref.dtype)

    def whole(shape):
        nd = len(shape)
        return pl.BlockSpec(tuple(shape), lambda i: (0,) * nd)

    in_specs = ([pl.BlockSpec((tile, r.shape[1]), lambda i: (i, 0)) for r in rows]
                + [whole(p.shape) for p in params])
    out_specs = ([pl.BlockSpec((tile, n), lambda i: (i, 0)) for n, _ in outs]
                 + [whole(s) for s in accs])
    out_shape = ([jax.ShapeDtypeStruct((S, n), dt) for n, dt in outs]
                 + [jax.ShapeDtypeStruct(tuple(s), F32) for s in accs])
    res = pl.pallas_call(
        body, name=name, grid=(S // tile,), in_specs=in_specs, out_specs=out_specs,
        out_shape=out_shape,
        compiler_params=pltpu.CompilerParams(
            dimension_semantics=("arbitrary",), vmem_limit_bytes=VMEM_LIMIT),
    )(*rows, *params)
    return res


def _modulate(x, sc, sh):
    return x * (1.0 + sc) + sh


def _gmlp_consts():
    lane = lax.broadcasted_iota(jnp.int32, (1, G_WIDTH), 1)
    gmask = [(lane // (G_WIDTH // G_GROUPS) == g).astype(F32) for g in range(G_GROUPS)]
    tril = (lax.broadcasted_iota(jnp.int32, (CHUNK, CHUNK), 0)
            >= lax.broadcasted_iota(jnp.int32, (CHUNK, CHUNK), 1))
    return gmask, tril


def _gmlp_core(gmask, tril, gsel, zu, zv, bu, bv, g, b, bst, *ws):
    u = _gelu(zu + bu)
    v = _ln(_gelu(zv + bv), g, b, LN_EPS)
    s = _pdot(bst, gsel)
    for gi in range(G_GROUPS):
        s = s + _bdot(jnp.where(tril, ws[gi], 0.0), v * gmask[gi])
    return u * s


def _pre_core(rowmask, bd, zr, zk, zv, zl, pr, pk, pv, pq, br, bk, bv, bl, mr, mk, mv, ml,
              w0, wd, a0, wa, wg, k_k, k_a):
    def mix(z, p, b, mu):
        zz = z + b
        return zz + ((p + b) * rowmask - zz) * mu

    r, k, v, l = mix(zr, pr, br, mr), mix(zk, pk, bk, mk), mix(zv, pv, bv, mv), mix(zl, pq, bl, ml)
    w_log = -_softplus(-(w0 + _bdot(jnp.tanh(l), wd))) - 0.5
    decay = jnp.exp(-jnp.exp(w_log))
    a = _sigmoid(a0 + _bdot(l, wa))
    g = _bdot(_sigmoid(l), wg)
    kk = k * k_k
    kkn = kk / jnp.maximum(jnp.sqrt(_pdot(kk * kk, bd)), 1e-12)
    k2 = k * (1.0 + (a - 1.0) * k_a)
    return r, decay, k2, v, -kkn, kkn * a, g


def _post_core(bd, y, r, k2, v, g, gain, bias, rk):
    inv = 1.0 / R_HEAD
    mu = _pdot(y, bd) * inv
    yc = y - mu
    var = _pdot(yc * yc, bd) * inv
    yn = yc * lax.rsqrt(var + GN_EPS) * gain + bias
    bonus = _pdot(r * k2 * rk, bd) * v
    return (yn + bonus) * g


def _merge_core(pa, pb, ga, gb, bga, bgb):
    return _sigmoid(ga + bga) * pa + _sigmoid(gb + bgb) * pb


def _ln1_core(x, mix, gt1, bout, g, b, sc2, sh2):
    h1 = _ln(ALPHA * x + gt1 * (mix + bout), g, b, LN_EPS)
    return h1, h1 * (1.0 + sc2) + sh2


def _ln2_loss_core(tgt, h1, ff, gt2, bff2, g, b):
    out = _ln(ALPHA * h1 + gt2 * (ff + bff2), g, b, LN_EPS)
    err = out - tgt
    return 0.5 * jnp.sum(err * err) * (1.0 / D_MODEL)


def _scan_consts():
    sub = lax.broadcasted_iota(jnp.int32, (R_HEAD, LANES), 0)
    lane = lax.broadcasted_iota(jnp.int32, (R_HEAD, LANES), 1)
    return lane < R_HEAD, sub == (lane & (R_HEAD - 1))


def _seg_sum(lo, xb):
    s_lo = jnp.sum(jnp.where(lo, xb, 0.0), axis=1, keepdims=True)
    s_hi = jnp.sum(jnp.where(lo, 0.0, xb), axis=1, keepdims=True)
    return jnp.where(lo, s_lo, s_hi)


def _seg_dot(lo, s, row):
    lo_row = lo[0:1, :]
    s_lo = jnp.sum(s * jnp.where(lo_row, row, 0.0), axis=1, keepdims=True)
    s_hi = jnp.sum(s * jnp.where(lo_row, 0.0, row), axis=1, keepdims=True)
    return jnp.where(lo, s_lo, s_hi)


def _row_of_col(eye, colb):
    return jnp.sum(jnp.where(eye, colb, 0.0), axis=0, keepdims=True)


def _head_ones():
    i = lax.broadcasted_iota(jnp.int32, (LANES, LANES), 0) // R_HEAD
    j = lax.broadcasted_iota(jnp.int32, (LANES, LANES), 1) // R_HEAD
    return (i == j).astype(jnp.bfloat16)


N_SPLIT = 1


def _split(x):
    parts, r = [], x
    for u in range(N_SPLIT):
        p = r.astype(jnp.bfloat16)
        parts.append(p)
        if u + 1 < N_SPLIT:
            r = r - p.astype(F32)
    return parts


def _ones_dot(parts, ones, n):
    res = lax.dot_general(jnp.concatenate(parts, axis=0), ones, (((1,), (0,)), ((), ())),
                          preferred_element_type=F32)
    out = []
    for m in range(n):
        t = [res[(N_SPLIT * m + u) * R_HEAD:(N_SPLIT * m + u + 1) * R_HEAD] for u in range(N_SPLIT)]
        out.append(functools.reduce(lambda p, q: p + q, t))
    return out


def _seg_sums_mxu(ones, mats):
    return _ones_dot([p for m in mats for p in _split(m)], ones, len(mats))


def _cols_of_rows_mxu(eye, ones, rows8):
    terms = [p.astype(F32) for p in _split(rows8)]
    parts = [jnp.where(eye, jnp.broadcast_to(t[i:i + 1, :], eye.shape), 0.0).astype(jnp.bfloat16)
             for i in range(rows8.shape[0]) for t in terms]
    return _ones_dot(parts, ones, rows8.shape[0])


N_BLK = R_WIDTH // LANES
ROW_GROUP = 8


def _scan_fwd(r, w, k, v, a, b, gather=()):
    S = r.shape[0]
    T = min(SCAN_T, S)
    nchunk = S // T
    ng = len(gather)

    def body(*refs):
        r_ref, w_ref, k_ref, v_ref, a_ref, b_ref = refs[:6]
        g_in, (y_ref, sv_ref), g_out = refs[6:6 + ng], refs[6 + ng:8 + ng], refs[8 + ng:8 + 2 * ng]
        st_ref, sems = refs[8 + 2 * ng], refs[9 + 2 * ng:]
        lo, eye = _scan_consts()
        ones = _head_ones()

        @pl.when(pl.program_id(0) == 0)
        def _():
            st_ref[...] = jnp.zeros(st_ref.shape, F32)
            if ng:
                _ag_start(g_in, g_out, *sems)

        sub8 = lax.broadcasted_iota(jnp.int32, (ROW_GROUP, LANES), 0)

        def group(gi, state):
            base = pl.multiple_of(gi * ROW_GROUP, ROW_GROUP)
            state = list(state)
            sls = [slice(q * LANES, (q + 1) * LANES) for q in range(N_BLK)]
            ld = lambda ref: [ref[pl.ds(base, ROW_GROUP), sl] for sl in sls]
            r8, w8, k8, v8, a8, b8 = ld(r_ref), ld(w_ref), ld(k_ref), ld(v_ref), ld(a_ref), ld(b_ref)
            vb = [_cols_of_rows_mxu(eye, ones, v8[q]) for q in range(N_BLK)]
            an = [pltpu.roll(a8[q], ROW_GROUP - 1, 0) for q in range(N_BLK)]
            ap = [w8[q] * an[q] for q in range(N_BLK)]
            beta = [_seg_sum(lo[:ROW_GROUP], b8[q] * an[q]) for q in range(N_BLK)]
            kappa = [_seg_sum(lo[:ROW_GROUP], k8[q] * an[q]) for q in range(N_BLK)]
            y8 = [jnp.zeros((ROW_GROUP, LANES), F32)] * N_BLK

            def emit_y(i, y8_):
                ycol = _seg_sums_mxu(ones, [state[q] * r8[q][i:i + 1, :] for q in range(N_BLK)])
                return [jnp.where(sub8 == i, _row_of_col(eye, ycol[q]), y8_[q]) for q in range(N_BLK)]

            for i in range(0, ROW_GROUP, 2):
                row = lambda t8, d=0: t8[i + d:i + d + 1, :]
                sa1 = [None] * N_BLK
                for q in range(N_BLK):
                    s = state[q]
                    sv_ref[base + i, :, sls[q]] = s
                    sa0 = _seg_dot(lo, s, row(a8[q]))
                    nxt = _seg_dot(lo, s, row(ap[q]))
                    sa1[q] = nxt + row(beta[q]) * sa0 + row(kappa[q]) * vb[q][i]
                    state[q] = s * row(w8[q]) + sa0 * row(b8[q]) + vb[q][i] * row(k8[q])
                y8 = emit_y(i, y8)
                for q in range(N_BLK):
                    s = state[q]
                    sv_ref[base + i + 1, :, sls[q]] = s
                    state[q] = s * row(w8[q], 1) + sa1[q] * row(b8[q], 1) + vb[q][i + 1] * row(k8[q], 1)
                y8 = emit_y(i + 1, y8)
            for q in range(N_BLK):
                y_ref[pl.ds(base, ROW_GROUP), sls[q]] = y8[q]
            return tuple(state)

        init = tuple(st_ref[:, q * LANES:(q + 1) * LANES] for q in range(N_BLK))
        fin = lax.fori_loop(0, T // ROW_GROUP, group, init)
        for q in range(N_BLK):
            st_ref[:, q * LANES:(q + 1) * LANES] = fin[q]

        if ng:
            @pl.when(pl.program_id(0) == nchunk - 1)
            def _():
                _ag_finish(g_in, g_out, *sems)

    blk = pl.BlockSpec((T, R_WIDTH), lambda i: (i, 0))
    res = pl.pallas_call(
        body, name="scan_fwd", grid=(nchunk,), in_specs=[blk] * 6 + [ANY] * ng,
        out_specs=[blk, pl.BlockSpec((T, R_HEAD, R_WIDTH), lambda i: (i, 0, 0))] + [ANY] * ng,
        out_shape=[jax.ShapeDtypeStruct((S, R_WIDTH), F32),
                   jax.ShapeDtypeStruct((S, R_HEAD, R_WIDTH), F32)] + _ag_out_shapes(gather),
        scratch_shapes=[pltpu.VMEM((R_HEAD, R_WIDTH), F32)] + (_ag_sems(ng) if ng else []),
        compiler_params=pltpu.CompilerParams(
            dimension_semantics=("arbitrary",), vmem_limit_bytes=VMEM_LIMIT),
    )(r, w, k, v, a, b, *gather)
    return res[0], res[1], list(res[2:])


def _scan_bwd(r, w, k, v, a, b, states, dy, scatter=()):
    S = r.shape[0]
    T = min(SCAN_T, S)
    nchunk = S // T
    ns = len(scatter)

    def body(*refs):
        r_ref, w_ref, k_ref, v_ref, a_ref, b_ref, sv_ref, dy_ref = refs[:8]
        x_in, x_out = refs[8:8 + ns], refs[14 + ns:14 + 2 * ns]
        dr_ref, dw_ref, dk_ref, dv_ref, da_ref, db_ref = refs[8 + ns:14 + ns]
        ds_ref, sems = refs[14 + 2 * ns], refs[15 + 2 * ns:]
        lo, eye = _scan_consts()
        ones = _head_ones()

        @pl.when(pl.program_id(0) == 0)
        def _():
            ds_ref[...] = jnp.zeros(ds_ref.shape, F32)
            if ns:
                _rsc_start(x_in, x_out, *sems)

        sub8 = lax.broadcasted_iota(jnp.int32, (ROW_GROUP, LANES), 0)

        def bgroup(n, dstate, chunk_end):
            base = (T // ROW_GROUP - 1 - n) * ROW_GROUP
            if not isinstance(n, int):
                base = pl.multiple_of(base, ROW_GROUP)
            dstate = list(dstate)
            sls = [slice(q * LANES, (q + 1) * LANES) for q in range(N_BLK)]
            ld = lambda ref: [ref[pl.ds(base, ROW_GROUP), sl] for sl in sls]
            r8, w8, k8, v8, a8, b8, dy8 = (ld(r_ref), ld(w_ref), ld(k_ref), ld(v_ref), ld(a_ref), ld(b_ref),
                                           ld(dy_ref))
            vbs = [_cols_of_rows_mxu(eye, ones, v8[q]) for q in range(N_BLK)]
            dycs = [_cols_of_rows_mxu(eye, ones, dy8[q]) for q in range(N_BLK)]
            acc = [{n_: jnp.zeros((ROW_GROUP, LANES), F32) for n_ in ("r", "w", "k", "v", "a", "b")}
                   for _ in range(N_BLK)]
            for i in reversed(range(ROW_GROUP)):
                row = lambda t8: t8[i:i + 1, :]
                put = lambda q_, n_, val: acc[q_].__setitem__(n_, jnp.where(sub8 == i, val, acc[q_][n_]))
                dks = []
                for q in range(N_BLK):
                    sp, vb = sv_ref[base + i, :, sls[q]], vbs[q][i]
                    wr, ar, br, kr, rr = row(w8[q]), row(a8[q]), row(b8[q]), row(k8[q]), row(r8[q])
                    sa = _seg_dot(lo, sp, ar)
                    if chunk_end and i == ROW_GROUP - 1:
                        st = sp * wr + sa * br + vb * kr
                    else:
                        st = sv_ref[base + i + 1, :, sls[q]]
                    dyc = dycs[q][i]
                    ds = dstate[q] + dyc * rr
                    put(q, "r", _colsum(st * dyc))
                    put(q, "w", _colsum(ds * sp))
                    put(q, "b", _colsum(ds * sa))
                    put(q, "k", _colsum(ds * vb))
                    dsa = _seg_dot(lo, ds, br)
                    dks.append(ds * kr)
                    put(q, "a", _colsum(sp * dsa))
                    dstate[q] = ds * wr + dsa * ar
                dvc = _seg_sums_mxu(ones, dks)
                for q in range(N_BLK):
                    put(q, "v", _row_of_col(eye, dvc[q]))
            for q in range(N_BLK):
                for n_, ref in (("r", dr_ref), ("w", dw_ref), ("k", dk_ref), ("v", dv_ref), ("a", da_ref), ("b", db_ref)):
                    ref[pl.ds(base, ROW_GROUP), sls[q]] = acc[q][n_]
            return tuple(dstate)

        fin = bgroup(0, tuple(ds_ref[:, q * LANES:(q + 1) * LANES] for q in range(N_BLK)), True)
        fin = lax.fori_loop(1, T // ROW_GROUP, lambda n, d: bgroup(n, d, False), fin)
        for q in range(N_BLK):
            ds_ref[:, q * LANES:(q + 1) * LANES] = fin[q]

        if ns:
            @pl.when(pl.program_id(0) == nchunk - 1)
            def _():
                _rsc_finish(x_in, x_out, *sems)

    blk = pl.BlockSpec((T, R_WIDTH), lambda i: (nchunk - 1 - i, 0))
    svb = pl.BlockSpec((T, R_HEAD, R_WIDTH), lambda i: (nchunk - 1 - i, 0, 0))
    return pl.pallas_call(
        body, name="scan_bwd", grid=(nchunk,), in_specs=[blk] * 6 + [svb, blk] + [ANY] * ns,
        out_specs=[blk] * 6 + [ANY] * ns,
        out_shape=[jax.ShapeDtypeStruct((S, R_WIDTH), F32)] * 6 + [jax.ShapeDtypeStruct(p.shape, p.dtype) for p in scatter],
        scratch_shapes=[pltpu.VMEM((R_HEAD, R_WIDTH), F32)] + (_rsc_sems(ns) if ns else []),
        compiler_params=pltpu.CompilerParams(
            dimension_semantics=("arbitrary",), vmem_limit_bytes=VMEM_LIMIT),
    )(r, w, k, v, a, b, states, dy, *scatter)


def _pad_in_cols(t, axis):
    cut = 2 * G_WIDTH + R_COLS
    lo, hi = lax.slice_in_dim(t, 0, cut, axis=axis), lax.slice_in_dim(t, cut, t.shape[axis], axis=axis)
    zshape = list(t.shape)
    zshape[axis] = LORA_PAD - LORA
    return jnp.concatenate([lo, jnp.zeros(zshape, t.dtype), hi], axis=axis)


def _unpad_in_cols(t, axis):
    cut = 2 * G_WIDTH + R_COLS
    return jnp.concatenate([lax.slice_in_dim(t, 0, cut, axis=axis),
                            lax.slice_in_dim(t, cut + LORA_PAD - LORA, t.shape[axis], axis=axis)], axis=axis)


def _pad_rows(t, lo, n):
    return jnp.zeros((n, t.shape[1]), t.dtype).at[lo:lo + t.shape[0]].set(t)


def _join_cols(w3):
    p, k, n = w3.shape
    return jnp.transpose(w3, (1, 0, 2)).reshape(k, p * n)


def _split_cols(w):
    k, n = w.shape
    return jnp.transpose(w.reshape(k, N_CHIPS, n // N_CHIPS), (1, 0, 2))


def _ada_dw(ccol, dmod):
    n = dmod.shape[1] // N_CHIPS
    tile = 256

    def body(c_ref, d_ref, o_ref):
        cc = c_ref[...]
        o_ref[0] = ((cc * _sigmoid(cc)) * d_ref[...]).astype(o_ref.dtype)

    return pl.pallas_call(
        body, name="ada_dw", grid=(N_CHIPS, D_MODEL // tile),
        in_specs=[pl.BlockSpec((tile, 1), lambda q, i: (i, 0)), pl.BlockSpec((1, n), lambda q, i: (0, q))],
        out_specs=pl.BlockSpec((1, tile, n), lambda q, i: (q, i, 0)),
        out_shape=jax.ShapeDtypeStruct((N_CHIPS, D_MODEL, n), WIRE_DT),
        compiler_params=pltpu.CompilerParams(dimension_semantics=("parallel", "parallel")),
    )(ccol, dmod)


def _shift_down(t):
    return jnp.concatenate([jnp.zeros((1, t.shape[1]), t.dtype), t[:-1]], axis=0)


def _shift_up(t):
    return jnp.concatenate([t[1:], jnp.zeros((1, t.shape[1]), t.dtype)], axis=0)


LATE = ("w_branch_a", "w_branch_b", "w_out", "w_ff1", "w_ff2")


def _local_step(x, c, tgt, W):
    S = x.shape[0]
    bf = MXU_DT
    G = {}

    hl = np.arange(R_WIDTH) // R_HEAD
    bd = jnp.asarray(hl[:, None] == hl[None, :], jnp.bfloat16)
    gsel = jnp.asarray(np.arange(LANES)[:, None] == (np.arange(G_WIDTH) // (G_WIDTH // G_GROUPS))[None, :],
                       jnp.bfloat16)
    w_in_p = _pad_in_cols(_join_cols(W["w_in"]), 1)
    b_in_p = _pad_in_cols(W["b_in"], 1)
    c_g, c_r = 2 * G_WIDTH, 2 * G_WIDTH + R_COLS_PAD
    w_g, w_r, w_gate = w_in_p[:, :c_g], w_in_p[:, c_g:c_r], w_in_p[:, c_r:]
    b_g, b_r, b_gate = b_in_p[:, :c_g], b_in_p[:, c_g:c_r], b_in_p[:, c_r:]
    mu_p = jnp.concatenate([W["mu_shift"], jnp.zeros((1, LORA_PAD - LORA), F32)], axis=1)
    wd_p = _pad_rows(W["w_decay_up"].astype(F32), 0, LORA_PAD)
    wa_p = _pad_rows(W["w_aaa_up"].astype(F32), DECAY_LORA, LORA_PAD)
    wg_p = _pad_rows(W["w_gate_up"].astype(F32), DECAY_LORA + AAA_LORA, LORA_PAD)
    ws2 = W["w_spatial"].reshape(G_GROUPS * CHUNK, CHUNK)
    bst = jnp.zeros((CHUNK, LANES), F32).at[:, :G_GROUPS].set(W["b_spatial"].T)
    rk = W["r_k"].reshape(1, R_WIDTH)

    c8 = jnp.broadcast_to(c, (8, D_MODEL))
    (ca8,) = _rows("ada_silu", lambda cc: cc * _sigmoid(cc), [c8], [], [(D_MODEL, bf)], tile=8)
    mod_raw = _mm("ada_mm", ca8, W["w_ada"], "nn")
    (mod8,) = _rows("ada_bias", lambda m, bb: m + bb, [mod_raw], [W["b_ada"]], [(6 * D_MODEL, F32)], tile=8)
    sh1, sc1, gt1, sh2, sc2, gt2 = [mod8[0:1, i * D_MODEL:(i + 1) * D_MODEL] for i in range(6)]

    (h,) = _rows("mod1", _modulate, [x], [sc1, sh1], [(D_MODEL, bf)])
    proj_g = _mm("proj_g", h, w_g, "nn")
    proj_r = _mm("proj_r", h, w_r, "nn")
    proj_gate = _mm("proj_gate", h, w_gate, "nn")

    def split2(t):
        return t[:, :G_WIDTH], t[:, G_WIDTH:]

    def gmlp_fwd(z, bz, g, b, bst_, gsel_, ws_):
        gmask, tril = _gmlp_consts()
        (zu, zv), (bu, bv) = split2(z), split2(bz)
        wsl = [ws_[i * CHUNK:(i + 1) * CHUNK] for i in range(G_GROUPS)]
        return _gmlp_core(gmask, tril, gsel_, zu, zv, bu, bv, g, b, bst_, *wsl)

    (y_a,) = _rows("gmlp_fwd", gmlp_fwd, [proj_g], [b_g, W["g_ln_v"], W["b_ln_v"], bst, gsel, ws2],
                   [(G_WIDTH, bf)], tile=CHUNK)

    r_cuts = (0, R_WIDTH, 2 * R_WIDTH, 3 * R_WIDTH, R_COLS_PAD)

    def split4(t):
        return [t[:, r_cuts[i]:r_cuts[i + 1]] for i in range(4)]

    tile_pre = min(256, S)

    def rowmask_of():
        grow = pl.program_id(0) * tile_pre + lax.broadcasted_iota(jnp.int32, (tile_pre, 1), 0)
        return (grow > 0).astype(F32)

    pre_params = [b_r, mu_p, W["w0"], wd_p, W["a0"], wa_p, wg_p, W["k_k"], W["k_a"], bd]

    def pre_fwd(z, p, bz, mu, w0, wd, a0, wa, wg, k_k, k_a, bd_):
        return _pre_core(rowmask_of(), bd_, *split4(z), *split4(p), *split4(bz), *split4(mu),
                         w0, wd, a0, wa, wg, k_k, k_a)

    proj_r_prev = _shift_down(proj_r)
    s_r, s_w, s_k, s_v, s_a, s_b, s_g = _rows(
        "rwkv_pre_fwd", pre_fwd, [proj_r, proj_r_prev], pre_params, [(R_WIDTH, F32)] * 7, tile=tile_pre)
    y_scan, states, late = yield "fwd", (s_r, s_w, s_k, s_v, s_a, s_b)
    W = {**W, **late}

    def post_fwd(y, r, k2, v, g, gain, bias, rk_, bd_):
        return _post_core(bd_, y, r, k2, v, g, gain, bias, rk_)

    post_params = [W["gn_gain"], W["gn_bias"], rk, bd]
    (y_b,) = _rows("rwkv_post_fwd", post_fwd, [y_scan, s_r, s_k, s_v, s_g], post_params, [(R_WIDTH, bf)])
    p_a = _mm("branch_a", y_a, W["w_branch_a"], "nn")
    p_b = _mm("branch_b", y_b, W["w_branch_b"], "nn")

    def merge_fwd(pa, pb, gz, bgz):
        return _merge_core(pa, pb, gz[:, :D_MODEL], gz[:, D_MODEL:], bgz[:, :D_MODEL], bgz[:, D_MODEL:])

    (merged,) = _rows("merge_fwd", merge_fwd, [p_a, p_b, proj_gate], [b_gate], [(D_MODEL, bf)])
    mix = _mm("out_proj", merged, W["w_out"], "nn")
    ln1_params = [gt1, W["b_out"], W["ln1_g"], W["ln1_b"], sc2, sh2]
    h1, h2 = _rows("ln1_fwd", _ln1_core, [x, mix], ln1_params, [(D_MODEL, F32), (D_MODEL, bf)])

    a1 = _mm("ff1", h2, W["w_ff1"], "nn")
    (act,) = _rows("ff_act", lambda z, bb: jnp.square(jnp.maximum(z + bb, 0.0)), [a1], [W["b_ff1"]], [(D_FF, bf)])
    ff = _mm("ff2", act, W["w_ff2"], "nn")

    def ln2_loss(h1_, ff_, tg, gt2_, bff2, g, b):
        loss, vjp = jax.vjp(functools.partial(_ln2_loss_core, tg), h1_, ff_, gt2_, bff2, g, b)
        return vjp(jnp.ones((), F32)) + (loss,)

    ln2_params = [gt2, W["b_ff2"], W["ln2_g"], W["ln2_b"]]
    dh1, dff, dgt2, G["b_ff2"], G["ln2_g"], G["ln2_b"], loss_acc = _rows(
        "ln2_loss", ln2_loss, [h1, ff, tgt], ln2_params, [(D_MODEL, F32), (D_MODEL, bf)],
        accs=[(1, D_MODEL)] * 4 + [(1, LANES)])
    loss = loss_acc[0, 0]

    dact = _mm("ff2_dx", dff, W["w_ff2"], "nt")
    G["w_ff2"] = _mm("ff2_dw", act, dff, "tn", WIRE_DT).reshape(N_CHIPS, D_FF // N_CHIPS, D_MODEL)

    def act_bwd(z, da, bb):
        d = da * 2.0 * jnp.maximum(z + bb, 0.0)
        return d, _colsum(d)

    da1, G["b_ff1"] = _rows("ff_act_bwd", act_bwd, [a1, dact], [W["b_ff1"]], [(D_FF, bf)], accs=[(1, D_FF)])
    dh2 = _mm("ff1_dx", da1, W["w_ff1"], "nt")
    G["w_ff1"] = _mm("ff1_dw", h2, da1, "tn", WIRE_DT, out_split=N_CHIPS)

    def ln1_bwd(x_, mix_, dh1_, dh2_, *ps):
        _, vjp = jax.vjp(_ln1_core, x_, mix_, *ps)
        return vjp((dh1_, dh2_))

    dx_res, dmix, dgt1, G["b_out"], G["ln1_g"], G["ln1_b"], dsc2, dsh2 = _rows(
        "ln1_bwd", ln1_bwd, [x, mix, dh1, dh2], ln1_params, [(D_MODEL, F32), (D_MODEL, bf)],
        accs=[(1, D_MODEL)] * 6)

    dmerged = _mm("out_proj_dx", dmix, W["w_out"], "nt")
    G["w_out"] = _mm("out_proj_dw", merged, dmix, "tn", WIRE_DT).reshape(N_CHIPS, D_MODEL // N_CHIPS, D_MODEL)

    def merge_bwd(pa, pb, gz, dm, bgz):
        args = (pa.astype(F32), pb.astype(F32), gz[:, :D_MODEL], gz[:, D_MODEL:], bgz[:, :D_MODEL], bgz[:, D_MODEL:])
        _, vjp = jax.vjp(_merge_core, *args)
        dpa, dpb, dga, dgb, dbga, dbgb = vjp(dm)
        return dpa, dpb, jnp.concatenate([dga, dgb], axis=1), jnp.concatenate([dbga, dbgb], axis=1)

    dp_a, dp_b, dgates, db_gate = _rows(
        "merge_bwd", merge_bwd, [p_a, p_b, proj_gate, dmerged], [b_gate],
        [(D_MODEL, bf), (D_MODEL, bf), (2 * D_MODEL, bf)], accs=[(1, 2 * D_MODEL)])
    dy_a = _mm("branch_a_dx", dp_a, W["w_branch_a"], "nt")
    G["w_branch_a"] = _mm("branch_a_dw", y_a, dp_a, "tn", WIRE_DT, out_split=N_CHIPS)
    dy_b = _mm("branch_b_dx", dp_b, W["w_branch_b"], "nt")
    G["w_branch_b"] = _mm("branch_b_dw", y_b, dp_b, "tn", WIRE_DT, out_split=N_CHIPS)

    def post_bwd(y, r, k2, v, g, dyb, gain, bias, rk_, bd_):
        _, vjp = jax.vjp(functools.partial(_post_core, bd_), y, r, k2, v, g, gain, bias, rk_)
        return vjp(dyb)

    dy_scan, dr_p, dk_p, dv_p, dg_p, G["gn_gain"], G["gn_bias"], drk = _rows(
        "rwkv_post_bwd", post_bwd, [y_scan, s_r, s_k, s_v, s_g, dy_b], post_params,
        [(R_WIDTH, F32)] * 5, accs=[(1, R_WIDTH)] * 3)
    G["r_k"] = drk.reshape(R_HEADS, R_HEAD)
    dr_s, dw_s, dk_s, dv_s, da_s, db_s = yield "bwd", (s_r, s_w, s_k, s_v, s_a, s_b, states, dy_scan), G

    def pre_bwd(z, p, dr1, dr2, dw, dk1, dk2, dv1, dv2, da, db, dg,
                bz, mu, w0, wd, a0, wa, wg, k_k, k_a, bd_):
        prim = (*split4(z), *split4(p), *split4(bz), *split4(mu), w0, wd, a0, wa, wg, k_k, k_a)
        _, vjp = jax.vjp(functools.partial(_pre_core, rowmask_of(), bd_), *prim)
        d = vjp((dr1 + dr2, dw, dk1 + dk2, dv1 + dv2, da, db, dg))
        cat = lambda parts: jnp.concatenate(parts, axis=1)
        return (cat(d[0:4]), cat(d[4:8]), cat(d[8:12]), cat(d[12:16])) + tuple(d[16:])

    dz_r, dprev, db_r, dmu_p, G["w0"], dwd_p, G["a0"], dwa_p, dwg_p, G["k_k"], G["k_a"] = _rows(
        "rwkv_pre_bwd", pre_bwd,
        [proj_r, proj_r_prev, dr_s, dr_p, dw_s, dk_s, dk_p, dv_s, dv_p, da_s, db_s, dg_p],
        pre_params, [(R_COLS_PAD, bf)] * 2,
        accs=[(1, R_COLS_PAD), (1, R_COLS_PAD), (1, R_WIDTH), (LORA_PAD, R_WIDTH), (1, R_WIDTH),
              (LORA_PAD, R_WIDTH), (LORA_PAD, R_WIDTH), (1, R_WIDTH), (1, R_WIDTH)],
        tile=tile_pre)
    G["mu_shift"] = dmu_p[:, :R_COLS]
    G["w_decay_up"] = dwd_p[:DECAY_LORA]
    G["w_aaa_up"] = dwa_p[DECAY_LORA:DECAY_LORA + AAA_LORA]
    G["w_gate_up"] = dwg_p[DECAY_LORA + AAA_LORA:LORA]

    def gmlp_bwd(z, dya, bz, g, b, bst_, gsel_, ws_):
        gmask, tril = _gmlp_consts()
        (zu, zv), (bu, bv) = split2(z), split2(bz)
        wsl = [ws_[i * CHUNK:(i + 1) * CHUNK] for i in range(G_GROUPS)]
        _, vjp = jax.vjp(functools.partial(_gmlp_core, gmask, tril, gsel_), zu, zv, bu, bv, g, b, bst_, *wsl)
        d = vjp(dya)
        return (jnp.concatenate(d[0:2], axis=1), jnp.concatenate(d[2:4], axis=1), d[4], d[5], d[6],
                jnp.concatenate(d[7:], axis=0))

    dz_g, db_g, G["g_ln_v"], G["b_ln_v"], dbst, dws2 = _rows(
        "gmlp_bwd", gmlp_bwd, [proj_g, dy_a], [b_g, W["g_ln_v"], W["b_ln_v"], bst, gsel, ws2],
        [(2 * G_WIDTH, bf)],
        accs=[(1, 2 * G_WIDTH), (1, G_WIDTH), (1, G_WIDTH), (CHUNK, LANES), (G_GROUPS * CHUNK, CHUNK)],
        tile=CHUNK)
    G["w_spatial"] = dws2.reshape(G_GROUPS, CHUNK, CHUNK)
    G["b_spatial"] = dbst[:, :G_GROUPS].T

    def dproj_cat(dzg, dzr, dpv, dgz):
        return jnp.concatenate([dzg, (dzr.astype(F32) + dpv.astype(F32)).astype(dzg.dtype), dgz], axis=1)

    (dproj,) = _rows("dproj_cat", dproj_cat, [dz_g, dz_r, _shift_up(dprev), dgates], [],
                     [(2 * G_WIDTH + R_COLS_PAD + 2 * D_MODEL, bf)])
    dh = _mm("proj_dx", dproj, w_in_p, "nt", tk=2432)
    G["w_in"] = _split_cols(_unpad_in_cols(_mm("proj_dw", h, dproj, "tn", WIRE_DT, tm=512, tn=2432), 1))
    G["b_in"] = _unpad_in_cols(jnp.concatenate([db_g, db_r, db_gate], axis=1), 1)

    def mod1_bwd(x_, dh_, dxr, sc):
        return dh_ * (1.0 + sc) + dxr, _colsum(dh_ * x_), _colsum(dh_)

    grad_x, dsc1, dsh1 = _rows("mod1_bwd", mod1_bwd, [x, dh, dx_res], [sc1], [(D_MODEL, F32)],
                               accs=[(1, D_MODEL)] * 2)

    dmod = jnp.concatenate([dsh1, dsc1, dgt1, dsh2, dsc2, dgt2], axis=1)
    G["b_ada"] = dmod
    G["w_ada"] = _ada_dw(c.reshape(D_MODEL, 1), dmod)
    return loss, grad_x, G


BIG = (("w_ada", (D_MODEL, 6 * D_MODEL), 1), ("w_in", (D_MODEL, 2 * G_WIDTH + R_COLS + 2 * D_MODEL), 1),
       ("w_branch_a", (G_WIDTH, D_MODEL), 1), ("w_branch_b", (R_WIDTH, D_MODEL), 1),
       ("w_out", (D_MODEL, D_MODEL), 0), ("w_ff1", (D_MODEL, D_FF), 1), ("w_ff2", (D_FF, D_MODEL), 0))
LORAS = (("w_decay_up", (DECAY_LORA, R_WIDTH), 1), ("w_aaa_up", (AAA_LORA, R_WIDTH), 1),
         ("w_gate_up", (GATE_LORA, R_WIDTH), 1))
SHARDED = BIG + LORAS
SMALL = (("b_ada", (1, 6 * D_MODEL)), ("b_in", (1, 2 * G_WIDTH + R_COLS + 2 * D_MODEL)),
         ("g_ln_v", (1, G_WIDTH)), ("b_ln_v", (1, G_WIDTH)), ("w_spatial", (G_GROUPS, CHUNK, CHUNK)),
         ("b_spatial", (G_GROUPS, CHUNK)), ("mu_shift", (1, R_COLS)), ("w0", (1, R_WIDTH)),
         ("a0", (1, R_WIDTH)), ("k_k", (1, R_WIDTH)), ("k_a", (1, R_WIDTH)), ("r_k", (R_HEADS, R_HEAD)),
         ("gn_gain", (1, R_WIDTH)), ("gn_bias", (1, R_WIDTH)), ("b_out", (1, D_MODEL)),
         ("ln1_g", (1, D_MODEL)), ("ln1_b", (1, D_MODEL)), ("b_ff1", (1, D_FF)), ("b_ff2", (1, D_MODEL)),
         ("ln2_g", (1, D_MODEL)), ("ln2_b", (1, D_MODEL)))
WEIGHT_ORDER = ("w_ada", "b_ada", "w_in", "b_in", "g_ln_v", "b_ln_v", "w_spatial", "b_spatial", "mu_shift",
                "w0", "w_decay_up", "a0", "w_aaa_up", "w_gate_up", "k_k", "k_a", "r_k", "gn_gain", "gn_bias",
                "w_branch_a", "w_branch_b", "w_out", "b_out", "ln1_g", "ln1_b", "w_ff1", "b_ff1", "w_ff2",
                "b_ff2", "ln2_g", "ln2_b")
N_CHIPS = 4


def _shard_shape(shape, axis):
    s = list(shape)
    s[axis] //= N_CHIPS
    return tuple(s)


def _numel(shape):
    return int(np.prod(shape))


def _round_up(n, q):
    return -(-n // q) * q


N_LORA = sum(_numel(_shard_shape(s, a)) for _, s, a in LORAS)
N_SMALL = sum(_numel(s) for _, s in SMALL)
ROWS_SW = _round_up(N_LORA, PACK_Q) // PACK_W
ROWS_SG = _round_up(N_LORA + N_SMALL, PACK_Q) // PACK_W


def _pack_small(loras, small, rows, dtype):
    parts = [loras[n] for n, _, _ in LORAS] + ([small[n] for n, _ in SMALL] if small is not None else [])
    flat = jnp.concatenate([p.reshape(-1).astype(dtype) for p in parts])
    flat = jnp.concatenate([flat, jnp.zeros((rows * PACK_W - flat.shape[0],), dtype)])
    return flat.reshape(rows, PACK_W)


def _unpack_small(pack, with_small):
    flat = pack.reshape(-1)
    out, off = {}, 0
    for n, s, a in LORAS:
        ss = _shard_shape(s, a)
        out[n] = flat[off:off + _numel(ss)].reshape(ss)
        off += _numel(ss)
    if with_small:
        for n, s in SMALL:
            out[n] = flat[off:off + _numel(s)].reshape(s)
            off += _numel(s)
    return out


def _pack_small_grads(G):
    segs = []
    for q in range(N_CHIPS):
        loras = {n: G[n][:, q * (s[1] // N_CHIPS):(q + 1) * (s[1] // N_CHIPS)] for n, s, _ in LORAS}
        segs.append(_pack_small(loras, G, ROWS_SG, F32))
    return jnp.stack(segs)


ANY = pl.BlockSpec(memory_space=pl.ANY)
MESH = pl.DeviceIdType.MESH


def _place():
    x, y, c = lax.axis_index("x"), lax.axis_index("y"), lax.axis_index("c")
    chips = [(1 - x, y), (x, 1 - y), (1 - x, 1 - y)]
    return x, y, c, chips


def _remote(src, dst, send_sem, recv_sem, to):
    return pltpu.make_async_remote_copy(src_ref=src, dst_ref=dst, send_sem=send_sem, recv_sem=recv_sem,
                                        device_id=to, device_id_type=MESH)


def _ag_copies(ins, outs, send_sems, recv_sems):
    x, y, c, chips = _place()
    s = 2 * x + y
    cps = []
    for a in range(len(ins)):
        H = ins[a].shape[0] // 2
        mine = ins[a].at[pl.ds(c * H, H)]
        cps += [_remote(mine, outs[a].at[2 * s + c], send_sems.at[6 * a + j], recv_sems.at[6 * a + j], (*chip, c))
                for j, chip in enumerate(chips)]
    return cps


def _ag_start(ins, outs, send_sems, recv_sems):
    for cp in _ag_copies(ins, outs, send_sems, recv_sems):
        cp.start()


def _ag_finish(ins, outs, send_sems, recv_sems):
    x, y, c, chips = _place()
    sibling = (x, y, 1 - c)
    slot = [2 * chip[0] + chip[1] for chip in chips]
    passed = []
    for a in range(len(ins)):
        for j in range(3):
            landed = outs[a].at[2 * slot[j] + c]
            _remote(landed, landed, send_sems.at[6 * a + j], recv_sems.at[6 * a + j], sibling).wait_recv()
            cp = _remote(landed, landed, send_sems.at[6 * a + 3 + j], recv_sems.at[6 * a + 3 + j], sibling)
            cp.start()
            passed.append(cp)
    for a in range(len(ins)):
        for j in range(3):
            got = outs[a].at[2 * slot[j] + 1 - c]
            _remote(got, got, send_sems.at[6 * a + 3 + j], recv_sems.at[6 * a + 3 + j], sibling).wait_recv()
    for cp in _ag_copies(ins, outs, send_sems, recv_sems) + passed:
        cp.wait_send()


def _ag_out_shapes(shards):
    return [jax.ShapeDtypeStruct((2 * N_CHIPS, a.shape[0] // 2, a.shape[1]), a.dtype) for a in shards]


def _ag_sems(n):
    return [pltpu.SemaphoreType.DMA((6 * n,)), pltpu.SemaphoreType.DMA((6 * n,))]


def _all_gather(shards):
    n = len(shards)

    def body(*refs):
        ins, outs, sems = refs[:n], refs[n:2 * n], refs[2 * n:]
        _ag_start(ins, outs, *sems)
        _ag_finish(ins, outs, *sems)

    return pl.pallas_call(
        body, name="ag_weights", in_specs=[ANY] * n, out_specs=[ANY] * n,
        out_shape=_ag_out_shapes(shards), scratch_shapes=_ag_sems(n),
    )(*shards)


def _rs_sibling_in(gps):
    n = len(gps)

    def body(*refs):
        ins, outs = refs[:n], refs[n:2 * n]
        send_sems, recv_sems = refs[2 * n:]
        x, y, c, _ = _place()
        cps = []
        for a in range(n):
            H = gps[a].shape[1] // 2
            cps += [_remote(ins[a].at[q, pl.ds((1 - c) * H, H)], outs[a].at[q], send_sems.at[N_CHIPS * a + q],
                            recv_sems.at[N_CHIPS * a + q], (x, y, 1 - c)) for q in range(N_CHIPS)]
        for cp in cps:
            cp.start()
        for cp in cps:
            cp.wait()

    return pl.pallas_call(
        body, name="rs_sibling_in", in_specs=[ANY] * n, out_specs=[ANY] * n,
        out_shape=[jax.ShapeDtypeStruct((N_CHIPS, g.shape[1] // 2, g.shape[2]), g.dtype) for g in gps],
        scratch_shapes=[pltpu.SemaphoreType.DMA((N_CHIPS * n,)), pltpu.SemaphoreType.DMA((N_CHIPS * n,))],
    )(*gps)


def _rs_add_own(name, gp, got, c_arr, tr=256):
    H, C = got.shape[1:]
    tr = _pick(H, tr, 8)
    nb = H // tr

    def body(c_ref, g_ref, r_ref, o_ref):
        o_ref[...] = (g_ref[...].astype(F32) + r_ref[...].astype(F32)).astype(o_ref.dtype)

    return pl.pallas_call(
        body, name="rs_add_own_" + name,
        grid_spec=pltpu.PrefetchScalarGridSpec(
            num_scalar_prefetch=1, grid=(N_CHIPS, nb),
            in_specs=[pl.BlockSpec((1, tr, C), lambda q, i, c_ref: (q, c_ref[0] * nb + i, 0)),
                      pl.BlockSpec((1, tr, C), lambda q, i, c_ref: (q, i, 0))],
            out_specs=pl.BlockSpec((1, tr, C), lambda q, i, c_ref: (q, i, 0))),
        out_shape=jax.ShapeDtypeStruct((N_CHIPS, H, C), gp.dtype),
        compiler_params=pltpu.CompilerParams(dimension_semantics=("arbitrary", "arbitrary")),
    )(c_arr, gp, got)


def _rs_chips(parts):
    n = len(parts)

    def body(*refs):
        ins, outs, sems = refs[:n], refs[n:2 * n], refs[2 * n:]
        _rsc_start(ins, outs, *sems)
        _rsc_finish(ins, outs, *sems)

    return pl.pallas_call(
        body, name="rs_chips", in_specs=[ANY] * n, out_specs=[ANY] * n,
        out_shape=[jax.ShapeDtypeStruct(p.shape, p.dtype) for p in parts], scratch_shapes=_rsc_sems(n),
    )(*parts)


def _rsc_copies(ins, outs, send_sems, recv_sems):
    x, y, c, chips = _place()
    s = 2 * x + y
    return [_remote(ins[a].at[2 * chip[0] + chip[1]], outs[a].at[s], send_sems.at[3 * a + j], recv_sems.at[3 * a + j],
                    (*chip, c)) for a in range(len(ins)) for j, chip in enumerate(chips)]


def _rsc_start(ins, outs, send_sems, recv_sems):
    for cp in _rsc_copies(ins, outs, send_sems, recv_sems):
        cp.start()


def _rsc_finish(ins, outs, send_sems, recv_sems):
    x, y, c, chips = _place()
    for cp in _rsc_copies(ins, outs, send_sems, recv_sems):
        cp.wait_send()
    for a in range(len(ins)):
        for j, chip in enumerate(chips):
            got = outs[a].at[2 * chip[0] + chip[1]]
            _remote(got, got, send_sems.at[3 * a + j], recv_sems.at[3 * a + j], (*chip, c)).wait_recv()


def _rsc_sems(n):
    return [pltpu.SemaphoreType.DMA((3 * n,)), pltpu.SemaphoreType.DMA((3 * n,))]


def _rs_add_chips(name, part, slots, sc_arr, tr=128):
    H, C = slots.shape[1:]
    tr = _pick(H, tr, 8)
    nb = H // tr

    def body(sc_ref, p_ref, s_ref, o_ref):
        acc = None
        for q in range(N_CHIPS):
            term = lax.cond(sc_ref[0] == q, lambda: p_ref[0].astype(F32), lambda q=q: s_ref[q].astype(F32))
            acc = term if acc is None else acc + term
        o_ref[...] = acc

    return pl.pallas_call(
        body, name="rs_add_chips_" + name,
        grid_spec=pltpu.PrefetchScalarGridSpec(
            num_scalar_prefetch=1, grid=(nb,),
            in_specs=[pl.BlockSpec((1, tr, C), lambda i, sc: (sc[0], i, 0)),
                      pl.BlockSpec((N_CHIPS, tr, C), lambda i, sc: (0, i, 0))],
            out_specs=pl.BlockSpec((tr, C), lambda i, sc: (sc[1] * nb + i, 0))),
        out_shape=jax.ShapeDtypeStruct((2 * H, C), F32),
        compiler_params=pltpu.CompilerParams(dimension_semantics=("arbitrary",)),
    )(sc_arr, part, slots)


def _rs_sibling_out(wholes):
    n = len(wholes)

    def body(*refs):
        ins, outs = refs[:n], refs[n:2 * n]
        send_sems, recv_sems = refs[2 * n:]
        x, y, c, _ = _place()
        cps = []
        for a in range(n):
            H = wholes[a].shape[0] // 2
            cps.append(_remote(ins[a].at[pl.ds(c * H, H)], outs[a].at[pl.ds(c * H, H)], send_sems.at[a], recv_sems.at[a],
                               (x, y, 1 - c)))
        for cp in cps:
            cp.start()
        for a in range(n):
            H = wholes[a].shape[0] // 2
            cps[a].wait_send()
            got = outs[a].at[pl.ds((1 - c) * H, H)]
            _remote(got, got, send_sems.at[a], recv_sems.at[a], (x, y, 1 - c)).wait_recv()

    return pl.pallas_call(
        body, name="rs_sibling_out", in_specs=[ANY] * n, out_specs=[ANY] * n,
        out_shape=[jax.ShapeDtypeStruct(w.shape, w.dtype) for w in wholes],
        input_output_aliases={a: a for a in range(n)},
        scratch_shapes=[pltpu.SemaphoreType.DMA((n,)), pltpu.SemaphoreType.DMA((n,))],
    )(*wholes)


def _adamw(name, w, g, m, v):
    def fn(w_, g_, m_, v_):
        m2 = ADAM_B1 * m_ + (1.0 - ADAM_B1) * g_
        v2 = ADAM_B2 * v_ + (1.0 - ADAM_B2) * (g_ * g_)
        m_hat = m2 / (1.0 - ADAM_B1 ** ADAM_STEP)
        v_hat = v2 / (1.0 - ADAM_B2 ** ADAM_STEP)
        return -ADAM_LR * (m_hat / (jnp.sqrt(v_hat) + ADAM_EPS) + ADAM_WD * w_), m2, v2

    return _rows("adamw_" + name, fn, [w, g, m, v], [], [(w.shape[1], F32)] * 3, tile=_pick(w.shape[0], 256, 8))


def kernel(x, c, w_ada, b_ada, w_in, b_in, g_ln_v, b_ln_v, w_spatial, b_spatial, mu_shift, w0, w_decay_up, a0, w_aaa_up, w_gate_up, k_k, k_a, r_k, gn_gain, gn_bias, w_branch_a, w_branch_b, w_out, b_out, ln1_g, ln1_b, w_ff1, b_ff1, w_ff2, b_ff2, ln2_g, ln2_b, loss_target, m_w_ada, m_b_ada, m_w_in, m_b_in, m_g_ln_v, m_b_ln_v, m_w_spatial, m_b_spatial, m_mu_shift, m_w0, m_w_decay_up, m_a0, m_w_aaa_up, m_w_gate_up, m_k_k, m_k_a, m_r_k, m_gn_gain, m_gn_bias, m_w_branch_a, m_w_branch_b, m_w_out, m_b_out, m_ln1_g, m_ln1_b, m_w_ff1, m_b_ff1, m_w_ff2, m_b_ff2, m_ln2_g, m_ln2_b, v_w_ada, v_b_ada, v_w_in, v_b_in, v_g_ln_v, v_b_ln_v, v_w_spatial, v_b_spatial, v_mu_shift, v_w0, v_w_decay_up, v_a0, v_w_aaa_up, v_w_gate_up, v_k_k, v_k_a, v_r_k, v_gn_gain, v_gn_bias, v_w_branch_a, v_w_branch_b, v_w_out, v_b_out, v_ln1_g, v_ln1_b, v_w_ff1, v_b_ff1, v_w_ff2, v_b_ff2, v_ln2_g, v_ln2_b):
    args = dict(locals())
    local_shape = {n: _shard_shape(s, a) for n, s, a in SHARDED}
    local_shape.update(dict(SMALL))
    wts = {n: args[n].reshape(local_shape[n]) for n in WEIGHT_ORDER}
    mom = {n: args["m_" + n].reshape(local_shape[n]) for n in WEIGHT_ORDER}
    var = {n: args["v_" + n].reshape(local_shape[n]) for n in WEIGHT_ORDER}
    big = [n for n, _, _ in BIG]
    late, early = [n for n in big if n in LATE], [n for n in big if n not in LATE]
    chip = 2 * lax.axis_index("x") + lax.axis_index("y")
    c_arr = lax.axis_index("c").astype(jnp.int32).reshape(1)
    sc_arr = jnp.stack([chip, lax.axis_index("c")]).astype(jnp.int32)
    mine = {n: wts[n].astype(MXU_DT) for n in big}

    def whole(n, got):
        _, s, a = next(t for t in BIG if t[0] == n)
        g = lax.dynamic_update_slice(got.reshape((N_CHIPS,) + mine[n].shape), mine[n][None], (chip, 0, 0))
        return g if a == 1 else g.reshape(s)

    def sibling_sums(names, G, small):
        gps = [G[n] for n in names] + ([_pack_small_grads(G)] if small else [])
        names = names + (["small"] if small else [])
        return [_rs_add_own(n, g, r, c_arr) for n, g, r in zip(names, gps, _rs_sibling_in(gps))]

    small_w = _pack_small(wts, None, ROWS_SW, MXU_DT)
    gath = _all_gather([mine[n] for n in early] + [small_w])
    W = {n: wts[n] for n, _ in SMALL}
    W.update({n: whole(n, g) for n, g in zip(early, gath)})
    small_g = lax.dynamic_update_slice(gath[-1].reshape((N_CHIPS,) + small_w.shape), small_w[None], (chip, 0, 0))
    lora_q = [_unpack_small(small_g[q], False) for q in range(N_CHIPS)]
    for n, _, _ in LORAS:
        W[n] = jnp.concatenate([lora_q[q][n] for q in range(N_CHIPS)], axis=1)

    step = _local_step(x[0], c, loss_target[0], W)
    _, scan_in = next(step)
    y_scan, states, got = _scan_fwd(*scan_in, gather=[mine[n] for n in late])
    _, scan_in, G = step.send((y_scan, states, {n: whole(n, g) for n, g in zip(late, got)}))
    parts_late = sibling_sums(late, G, False)
    res = _scan_bwd(*scan_in, scatter=parts_late)
    try:
        step.send(tuple(res[:6]))
    except StopIteration as done:
        loss, grad_x, G = done.value
    loss = lax.psum(loss, MESH_AXES)
    parts_early = sibling_sums(early, G, True)
    names = late + early + ["small"]
    parts, slots = parts_late + parts_early, list(res[6:]) + list(_rs_chips(parts_early))
    segs = _rs_sibling_out([_rs_add_chips(n, p, s, sc_arr) for n, p, s in zip(names, parts, slots)])

    out = {}
    for n, g in zip(names[:-1], segs[:-1]):
        out[n] = (g,) + tuple(_adamw(n, wts[n], g, mom[n], var[n]))
    packs = [_pack_small(t, t, ROWS_SG, F32) for t in (wts, mom, var)]
    small4 = [_unpack_small(t, True) for t in (segs[-1],) + tuple(_adamw("small", packs[0], segs[-1], packs[1], packs[2]))]
    res = [loss, grad_x[None]]
    for k in range(4):
        res += [(out[n][k] if n in out else small4[k][n]).reshape(args[n].shape) for n in WEIGHT_ORDER]
    return tuple(res)
```

```python
import functools

import numpy as np
import jax
import jax.numpy as jnp
from jax import lax
from jax.experimental import pallas as pl
from jax.experimental.pallas import tpu as pltpu

F32 = jnp.float32
MXU_DT = jnp.bfloat16
WIRE_DT = jnp.bfloat16

D_MODEL = 1024
G_GROUPS = 8
G_WIDTH = 512
CHUNK = 128
R_WIDTH = 512
R_HEAD = 64
R_HEADS = 8
DECAY_LORA = 32
AAA_LORA = 32
GATE_LORA = 96
LORA = DECAY_LORA + AAA_LORA + GATE_LORA
LORA_PAD = 256
R_COLS = 3 * R_WIDTH + LORA
R_COLS_PAD = 3 * R_WIDTH + LORA_PAD
D_FF = 4 * D_MODEL
ALPHA = 2.0 ** 0.25
LN_EPS = 1e-5
GN_EPS = 64e-5
ADAM_LR = 0.001
ADAM_B1 = 0.9
ADAM_B2 = 0.999
ADAM_EPS = 1e-08
ADAM_WD = 0.01
ADAM_STEP = 10

LANES = 128
PACK_W = 512
PACK_Q = 2 * 16 * PACK_W
VMEM_LIMIT = 48 * 1024 * 1024
SCAN_T = 64

MESH_AXES = ("x", "y", "c")


def _dg(a, b, dims):
    return lax.dot_general(a.astype(MXU_DT), b.astype(MXU_DT), (dims, ((), ())),
                           preferred_element_type=F32)


@jax.custom_vjp
def _bdot(a, b):
    return _dg(a, b, ((1,), (0,)))


def _bdot_fwd(a, b):
    return _bdot(a, b), (a, b)


def _bdot_bwd(res, g):
    a, b = res
    return (_dg(g, b, ((1,), (1,))).astype(a.dtype), _dg(a, g, ((0,), (0,))).astype(b.dtype))


_bdot.defvjp(_bdot_fwd, _bdot_bwd)


def _split_dot(x, m, dims):
    hi = x.astype(jnp.bfloat16)
    lo = (x - hi.astype(F32)).astype(jnp.bfloat16)
    dn = (dims, ((), ()))
    return (lax.dot_general(hi, m, dn, preferred_element_type=F32)
            + lax.dot_general(lo, m, dn, preferred_element_type=F32))


@jax.custom_vjp
def _pdot(x, m):
    return _split_dot(x, m, ((1,), (0,)))


def _pdot_fwd(x, m):
    return _pdot(x, m), m


def _pdot_bwd(m, g):
    return _split_dot(g, m, ((1,), (1,))), None


_pdot.defvjp(_pdot_fwd, _pdot_bwd)


def _sigmoid(x):
    return 1.0 / (1.0 + jnp.exp(-x))


def _softplus(x):
    return jnp.maximum(x, 0.0) + jnp.log(1.0 + jnp.exp(-jnp.maximum(x, -x)))


def _gelu(x):
    return 0.5 * x * (1.0 + jnp.tanh(0.7978845608028654 * (x + 0.044715 * (x * x * x))))


def _ln(x, g, b, eps):
    mu = jnp.mean(x, axis=-1, keepdims=True)
    xc = x - mu
    var = jnp.mean(xc * xc, axis=-1, keepdims=True)
    return xc * lax.rsqrt(var + eps) * g + b


def _colsum(x):
    return jnp.sum(x, axis=0, keepdims=True)


def _pick(n, target, q=LANES):
    if n <= target:
        return n
    best = None
    for t in range(q, target + 1, q):
        if n % t == 0:
            best = t
    assert best is not None, (n, target)
    return best


def _mm(name, a, b, mode, out_dtype=F32, out_split=1, tm=1024, tn=1024, tk=1024):
    bs = b.shape[0] if b.ndim == 3 else 1
    br, bc = b.shape[-2:]
    if mode == "nn":
        (M, K), K2, N = a.shape, br, bc * bs
    elif mode == "nt":
        (M, K), N, K2 = a.shape, br, bc * bs
    else:
        assert bs == 1
        (K, M), K2, N = a.shape, br, bc
    assert K == K2, (name, a.shape, b.shape, mode)
    n_piece = N // max(bs if mode == "nn" else 1, out_split)
    k_piece = K // (bs if mode == "nt" else 1)
    tm, tn, tk = _pick(M, tm, 8 if M < LANES else LANES), _pick(n_piece, tn), _pick(k_piece, tk)
    nk, npj, npk = K // tk, n_piece // tn, k_piece // tk
    dims = {"nn": ((1,), (0,)), "nt": ((1,), (1,)), "tn": ((0,), (0,))}[mode]

    def body(a_ref, b_ref, o_ref, acc_ref):
        k = pl.program_id(2)

        @pl.when(k == 0)
        def _():
            acc_ref[...] = jnp.zeros(acc_ref.shape, F32)

        acc_ref[...] += _dg(a_ref[...], b_ref[0] if bs > 1 else b_ref[...], dims)

        @pl.when(k == nk - 1)
        def _():
            if out_split > 1:
                o_ref[0] = acc_ref[...].astype(o_ref.dtype)
            else:
                o_ref[...] = acc_ref[...].astype(o_ref.dtype)

    if mode == "nn":
        a_spec = pl.BlockSpec((tm, tk), lambda i, j, k: (i, k))
        b_spec = (pl.BlockSpec((tk, tn), lambda i, j, k: (k, j)) if bs == 1 else
                  pl.BlockSpec((1, tk, tn), lambda i, j, k: (j // npj, k, j % npj)))
    elif mode == "nt":
        a_spec = pl.BlockSpec((tm, tk), lambda i, j, k: (i, k))
        b_spec = (pl.BlockSpec((tn, tk), lambda i, j, k: (j, k)) if bs == 1 else
                  pl.BlockSpec((1, tn, tk), lambda i, j, k: (k // npk, j, k % npk)))
    else:
        a_spec = pl.BlockSpec((tk, tm), lambda i, j, k: (k, i))
        b_spec = pl.BlockSpec((tk, tn), lambda i, j, k: (k, j))
    if out_split > 1:
        o_spec = pl.BlockSpec((1, tm, tn), lambda i, j, k: (j // npj, i, j % npj))
        o_shape = jax.ShapeDtypeStruct((out_split, M, n_piece), out_dtype)
    else:
        o_spec = pl.BlockSpec((tm, tn), lambda i, j, k: (i, j))
        o_shape = jax.ShapeDtypeStruct((M, N), out_dtype)
    return pl.pallas_call(
        body, name=name, grid=(M // tm, N // tn, nk),
        in_specs=[a_spec, b_spec], out_specs=o_spec, out_shape=o_shape,
        scratch_shapes=[pltpu.VMEM((tm, tn), F32)],
        compiler_params=pltpu.CompilerParams(
            dimension_semantics=("parallel", "parallel", "arbitrary"), vmem_limit_bytes=VMEM_LIMIT),
    )(a, b)


def _rows(name, fn, rows, params, outs, accs=(), tile=256):
    S = rows[0].shape[0]
    tile = min(tile, S)
    assert S % tile == 0, (name, S, tile)
    nr, npar, no, na = len(rows), len(params), len(outs), len(accs)

    def body(*refs):
        rin, pin = refs[:nr], refs[nr:nr + npar]
        oref, aref = refs[nr + npar:nr + npar + no], refs[nr + npar + no:]
        res = fn(*[r[...] for r in rin], *[p[...] for p in pin])
        if not isinstance(res, (tuple, list)):
            res = (res,)
        assert len(res) == no + na, (name, len(res), no, na)
        for ref, val in zip(oref, res[:no]):
            ref[...] = val.astype(ref.dtype)
        if na:
            @pl.when(pl.program_id(0) == 0)
            def _():
                for ref in aref:
                    ref[...] = jnp.zeros(ref.shape, ref.dtype)

            for ref, val in zip(aref, res[no:]):
                ref[...] += jnp.broadcast_to(val, ref.shape).astype(ref.dtype)

    def whole(shape):
        nd = len(shape)
        return pl.BlockSpec(tuple(shape), lambda i: (0,) * nd)

    in_specs = ([pl.BlockSpec((tile, r.shape[1]), lambda i: (i, 0)) for r in rows]
                + [whole(p.shape) for p in params])
    out_specs = ([pl.BlockSpec((tile, n), lambda i: (i, 0)) for n, _ in outs]
                 + [whole(s) for s in accs])
    out_shape = ([jax.ShapeDtypeStruct((S, n), dt) for n, dt in outs]
                 + [jax.ShapeDtypeStruct(tuple(s), F32) for s in accs])
    res = pl.pallas_call(
        body, name=name, grid=(S // tile,), in_specs=in_specs, out_specs=out_specs,
        out_shape=out_shape,
        compiler_params=pltpu.CompilerParams(
            dimension_semantics=("arbitrary",), vmem_limit_bytes=VMEM_LIMIT),
    )(*rows, *params)
    return res


def _modulate(x, sc, sh):
    return x * (1.0 + sc) + sh


def _gmlp_consts():
    lane = lax.broadcasted_iota(jnp.int32, (1, G_WIDTH), 1)
    gmask = [(lane // (G_WIDTH // G_GROUPS) == g).astype(F32) for g in range(G_GROUPS)]
    tril = (lax.broadcasted_iota(jnp.int32, (CHUNK, CHUNK), 0)
            >= lax.broadcasted_iota(jnp.int32, (CHUNK, CHUNK), 1))
    return gmask, tril


def _gmlp_core(gmask, tril, gsel, zu, zv, bu, bv, g, b, bst, *ws):
    u = _gelu(zu + bu)
    v = _ln(_gelu(zv + bv), g, b, LN_EPS)
    s = _pdot(bst, gsel)
    for gi in range(G_GROUPS):
        s = s + _bdot(jnp.where(tril, ws[gi], 0.0), v * gmask[gi])
    return u * s


def _pre_core(rowmask, bd, zr, zk, zv, zl, pr, pk, pv, pq, br, bk, bv, bl, mr, mk, mv, ml,
              w0, wd, a0, wa, wg, k_k, k_a):
    def mix(z, p, b, mu):
        zz = z + b
        return zz + ((p + b) * rowmask - zz) * mu

    r, k, v, l = mix(zr, pr, br, mr), mix(zk, pk, bk, mk), mix(zv, pv, bv, mv), mix(zl, pq, bl, ml)
    w_log = -_softplus(-(w0 + _bdot(jnp.tanh(l), wd))) - 0.5
    decay = jnp.exp(-jnp.exp(w_log))
    a = _sigmoid(a0 + _bdot(l, wa))
    g = _bdot(_sigmoid(l), wg)
    kk = k * k_k
    kkn = kk / jnp.maximum(jnp.sqrt(_pdot(kk * kk, bd)), 1e-12)
    k2 = k * (1.0 + (a - 1.0) * k_a)
    return r, decay, k2, v, -kkn, kkn * a, g


def _post_core(bd, y, r, k2, v, g, gain, bias, rk):
    inv = 1.0 / R_HEAD
    mu = _pdot(y, bd) * inv
    yc = y - mu
    var = _pdot(yc * yc, bd) * inv
    yn = yc * lax.rsqrt(var + GN_EPS) * gain + bias
    bonus = _pdot(r * k2 * rk, bd) * v
    return (yn + bonus) * g


def _merge_core(pa, pb, ga, gb, bga, bgb):
    return _sigmoid(ga + bga) * pa + _sigmoid(gb + bgb) * pb


def _ln1_core(x, mix, gt1, bout, g, b, sc2, sh2):
    h1 = _ln(ALPHA * x + gt1 * (mix + bout), g, b, LN_EPS)
    return h1, h1 * (1.0 + sc2) + sh2


def _ln2_loss_core(tgt, h1, ff, gt2, bff2, g, b):
    out = _ln(ALPHA * h1 + gt2 * (ff + bff2), g, b, LN_EPS)
    err = out - tgt
    return 0.5 * jnp.sum(err * err) * (1.0 / D_MODEL)


def _scan_consts():
    sub = lax.broadcasted_iota(jnp.int32, (R_HEAD, LANES), 0)
    lane = lax.broadcasted_iota(jnp.int32, (R_HEAD, LANES), 1)
    return lane < R_HEAD, sub == (lane & (R_HEAD - 1))


def _seg_sum(lo, xb):
    s_lo = jnp.sum(jnp.where(lo, xb, 0.0), axis=1, keepdims=True)
    s_hi = jnp.sum(jnp.where(lo, 0.0, xb), axis=1, keepdims=True)
    return jnp.where(lo, s_lo, s_hi)


def _seg_dot(lo, s, row):
    lo_row = lo[0:1, :]
    s_lo = jnp.sum(s * jnp.where(lo_row, row, 0.0), axis=1, keepdims=True)
    s_hi = jnp.sum(s * jnp.where(lo_row, 0.0, row), axis=1, keepdims=True)
    return jnp.where(lo, s_lo, s_hi)


def _row_of_col(eye, colb):
    return jnp.sum(jnp.where(eye, colb, 0.0), axis=0, keepdims=True)


def _head_ones():
    i = lax.broadcasted_iota(jnp.int32, (LANES, LANES), 0) // R_HEAD
    j = lax.broadcasted_iota(jnp.int32, (LANES, LANES), 1) // R_HEAD
    return (i == j).astype(jnp.bfloat16)


N_SPLIT = 1


def _split(x):
    parts, r = [], x
    for u in range(N_SPLIT):
        p = r.astype(jnp.bfloat16)
        parts.append(p)
        if u + 1 < N_SPLIT:
            r = r - p.astype(F32)
    return parts


def _ones_dot(parts, ones, n):
    res = lax.dot_general(jnp.concatenate(parts, axis=0), ones, (((1,), (0,)), ((), ())),
                          preferred_element_type=F32)
    out = []
    for m in range(n):
        t = [res[(N_SPLIT * m + u) * R_HEAD:(N_SPLIT * m + u + 1) * R_HEAD] for u in range(N_SPLIT)]
        out.append(functools.reduce(lambda p, q: p + q, t))
    return out


def _seg_sums_mxu(ones, mats):
    return _ones_dot([p for m in mats for p in _split(m)], ones, len(mats))


def _cols_of_rows_mxu(eye, ones, rows8):
    terms = [p.astype(F32) for p in _split(rows8)]
    parts = [jnp.where(eye, jnp.broadcast_to(t[i:i + 1, :], eye.shape), 0.0).astype(jnp.bfloat16)
             for i in range(rows8.shape[0]) for t in terms]
    return _ones_dot(parts, ones, rows8.shape[0])


N_BLK = R_WIDTH // LANES
ROW_GROUP = 8


def _scan_fwd(r, w, k, v, a, b, gather=()):
    S = r.shape[0]
    T = min(SCAN_T, S)
    nchunk = S // T
    ng = len(gather)

    def body(*refs):
        r_ref, w_ref, k_ref, v_ref, a_ref, b_ref = refs[:6]
        g_in, (y_ref, sv_ref), g_out = refs[6:6 + ng], refs[6 + ng:8 + ng], refs[8 + ng:8 + 2 * ng]
        st_ref, sems = refs[8 + 2 * ng], refs[9 + 2 * ng:]
        lo, eye = _scan_consts()
        ones = _head_ones()

        @pl.when(pl.program_id(0) == 0)
        def _():
            st_ref[...] = jnp.zeros(st_ref.shape, F32)
            if ng:
                _ag_start(g_in, g_out, *sems)

        sub8 = lax.broadcasted_iota(jnp.int32, (ROW_GROUP, LANES), 0)

        def group(gi, state):
            base = pl.multiple_of(gi * ROW_GROUP, ROW_GROUP)
            state = list(state)
            sls = [slice(q * LANES, (q + 1) * LANES) for q in range(N_BLK)]
            ld = lambda ref: [ref[pl.ds(base, ROW_GROUP), sl] for sl in sls]
            r8, w8, k8, v8, a8, b8 = ld(r_ref), ld(w_ref), ld(k_ref), ld(v_ref), ld(a_ref), ld(b_ref)
            vb = [_cols_of_rows_mxu(eye, ones, v8[q]) for q in range(N_BLK)]
            an = [pltpu.roll(a8[q], ROW_GROUP - 1, 0) for q in range(N_BLK)]
            ap = [w8[q] * an[q] for q in range(N_BLK)]
            beta = [_seg_sum(lo[:ROW_GROUP], b8[q] * an[q]) for q in range(N_BLK)]
            kappa = [_seg_sum(lo[:ROW_GROUP], k8[q] * an[q]) for q in range(N_BLK)]
            y8 = [jnp.zeros((ROW_GROUP, LANES), F32)] * N_BLK

            def emit_y(i, y8_):
                ycol = _seg_sums_mxu(ones, [state[q] * r8[q][i:i + 1, :] for q in range(N_BLK)])
                return [jnp.where(sub8 == i, _row_of_col(eye, ycol[q]), y8_[q]) for q in range(N_BLK)]

            for i in range(0, ROW_GROUP, 2):
                row = lambda t8, d=0: t8[i + d:i + d + 1, :]
                sa1 = [None] * N_BLK
                for q in range(N_BLK):
                    s = state[q]
                    sv_ref[base + i, :, sls[q]] = s
                    sa0 = _seg_dot(lo, s, row(a8[q]))
                    nxt = _seg_dot(lo, s, row(ap[q]))
                    sa1[q] = nxt + row(beta[q]) * sa0 + row(kappa[q]) * vb[q][i]
                    state[q] = s * row(w8[q]) + sa0 * row(b8[q]) + vb[q][i] * row(k8[q])
                y8 = emit_y(i, y8)
                for q in range(N_BLK):
                    s = state[q]
                    sv_ref[base + i + 1, :, sls[q]] = s
                    state[q] = s * row(w8[q], 1) + sa1[q] * row(b8[q], 1) + vb[q][i + 1] * row(k8[q], 1)
                y8 = emit_y(i + 1, y8)
            for q in range(N_BLK):
                y_ref[pl.ds(base, ROW_GROUP), sls[q]] = y8[q]
            return tuple(state)

        init = tuple(st_ref[:, q * LANES:(q + 1) * LANES] for q in range(N_BLK))
        fin = lax.fori_loop(0, T // ROW_GROUP, group, init)
        for q in range(N_BLK):
            st_ref[:, q * LANES:(q + 1) * LANES] = fin[q]

        if ng:
            @pl.when(pl.program_id(0) == nchunk - 1)
            def _():
                _ag_finish(g_in, g_out, *sems)

    blk = pl.BlockSpec((T, R_WIDTH), lambda i: (i, 0))
    res = pl.pallas_call(
        body, name="scan_fwd", grid=(nchunk,), in_specs=[blk] * 6 + [ANY] * ng,
        out_specs=[blk, pl.BlockSpec((T, R_HEAD, R_WIDTH), lambda i: (i, 0, 0))] + [ANY] * ng,
        out_shape=[jax.ShapeDtypeStruct((S, R_WIDTH), F32),
                   jax.ShapeDtypeStruct((S, R_HEAD, R_WIDTH), F32)] + _ag_out_shapes(gather),
        scratch_shapes=[pltpu.VMEM((R_HEAD, R_WIDTH), F32)] + (_ag_sems(ng) if ng else []),
        compiler_params=pltpu.CompilerParams(
            dimension_semantics=("arbitrary",), vmem_limit_bytes=VMEM_LIMIT),
    )(r, w, k, v, a, b, *gather)
    return res[0], res[1], list(res[2:])


def _scan_bwd(r, w, k, v, a, b, states, dy, scatter=()):
    S = r.shape[0]
    T = min(SCAN_T, S)
    nchunk = S // T
    ns = len(scatter)

    def body(*refs):
        r_ref, w_ref, k_ref, v_ref, a_ref, b_ref, sv_ref, dy_ref = refs[:8]
        x_in, x_out = refs[8:8 + ns], refs[14 + ns:14 + 2 * ns]
        dr_ref, dw_ref, dk_ref, dv_ref, da_ref, db_ref = refs[8 + ns:14 + ns]
        ds_ref, sems = refs[14 + 2 * ns], refs[15 + 2 * ns:]
        lo, eye = _scan_consts()
        ones = _head_ones()

        @pl.when(pl.program_id(0) == 0)
        def _():
            ds_ref[...] = jnp.zeros(ds_ref.shape, F32)
            if ns:
                _rsc_start(x_in, x_out, *sems)

        sub8 = lax.broadcasted_iota(jnp.int32, (ROW_GROUP, LANES), 0)

        def bgroup(n, dstate, chunk_end):
            base = (T // ROW_GROUP - 1 - n) * ROW_GROUP
            if not isinstance(n, int):
                base = pl.multiple_of(base, ROW_GROUP)
            dstate = list(dstate)
            sls = [slice(q * LANES, (q + 1) * LANES) for q in range(N_BLK)]
            ld = lambda ref: [ref[pl.ds(base, ROW_GROUP), sl] for sl in sls]
            r8, w8, k8, v8, a8, b8, dy8 = (ld(r_ref), ld(w_ref), ld(k_ref), ld(v_ref), ld(a_ref), ld(b_ref),
                                           ld(dy_ref))
            vbs = [_cols_of_rows_mxu(eye, ones, v8[q]) for q in range(N_BLK)]
            dycs = [_cols_of_rows_mxu(eye, ones, dy8[q]) for q in range(N_BLK)]
            acc = [{n_: jnp.zeros((ROW_GROUP, LANES), F32) for n_ in ("r", "w", "k", "v", "a", "b")}
                   for _ in range(N_BLK)]
            for i in reversed(range(ROW_GROUP)):
                row = lambda t8: t8[i:i + 1, :]
                put = lambda q_, n_, val: acc[q_].__setitem__(n_, jnp.where(sub8 == i, val, acc[q_][n_]))
                dks = []
                for q in range(N_BLK):
                    sp, vb = sv_ref[base + i, :, sls[q]], vbs[q][i]
                    wr, ar, br, kr, rr = row(w8[q]), row(a8[q]), row(b8[q]), row(k8[q]), row(r8[q])
                    sa = _seg_dot(lo, sp, ar)
                    if chunk_end and i == ROW_GROUP - 1:
                        st = sp * wr + sa * br + vb * kr
                    else:
                        st = sv_ref[base + i + 1, :, sls[q]]
                    dyc = dycs[q][i]
                    ds = dstate[q] + dyc * rr
                    put(q, "r", _colsum(st * dyc))
                    put(q, "w", _colsum(ds * sp))
                    put(q, "b", _colsum(ds * sa))
                    put(q, "k", _colsum(ds * vb))
                    dsa = _seg_dot(lo, ds, br)
                    dks.append(ds * kr)
                    put(q, "a", _colsum(sp * dsa))
                    dstate[q] = ds * wr + dsa * ar
                dvc = _seg_sums_mxu(ones, dks)
                for q in range(N_BLK):
                    put(q, "v", _row_of_col(eye, dvc[q]))
            for q in range(N_BLK):
                for n_, ref in (("r", dr_ref), ("w", dw_ref), ("k", dk_ref), ("v", dv_ref), ("a", da_ref), ("b", db_ref)):
                    ref[pl.ds(base, ROW_GROUP), sls[q]] = acc[q][n_]
            return tuple(dstate)

        fin = bgroup(0, tuple(ds_ref[:, q * LANES:(q + 1) * LANES] for q in range(N_BLK)), True)
        fin = lax.fori_loop(1, T // ROW_GROUP, lambda n, d: bgroup(n, d, False), fin)
        for q in range(N_BLK):
            ds_ref[:, q * LANES:(q + 1) * LANES] = fin[q]

        if ns:
            @pl.when(pl.program_id(0) == nchunk - 1)
            def _():
                _rsc_finish(x_in, x_out, *sems)

    blk = pl.BlockSpec((T, R_WIDTH), lambda i: (nchunk - 1 - i, 0))
    svb = pl.BlockSpec((T, R_HEAD, R_WIDTH), lambda i: (nchunk - 1 - i, 0, 0))
    return pl.pallas_call(
        body, name="scan_bwd", grid=(nchunk,), in_specs=[blk] * 6 + [svb, blk] + [ANY] * ns,
        out_specs=[blk] * 6 + [ANY] * ns,
        out_shape=[jax.ShapeDtypeStruct((S, R_WIDTH), F32)] * 6 + [jax.ShapeDtypeStruct(p.shape, p.dtype) for p in scatter],
        scratch_shapes=[pltpu.VMEM((R_HEAD, R_WIDTH), F32)] + (_rsc_sems(ns) if ns else []),
        compiler_params=pltpu.CompilerParams(
            dimension_semantics=("arbitrary",), vmem_limit_bytes=VMEM_LIMIT),
    )(r, w, k, v, a, b, states, dy, *scatter)


def _pad_in_cols(t, axis):
    cut = 2 * G_WIDTH + R_COLS
    lo, hi = lax.slice_in_dim(t, 0, cut, axis=axis), lax.slice_in_dim(t, cut, t.shape[axis], axis=axis)
    zshape = list(t.shape)
    zshape[axis] = LORA_PAD - LORA
    return jnp.concatenate([lo, jnp.zeros(zshape, t.dtype), hi], axis=axis)


def _unpad_in_cols(t, axis):
    cut = 2 * G_WIDTH + R_COLS
    return jnp.concatenate([lax.slice_in_dim(t, 0, cut, axis=axis),
                            lax.slice_in_dim(t, cut + LORA_PAD - LORA, t.shape[axis], axis=axis)], axis=axis)


def _pad_rows(t, lo, n):
    return jnp.zeros((n, t.shape[1]), t.dtype).at[lo:lo + t.shape[0]].set(t)


def _join_cols(w3):
    p, k, n = w3.shape
    return jnp.transpose(w3, (1, 0, 2)).reshape(k, p * n)


def _split_cols(w):
    k, n = w.shape
    return jnp.transpose(w.reshape(k, N_CHIPS, n // N_CHIPS), (1, 0, 2))


def _ada_dw(ccol, dmod):
    n = dmod.shape[1] // N_CHIPS
    tile = 256

    def body(c_ref, d_ref, o_ref):
        cc = c_ref[...]
        o_ref[0] = ((cc * _sigmoid(cc)) * d_ref[...]).astype(o_ref.dtype)

    return pl.pallas_call(
        body, name="ada_dw", grid=(N_CHIPS, D_MODEL // tile),
        in_specs=[pl.BlockSpec((tile, 1), lambda q, i: (i, 0)), pl.BlockSpec((1, n), lambda q, i: (0, q))],
        out_specs=pl.BlockSpec((1, tile, n), lambda q, i: (q, i, 0)),
        out_shape=jax.ShapeDtypeStruct((N_CHIPS, D_MODEL, n), WIRE_DT),
        compiler_params=pltpu.CompilerParams(dimension_semantics=("parallel", "parallel")),
    )(ccol, dmod)


def _shift_down(t):
    return jnp.concatenate([jnp.zeros((1, t.shape[1]), t.dtype), t[:-1]], axis=0)


def _shift_up(t):
    return jnp.concatenate([t[1:], jnp.zeros((1, t.shape[1]), t.dtype)], axis=0)


LATE = ("w_branch_a", "w_branch_b", "w_out", "w_ff1", "w_ff2")


def _local_step(x, c, tgt, W):
    S = x.shape[0]
    bf = MXU_DT
    G = {}

    hl = np.arange(R_WIDTH) // R_HEAD
    bd = jnp.asarray(hl[:, None] == hl[None, :], jnp.bfloat16)
    gsel = jnp.asarray(np.arange(LANES)[:, None] == (np.arange(G_WIDTH) // (G_WIDTH // G_GROUPS))[None, :],
                       jnp.bfloat16)
    w_in_p = _pad_in_cols(_join_cols(W["w_in"]), 1)
    b_in_p = _pad_in_cols(W["b_in"], 1)
    c_g, c_r = 2 * G_WIDTH, 2 * G_WIDTH + R_COLS_PAD
    w_g, w_r, w_gate = w_in_p[:, :c_g], w_in_p[:, c_g:c_r], w_in_p[:, c_r:]
    b_g, b_r, b_gate = b_in_p[:, :c_g], b_in_p[:, c_g:c_r], b_in_p[:, c_r:]
    mu_p = jnp.concatenate([W["mu_shift"], jnp.zeros((1, LORA_PAD - LORA), F32)], axis=1)
    wd_p = _pad_rows(W["w_decay_up"].astype(F32), 0, LORA_PAD)
    wa_p = _pad_rows(W["w_aaa_up"].astype(F32), DECAY_LORA, LORA_PAD)
    wg_p = _pad_rows(W["w_gate_up"].astype(F32), DECAY_LORA + AAA_LORA, LORA_PAD)
    ws2 = W["w_spatial"].reshape(G_GROUPS * CHUNK, CHUNK)
    bst = jnp.zeros((CHUNK, LANES), F32).at[:, :G_GROUPS].set(W["b_spatial"].T)
    rk = W["r_k"].reshape(1, R_WIDTH)

    c8 = jnp.broadcast_to(c, (8, D_MODEL))
    (ca8,) = _rows("ada_silu", lambda cc: cc * _sigmoid(cc), [c8], [], [(D_MODEL, bf)], tile=8)
    mod_raw = _mm("ada_mm", ca8, W["w_ada"], "nn")
    (mod8,) = _rows("ada_bias", lambda m, bb: m + bb, [mod_raw], [W["b_ada"]], [(6 * D_MODEL, F32)], tile=8)
    sh1, sc1, gt1, sh2, sc2, gt2 = [mod8[0:1, i * D_MODEL:(i + 1) * D_MODEL] for i in range(6)]

    (h,) = _rows("mod1", _modulate, [x], [sc1, sh1], [(D_MODEL, bf)])
    proj_g = _mm("proj_g", h, w_g, "nn")
    proj_r = _mm("proj_r", h, w_r, "nn")
    proj_gate = _mm("proj_gate", h, w_gate, "nn")

    def split2(t):
        return t[:, :G_WIDTH], t[:, G_WIDTH:]

    def gmlp_fwd(z, bz, g, b, bst_, gsel_, ws_):
        gmask, tril = _gmlp_consts()
        (zu, zv), (bu, bv) = split2(z), split2(bz)
        wsl = [ws_[i * CHUNK:(i + 1) * CHUNK] for i in range(G_GROUPS)]
        return _gmlp_core(gmask, tril, gsel_, zu, zv, bu, bv, g, b, bst_, *wsl)

    (y_a,) = _rows("gmlp_fwd", gmlp_fwd, [proj_g], [b_g, W["g_ln_v"], W["b_ln_v"], bst, gsel, ws2],
                   [(G_WIDTH, bf)], tile=CHUNK)

    r_cuts = (0, R_WIDTH, 2 * R_WIDTH, 3 * R_WIDTH, R_COLS_PAD)

    def split4(t):
        return [t[:, r_cuts[i]:r_cuts[i + 1]] for i in range(4)]

    tile_pre = min(256, S)

    def rowmask_of():
        grow = pl.program_id(0) * tile_pre + lax.broadcasted_iota(jnp.int32, (tile_pre, 1), 0)
        return (grow > 0).astype(F32)

    pre_params = [b_r, mu_p, W["w0"], wd_p, W["a0"], wa_p, wg_p, W["k_k"], W["k_a"], bd]

    def pre_fwd(z, p, bz, mu, w0, wd, a0, wa, wg, k_k, k_a, bd_):
        return _pre_core(rowmask_of(), bd_, *split4(z), *split4(p), *split4(bz), *split4(mu),
                         w0, wd, a0, wa, wg, k_k, k_a)

    proj_r_prev = _shift_down(proj_r)
    s_r, s_w, s_k, s_v, s_a, s_b, s_g = _rows(
        "rwkv_pre_fwd", pre_fwd, [proj_r, proj_r_prev], pre_params, [(R_WIDTH, F32)] * 7, tile=tile_pre)
    y_scan, states, late = yield "fwd", (s_r, s_w, s_k, s_v, s_a, s_b)
    W = {**W, **late}

    def post_fwd(y, r, k2, v, g, gain, bias, rk_, bd_):
        return _post_core(bd_, y, r, k2, v, g, gain, bias, rk_)

    post_params = [W["gn_gain"], W["gn_bias"], rk, bd]
    (y_b,) = _rows("rwkv_post_fwd", post_fwd, [y_scan, s_r, s_k, s_v, s_g], post_params, [(R_WIDTH, bf)])
    p_a = _mm("branch_a", y_a, W["w_branch_a"], "nn")
    p_b = _mm("branch_b", y_b, W["w_branch_b"], "nn")

    def merge_fwd(pa, pb, gz, bgz):
        return _merge_core(pa, pb, gz[:, :D_MODEL], gz[:, D_MODEL:], bgz[:, :D_MODEL], bgz[:, D_MODEL:])

    (merged,) = _rows("merge_fwd", merge_fwd, [p_a, p_b, proj_gate], [b_gate], [(D_MODEL, bf)])
    mix = _mm("out_proj", merged, W["w_out"], "nn")
    ln1_params = [gt1, W["b_out"], W["ln1_g"], W["ln1_b"], sc2, sh2]
    h1, h2 = _rows("ln1_fwd", _ln1_core, [x, mix], ln1_params, [(D_MODEL, F32), (D_MODEL, bf)])

    a1 = _mm("ff1", h2, W["w_ff1"], "nn")
    (act,) = _rows("ff_act", lambda z, bb: jnp.square(jnp.maximum(z + bb, 0.0)), [a1], [W["b_ff1"]], [(D_FF, bf)])
    ff = _mm("ff2", act, W["w_ff2"], "nn")

    def ln2_loss(h1_, ff_, tg, gt2_, bff2, g, b):
        loss, vjp = jax.vjp(functools.partial(_ln2_loss_core, tg), h1_, ff_, gt2_, bff2, g, b)
        return vjp(jnp.ones((), F32)) + (loss,)

    ln2_params = [gt2, W["b_ff2"], W["ln2_g"], W["ln2_b"]]
    dh1, dff, dgt2, G["b_ff2"], G["ln2_g"], G["ln2_b"], loss_acc = _rows(
        "ln2_loss", ln2_loss, [h1, ff, tgt], ln2_params, [(D_MODEL, F32), (D_MODEL, bf)],
        accs=[(1, D_MODEL)] * 4 + [(1, LANES)])
    loss = loss_acc[0, 0]

    dact = _mm("ff2_dx", dff, W["w_ff2"], "nt")
    G["w_ff2"] = _mm("ff2_dw", act, dff, "tn", WIRE_DT).reshape(N_CHIPS, D_FF // N_CHIPS, D_MODEL)

    def act_bwd(z, da, bb):
        d = da * 2.0 * jnp.maximum(z + bb, 0.0)
        return d, _colsum(d)

    da1, G["b_ff1"] = _rows("ff_act_bwd", act_bwd, [a1, dact], [W["b_ff1"]], [(D_FF, bf)], accs=[(1, D_FF)])
    dh2 = _mm("ff1_dx", da1, W["w_ff1"], "nt")
    G["w_ff1"] = _mm("ff1_dw", h2, da1, "tn", WIRE_DT, out_split=N_CHIPS)

    def ln1_bwd(x_, mix_, dh1_, dh2_, *ps):
        _, vjp = jax.vjp(_ln1_core, x_, mix_, *ps)
        return vjp((dh1_, dh2_))

    dx_res, dmix, dgt1, G["b_out"], G["ln1_g"], G["ln1_b"], dsc2, dsh2 = _rows(
        "ln1_bwd", ln1_bwd, [x, mix, dh1, dh2], ln1_params, [(D_MODEL, F32), (D_MODEL, bf)],
        accs=[(1, D_MODEL)] * 6)

    dmerged = _mm("out_proj_dx", dmix, W["w_out"], "nt")
    G["w_out"] = _mm("out_proj_dw", merged, dmix, "tn", WIRE_DT).reshape(N_CHIPS, D_MODEL // N_CHIPS, D_MODEL)

    def merge_bwd(pa, pb, gz, dm, bgz):
        args = (pa.astype(F32), pb.astype(F32), gz[:, :D_MODEL], gz[:, D_MODEL:], bgz[:, :D_MODEL], bgz[:, D_MODEL:])
        _, vjp = jax.vjp(_merge_core, *args)
        dpa, dpb, dga, dgb, dbga, dbgb = vjp(dm)
        return dpa, dpb, jnp.concatenate([dga, dgb], axis=1), jnp.concatenate([dbga, dbgb], axis=1)

    dp_a, dp_b, dgates, db_gate = _rows(
        "merge_bwd", merge_bwd, [p_a, p_b, proj_gate, dmerged], [b_gate],
        [(D_MODEL, bf), (D_MODEL, bf), (2 * D_MODEL, bf)], accs=[(1, 2 * D_MODEL)])
    dy_a = _mm("branch_a_dx", dp_a, W["w_branch_a"], "nt")
    G["w_branch_a"] = _mm("branch_a_dw", y_a, dp_a, "tn", WIRE_DT, out_split=N_CHIPS)
    dy_b = _mm("branch_b_dx", dp_b, W["w_branch_b"], "nt")
    G["w_branch_b"] = _mm("branch_b_dw", y_b, dp_b, "tn", WIRE_DT, out_split=N_CHIPS)

    def post_bwd(y, r, k2, v, g, dyb, gain, bias, rk_, bd_):
        _, vjp = jax.vjp(functools.partial(_post_core, bd_), y, r, k2, v, g, gain, bias, rk_)
        return vjp(dyb)

    dy_scan, dr_p, dk_p, dv_p, dg_p, G["gn_gain"], G["gn_bias"], drk = _rows(
        "rwkv_post_bwd", post_bwd, [y_scan, s_r, s_k, s_v, s_g, dy_b], post_params,
        [(R_WIDTH, F32)] * 5, accs=[(1, R_WIDTH)] * 3)
    G["r_k"] = drk.reshape(R_HEADS, R_HEAD)
    dr_s, dw_s, dk_s, dv_s, da_s, db_s = yield "bwd", (s_r, s_w, s_k, s_v, s_a, s_b, states, dy_scan), G

    def pre_bwd(z, p, dr1, dr2, dw, dk1, dk2, dv1, dv2, da, db, dg,
                bz, mu, w0, wd, a0, wa, wg, k_k, k_a, bd_):
        prim = (*split4(z), *split4(p), *split4(bz), *split4(mu), w0, wd, a0, wa, wg, k_k, k_a)
        _, vjp = jax.vjp(functools.partial(_pre_core, rowmask_of(), bd_), *prim)
        d = vjp((dr1 + dr2, dw, dk1 + dk2, dv1 + dv2, da, db, dg))
        cat = lambda parts: jnp.concatenate(parts, axis=1)
        return (cat(d[0:4]), cat(d[4:8]), cat(d[8:12]), cat(d[12:16])) + tuple(d[16:])

    dz_r, dprev, db_r, dmu_p, G["w0"], dwd_p, G["a0"], dwa_p, dwg_p, G["k_k"], G["k_a"] = _rows(
        "rwkv_pre_bwd", pre_bwd,
        [proj_r, proj_r_prev, dr_s, dr_p, dw_s, dk_s, dk_p, dv_s, dv_p, da_s, db_s, dg_p],
        pre_params, [(R_COLS_PAD, bf)] * 2,
        accs=[(1, R_COLS_PAD), (1, R_COLS_PAD), (1, R_WIDTH), (LORA_PAD, R_WIDTH), (1, R_WIDTH),
              (LORA_PAD, R_WIDTH), (LORA_PAD, R_WIDTH), (1, R_WIDTH), (1, R_WIDTH)],
        tile=tile_pre)
    G["mu_shift"] = dmu_p[:, :R_COLS]
    G["w_decay_up"] = dwd_p[:DECAY_LORA]
    G["w_aaa_up"] = dwa_p[DECAY_LORA:DECAY_LORA + AAA_LORA]
    G["w_gate_up"] = dwg_p[DECAY_LORA + AAA_LORA:LORA]

    def gmlp_bwd(z, dya, bz, g, b, bst_, gsel_, ws_):
        gmask, tril = _gmlp_consts()
        (zu, zv), (bu, bv) = split2(z), split2(bz)
        wsl = [ws_[i * CHUNK:(i + 1) * CHUNK] for i in range(G_GROUPS)]
        _, vjp = jax.vjp(functools.partial(_gmlp_core, gmask, tril, gsel_), zu, zv, bu, bv, g, b, bst_, *wsl)
        d = vjp(dya)
        return (jnp.concatenate(d[0:2], axis=1), jnp.concatenate(d[2:4], axis=1), d[4], d[5], d[6],
                jnp.concatenate(d[7:], axis=0))

    dz_g, db_g, G["g_ln_v"], G["b_ln_v"], dbst, dws2 = _rows(
        "gmlp_bwd", gmlp_bwd, [proj_g, dy_a], [b_g, W["g_ln_v"], W["b_ln_v"], bst, gsel, ws2],
        [(2 * G_WIDTH, bf)],
        accs=[(1, 2 * G_WIDTH), (1, G_WIDTH), (1, G_WIDTH), (CHUNK, LANES), (G_GROUPS * CHUNK, CHUNK)],
        tile=CHUNK)
    G["w_spatial"] = dws2.reshape(G_GROUPS, CHUNK, CHUNK)
    G["b_spatial"] = dbst[:, :G_GROUPS].T

    def dproj_cat(dzg, dzr, dpv, dgz):
        return jnp.concatenate([dzg, (dzr.astype(F32) + dpv.astype(F32)).astype(dzg.dtype), dgz], axis=1)

    (dproj,) = _rows("dproj_cat", dproj_cat, [dz_g, dz_r, _shift_up(dprev), dgates], [],
                     [(2 * G_WIDTH + R_COLS_PAD + 2 * D_MODEL, bf)])
    dh = _mm("proj_dx", dproj, w_in_p, "nt", tk=2432)
    G["w_in"] = _split_cols(_unpad_in_cols(_mm("proj_dw", h, dproj, "tn", WIRE_DT, tm=512, tn=2432), 1))
    G["b_in"] = _unpad_in_cols(jnp.concatenate([db_g, db_r, db_gate], axis=1), 1)

    def mod1_bwd(x_, dh_, dxr, sc):
        return dh_ * (1.0 + sc) + dxr, _colsum(dh_ * x_), _colsum(dh_)

    grad_x, dsc1, dsh1 = _rows("mod1_bwd", mod1_bwd, [x, dh, dx_res], [sc1], [(D_MODEL, F32)],
                               accs=[(1, D_MODEL)] * 2)

    dmod = jnp.concatenate([dsh1, dsc1, dgt1, dsh2, dsc2, dgt2], axis=1)
    G["b_ada"] = dmod
    G["w_ada"] = _ada_dw(c.reshape(D_MODEL, 1), dmod)
    return loss, grad_x, G


BIG = (("w_ada", (D_MODEL, 6 * D_MODEL), 1), ("w_in", (D_MODEL, 2 * G_WIDTH + R_COLS + 2 * D_MODEL), 1),
       ("w_branch_a", (G_WIDTH, D_MODEL), 1), ("w_branch_b", (R_WIDTH, D_MODEL), 1),
       ("w_out", (D_MODEL, D_MODEL), 0), ("w_ff1", (D_MODEL, D_FF), 1), ("w_ff2", (D_FF, D_MODEL), 0))
LORAS = (("w_decay_up", (DECAY_LORA, R_WIDTH), 1), ("w_aaa_up", (AAA_LORA, R_WIDTH), 1),
         ("w_gate_up", (GATE_LORA, R_WIDTH), 1))
SHARDED = BIG + LORAS
SMALL = (("b_ada", (1, 6 * D_MODEL)), ("b_in", (1, 2 * G_WIDTH + R_COLS + 2 * D_MODEL)),
         ("g_ln_v", (1, G_WIDTH)), ("b_ln_v", (1, G_WIDTH)), ("w_spatial", (G_GROUPS, CHUNK, CHUNK)),
         ("b_spatial", (G_GROUPS, CHUNK)), ("mu_shift", (1, R_COLS)), ("w0", (1, R_WIDTH)),
         ("a0", (1, R_WIDTH)), ("k_k", (1, R_WIDTH)), ("k_a", (1, R_WIDTH)), ("r_k", (R_HEADS, R_HEAD)),
         ("gn_gain", (1, R_WIDTH)), ("gn_bias", (1, R_WIDTH)), ("b_out", (1, D_MODEL)),
         ("ln1_g", (1, D_MODEL)), ("ln1_b", (1, D_MODEL)), ("b_ff1", (1, D_FF)), ("b_ff2", (1, D_MODEL)),
         ("ln2_g", (1, D_MODEL)), ("ln2_b", (1, D_MODEL)))
WEIGHT_ORDER = ("w_ada", "b_ada", "w_in", "b_in", "g_ln_v", "b_ln_v", "w_spatial", "b_spatial", "mu_shift",
                "w0", "w_decay_up", "a0", "w_aaa_up", "w_gate_up", "k_k", "k_a", "r_k", "gn_gain", "gn_bias",
                "w_branch_a", "w_branch_b", "w_out", "b_out", "ln1_g", "ln1_b", "w_ff1", "b_ff1", "w_ff2",
                "b_ff2", "ln2_g", "ln2_b")
N_CHIPS = 4


def _shard_shape(shape, axis):
    s = list(shape)
    s[axis] //= N_CHIPS
    return tuple(s)


def _numel(shape):
    return int(np.prod(shape))


def _round_up(n, q):
    return -(-n // q) * q


PIECE_Q = 1024


def _piece(shape):
    return _round_up(_numel(shape), PIECE_Q)


N_LORA = sum(_piece(_shard_shape(s, a)) for _, s, a in LORAS)
N_SMALL = sum(_piece(s) for _, s in SMALL)
ROWS_SW = _round_up(N_LORA, PACK_Q) // PACK_W
ROWS_SG = _round_up(N_LORA + N_SMALL, PACK_Q) // PACK_W


def _flat_pieces(parts, dtype, total=None):
    out, n = [], 0
    for p in parts:
        out.append(p.reshape(-1).astype(dtype))
        pad = _piece(p.shape) - out[-1].shape[0]
        if pad:
            out.append(jnp.zeros((pad,), dtype))
        n += _piece(p.shape)
    if total is not None and total > n:
        out.append(jnp.zeros((total - n,), dtype))
    return jnp.concatenate(out)


def _pack_small(loras, small, rows, dtype):
    parts = [loras[n] for n, _, _ in LORAS] + ([small[n] for n, _ in SMALL] if small is not None else [])
    return _flat_pieces(parts, dtype, rows * PACK_W).reshape(rows, PACK_W)


def _unpack_small(pack, with_small):
    flat = pack.reshape(-1)
    out, off = {}, 0
    for n, s, a in LORAS:
        ss = _shard_shape(s, a)
        out[n] = flat[off:off + _numel(ss)].reshape(ss)
        off += _piece(ss)
    if with_small:
        for n, s in SMALL:
            out[n] = flat[off:off + _numel(s)].reshape(s)
            off += _piece(s)
    return out


def _pack_small_grads(G):
    common = _flat_pieces([G[n] for n, _ in SMALL], F32, ROWS_SG * PACK_W - N_LORA)
    segs = []
    for q in range(N_CHIPS):
        loras = [G[n][:, q * (s[1] // N_CHIPS):(q + 1) * (s[1] // N_CHIPS)] for n, s, _ in LORAS]
        segs.append(jnp.concatenate([_flat_pieces(loras, F32), common]).reshape(ROWS_SG, PACK_W))
    return jnp.stack(segs)


ANY = pl.BlockSpec(memory_space=pl.ANY)
MESH = pl.DeviceIdType.MESH


def _place():
    x, y, c = lax.axis_index("x"), lax.axis_index("y"), lax.axis_index("c")
    chips = [(1 - x, y), (x, 1 - y), (1 - x, 1 - y)]
    return x, y, c, chips


def _remote(src, dst, send_sem, recv_sem, to):
    return pltpu.make_async_remote_copy(src_ref=src, dst_ref=dst, send_sem=send_sem, recv_sem=recv_sem,
                                        device_id=to, device_id_type=MESH)


def _ag_copies(ins, outs, send_sems, recv_sems):
    x, y, c, chips = _place()
    s = 2 * x + y
    cps = []
    for a in range(len(ins)):
        H = ins[a].shape[0] // 2
        mine = ins[a].at[pl.ds(c * H, H)]
        cps += [_remote(mine, outs[a].at[2 * s + c], send_sems.at[6 * a + j], recv_sems.at[6 * a + j], (*chip, c))
                for j, chip in enumerate(chips)]
    return cps


def _ag_start(ins, outs, send_sems, recv_sems):
    for cp in _ag_copies(ins, outs, send_sems, recv_sems):
        cp.start()


def _ag_finish(ins, outs, send_sems, recv_sems):
    x, y, c, chips = _place()
    sibling = (x, y, 1 - c)
    slot = [2 * chip[0] + chip[1] for chip in chips]
    passed = []
    for a in range(len(ins)):
        for j in range(3):
            landed = outs[a].at[2 * slot[j] + c]
            _remote(landed, landed, send_sems.at[6 * a + j], recv_sems.at[6 * a + j], sibling).wait_recv()
            cp = _remote(landed, landed, send_sems.at[6 * a + 3 + j], recv_sems.at[6 * a + 3 + j], sibling)
            cp.start()
            passed.append(cp)
    for a in range(len(ins)):
        for j in range(3):
            got = outs[a].at[2 * slot[j] + 1 - c]
            _remote(got, got, send_sems.at[6 * a + 3 + j], recv_sems.at[6 * a + 3 + j], sibling).wait_recv()
    for cp in _ag_copies(ins, outs, send_sems, recv_sems) + passed:
        cp.wait_send()


def _ag_out_shapes(shards):
    return [jax.ShapeDtypeStruct((2 * N_CHIPS, a.shape[0] // 2, a.shape[1]), a.dtype) for a in shards]


def _ag_sems(n):
    return [pltpu.SemaphoreType.DMA((6 * n,)), pltpu.SemaphoreType.DMA((6 * n,))]


def _all_gather(shards):
    n = len(shards)

    def body(*refs):
        ins, outs, sems = refs[:n], refs[n:2 * n], refs[2 * n:]
        _ag_start(ins, outs, *sems)
        _ag_finish(ins, outs, *sems)

    return pl.pallas_call(
        body, name="ag_weights", in_specs=[ANY] * n, out_specs=[ANY] * n,
        out_shape=_ag_out_shapes(shards), scratch_shapes=_ag_sems(n),
    )(*shards)


def _rs_sibling_in(gps):
    n = len(gps)

    def body(*refs):
        ins, outs = refs[:n], refs[n:2 * n]
        send_sems, recv_sems = refs[2 * n:]
        x, y, c, _ = _place()
        cps = []
        for a in range(n):
            H = gps[a].shape[1] // 2
            cps += [_remote(ins[a].at[q, pl.ds((1 - c) * H, H)], outs[a].at[q], send_sems.at[N_CHIPS * a + q],
                            recv_sems.at[N_CHIPS * a + q], (x, y, 1 - c)) for q in range(N_CHIPS)]
        for cp in cps:
            cp.start()
        for cp in cps:
            cp.wait()

    return pl.pallas_call(
        body, name="rs_sibling_in", in_specs=[ANY] * n, out_specs=[ANY] * n,
        out_shape=[jax.ShapeDtypeStruct((N_CHIPS, g.shape[1] // 2, g.shape[2]), g.dtype) for g in gps],
        scratch_shapes=[pltpu.SemaphoreType.DMA((N_CHIPS * n,)), pltpu.SemaphoreType.DMA((N_CHIPS * n,))],
    )(*gps)


def _rs_add_own(name, gp, got, c_arr, tr=256):
    H, C = got.shape[1:]
    tr = _pick(H, tr, 8)
    nb = H // tr

    def body(c_ref, g_ref, r_ref, o_ref):
        o_ref[...] = (g_ref[...].astype(F32) + r_ref[...].astype(F32)).astype(o_ref.dtype)

    return pl.pallas_call(
        body, name="rs_add_own_" + name,
        grid_spec=pltpu.PrefetchScalarGridSpec(
            num_scalar_prefetch=1, grid=(N_CHIPS, nb),
            in_specs=[pl.BlockSpec((1, tr, C), lambda q, i, c_ref: (q, c_ref[0] * nb + i, 0)),
                      pl.BlockSpec((1, tr, C), lambda q, i, c_ref: (q, i, 0))],
            out_specs=pl.BlockSpec((1, tr, C), lambda q, i, c_ref: (q, i, 0))),
        out_shape=jax.ShapeDtypeStruct((N_CHIPS, H, C), gp.dtype),
        compiler_params=pltpu.CompilerParams(dimension_semantics=("arbitrary", "arbitrary")),
    )(c_arr, gp, got)


def _rs_chips(parts):
    n = len(parts)

    def body(*refs):
        ins, outs, sems = refs[:n], refs[n:2 * n], refs[2 * n:]
        _rsc_start(ins, outs, *sems)
        _rsc_finish(ins, outs, *sems)

    return pl.pallas_call(
        body, name="rs_chips", in_specs=[ANY] * n, out_specs=[ANY] * n,
        out_shape=[jax.ShapeDtypeStruct(p.shape, p.dtype) for p in parts], scratch_shapes=_rsc_sems(n),
    )(*parts)


def _rsc_copies(ins, outs, send_sems, recv_sems):
    x, y, c, chips = _place()
    s = 2 * x + y
    return [_remote(ins[a].at[2 * chip[0] + chip[1]], outs[a].at[s], send_sems.at[3 * a + j], recv_sems.at[3 * a + j],
                    (*chip, c)) for a in range(len(ins)) for j, chip in enumerate(chips)]


def _rsc_start(ins, outs, send_sems, recv_sems):
    for cp in _rsc_copies(ins, outs, send_sems, recv_sems):
        cp.start()


def _rsc_finish(ins, outs, send_sems, recv_sems):
    x, y, c, chips = _place()
    for cp in _rsc_copies(ins, outs, send_sems, recv_sems):
        cp.wait_send()
    for a in range(len(ins)):
        for j, chip in enumerate(chips):
            got = outs[a].at[2 * chip[0] + chip[1]]
            _remote(got, got, send_sems.at[3 * a + j], recv_sems.at[3 * a + j], (*chip, c)).wait_recv()


def _rsc_sems(n):
    return [pltpu.SemaphoreType.DMA((3 * n,)), pltpu.SemaphoreType.DMA((3 * n,))]


def _rs_add_chips(name, part, slots, sc_arr, tr=128):
    H, C = slots.shape[1:]
    tr = _pick(H, tr, 8)
    nb = H // tr

    def body(sc_ref, p_ref, s_ref, o_ref):
        acc = None
        for q in range(N_CHIPS):
            term = lax.cond(sc_ref[0] == q, lambda: p_ref[0].astype(F32), lambda q=q: s_ref[q].astype(F32))
            acc = term if acc is None else acc + term
        o_ref[...] = acc

    return pl.pallas_call(
        body, name="rs_add_chips_" + name,
        grid_spec=pltpu.PrefetchScalarGridSpec(
            num_scalar_prefetch=1, grid=(nb,),
            in_specs=[pl.BlockSpec((1, tr, C), lambda i, sc: (sc[0], i, 0)),
                      pl.BlockSpec((N_CHIPS, tr, C), lambda i, sc: (0, i, 0))],
            out_specs=pl.BlockSpec((tr, C), lambda i, sc: (sc[1] * nb + i, 0))),
        out_shape=jax.ShapeDtypeStruct((2 * H, C), F32),
        compiler_params=pltpu.CompilerParams(dimension_semantics=("arbitrary",)),
    )(sc_arr, part, slots)


def _rs_sibling_out(wholes):
    n = len(wholes)

    def body(*refs):
        ins, outs = refs[:n], refs[n:2 * n]
        send_sems, recv_sems = refs[2 * n:]
        x, y, c, _ = _place()
        cps = []
        for a in range(n):
            H = wholes[a].shape[0] // 2
            cps.append(_remote(ins[a].at[pl.ds(c * H, H)], outs[a].at[pl.ds(c * H, H)], send_sems.at[a], recv_sems.at[a],
                               (x, y, 1 - c)))
        for cp in cps:
            cp.start()
        for a in range(n):
            H = wholes[a].shape[0] // 2
            cps[a].wait_send()
            got = outs[a].at[pl.ds((1 - c) * H, H)]
            _remote(got, got, send_sems.at[a], recv_sems.at[a], (x, y, 1 - c)).wait_recv()

    return pl.pallas_call(
        body, name="rs_sibling_out", in_specs=[ANY] * n, out_specs=[ANY] * n,
        out_shape=[jax.ShapeDtypeStruct(w.shape, w.dtype) for w in wholes],
        input_output_aliases={a: a for a in range(n)},
        scratch_shapes=[pltpu.SemaphoreType.DMA((n,)), pltpu.SemaphoreType.DMA((n,))],
    )(*wholes)


def _adamw(name, w, g, m, v):
    def fn(w_, g_, m_, v_):
        m2 = ADAM_B1 * m_ + (1.0 - ADAM_B1) * g_
        v2 = ADAM_B2 * v_ + (1.0 - ADAM_B2) * (g_ * g_)
        m_hat = m2 / (1.0 - ADAM_B1 ** ADAM_STEP)
        v_hat = v2 / (1.0 - ADAM_B2 ** ADAM_STEP)
        return -ADAM_LR * (m_hat / (jnp.sqrt(v_hat) + ADAM_EPS) + ADAM_WD * w_), m2, v2

    return _rows("adamw_" + name, fn, [w, g, m, v], [], [(w.shape[1], F32)] * 3, tile=_pick(w.shape[0], 256, 8))


def kernel(x, c, w_ada, b_ada, w_in, b_in, g_ln_v, b_ln_v, w_spatial, b_spatial, mu_shift, w0, w_decay_up, a0, w_aaa_up, w_gate_up, k_k, k_a, r_k, gn_gain, gn_bias, w_branch_a, w_branch_b, w_out, b_out, ln1_g, ln1_b, w_ff1, b_ff1, w_ff2, b_ff2, ln2_g, ln2_b, loss_target, m_w_ada, m_b_ada, m_w_in, m_b_in, m_g_ln_v, m_b_ln_v, m_w_spatial, m_b_spatial, m_mu_shift, m_w0, m_w_decay_up, m_a0, m_w_aaa_up, m_w_gate_up, m_k_k, m_k_a, m_r_k, m_gn_gain, m_gn_bias, m_w_branch_a, m_w_branch_b, m_w_out, m_b_out, m_ln1_g, m_ln1_b, m_w_ff1, m_b_ff1, m_w_ff2, m_b_ff2, m_ln2_g, m_ln2_b, v_w_ada, v_b_ada, v_w_in, v_b_in, v_g_ln_v, v_b_ln_v, v_w_spatial, v_b_spatial, v_mu_shift, v_w0, v_w_decay_up, v_a0, v_w_aaa_up, v_w_gate_up, v_k_k, v_k_a, v_r_k, v_gn_gain, v_gn_bias, v_w_branch_a, v_w_branch_b, v_w_out, v_b_out, v_ln1_g, v_ln1_b, v_w_ff1, v_b_ff1, v_w_ff2, v_b_ff2, v_ln2_g, v_ln2_b):
    args = dict(locals())
    local_shape = {n: _shard_shape(s, a) for n, s, a in SHARDED}
    local_shape.update(dict(SMALL))
    wts = {n: args[n].reshape(local_shape[n]) for n in WEIGHT_ORDER}
    mom = {n: args["m_" + n].reshape(local_shape[n]) for n in WEIGHT_ORDER}
    var = {n: args["v_" + n].reshape(local_shape[n]) for n in WEIGHT_ORDER}
    big = [n for n, _, _ in BIG]
    late, early = [n for n in big if n in LATE], [n for n in big if n not in LATE]
    chip = 2 * lax.axis_index("x") + lax.axis_index("y")
    c_arr = lax.axis_index("c").astype(jnp.int32).reshape(1)
    sc_arr = jnp.stack([chip, lax.axis_index("c")]).astype(jnp.int32)
    mine = {n: wts[n].astype(MXU_DT) for n in big}

    def whole(n, got):
        _, s, a = next(t for t in BIG if t[0] == n)
        g = lax.dynamic_update_slice(got.reshape((N_CHIPS,) + mine[n].shape), mine[n][None], (chip, 0, 0))
        return g if a == 1 else g.reshape(s)

    def sibling_sums(names, G, small):
        gps = [G[n] for n in names] + ([_pack_small_grads(G)] if small else [])
        names = names + (["small"] if small else [])
        return [_rs_add_own(n, g, r, c_arr) for n, g, r in zip(names, gps, _rs_sibling_in(gps))]

    small_w = _pack_small(wts, None, ROWS_SW, MXU_DT)
    gath = _all_gather([mine[n] for n in early] + [small_w])
    W = {n: wts[n] for n, _ in SMALL}
    W.update({n: whole(n, g) for n, g in zip(early, gath)})
    small_g = lax.dynamic_update_slice(gath[-1].reshape((N_CHIPS,) + small_w.shape), small_w[None], (chip, 0, 0))
    lora_q = [_unpack_small(small_g[q], False) for q in range(N_CHIPS)]
    for n, _, _ in LORAS:
        W[n] = jnp.concatenate([lora_q[q][n] for q in range(N_CHIPS)], axis=1)

    step = _local_step(x[0], c, loss_target[0], W)
    _, scan_in = next(step)
    y_scan, states, got = _scan_fwd(*scan_in, gather=[mine[n] for n in late])
    _, scan_in, G = step.send((y_scan, states, {n: whole(n, g) for n, g in zip(late, got)}))
    parts_late = sibling_sums(late, G, False)
    res = _scan_bwd(*scan_in, scatter=parts_late)
    try:
        step.send(tuple(res[:6]))
    except StopIteration as done:
        loss, grad_x, G = done.value
    loss = lax.psum(loss, MESH_AXES)
    parts_early = sibling_sums(early, G, True)
    names = late + early + ["small"]
    parts, slots = parts_late + parts_early, list(res[6:]) + list(_rs_chips(parts_early))
    segs = _rs_sibling_out([_rs_add_chips(n, p, s, sc_arr) for n, p, s in zip(names, parts, slots)])

    out = {}
    for n, g in zip(names[:-1], segs[:-1]):
        out[n] = (g,) + tuple(_adamw(n, wts[n], g, mom[n], var[n]))
    packs = [_pack_small(t, t, ROWS_SG, F32) for t in (wts, mom, var)]
    small4 = [_unpack_small(t, True) for t in (segs[-1],) + tuple(_adamw("small", packs[0], segs[-1], packs[1], packs[2]))]
    res = [loss, grad_x[None]]
    for k in range(4):
        res += [(out[n][k] if n in out else small4[k][n]).reshape(args[n].shape) for n in WEIGHT_ORDER]
    return tuple(res)
```

```python
import functools

import numpy as np
import jax
import jax.numpy as jnp
from jax import lax
from jax.experimental import pallas as pl
from jax.experimental.pallas import tpu as pltpu

F32 = jnp.float32
MXU_DT = jnp.bfloat16
WIRE_DT = jnp.bfloat16

D_MODEL = 1024
G_GROUPS = 8
G_WIDTH = 512
CHUNK = 128
R_WIDTH = 512
R_HEAD = 64
R_HEADS = 8
DECAY_LORA = 32
AAA_LORA = 32
GATE_LORA = 96
LORA = DECAY_LORA + AAA_LORA + GATE_LORA
LORA_PAD = 256
R_COLS = 3 * R_WIDTH + LORA
R_COLS_PAD = 3 * R_WIDTH + LORA_PAD
D_FF = 4 * D_MODEL
ALPHA = 2.0 ** 0.25
LN_EPS = 1e-5
GN_EPS = 64e-5
ADAM_LR = 0.001
ADAM_B1 = 0.9
ADAM_B2 = 0.999
ADAM_EPS = 1e-08
ADAM_WD = 0.01
ADAM_STEP = 10

LANES = 128
PACK_W = 512
PACK_Q = 2 * 16 * PACK_W
VMEM_LIMIT = 48 * 1024 * 1024
SCAN_T = 64

MESH_AXES = ("x", "y", "c")


def _dg(a, b, dims):
    return lax.dot_general(a.astype(MXU_DT), b.astype(MXU_DT), (dims, ((), ())),
                           preferred_element_type=F32)


@jax.custom_vjp
def _bdot(a, b):
    return _dg(a, b, ((1,), (0,)))


def _bdot_fwd(a, b):
    return _bdot(a, b), (a, b)


def _bdot_bwd(res, g):
    a, b = res
    return (_dg(g, b, ((1,), (1,))).astype(a.dtype), _dg(a, g, ((0,), (0,))).astype(b.dtype))


_bdot.defvjp(_bdot_fwd, _bdot_bwd)


def _split_dot(x, m, dims):
    hi = x.astype(jnp.bfloat16)
    lo = (x - hi.astype(F32)).astype(jnp.bfloat16)
    dn = (dims, ((), ()))
    return (lax.dot_general(hi, m, dn, preferred_element_type=F32)
            + lax.dot_general(lo, m, dn, preferred_element_type=F32))


@jax.custom_vjp
def _pdot(x, m):
    return _split_dot(x, m, ((1,), (0,)))


def _pdot_fwd(x, m):
    return _pdot(x, m), m


def _pdot_bwd(m, g):
    return _split_dot(g, m, ((1,), (1,))), None


_pdot.defvjp(_pdot_fwd, _pdot_bwd)


def _sigmoid(x):
    return 1.0 / (1.0 + jnp.exp(-x))


def _softplus(x):
    return jnp.maximum(x, 0.0) + jnp.log(1.0 + jnp.exp(-jnp.maximum(x, -x)))


def _gelu(x):
    return 0.5 * x * (1.0 + jnp.tanh(0.7978845608028654 * (x + 0.044715 * (x * x * x))))


def _ln(x, g, b, eps):
    mu = jnp.mean(x, axis=-1, keepdims=True)
    xc = x - mu
    var = jnp.mean(xc * xc, axis=-1, keepdims=True)
    return xc * lax.rsqrt(var + eps) * g + b


def _colsum(x):
    return jnp.sum(x, axis=0, keepdims=True)


def _pick(n, target, q=LANES):
    if n <= target:
        return n
    best = None
    for t in range(q, target + 1, q):
        if n % t == 0:
            best = t
    assert best is not None, (n, target)
    return best


def _mm(name, a, b, mode, out_dtype=F32, out_split=1, tm=1024, tn=1024, tk=1024):
    bs = b.shape[0] if b.ndim == 3 else 1
    br, bc = b.shape[-2:]
    if mode == "nn":
        (M, K), K2, N = a.shape, br, bc * bs
    elif mode == "nt":
        (M, K), N, K2 = a.shape, br, bc * bs
    else:
        assert bs == 1
        (K, M), K2, N = a.shape, br, bc
    assert K == K2, (name, a.shape, b.shape, mode)
    n_piece = N // max(bs if mode == "nn" else 1, out_split)
    k_piece = K // (bs if mode == "nt" else 1)
    tm, tn, tk = _pick(M, tm, 8 if M < LANES else LANES), _pick(n_piece, tn), _pick(k_piece, tk)
    nk, npj, npk = K // tk, n_piece // tn, k_piece // tk
    dims = {"nn": ((1,), (0,)), "nt": ((1,), (1,)), "tn": ((0,), (0,))}[mode]

    def body(a_ref, b_ref, o_ref, acc_ref):
        k = pl.program_id(2)

        @pl.when(k == 0)
        def _():
            acc_ref[...] = jnp.zeros(acc_ref.shape, F32)

        acc_ref[...] += _dg(a_ref[...], b_ref[0] if bs > 1 else b_ref[...], dims)

        @pl.when(k == nk - 1)
        def _():
            if out_split > 1:
                o_ref[0] = acc_ref[...].astype(o_ref.dtype)
            else:
                o_ref[...] = acc_ref[...].astype(o_ref.dtype)

    if mode == "nn":
        a_spec = pl.BlockSpec((tm, tk), lambda i, j, k: (i, k))
        b_spec = (pl.BlockSpec((tk, tn), lambda i, j, k: (k, j)) if bs == 1 else
                  pl.BlockSpec((1, tk, tn), lambda i, j, k: (j // npj, k, j % npj)))
    elif mode == "nt":
        a_spec = pl.BlockSpec((tm, tk), lambda i, j, k: (i, k))
        b_spec = (pl.BlockSpec((tn, tk), lambda i, j, k: (j, k)) if bs == 1 else
                  pl.BlockSpec((1, tn, tk), lambda i, j, k: (k // npk, j, k % npk)))
    else:
        a_spec = pl.BlockSpec((tk, tm), lambda i, j, k: (k, i))
        b_spec = pl.BlockSpec((tk, tn), lambda i, j, k: (k, j))
    if out_split > 1:
        o_spec = pl.BlockSpec((1, tm, tn), lambda i, j, k: (j // npj, i, j % npj))
        o_shape = jax.ShapeDtypeStruct((out_split, M, n_piece), out_dtype)
    else:
        o_spec = pl.BlockSpec((tm, tn), lambda i, j, k: (i, j))
        o_shape = jax.ShapeDtypeStruct((M, N), out_dtype)
    return pl.pallas_call(
        body, name=name, grid=(M // tm, N // tn, nk),
        in_specs=[a_spec, b_spec], out_specs=o_spec, out_shape=o_shape,
        scratch_shapes=[pltpu.VMEM((tm, tn), F32)],
        compiler_params=pltpu.CompilerParams(
            dimension_semantics=("parallel", "parallel", "arbitrary"), vmem_limit_bytes=VMEM_LIMIT),
    )(a, b)


def _rows(name, fn, rows, params, outs, accs=(), tile=256):
    S = rows[0].shape[0]
    tile = min(tile, S)
    assert S % tile == 0, (name, S, tile)
    nr, npar, no, na = len(rows), len(params), len(outs), len(accs)

    def body(*refs):
        rin, pin = refs[:nr], refs[nr:nr + npar]
        oref, aref = refs[nr + npar:nr + npar + no], refs[nr + npar + no:]
        res = fn(*[r[...] for r in rin], *[p[...] for p in pin])
        if not isinstance(res, (tuple, list)):
            res = (res,)
        assert len(res) == no + na, (name, len(res), no, na)
        for ref, val in zip(oref, res[:no]):
            ref[...] = val.astype(ref.dtype)
        if na:
            @pl.when(pl.program_id(0) == 0)
            def _():
                for ref in aref:
                    ref[...] = jnp.zeros(ref.shape, ref.dtype)

            for ref, val in zip(aref, res[no:]):
                ref[...] += jnp.broadcast_to(val, ref.shape).astype(ref.dtype)

    def whole(shape):
        nd = len(shape)
        return pl.BlockSpec(tuple(shape), lambda i: (0,) * nd)

    in_specs = ([pl.BlockSpec((tile, r.shape[1]), lambda i: (i, 0)) for r in rows]
                + [whole(p.shape) for p in params])
    out_specs = ([pl.BlockSpec((tile, n), lambda i: (i, 0)) for n, _ in outs]
                 + [whole(s) for s in accs])
    out_shape = ([jax.ShapeDtypeStruct((S, n), dt) for n, dt in outs]
                 + [jax.ShapeDtypeStruct(tuple(s), F32) for s in accs])
    res = pl.pallas_call(
        body, name=name, grid=(S // tile,), in_specs=in_specs, out_specs=out_specs,
        out_shape=out_shape,
        compiler_params=pltpu.CompilerParams(
            dimension_semantics=("arbitrary",), vmem_limit_bytes=VMEM_LIMIT),
    )(*rows, *params)
    return res


def _modulate(x, sc, sh):
    return x * (1.0 + sc) + sh


def _gmlp_consts():
    lane = lax.broadcasted_iota(jnp.int32, (1, G_WIDTH), 1)
    gmask = [(lane // (G_WIDTH // G_GROUPS) == g).astype(F32) for g in range(G_GROUPS)]
    tril = (lax.broadcasted_iota(jnp.int32, (CHUNK, CHUNK), 0)
            >= lax.broadcasted_iota(jnp.int32, (CHUNK, CHUNK), 1))
    return gmask, tril


def _gmlp_core(gmask, tril, gsel, zu, zv, bu, bv, g, b, bst, *ws):
    u = _gelu(zu + bu)
    v = _ln(_gelu(zv + bv), g, b, LN_EPS)
    s = _pdot(bst, gsel)
    for gi in range(G_GROUPS):
        s = s + _bdot(jnp.where(tril, ws[gi], 0.0), v * gmask[gi])
    return u * s


def _pre_core(rowmask, bd, zr, zk, zv, zl, pr, pk, pv, pq, br, bk, bv, bl, mr, mk, mv, ml,
              w0, wd, a0, wa, wg, k_k, k_a):
    def mix(z, p, b, mu):
        zz = z + b
        return zz + ((p + b) * rowmask - zz) * mu

    r, k, v, l = mix(zr, pr, br, mr), mix(zk, pk, bk, mk), mix(zv, pv, bv, mv), mix(zl, pq, bl, ml)
    w_log = -_softplus(-(w0 + _bdot(jnp.tanh(l), wd))) - 0.5
    decay = jnp.exp(-jnp.exp(w_log))
    a = _sigmoid(a0 + _bdot(l, wa))
    g = _bdot(_sigmoid(l), wg)
    kk = k * k_k
    kkn = kk / jnp.maximum(jnp.sqrt(_pdot(kk * kk, bd)), 1e-12)
    k2 = k * (1.0 + (a - 1.0) * k_a)
    return r, decay, k2, v, -kkn, kkn * a, g


def _post_core(bd, y, r, k2, v, g, gain, bias, rk):
    inv = 1.0 / R_HEAD
    mu = _pdot(y, bd) * inv
    yc = y - mu
    var = _pdot(yc * yc, bd) * inv
    yn = yc * lax.rsqrt(var + GN_EPS) * gain + bias
    bonus = _pdot(r * k2 * rk, bd) * v
    return (yn + bonus) * g


def _merge_core(pa, pb, ga, gb, bga, bgb):
    return _sigmoid(ga + bga) * pa + _sigmoid(gb + bgb) * pb


def _ln1_core(x, mix, gt1, bout, g, b, sc2, sh2):
    h1 = _ln(ALPHA * x + gt1 * (mix + bout), g, b, LN_EPS)
    return h1, h1 * (1.0 + sc2) + sh2


def _ln2_loss_core(tgt, h1, ff, gt2, bff2, g, b):
    out = _ln(ALPHA * h1 + gt2 * (ff + bff2), g, b, LN_EPS)
    err = out - tgt
    return 0.5 * jnp.sum(err * err) * (1.0 / D_MODEL)


def _scan_consts():
    sub = lax.broadcasted_iota(jnp.int32, (R_HEAD, LANES), 0)
    lane = lax.broadcasted_iota(jnp.int32, (R_HEAD, LANES), 1)
    return lane < R_HEAD, sub == (lane & (R_HEAD - 1))


def _seg_sum(lo, xb):
    s_lo = jnp.sum(jnp.where(lo, xb, 0.0), axis=1, keepdims=True)
    s_hi = jnp.sum(jnp.where(lo, 0.0, xb), axis=1, keepdims=True)
    return jnp.where(lo, s_lo, s_hi)


def _seg_dot(lo, s, row):
    lo_row = lo[0:1, :]
    s_lo = jnp.sum(s * jnp.where(lo_row, row, 0.0), axis=1, keepdims=True)
    s_hi = jnp.sum(s * jnp.where(lo_row, 0.0, row), axis=1, keepdims=True)
    return jnp.where(lo, s_lo, s_hi)


def _row_of_col(eye, colb):
    return jnp.sum(jnp.where(eye, colb, 0.0), axis=0, keepdims=True)


def _head_ones():
    i = lax.broadcasted_iota(jnp.int32, (LANES, LANES), 0) // R_HEAD
    j = lax.broadcasted_iota(jnp.int32, (LANES, LANES), 1) // R_HEAD
    return (i == j).astype(jnp.bfloat16)


N_SPLIT = 1


def _split(x):
    parts, r = [], x
    for u in range(N_SPLIT):
        p = r.astype(jnp.bfloat16)
        parts.append(p)
        if u + 1 < N_SPLIT:
            r = r - p.astype(F32)
    return parts


def _ones_dot(parts, ones, n):
    res = lax.dot_general(jnp.concatenate(parts, axis=0), ones, (((1,), (0,)), ((), ())),
                          preferred_element_type=F32)
    out = []
    for m in range(n):
        t = [res[(N_SPLIT * m + u) * R_HEAD:(N_SPLIT * m + u + 1) * R_HEAD] for u in range(N_SPLIT)]
        out.append(functools.reduce(lambda p, q: p + q, t))
    return out


def _seg_sums_mxu(ones, mats):
    return _ones_dot([p for m in mats for p in _split(m)], ones, len(mats))


def _cols_of_rows_mxu(eye, ones, rows8):
    terms = [p.astype(F32) for p in _split(rows8)]
    parts = [jnp.where(eye, jnp.broadcast_to(t[i:i + 1, :], eye.shape), 0.0).astype(jnp.bfloat16)
             for i in range(rows8.shape[0]) for t in terms]
    return _ones_dot(parts, ones, rows8.shape[0])


N_BLK = R_WIDTH // LANES
ROW_GROUP = 8


def _scan_fwd(r, w, k, v, a, b, gather=()):
    S = r.shape[0]
    T = min(SCAN_T, S)
    nchunk = S // T
    ng = len(gather)

    def body(*refs):
        r_ref, w_ref, k_ref, v_ref, a_ref, b_ref = refs[:6]
        g_in, (y_ref, sv_ref), g_out = refs[6:6 + ng], refs[6 + ng:8 + ng], refs[8 + ng:8 + 2 * ng]
        st_ref, sems = refs[8 + 2 * ng], refs[9 + 2 * ng:]
        lo, eye = _scan_consts()
        ones = _head_ones()

        @pl.when(pl.program_id(0) == 0)
        def _():
            st_ref[...] = jnp.zeros(st_ref.shape, F32)
            if ng:
                _ag_start(g_in, g_out, *sems)

        sub8 = lax.broadcasted_iota(jnp.int32, (ROW_GROUP, LANES), 0)

        def group(gi, state):
            base = pl.multiple_of(gi * ROW_GROUP, ROW_GROUP)
            state = list(state)
            sls = [slice(q * LANES, (q + 1) * LANES) for q in range(N_BLK)]
            ld = lambda ref: [ref[pl.ds(base, ROW_GROUP), sl] for sl in sls]
            r8, w8, k8, v8, a8, b8 = ld(r_ref), ld(w_ref), ld(k_ref), ld(v_ref), ld(a_ref), ld(b_ref)
            vb = [_cols_of_rows_mxu(eye, ones, v8[q]) for q in range(N_BLK)]
            an = [pltpu.roll(a8[q], ROW_GROUP - 1, 0) for q in range(N_BLK)]
            ap = [w8[q] * an[q] for q in range(N_BLK)]
            beta = [_seg_sum(lo[:ROW_GROUP], b8[q] * an[q]) for q in range(N_BLK)]
            kappa = [_seg_sum(lo[:ROW_GROUP], k8[q] * an[q]) for q in range(N_BLK)]
            y8 = [jnp.zeros((ROW_GROUP, LANES), F32)] * N_BLK

            def emit_y(i, y8_):
                ycol = _seg_sums_mxu(ones, [state[q] * r8[q][i:i + 1, :] for q in range(N_BLK)])
                return [jnp.where(sub8 == i, _row_of_col(eye, ycol[q]), y8_[q]) for q in range(N_BLK)]

            for i in range(0, ROW_GROUP, 2):
                row = lambda t8, d=0: t8[i + d:i + d + 1, :]
                sa1 = [None] * N_BLK
                for q in range(N_BLK):
                    s = state[q]
                    sv_ref[base + i, :, sls[q]] = s
                    sa0 = _seg_dot(lo, s, row(a8[q]))
                    nxt = _seg_dot(lo, s, row(ap[q]))
                    sa1[q] = nxt + row(beta[q]) * sa0 + row(kappa[q]) * vb[q][i]
                    state[q] = s * row(w8[q]) + sa0 * row(b8[q]) + vb[q][i] * row(k8[q])
                y8 = emit_y(i, y8)
                for q in range(N_BLK):
                    s = state[q]
                    sv_ref[base + i + 1, :, sls[q]] = s
                    state[q] = s * row(w8[q], 1) + sa1[q] * row(b8[q], 1) + vb[q][i + 1] * row(k8[q], 1)
                y8 = emit_y(i + 1, y8)
            for q in range(N_BLK):
                y_ref[pl.ds(base, ROW_GROUP), sls[q]] = y8[q]
            return tuple(state)

        init = tuple(st_ref[:, q * LANES:(q + 1) * LANES] for q in range(N_BLK))
        fin = lax.fori_loop(0, T // ROW_GROUP, group, init)
        for q in range(N_BLK):
            st_ref[:, q * LANES:(q + 1) * LANES] = fin[q]

        if ng:
            @pl.when(pl.program_id(0) == nchunk - 1)
            def _():
                _ag_finish(g_in, g_out, *sems)

    blk = pl.BlockSpec((T, R_WIDTH), lambda i: (i, 0))
    res = pl.pallas_call(
        body, name="scan_fwd", grid=(nchunk,), in_specs=[blk] * 6 + [ANY] * ng,
        out_specs=[blk, pl.BlockSpec((T, R_HEAD, R_WIDTH), lambda i: (i, 0, 0))] + [ANY] * ng,
        out_shape=[jax.ShapeDtypeStruct((S, R_WIDTH), F32),
                   jax.ShapeDtypeStruct((S, R_HEAD, R_WIDTH), F32)] + _ag_out_shapes(gather),
        scratch_shapes=[pltpu.VMEM((R_HEAD, R_WIDTH), F32)] + (_ag_sems(ng) if ng else []),
        compiler_params=pltpu.CompilerParams(
            dimension_semantics=("arbitrary",), vmem_limit_bytes=VMEM_LIMIT),
    )(r, w, k, v, a, b, *gather)
    return res[0], res[1], list(res[2:])


def _scan_bwd(r, w, k, v, a, b, states, dy, scatter=()):
    S = r.shape[0]
    T = min(SCAN_T, S)
    nchunk = S // T
    ns = len(scatter)

    def body(*refs):
        r_ref, w_ref, k_ref, v_ref, a_ref, b_ref, sv_ref, dy_ref = refs[:8]
        x_in, x_out = refs[8:8 + ns], refs[14 + ns:14 + 2 * ns]
        dr_ref, dw_ref, dk_ref, dv_ref, da_ref, db_ref = refs[8 + ns:14 + ns]
        ds_ref, sems = refs[14 + 2 * ns], refs[15 + 2 * ns:]
        lo, eye = _scan_consts()
        ones = _head_ones()

        @pl.when(pl.program_id(0) == 0)
        def _():
            ds_ref[...] = jnp.zeros(ds_ref.shape, F32)
            if ns:
                _rsc_start(x_in, x_out, *sems)

        sub8 = lax.broadcasted_iota(jnp.int32, (ROW_GROUP, LANES), 0)

        def bgroup(n, dstate, chunk_end):
            base = (T // ROW_GROUP - 1 - n) * ROW_GROUP
            if not isinstance(n, int):
                base = pl.multiple_of(base, ROW_GROUP)
            dstate = list(dstate)
            sls = [slice(q * LANES, (q + 1) * LANES) for q in range(N_BLK)]
            ld = lambda ref: [ref[pl.ds(base, ROW_GROUP), sl] for sl in sls]
            r8, w8, k8, v8, a8, b8, dy8 = (ld(r_ref), ld(w_ref), ld(k_ref), ld(v_ref), ld(a_ref), ld(b_ref),
                                           ld(dy_ref))
            vbs = [_cols_of_rows_mxu(eye, ones, v8[q]) for q in range(N_BLK)]
            dycs = [_cols_of_rows_mxu(eye, ones, dy8[q]) for q in range(N_BLK)]
            acc = [{n_: jnp.zeros((ROW_GROUP, LANES), F32) for n_ in ("r", "w", "k", "v", "a", "b")}
                   for _ in range(N_BLK)]
            for i in reversed(range(ROW_GROUP)):
                row = lambda t8: t8[i:i + 1, :]
                put = lambda q_, n_, val: acc[q_].__setitem__(n_, jnp.where(sub8 == i, val, acc[q_][n_]))
                dks = []
                for q in range(N_BLK):
                    sp, vb = sv_ref[base + i, :, sls[q]], vbs[q][i]
                    wr, ar, br, kr, rr = row(w8[q]), row(a8[q]), row(b8[q]), row(k8[q]), row(r8[q])
                    sa = _seg_dot(lo, sp, ar)
                    if chunk_end and i == ROW_GROUP - 1:
                        st = sp * wr + sa * br + vb * kr
                    else:
                        st = sv_ref[base + i + 1, :, sls[q]]
                    dyc = dycs[q][i]
                    ds = dstate[q] + dyc * rr
                    put(q, "r", _colsum(st * dyc))
                    put(q, "w", _colsum(ds * sp))
                    put(q, "b", _colsum(ds * sa))
                    put(q, "k", _colsum(ds * vb))
                    dsa = _seg_dot(lo, ds, br)
                    dks.append(ds * kr)
                    put(q, "a", _colsum(sp * dsa))
                    dstate[q] = ds * wr + dsa * ar
                dvc = _seg_sums_mxu(ones, dks)
                for q in range(N_BLK):
                    put(q, "v", _row_of_col(eye, dvc[q]))
            for q in range(N_BLK):
                for n_, ref in (("r", dr_ref), ("w", dw_ref), ("k", dk_ref), ("v", dv_ref), ("a", da_ref), ("b", db_ref)):
                    ref[pl.ds(base, ROW_GROUP), sls[q]] = acc[q][n_]
            return tuple(dstate)

        fin = bgroup(0, tuple(ds_ref[:, q * LANES:(q + 1) * LANES] for q in range(N_BLK)), True)
        fin = lax.fori_loop(1, T // ROW_GROUP, lambda n, d: bgroup(n, d, False), fin)
        for q in range(N_BLK):
            ds_ref[:, q * LANES:(q + 1) * LANES] = fin[q]

        if ns:
            @pl.when(pl.program_id(0) == nchunk - 1)
            def _():
                _rsc_finish(x_in, x_out, *sems)

    blk = pl.BlockSpec((T, R_WIDTH), lambda i: (nchunk - 1 - i, 0))
    svb = pl.BlockSpec((T, R_HEAD, R_WIDTH), lambda i: (nchunk - 1 - i, 0, 0))
    return pl.pallas_call(
        body, name="scan_bwd", grid=(nchunk,), in_specs=[blk] * 6 + [svb, blk] + [ANY] * ns,
        out_specs=[blk] * 6 + [ANY] * ns,
        out_shape=[jax.ShapeDtypeStruct((S, R_WIDTH), F32)] * 6 + [jax.ShapeDtypeStruct(p.shape, p.dtype) for p in scatter],
        scratch_shapes=[pltpu.VMEM((R_HEAD, R_WIDTH), F32)] + (_rsc_sems(ns) if ns else []),
        compiler_params=pltpu.CompilerParams(
            dimension_semantics=("arbitrary",), vmem_limit_bytes=VMEM_LIMIT),
    )(r, w, k, v, a, b, states, dy, *scatter)


def _pad_in_cols(t, axis):
    cut = 2 * G_WIDTH + R_COLS
    lo, hi = lax.slice_in_dim(t, 0, cut, axis=axis), lax.slice_in_dim(t, cut, t.shape[axis], axis=axis)
    zshape = list(t.shape)
    zshape[axis] = LORA_PAD - LORA
    return jnp.concatenate([lo, jnp.zeros(zshape, t.dtype), hi], axis=axis)


def _unpad_in_cols(t, axis):
    cut = 2 * G_WIDTH + R_COLS
    return jnp.concatenate([lax.slice_in_dim(t, 0, cut, axis=axis),
                            lax.slice_in_dim(t, cut + LORA_PAD - LORA, t.shape[axis], axis=axis)], axis=axis)


def _pad_rows(t, lo, n):
    return jnp.zeros((n, t.shape[1]), t.dtype).at[lo:lo + t.shape[0]].set(t)


def _join_cols(w3):
    p, k, n = w3.shape
    return jnp.transpose(w3, (1, 0, 2)).reshape(k, p * n)


def _split_cols(w):
    k, n = w.shape
    return jnp.transpose(w.reshape(k, N_CHIPS, n // N_CHIPS), (1, 0, 2))


def _ada_dw(ccol, dmod):
    n = dmod.shape[1] // N_CHIPS
    tile = 256

    def body(c_ref, d_ref, o_ref):
        cc = c_ref[...]
        o_ref[0] = ((cc * _sigmoid(cc)) * d_ref[...]).astype(o_ref.dtype)

    return pl.pallas_call(
        body, name="ada_dw", grid=(N_CHIPS, D_MODEL // tile),
        in_specs=[pl.BlockSpec((tile, 1), lambda q, i: (i, 0)), pl.BlockSpec((1, n), lambda q, i: (0, q))],
        out_specs=pl.BlockSpec((1, tile, n), lambda q, i: (q, i, 0)),
        out_shape=jax.ShapeDtypeStruct((N_CHIPS, D_MODEL, n), WIRE_DT),
        compiler_params=pltpu.CompilerParams(dimension_semantics=("parallel", "parallel")),
    )(ccol, dmod)


def _shift_down(t):
    return jnp.concatenate([jnp.zeros((1, t.shape[1]), t.dtype), t[:-1]], axis=0)


def _shift_up(t):
    return jnp.concatenate([t[1:], jnp.zeros((1, t.shape[1]), t.dtype)], axis=0)


LATE = ("w_branch_a", "w_branch_b", "w_out", "w_ff1", "w_ff2")


def _local_step(x, c, tgt, W):
    S = x.shape[0]
    bf = MXU_DT
    G = {}

    hl = np.arange(R_WIDTH) // R_HEAD
    bd = jnp.asarray(hl[:, None] == hl[None, :], jnp.bfloat16)
    gsel = jnp.asarray(np.arange(LANES)[:, None] == (np.arange(G_WIDTH) // (G_WIDTH // G_GROUPS))[None, :],
                       jnp.bfloat16)
    w_in_p = _pad_in_cols(_join_cols(W["w_in"]), 1)
    b_in_p = _pad_in_cols(W["b_in"], 1)
    c_g, c_r = 2 * G_WIDTH, 2 * G_WIDTH + R_COLS_PAD
    w_g, w_r, w_gate = w_in_p[:, :c_g], w_in_p[:, c_g:c_r], w_in_p[:, c_r:]
    b_g, b_r, b_gate = b_in_p[:, :c_g], b_in_p[:, c_g:c_r], b_in_p[:, c_r:]
    mu_p = jnp.concatenate([W["mu_shift"], jnp.zeros((1, LORA_PAD - LORA), F32)], axis=1)
    wd_p = _pad_rows(W["w_decay_up"].astype(F32), 0, LORA_PAD)
    wa_p = _pad_rows(W["w_aaa_up"].astype(F32), DECAY_LORA, LORA_PAD)
    wg_p = _pad_rows(W["w_gate_up"].astype(F32), DECAY_LORA + AAA_LORA, LORA_PAD)
    ws2 = W["w_spatial"].reshape(G_GROUPS * CHUNK, CHUNK)
    bst = jnp.zeros((CHUNK, LANES), F32).at[:, :G_GROUPS].set(W["b_spatial"].T)
    rk = W["r_k"].reshape(1, R_WIDTH)

    c8 = jnp.broadcast_to(c, (8, D_MODEL))
    (ca8,) = _rows("ada_silu", lambda cc: cc * _sigmoid(cc), [c8], [], [(D_MODEL, bf)], tile=8)
    mod_raw = _mm("ada_mm", ca8, W["w_ada"], "nn")
    (mod8,) = _rows("ada_bias", lambda m, bb: m + bb, [mod_raw], [W["b_ada"]], [(6 * D_MODEL, F32)], tile=8)
    sh1, sc1, gt1, sh2, sc2, gt2 = [mod8[0:1, i * D_MODEL:(i + 1) * D_MODEL] for i in range(6)]

    (h,) = _rows("mod1", _modulate, [x], [sc1, sh1], [(D_MODEL, bf)])
    proj_g = _mm("proj_g", h, w_g, "nn")
    proj_r = _mm("proj_r", h, w_r, "nn")
    proj_gate = _mm("proj_gate", h, w_gate, "nn")

    def split2(t):
        return t[:, :G_WIDTH], t[:, G_WIDTH:]

    def gmlp_fwd(z, bz, g, b, bst_, gsel_, ws_):
        gmask, tril = _gmlp_consts()
        (zu, zv), (bu, bv) = split2(z), split2(bz)
        wsl = [ws_[i * CHUNK:(i + 1) * CHUNK] for i in range(G_GROUPS)]
        return _gmlp_core(gmask, tril, gsel_, zu, zv, bu, bv, g, b, bst_, *wsl)

    (y_a,) = _rows("gmlp_fwd", gmlp_fwd, [proj_g], [b_g, W["g_ln_v"], W["b_ln_v"], bst, gsel, ws2],
                   [(G_WIDTH, bf)], tile=CHUNK)

    r_cuts = (0, R_WIDTH, 2 * R_WIDTH, 3 * R_WIDTH, R_COLS_PAD)

    def split4(t):
        return [t[:, r_cuts[i]:r_cuts[i + 1]] for i in range(4)]

    tile_pre = min(256, S)

    def rowmask_of():
        grow = pl.program_id(0) * tile_pre + lax.broadcasted_iota(jnp.int32, (tile_pre, 1), 0)
        return (grow > 0).astype(F32)

    pre_params = [b_r, mu_p, W["w0"], wd_p, W["a0"], wa_p, wg_p, W["k_k"], W["k_a"], bd]

    def pre_fwd(z, p, bz, mu, w0, wd, a0, wa, wg, k_k, k_a, bd_):
        return _pre_core(rowmask_of(), bd_, *split4(z), *split4(p), *split4(bz), *split4(mu),
                         w0, wd, a0, wa, wg, k_k, k_a)

    proj_r_prev = _shift_down(proj_r)
    s_r, s_w, s_k, s_v, s_a, s_b, s_g = _rows(
        "rwkv_pre_fwd", pre_fwd, [proj_r, proj_r_prev], pre_params, [(R_WIDTH, F32)] * 7, tile=tile_pre)
    y_scan, states, late = yield "fwd", (s_r, s_w, s_k, s_v, s_a, s_b)
    W = {**W, **late}

    def post_fwd(y, r, k2, v, g, gain, bias, rk_, bd_):
        return _post_core(bd_, y, r, k2, v, g, gain, bias, rk_)

    post_params = [W["gn_gain"], W["gn_bias"], rk, bd]
    (y_b,) = _rows("rwkv_post_fwd", post_fwd, [y_scan, s_r, s_k, s_v, s_g], post_params, [(R_WIDTH, bf)])
    p_a = _mm("branch_a", y_a, W["w_branch_a"], "nn")
    p_b = _mm("branch_b", y_b, W["w_branch_b"], "nn")

    def merge_fwd(pa, pb, gz, bgz):
        return _merge_core(pa, pb, gz[:, :D_MODEL], gz[:, D_MODEL:], bgz[:, :D_MODEL], bgz[:, D_MODEL:])

    (merged,) = _rows("merge_fwd", merge_fwd, [p_a, p_b, proj_gate], [b_gate], [(D_MODEL, bf)])
    mix = _mm("out_proj", merged, W["w_out"], "nn")
    ln1_params = [gt1, W["b_out"], W["ln1_g"], W["ln1_b"], sc2, sh2]
    h1, h2 = _rows("ln1_fwd", _ln1_core, [x, mix], ln1_params, [(D_MODEL, F32), (D_MODEL, bf)])

    a1 = _mm("ff1", h2, W["w_ff1"], "nn")
    (act,) = _rows("ff_act", lambda z, bb: jnp.square(jnp.maximum(z + bb, 0.0)), [a1], [W["b_ff1"]], [(D_FF, bf)])
    ff = _mm("ff2", act, W["w_ff2"], "nn")

    def ln2_loss(h1_, ff_, tg, gt2_, bff2, g, b):
        loss, vjp = jax.vjp(functools.partial(_ln2_loss_core, tg), h1_, ff_, gt2_, bff2, g, b)
        return vjp(jnp.ones((), F32)) + (loss,)

    ln2_params = [gt2, W["b_ff2"], W["ln2_g"], W["ln2_b"]]
    dh1, dff, dgt2, G["b_ff2"], G["ln2_g"], G["ln2_b"], loss_acc = _rows(
        "ln2_loss", ln2_loss, [h1, ff, tgt], ln2_params, [(D_MODEL, F32), (D_MODEL, bf)],
        accs=[(1, D_MODEL)] * 4 + [(1, LANES)])
    loss = loss_acc[0, 0]

    dact = _mm("ff2_dx", dff, W["w_ff2"], "nt")
    G["w_ff2"] = _mm("ff2_dw", act, dff, "tn", WIRE_DT).reshape(N_CHIPS, D_FF // N_CHIPS, D_MODEL)

    def act_bwd(z, da, bb):
        d = da * 2.0 * jnp.maximum(z + bb, 0.0)
        return d, _colsum(d)

    da1, G["b_ff1"] = _rows("ff_act_bwd", act_bwd, [a1, dact], [W["b_ff1"]], [(D_FF, bf)], accs=[(1, D_FF)])
    dh2 = _mm("ff1_dx", da1, W["w_ff1"], "nt")
    G["w_ff1"] = _mm("ff1_dw", h2, da1, "tn", WIRE_DT, out_split=N_CHIPS)

    def ln1_bwd(x_, mix_, dh1_, dh2_, *ps):
        _, vjp = jax.vjp(_ln1_core, x_, mix_, *ps)
        return vjp((dh1_, dh2_))

    dx_res, dmix, dgt1, G["b_out"], G["ln1_g"], G["ln1_b"], dsc2, dsh2 = _rows(
        "ln1_bwd", ln1_bwd, [x, mix, dh1, dh2], ln1_params, [(D_MODEL, F32), (D_MODEL, bf)],
        accs=[(1, D_MODEL)] * 6)

    dmerged = _mm("out_proj_dx", dmix, W["w_out"], "nt")
    G["w_out"] = _mm("out_proj_dw", merged, dmix, "tn", WIRE_DT).reshape(N_CHIPS, D_MODEL // N_CHIPS, D_MODEL)

    def merge_bwd(pa, pb, gz, dm, bgz):
        args = (pa.astype(F32), pb.astype(F32), gz[:, :D_MODEL], gz[:, D_MODEL:], bgz[:, :D_MODEL], bgz[:, D_MODEL:])
        _, vjp = jax.vjp(_merge_core, *args)
        dpa, dpb, dga, dgb, dbga, dbgb = vjp(dm)
        return dpa, dpb, jnp.concatenate([dga, dgb], axis=1), jnp.concatenate([dbga, dbgb], axis=1)

    dp_a, dp_b, dgates, db_gate = _rows(
        "merge_bwd", merge_bwd, [p_a, p_b, proj_gate, dmerged], [b_gate],
        [(D_MODEL, bf), (D_MODEL, bf), (2 * D_MODEL, bf)], accs=[(1, 2 * D_MODEL)])
    dy_a = _mm("branch_a_dx", dp_a, W["w_branch_a"], "nt")
    G["w_branch_a"] = _mm("branch_a_dw", y_a, dp_a, "tn", WIRE_DT, out_split=N_CHIPS)
    dy_b = _mm("branch_b_dx", dp_b, W["w_branch_b"], "nt")
    G["w_branch_b"] = _mm("branch_b_dw", y_b, dp_b, "tn", WIRE_DT, out_split=N_CHIPS)

    def post_bwd(y, r, k2, v, g, dyb, gain, bias, rk_, bd_):
        _, vjp = jax.vjp(functools.partial(_post_core, bd_), y, r, k2, v, g, gain, bias, rk_)
        return vjp(dyb)

    dy_scan, dr_p, dk_p, dv_p, dg_p, G["gn_gain"], G["gn_bias"], drk = _rows(
        "rwkv_post_bwd", post_bwd, [y_scan, s_r, s_k, s_v, s_g, dy_b], post_params,
        [(R_WIDTH, F32)] * 5, accs=[(1, R_WIDTH)] * 3)
    G["r_k"] = drk.reshape(R_HEADS, R_HEAD)
    dr_s, dw_s, dk_s, dv_s, da_s, db_s = yield "bwd", (s_r, s_w, s_k, s_v, s_a, s_b, states, dy_scan), G

    def pre_bwd(z, p, dr1, dr2, dw, dk1, dk2, dv1, dv2, da, db, dg,
                bz, mu, w0, wd, a0, wa, wg, k_k, k_a, bd_):
        prim = (*split4(z), *split4(p), *split4(bz), *split4(mu), w0, wd, a0, wa, wg, k_k, k_a)
        _, vjp = jax.vjp(functools.partial(_pre_core, rowmask_of(), bd_), *prim)
        d = vjp((dr1 + dr2, dw, dk1 + dk2, dv1 + dv2, da, db, dg))
        cat = lambda parts: jnp.concatenate(parts, axis=1)
        return (cat(d[0:4]), cat(d[4:8]), cat(d[8:12]), cat(d[12:16])) + tuple(d[16:])

    dz_r, dprev, db_r, dmu_p, G["w0"], dwd_p, G["a0"], dwa_p, dwg_p, G["k_k"], G["k_a"] = _rows(
        "rwkv_pre_bwd", pre_bwd,
        [proj_r, proj_r_prev, dr_s, dr_p, dw_s, dk_s, dk_p, dv_s, dv_p, da_s, db_s, dg_p],
        pre_params, [(R_COLS_PAD, bf)] * 2,
        accs=[(1, R_COLS_PAD), (1, R_COLS_PAD), (1, R_WIDTH), (LORA_PAD, R_WIDTH), (1, R_WIDTH),
              (LORA_PAD, R_WIDTH), (LORA_PAD, R_WIDTH), (1, R_WIDTH), (1, R_WIDTH)],
        tile=tile_pre)
    G["mu_shift"] = dmu_p[:, :R_COLS]
    G["w_decay_up"] = dwd_p[:DECAY_LORA]
    G["w_aaa_up"] = dwa_p[DECAY_LORA:DECAY_LORA + AAA_LORA]
    G["w_gate_up"] = dwg_p[DECAY_LORA + AAA_LORA:LORA]

    def gmlp_bwd(z, dya, bz, g, b, bst_, gsel_, ws_):
        gmask, tril = _gmlp_consts()
        (zu, zv), (bu, bv) = split2(z), split2(bz)
        wsl = [ws_[i * CHUNK:(i + 1) * CHUNK] for i in range(G_GROUPS)]
        _, vjp = jax.vjp(functools.partial(_gmlp_core, gmask, tril, gsel_), zu, zv, bu, bv, g, b, bst_, *wsl)
        d = vjp(dya)
        return (jnp.concatenate(d[0:2], axis=1), jnp.concatenate(d[2:4], axis=1), d[4], d[5], d[6],
                jnp.concatenate(d[7:], axis=0))

    dz_g, db_g, G["g_ln_v"], G["b_ln_v"], dbst, dws2 = _rows(
        "gmlp_bwd", gmlp_bwd, [proj_g, dy_a], [b_g, W["g_ln_v"], W["b_ln_v"], bst, gsel, ws2],
        [(2 * G_WIDTH, bf)],
        accs=[(1, 2 * G_WIDTH), (1, G_WIDTH), (1, G_WIDTH), (CHUNK, LANES), (G_GROUPS * CHUNK, CHUNK)],
        tile=CHUNK)
    G["w_spatial"] = dws2.reshape(G_GROUPS, CHUNK, CHUNK)
    G["b_spatial"] = dbst[:, :G_GROUPS].T

    def dproj_cat(dzg, dzr, dpv, dgz):
        return jnp.concatenate([dzg, (dzr.astype(F32) + dpv.astype(F32)).astype(dzg.dtype), dgz], axis=1)

    (dproj,) = _rows("dproj_cat", dproj_cat, [dz_g, dz_r, _shift_up(dprev), dgates], [],
                     [(2 * G_WIDTH + R_COLS_PAD + 2 * D_MODEL, bf)])
    dh = _mm("proj_dx", dproj, w_in_p, "nt", tk=2432)
    G["w_in"] = _split_cols(_unpad_in_cols(_mm("proj_dw", h, dproj, "tn", WIRE_DT, tm=512, tn=2432), 1))
    G["b_in"] = _unpad_in_cols(jnp.concatenate([db_g, db_r, db_gate], axis=1), 1)

    def mod1_bwd(x_, dh_, dxr, sc):
        return dh_ * (1.0 + sc) + dxr, _colsum(dh_ * x_), _colsum(dh_)

    grad_x, dsc1, dsh1 = _rows("mod1_bwd", mod1_bwd, [x, dh, dx_res], [sc1], [(D_MODEL, F32)],
                               accs=[(1, D_MODEL)] * 2)

    dmod = jnp.concatenate([dsh1, dsc1, dgt1, dsh2, dsc2, dgt2], axis=1)
    G["b_ada"] = dmod
    G["w_ada"] = _ada_dw(c.reshape(D_MODEL, 1), dmod)
    return loss, grad_x, G


BIG = (("w_ada", (D_MODEL, 6 * D_MODEL), 1), ("w_in", (D_MODEL, 2 * G_WIDTH + R_COLS + 2 * D_MODEL), 1),
       ("w_branch_a", (G_WIDTH, D_MODEL), 1), ("w_branch_b", (R_WIDTH, D_MODEL), 1),
       ("w_out", (D_MODEL, D_MODEL), 0), ("w_ff1", (D_MODEL, D_FF), 1), ("w_ff2", (D_FF, D_MODEL), 0))
LORAS = (("w_decay_up", (DECAY_LORA, R_WIDTH), 1), ("w_aaa_up", (AAA_LORA, R_WIDTH), 1),
         ("w_gate_up", (GATE_LORA, R_WIDTH), 1))
SHARDED = BIG + LORAS
SMALL = (("b_ada", (1, 6 * D_MODEL)), ("b_in", (1, 2 * G_WIDTH + R_COLS + 2 * D_MODEL)),
         ("g_ln_v", (1, G_WIDTH)), ("b_ln_v", (1, G_WIDTH)), ("w_spatial", (G_GROUPS, CHUNK, CHUNK)),
         ("b_spatial", (G_GROUPS, CHUNK)), ("mu_shift", (1, R_COLS)), ("w0", (1, R_WIDTH)),
         ("a0", (1, R_WIDTH)), ("k_k", (1, R_WIDTH)), ("k_a", (1, R_WIDTH)), ("r_k", (R_HEADS, R_HEAD)),
         ("gn_gain", (1, R_WIDTH)), ("gn_bias", (1, R_WIDTH)), ("b_out", (1, D_MODEL)),
         ("ln1_g", (1, D_MODEL)), ("ln1_b", (1, D_MODEL)), ("b_ff1", (1, D_FF)), ("b_ff2", (1, D_MODEL)),
         ("ln2_g", (1, D_MODEL)), ("ln2_b", (1, D_MODEL)))
WEIGHT_ORDER = ("w_ada", "b_ada", "w_in", "b_in", "g_ln_v", "b_ln_v", "w_spatial", "b_spatial", "mu_shift",
                "w0", "w_decay_up", "a0", "w_aaa_up", "w_gate_up", "k_k", "k_a", "r_k", "gn_gain", "gn_bias",
                "w_branch_a", "w_branch_b", "w_out", "b_out", "ln1_g", "ln1_b", "w_ff1", "b_ff1", "w_ff2",
                "b_ff2", "ln2_g", "ln2_b")
N_CHIPS = 4


def _shard_shape(shape, axis):
    s = list(shape)
    s[axis] //= N_CHIPS
    return tuple(s)


def _numel(shape):
    return int(np.prod(shape))


def _round_up(n, q):
    return -(-n // q) * q


PIECE_Q = 1


def _piece(shape):
    return _round_up(_numel(shape), PIECE_Q)


N_LORA = sum(_piece(_shard_shape(s, a)) for _, s, a in LORAS)
N_SMALL = sum(_piece(s) for _, s in SMALL)
ROWS_SW = _round_up(N_LORA, PACK_Q) // PACK_W
ROWS_SG = _round_up(N_LORA + N_SMALL, PACK_Q) // PACK_W


def _flat_pieces(parts, dtype, total=None):
    out, n = [], 0
    for p in parts:
        out.append(p.reshape(-1).astype(dtype))
        pad = _piece(p.shape) - out[-1].shape[0]
        if pad:
            out.append(jnp.zeros((pad,), dtype))
        n += _piece(p.shape)
    if total is not None and total > n:
        out.append(jnp.zeros((total - n,), dtype))
    return jnp.concatenate(out)


def _pack_small(loras, small, rows, dtype):
    parts = [loras[n] for n, _, _ in LORAS] + ([small[n] for n, _ in SMALL] if small is not None else [])
    return _flat_pieces(parts, dtype, rows * PACK_W).reshape(rows, PACK_W)


def _unpack_small(pack, with_small):
    flat = pack.reshape(-1)
    out, off = {}, 0
    for n, s, a in LORAS:
        ss = _shard_shape(s, a)
        out[n] = flat[off:off + _numel(ss)].reshape(ss)
        off += _piece(ss)
    if with_small:
        for n, s in SMALL:
            out[n] = flat[off:off + _numel(s)].reshape(s)
            off += _piece(s)
    return out


def _pack_small_grads(G):
    common = _flat_pieces([G[n] for n, _ in SMALL], F32, ROWS_SG * PACK_W - N_LORA)
    segs = []
    for q in range(N_CHIPS):
        loras = [G[n][:, q * (s[1] // N_CHIPS):(q + 1) * (s[1] // N_CHIPS)] for n, s, _ in LORAS]
        segs.append(jnp.concatenate([_flat_pieces(loras, F32), common]).reshape(ROWS_SG, PACK_W))
    return jnp.stack(segs)


ANY = pl.BlockSpec(memory_space=pl.ANY)
MESH = pl.DeviceIdType.MESH


def _place():
    x, y, c = lax.axis_index("x"), lax.axis_index("y"), lax.axis_index("c")
    chips = [(1 - x, y), (x, 1 - y), (1 - x, 1 - y)]
    return x, y, c, chips


def _remote(src, dst, send_sem, recv_sem, to):
    return pltpu.make_async_remote_copy(src_ref=src, dst_ref=dst, send_sem=send_sem, recv_sem=recv_sem,
                                        device_id=to, device_id_type=MESH)


def _ag_copies(ins, outs, send_sems, recv_sems):
    x, y, c, chips = _place()
    s = 2 * x + y
    cps = []
    for a in range(len(ins)):
        H = ins[a].shape[0] // 2
        mine = ins[a].at[pl.ds(c * H, H)]
        cps += [_remote(mine, outs[a].at[2 * s + c], send_sems.at[6 * a + j], recv_sems.at[6 * a + j], (*chip, c))
                for j, chip in enumerate(chips)]
    return cps


def _ag_start(ins, outs, send_sems, recv_sems):
    for cp in _ag_copies(ins, outs, send_sems, recv_sems):
        cp.start()


def _ag_finish(ins, outs, send_sems, recv_sems):
    x, y, c, chips = _place()
    sibling = (x, y, 1 - c)
    slot = [2 * chip[0] + chip[1] for chip in chips]
    passed = []
    for a in range(len(ins)):
        for j in range(3):
            landed = outs[a].at[2 * slot[j] + c]
            _remote(landed, landed, send_sems.at[6 * a + j], recv_sems.at[6 * a + j], sibling).wait_recv()
            cp = _remote(landed, landed, send_sems.at[6 * a + 3 + j], recv_sems.at[6 * a + 3 + j], sibling)
            cp.start()
            passed.append(cp)
    for a in range(len(ins)):
        for j in range(3):
            got = outs[a].at[2 * slot[j] + 1 - c]
            _remote(got, got, send_sems.at[6 * a + 3 + j], recv_sems.at[6 * a + 3 + j], sibling).wait_recv()
    for cp in _ag_copies(ins, outs, send_sems, recv_sems) + passed:
        cp.wait_send()


def _ag_out_shapes(shards):
    return [jax.ShapeDtypeStruct((2 * N_CHIPS, a.shape[0] // 2, a.shape[1]), a.dtype) for a in shards]


def _ag_sems(n):
    return [pltpu.SemaphoreType.DMA((6 * n,)), pltpu.SemaphoreType.DMA((6 * n,))]


def _all_gather(shards):
    n = len(shards)

    def body(*refs):
        ins, outs, sems = refs[:n], refs[n:2 * n], refs[2 * n:]
        _ag_start(ins, outs, *sems)
        _ag_finish(ins, outs, *sems)

    return pl.pallas_call(
        body, name="ag_weights", in_specs=[ANY] * n, out_specs=[ANY] * n,
        out_shape=_ag_out_shapes(shards), scratch_shapes=_ag_sems(n),
    )(*shards)


def _rs_sibling_in(gps):
    n = len(gps)

    def body(*refs):
        ins, outs = refs[:n], refs[n:2 * n]
        send_sems, recv_sems = refs[2 * n:]
        x, y, c, _ = _place()
        cps = []
        for a in range(n):
            H = gps[a].shape[1] // 2
            cps += [_remote(ins[a].at[q, pl.ds((1 - c) * H, H)], outs[a].at[q], send_sems.at[N_CHIPS * a + q],
                            recv_sems.at[N_CHIPS * a + q], (x, y, 1 - c)) for q in range(N_CHIPS)]
        for cp in cps:
            cp.start()
        for cp in cps:
            cp.wait()

    return pl.pallas_call(
        body, name="rs_sibling_in", in_specs=[ANY] * n, out_specs=[ANY] * n,
        out_shape=[jax.ShapeDtypeStruct((N_CHIPS, g.shape[1] // 2, g.shape[2]), g.dtype) for g in gps],
        scratch_shapes=[pltpu.SemaphoreType.DMA((N_CHIPS * n,)), pltpu.SemaphoreType.DMA((N_CHIPS * n,))],
    )(*gps)


def _rs_add_own(name, gp, got, c_arr, tr=256):
    H, C = got.shape[1:]
    tr = _pick(H, tr, 8)
    nb = H // tr

    def body(c_ref, g_ref, r_ref, o_ref):
        o_ref[...] = (g_ref[...].astype(F32) + r_ref[...].astype(F32)).astype(o_ref.dtype)

    return pl.pallas_call(
        body, name="rs_add_own_" + name,
        grid_spec=pltpu.PrefetchScalarGridSpec(
            num_scalar_prefetch=1, grid=(N_CHIPS, nb),
            in_specs=[pl.BlockSpec((1, tr, C), lambda q, i, c_ref: (q, c_ref[0] * nb + i, 0)),
                      pl.BlockSpec((1, tr, C), lambda q, i, c_ref: (q, i, 0))],
            out_specs=pl.BlockSpec((1, tr, C), lambda q, i, c_ref: (q, i, 0))),
        out_shape=jax.ShapeDtypeStruct((N_CHIPS, H, C), gp.dtype),
        compiler_params=pltpu.CompilerParams(dimension_semantics=("arbitrary", "arbitrary")),
    )(c_arr, gp, got)


def _rs_chips(parts):
    n = len(parts)

    def body(*refs):
        ins, outs, sems = refs[:n], refs[n:2 * n], refs[2 * n:]
        _rsc_start(ins, outs, *sems)
        _rsc_finish(ins, outs, *sems)

    return pl.pallas_call(
        body, name="rs_chips", in_specs=[ANY] * n, out_specs=[ANY] * n,
        out_shape=[jax.ShapeDtypeStruct(p.shape, p.dtype) for p in parts], scratch_shapes=_rsc_sems(n),
    )(*parts)


def _rsc_copies(ins, outs, send_sems, recv_sems):
    x, y, c, chips = _place()
    s = 2 * x + y
    return [_remote(ins[a].at[2 * chip[0] + chip[1]], outs[a].at[s], send_sems.at[3 * a + j], recv_sems.at[3 * a + j],
                    (*chip, c)) for a in range(len(ins)) for j, chip in enumerate(chips)]


def _rsc_start(ins, outs, send_sems, recv_sems):
    for cp in _rsc_copies(ins, outs, send_sems, recv_sems):
        cp.start()


def _rsc_finish(ins, outs, send_sems, recv_sems):
    x, y, c, chips = _place()
    for cp in _rsc_copies(ins, outs, send_sems, recv_sems):
        cp.wait_send()
    for a in range(len(ins)):
        for j, chip in enumerate(chips):
            got = outs[a].at[2 * chip[0] + chip[1]]
            _remote(got, got, send_sems.at[3 * a + j], recv_sems.at[3 * a + j], (*chip, c)).wait_recv()


def _rsc_sems(n):
    return [pltpu.SemaphoreType.DMA((3 * n,)), pltpu.SemaphoreType.DMA((3 * n,))]


def _rs_add_chips(name, part, slots, sc_arr, tr=128):
    H, C = slots.shape[1:]
    tr = _pick(H, tr, 8)
    nb = H // tr

    def body(sc_ref, p_ref, s_ref, o_ref):
        acc = None
        for q in range(N_CHIPS):
            term = lax.cond(sc_ref[0] == q, lambda: p_ref[0].astype(F32), lambda q=q: s_ref[q].astype(F32))
            acc = term if acc is None else acc + term
        o_ref[...] = acc

    return pl.pallas_call(
        body, name="rs_add_chips_" + name,
        grid_spec=pltpu.PrefetchScalarGridSpec(
            num_scalar_prefetch=1, grid=(nb,),
            in_specs=[pl.BlockSpec((1, tr, C), lambda i, sc: (sc[0], i, 0)),
                      pl.BlockSpec((N_CHIPS, tr, C), lambda i, sc: (0, i, 0))],
            out_specs=pl.BlockSpec((tr, C), lambda i, sc: (sc[1] * nb + i, 0))),
        out_shape=jax.ShapeDtypeStruct((2 * H, C), F32),
        compiler_params=pltpu.CompilerParams(dimension_semantics=("arbitrary",)),
    )(sc_arr, part, slots)


def _rs_sibling_out(wholes):
    n = len(wholes)

    def body(*refs):
        ins, outs = refs[:n], refs[n:2 * n]
        send_sems, recv_sems = refs[2 * n:]
        x, y, c, _ = _place()
        cps = []
        for a in range(n):
            H = wholes[a].shape[0] // 2
            cps.append(_remote(ins[a].at[pl.ds(c * H, H)], outs[a].at[pl.ds(c * H, H)], send_sems.at[a], recv_sems.at[a],
                               (x, y, 1 - c)))
        for cp in cps:
            cp.start()
        for a in range(n):
            H = wholes[a].shape[0] // 2
            cps[a].wait_send()
            got = outs[a].at[pl.ds((1 - c) * H, H)]
            _remote(got, got, send_sems.at[a], recv_sems.at[a], (x, y, 1 - c)).wait_recv()

    return pl.pallas_call(
        body, name="rs_sibling_out", in_specs=[ANY] * n, out_specs=[ANY] * n,
        out_shape=[jax.ShapeDtypeStruct(w.shape, w.dtype) for w in wholes],
        input_output_aliases={a: a for a in range(n)},
        scratch_shapes=[pltpu.SemaphoreType.DMA((n,)), pltpu.SemaphoreType.DMA((n,))],
    )(*wholes)


def _adamw_math(w_, g_, m_, v_):
    m2 = ADAM_B1 * m_ + (1.0 - ADAM_B1) * g_
    v2 = ADAM_B2 * v_ + (1.0 - ADAM_B2) * (g_ * g_)
    m_hat = m2 / (1.0 - ADAM_B1 ** ADAM_STEP)
    v_hat = v2 / (1.0 - ADAM_B2 ** ADAM_STEP)
    return -ADAM_LR * (m_hat / (jnp.sqrt(v_hat) + ADAM_EPS) + ADAM_WD * w_), m2, v2


def _adamw(name, w, g, m, v):
    return _rows("adamw_" + name, _adamw_math, [w, g, m, v], [], [(w.shape[1], F32)] * 3,
                 tile=_pick(w.shape[0], 256, 8))


def _adamw_small(ws, gs, ms, vs):
    n = len(ws)

    def body(*refs):
        for i in range(n):
            res = _adamw_math(*[refs[j * n + i][...] for j in range(4)])
            for j in range(3):
                refs[(4 + j) * n + i][...] = res[j]

    out = pl.pallas_call(
        body, name="adamw_small",
        out_shape=[jax.ShapeDtypeStruct(w.shape, F32) for _ in range(3) for w in ws],
        compiler_params=pltpu.CompilerParams(vmem_limit_bytes=VMEM_LIMIT),
    )(*ws, *gs, *ms, *vs)
    return out[:n], out[n:2 * n], out[2 * n:]


def kernel(x, c, w_ada, b_ada, w_in, b_in, g_ln_v, b_ln_v, w_spatial, b_spatial, mu_shift, w0, w_decay_up, a0, w_aaa_up, w_gate_up, k_k, k_a, r_k, gn_gain, gn_bias, w_branch_a, w_branch_b, w_out, b_out, ln1_g, ln1_b, w_ff1, b_ff1, w_ff2, b_ff2, ln2_g, ln2_b, loss_target, m_w_ada, m_b_ada, m_w_in, m_b_in, m_g_ln_v, m_b_ln_v, m_w_spatial, m_b_spatial, m_mu_shift, m_w0, m_w_decay_up, m_a0, m_w_aaa_up, m_w_gate_up, m_k_k, m_k_a, m_r_k, m_gn_gain, m_gn_bias, m_w_branch_a, m_w_branch_b, m_w_out, m_b_out, m_ln1_g, m_ln1_b, m_w_ff1, m_b_ff1, m_w_ff2, m_b_ff2, m_ln2_g, m_ln2_b, v_w_ada, v_b_ada, v_w_in, v_b_in, v_g_ln_v, v_b_ln_v, v_w_spatial, v_b_spatial, v_mu_shift, v_w0, v_w_decay_up, v_a0, v_w_aaa_up, v_w_gate_up, v_k_k, v_k_a, v_r_k, v_gn_gain, v_gn_bias, v_w_branch_a, v_w_branch_b, v_w_out, v_b_out, v_ln1_g, v_ln1_b, v_w_ff1, v_b_ff1, v_w_ff2, v_b_ff2, v_ln2_g, v_ln2_b):
    args = dict(locals())
    local_shape = {n: _shard_shape(s, a) for n, s, a in SHARDED}
    local_shape.update(dict(SMALL))
    wts = {n: args[n].reshape(local_shape[n]) for n in WEIGHT_ORDER}
    mom = {n: args["m_" + n].reshape(local_shape[n]) for n in WEIGHT_ORDER}
    var = {n: args["v_" + n].reshape(local_shape[n]) for n in WEIGHT_ORDER}
    big = [n for n, _, _ in BIG]
    late, early = [n for n in big if n in LATE], [n for n in big if n not in LATE]
    chip = 2 * lax.axis_index("x") + lax.axis_index("y")
    c_arr = lax.axis_index("c").astype(jnp.int32).reshape(1)
    sc_arr = jnp.stack([chip, lax.axis_index("c")]).astype(jnp.int32)
    mine = {n: wts[n].astype(MXU_DT) for n in big}

    def whole(n, got):
        _, s, a = next(t for t in BIG if t[0] == n)
        g = lax.dynamic_update_slice(got.reshape((N_CHIPS,) + mine[n].shape), mine[n][None], (chip, 0, 0))
        return g if a == 1 else g.reshape(s)

    def sibling_sums(names, G, small):
        gps = [G[n] for n in names] + ([_pack_small_grads(G)] if small else [])
        names = names + (["small"] if small else [])
        return [_rs_add_own(n, g, r, c_arr) for n, g, r in zip(names, gps, _rs_sibling_in(gps))]

    small_w = _pack_small(wts, None, ROWS_SW, MXU_DT)
    gath = _all_gather([mine[n] for n in early] + [small_w])
    W = {n: wts[n] for n, _ in SMALL}
    W.update({n: whole(n, g) for n, g in zip(early, gath)})
    small_g = lax.dynamic_update_slice(gath[-1].reshape((N_CHIPS,) + small_w.shape), small_w[None], (chip, 0, 0))
    lora_q = [_unpack_small(small_g[q], False) for q in range(N_CHIPS)]
    for n, _, _ in LORAS:
        W[n] = jnp.concatenate([lora_q[q][n] for q in range(N_CHIPS)], axis=1)

    step = _local_step(x[0], c, loss_target[0], W)
    _, scan_in = next(step)
    y_scan, states, got = _scan_fwd(*scan_in, gather=[mine[n] for n in late])
    _, scan_in, G = step.send((y_scan, states, {n: whole(n, g) for n, g in zip(late, got)}))
    parts_late = sibling_sums(late, G, False)
    res = _scan_bwd(*scan_in, scatter=parts_late)
    try:
        step.send(tuple(res[:6]))
    except StopIteration as done:
        loss, grad_x, G = done.value
    loss = lax.psum(loss, MESH_AXES)
    parts_early = sibling_sums(early, G, True)
    names = late + early + ["small"]
    parts, slots = parts_late + parts_early, list(res[6:]) + list(_rs_chips(parts_early))
    segs = _rs_sibling_out([_rs_add_chips(n, p, s, sc_arr) for n, p, s in zip(names, parts, slots)])

    out = {}
    for n, g in zip(names[:-1], segs[:-1]):
        out[n] = (g,) + tuple(_adamw(n, wts[n], g, mom[n], var[n]))
    small = [n for n, _, _ in LORAS] + [n for n, _ in SMALL]
    g_small = _unpack_small(segs[-1], True)
    upd = _adamw_small(*[[t[n] for n in small] for t in (wts, g_small, mom, var)])
    for i, n in enumerate(small):
        out[n] = (g_small[n], upd[0][i], upd[1][i], upd[2][i])
    res = [loss, grad_x[None]]
    for k in range(4):
        res += [out[n][k].reshape(args[n].shape) for n in WEIGHT_ORDER]
    return tuple(res)
```

```python
import functools

import numpy as np
import jax
import jax.numpy as jnp
from jax import lax
from jax.experimental import pallas as pl
from jax.experimental.pallas import tpu as pltpu

F32 = jnp.float32
MXU_DT = jnp.bfloat16
WIRE_DT = jnp.bfloat16

D_MODEL = 1024
G_GROUPS = 8
G_WIDTH = 512
CHUNK = 128
R_WIDTH = 512
R_HEAD = 64
R_HEADS = 8
DECAY_LORA = 32
AAA_LORA = 32
GATE_LORA = 96
LORA = DECAY_LORA + AAA_LORA + GATE_LORA
LORA_PAD = 256
R_COLS = 3 * R_WIDTH + LORA
R_COLS_PAD = 3 * R_WIDTH + LORA_PAD
D_FF = 4 * D_MODEL
ALPHA = 2.0 ** 0.25
LN_EPS = 1e-5
GN_EPS = 64e-5
ADAM_LR = 0.001
ADAM_B1 = 0.9
ADAM_B2 = 0.999
ADAM_EPS = 1e-08
ADAM_WD = 0.01
ADAM_STEP = 10

LANES = 128
PACK_W = 512
PACK_Q = 2 * 16 * PACK_W
VMEM_LIMIT = 48 * 1024 * 1024
SCAN_T = 64

MESH_AXES = ("x", "y", "c")


def _dg(a, b, dims):
    return lax.dot_general(a.astype(MXU_DT), b.astype(MXU_DT), (dims, ((), ())),
                           preferred_element_type=F32)


@jax.custom_vjp
def _bdot(a, b):
    return _dg(a, b, ((1,), (0,)))


def _bdot_fwd(a, b):
    return _bdot(a, b), (a, b)


def _bdot_bwd(res, g):
    a, b = res
    return (_dg(g, b, ((1,), (1,))).astype(a.dtype), _dg(a, g, ((0,), (0,))).astype(b.dtype))


_bdot.defvjp(_bdot_fwd, _bdot_bwd)


def _split_dot(x, m, dims):
    hi = x.astype(jnp.bfloat16)
    lo = (x - hi.astype(F32)).astype(jnp.bfloat16)
    dn = (dims, ((), ()))
    return (lax.dot_general(hi, m, dn, preferred_element_type=F32)
            + lax.dot_general(lo, m, dn, preferred_element_type=F32))


@jax.custom_vjp
def _pdot(x, m):
    return _split_dot(x, m, ((1,), (0,)))


def _pdot_fwd(x, m):
    return _pdot(x, m), m


def _pdot_bwd(m, g):
    return _split_dot(g, m, ((1,), (1,))), None


_pdot.defvjp(_pdot_fwd, _pdot_bwd)


def _sigmoid(x):
    return 1.0 / (1.0 + jnp.exp(-x))


def _softplus(x):
    return jnp.maximum(x, 0.0) + jnp.log(1.0 + jnp.exp(-jnp.maximum(x, -x)))


def _gelu(x):
    return 0.5 * x * (1.0 + jnp.tanh(0.7978845608028654 * (x + 0.044715 * (x * x * x))))


def _ln(x, g, b, eps):
    mu = jnp.mean(x, axis=-1, keepdims=True)
    xc = x - mu
    var = jnp.mean(xc * xc, axis=-1, keepdims=True)
    return xc * lax.rsqrt(var + eps) * g + b


def _colsum(x):
    return jnp.sum(x, axis=0, keepdims=True)


def _pick(n, target, q=LANES):
    if n <= target:
        return n
    best = None
    for t in range(q, target + 1, q):
        if n % t == 0:
            best = t
    assert best is not None, (n, target)
    return best


def _mm(name, a, b, mode, out_dtype=F32, out_split=1, tm=1024, tn=1024, tk=1024, epilogue=None, side=()):
    bs = b.shape[0] if b.ndim == 3 else 1
    br, bc = b.shape[-2:]
    if mode == "nn":
        (M, K), K2, N = a.shape, br, bc * bs
    elif mode == "nt":
        (M, K), N, K2 = a.shape, br, bc * bs
    else:
        assert bs == 1
        (K, M), K2, N = a.shape, br, bc
    assert K == K2, (name, a.shape, b.shape, mode)
    n_piece = N // max(bs if mode == "nn" else 1, out_split)
    k_piece = K // (bs if mode == "nt" else 1)
    tm, tn, tk = _pick(M, tm, 8 if M < LANES else LANES), _pick(n_piece, tn), _pick(k_piece, tk)
    nk, npj, npk = K // tk, n_piece // tn, k_piece // tk
    dims = {"nn": ((1,), (0,)), "nt": ((1,), (1,)), "tn": ((0,), (0,))}[mode]

    ns = len(side)
    multi = isinstance(out_dtype, (tuple, list))
    out_dtypes = tuple(out_dtype) if multi else (out_dtype,)
    assert out_split == 1 or (epilogue is None and not multi)

    def body(a_ref, b_ref, *rest):
        side_refs, o_refs, acc_ref = rest[:ns], rest[ns:-1], rest[-1]
        o_ref = o_refs[0]
        k = pl.program_id(2)

        @pl.when(k == 0)
        def _():
            acc_ref[...] = jnp.zeros(acc_ref.shape, F32)

        acc_ref[...] += _dg(a_ref[...], b_ref[0] if bs > 1 else b_ref[...], dims)

        @pl.when(k == nk - 1)
        def _():
            if out_split > 1:
                o_ref[0] = acc_ref[...].astype(o_ref.dtype)
            elif epilogue is None:
                o_ref[...] = acc_ref[...].astype(o_ref.dtype)
            else:
                vals = epilogue(acc_ref[...], *[r[...] for r in side_refs])
                for ref, val in zip(o_refs, vals if isinstance(vals, (tuple, list)) else (vals,)):
                    ref[...] = val.astype(ref.dtype)

    if mode == "nn":
        a_spec = pl.BlockSpec((tm, tk), lambda i, j, k: (i, k))
        b_spec = (pl.BlockSpec((tk, tn), lambda i, j, k: (k, j)) if bs == 1 else
                  pl.BlockSpec((1, tk, tn), lambda i, j, k: (j // npj, k, j % npj)))
    elif mode == "nt":
        a_spec = pl.BlockSpec((tm, tk), lambda i, j, k: (i, k))
        b_spec = (pl.BlockSpec((tn, tk), lambda i, j, k: (j, k)) if bs == 1 else
                  pl.BlockSpec((1, tn, tk), lambda i, j, k: (k // npk, j, k % npk)))
    else:
        a_spec = pl.BlockSpec((tk, tm), lambda i, j, k: (k, i))
        b_spec = pl.BlockSpec((tk, tn), lambda i, j, k: (k, j))
    if out_split > 1:
        o_spec = pl.BlockSpec((1, tm, tn), lambda i, j, k: (j // npj, i, j % npj))
        o_shape = jax.ShapeDtypeStruct((out_split, M, n_piece), out_dtype)
    else:
        o_spec = [pl.BlockSpec((tm, tn), lambda i, j, k: (i, j)) for _ in out_dtypes]
        o_shape = [jax.ShapeDtypeStruct((M, N), dt) for dt in out_dtypes]
    side_specs = [pl.BlockSpec((1, tn), lambda i, j, k: (0, j)) if t.shape[0] == 1 else
                  pl.BlockSpec((tm, tn), lambda i, j, k: (i, j)) for t in side]
    res = pl.pallas_call(
        body, name=name, grid=(M // tm, N // tn, nk),
        in_specs=[a_spec, b_spec] + side_specs, out_specs=o_spec, out_shape=o_shape,
        scratch_shapes=[pltpu.VMEM((tm, tn), F32)],
        compiler_params=pltpu.CompilerParams(
            dimension_semantics=("parallel", "parallel", "arbitrary"), vmem_limit_bytes=VMEM_LIMIT),
    )(a, b, *side)
    return res if (multi or out_split > 1) else res[0]


def _rows(name, fn, rows, params, outs, accs=(), tile=256):
    S = rows[0].shape[0]
    tile = min(tile, S)
    assert S % tile == 0, (name, S, tile)
    nr, npar, no, na = len(rows), len(params), len(outs), len(accs)

    def body(*refs):
        rin, pin = refs[:nr], refs[nr:nr + npar]
        oref, aref = refs[nr + npar:nr + npar + no], refs[nr + npar + no:]
        res = fn(*[r[...] for r in rin], *[p[...] for p in pin])
        if not isinstance(res, (tuple, list)):
            res = (res,)
        assert len(res) == no + na, (name, len(res), no, na)
        for ref, val in zip(oref, res[:no]):
            ref[...] = val.astype(ref.dtype)
        if na:
            @pl.when(pl.program_id(0) == 0)
            def _():
                for ref in aref:
                    ref[...] = jnp.zeros(ref.shape, ref.dtype)

            for ref, val in zip(aref, res[no:]):
                ref[...] += jnp.broadcast_to(val, ref.shape).astype(ref.dtype)

    def whole(shape):
        nd = len(shape)
        return pl.BlockSpec(tuple(shape), lambda i: (0,) * nd)

    in_specs = ([pl.BlockSpec((tile, r.shape[1]), lambda i: (i, 0)) for r in rows]
                + [whole(p.shape) for p in params])
    out_specs = ([pl.BlockSpec((tile, n), lambda i: (i, 0)) for n, _ in outs]
                 + [whole(s) for s in accs])
    out_shape = ([jax.ShapeDtypeStruct((S, n), dt) for n, dt in outs]
                 + [jax.ShapeDtypeStruct(tuple(s), F32) for s in accs])
    res = pl.pallas_call(
        body, name=name, grid=(S // tile,), in_specs=in_specs, out_specs=out_specs,
        out_shape=out_shape,
        compiler_params=pltpu.CompilerParams(
            dimension_semantics=("arbitrary",), vmem_limit_bytes=VMEM_LIMIT),
    )(*rows, *params)
    return res


def _modulate(x, sc, sh):
    return x * (1.0 + sc) + sh


def _gmlp_consts():
    lane = lax.broadcasted_iota(jnp.int32, (1, G_WIDTH), 1)
    gmask = [(lane // (G_WIDTH // G_GROUPS) == g).astype(F32) for g in range(G_GROUPS)]
    tril = (lax.broadcasted_iota(jnp.int32, (CHUNK, CHUNK), 0)
            >= lax.broadcasted_iota(jnp.int32, (CHUNK, CHUNK), 1))
    return gmask, tril


def _gmlp_core(gmask, tril, gsel, zu, zv, bu, bv, g, b, bst, *ws):
    u = _gelu(zu + bu)
    v = _ln(_gelu(zv + bv), g, b, LN_EPS)
    s = _pdot(bst, gsel)
    for gi in range(G_GROUPS):
        s = s + _bdot(jnp.where(tril, ws[gi], 0.0), v * gmask[gi])
    return u * s


def _pre_core(rowmask, bd, zr, zk, zv, zl, pr, pk, pv, pq, br, bk, bv, bl, mr, mk, mv, ml,
              w0, wd, a0, wa, wg, k_k, k_a):
    def mix(z, p, b, mu):
        zz = z + b
        return zz + ((p + b) * rowmask - zz) * mu

    r, k, v, l = mix(zr, pr, br, mr), mix(zk, pk, bk, mk), mix(zv, pv, bv, mv), mix(zl, pq, bl, ml)
    w_log = -_softplus(-(w0 + _bdot(jnp.tanh(l), wd))) - 0.5
    decay = jnp.exp(-jnp.exp(w_log))
    a = _sigmoid(a0 + _bdot(l, wa))
    g = _bdot(_sigmoid(l), wg)
    kk = k * k_k
    kkn = kk / jnp.maximum(jnp.sqrt(_pdot(kk * kk, bd)), 1e-12)
    k2 = k * (1.0 + (a - 1.0) * k_a)
    return r, decay, k2, v, -kkn, kkn * a, g


def _post_core(bd, y, r, k2, v, g, gain, bias, rk):
    inv = 1.0 / R_HEAD
    mu = _pdot(y, bd) * inv
    yc = y - mu
    var = _pdot(yc * yc, bd) * inv
    yn = yc * lax.rsqrt(var + GN_EPS) * gain + bias
    bonus = _pdot(r * k2 * rk, bd) * v
    return (yn + bonus) * g


def _merge_core(pa, pb, ga, gb, bga, bgb):
    return _sigmoid(ga + bga) * pa + _sigmoid(gb + bgb) * pb


def _ln1_core(x, mix, gt1, bout, g, b, sc2, sh2):
    h1 = _ln(ALPHA * x + gt1 * (mix + bout), g, b, LN_EPS)
    return h1, h1 * (1.0 + sc2) + sh2


def _ln2_loss_core(tgt, h1, ff, gt2, bff2, g, b):
    out = _ln(ALPHA * h1 + gt2 * (ff + bff2), g, b, LN_EPS)
    err = out - tgt
    return 0.5 * jnp.sum(err * err) * (1.0 / D_MODEL)


def _scan_consts():
    sub = lax.broadcasted_iota(jnp.int32, (R_HEAD, LANES), 0)
    lane = lax.broadcasted_iota(jnp.int32, (R_HEAD, LANES), 1)
    return lane < R_HEAD, sub == (lane & (R_HEAD - 1))


def _seg_sum(lo, xb):
    s_lo = jnp.sum(jnp.where(lo, xb, 0.0), axis=1, keepdims=True)
    s_hi = jnp.sum(jnp.where(lo, 0.0, xb), axis=1, keepdims=True)
    return jnp.where(lo, s_lo, s_hi)


def _seg_dot(lo, s, row):
    lo_row = lo[0:1, :]
    s_lo = jnp.sum(s * jnp.where(lo_row, row, 0.0), axis=1, keepdims=True)
    s_hi = jnp.sum(s * jnp.where(lo_row, 0.0, row), axis=1, keepdims=True)
    return jnp.where(lo, s_lo, s_hi)


def _row_of_col(eye, colb):
    return jnp.sum(jnp.where(eye, colb, 0.0), axis=0, keepdims=True)


def _head_ones():
    i = lax.broadcasted_iota(jnp.int32, (LANES, LANES), 0) // R_HEAD
    j = lax.broadcasted_iota(jnp.int32, (LANES, LANES), 1) // R_HEAD
    return (i == j).astype(jnp.bfloat16)


N_SPLIT = 1


def _split(x):
    parts, r = [], x
    for u in range(N_SPLIT):
        p = r.astype(jnp.bfloat16)
        parts.append(p)
        if u + 1 < N_SPLIT:
            r = r - p.astype(F32)
    return parts


def _ones_dot(parts, ones, n):
    res = lax.dot_general(jnp.concatenate(parts, axis=0), ones, (((1,), (0,)), ((), ())),
                          preferred_element_type=F32)
    out = []
    for m in range(n):
        t = [res[(N_SPLIT * m + u) * R_HEAD:(N_SPLIT * m + u + 1) * R_HEAD] for u in range(N_SPLIT)]
        out.append(functools.reduce(lambda p, q: p + q, t))
    return out


def _seg_sums_mxu(ones, mats):
    return _ones_dot([p for m in mats for p in _split(m)], ones, len(mats))


def _cols_of_rows_mxu(eye, ones, rows8):
    terms = [p.astype(F32) for p in _split(rows8)]
    parts = [jnp.where(eye, jnp.broadcast_to(t[i:i + 1, :], eye.shape), 0.0).astype(jnp.bfloat16)
             for i in range(rows8.shape[0]) for t in terms]
    return _ones_dot(parts, ones, rows8.shape[0])


N_BLK = R_WIDTH // LANES
ROW_GROUP = 8


def _scan_fwd(r, w, k, v, a, b, gather=()):
    S = r.shape[0]
    T = min(SCAN_T, S)
    nchunk = S // T
    ng = len(gather)

    def body(*refs):
        r_ref, w_ref, k_ref, v_ref, a_ref, b_ref = refs[:6]
        g_in, (y_ref, sv_ref), g_out = refs[6:6 + ng], refs[6 + ng:8 + ng], refs[8 + ng:8 + 2 * ng]
        st_ref, sems = refs[8 + 2 * ng], refs[9 + 2 * ng:]
        lo, eye = _scan_consts()
        ones = _head_ones()

        @pl.when(pl.program_id(0) == 0)
        def _():
            st_ref[...] = jnp.zeros(st_ref.shape, F32)
            if ng:
                _ag_start(g_in, g_out, *sems)

        sub8 = lax.broadcasted_iota(jnp.int32, (ROW_GROUP, LANES), 0)

        def group(gi, state):
            base = pl.multiple_of(gi * ROW_GROUP, ROW_GROUP)
            state = list(state)
            sls = [slice(q * LANES, (q + 1) * LANES) for q in range(N_BLK)]
            ld = lambda ref: [ref[pl.ds(base, ROW_GROUP), sl] for sl in sls]
            r8, w8, k8, v8, a8, b8 = ld(r_ref), ld(w_ref), ld(k_ref), ld(v_ref), ld(a_ref), ld(b_ref)
            vb = [_cols_of_rows_mxu(eye, ones, v8[q]) for q in range(N_BLK)]
            an = [pltpu.roll(a8[q], ROW_GROUP - 1, 0) for q in range(N_BLK)]
            ap = [w8[q] * an[q] for q in range(N_BLK)]
            beta = [_seg_sum(lo[:ROW_GROUP], b8[q] * an[q]) for q in range(N_BLK)]
            kappa = [_seg_sum(lo[:ROW_GROUP], k8[q] * an[q]) for q in range(N_BLK)]
            y8 = [jnp.zeros((ROW_GROUP, LANES), F32)] * N_BLK

            def emit_y(i, y8_):
                ycol = _seg_sums_mxu(ones, [state[q] * r8[q][i:i + 1, :] for q in range(N_BLK)])
                return [jnp.where(sub8 == i, _row_of_col(eye, ycol[q]), y8_[q]) for q in range(N_BLK)]

            for i in range(0, ROW_GROUP, 2):
                row = lambda t8, d=0: t8[i + d:i + d + 1, :]
                sa1 = [None] * N_BLK
                for q in range(N_BLK):
                    s = state[q]
                    sv_ref[base + i, :, sls[q]] = s
                    sa0 = _seg_dot(lo, s, row(a8[q]))
                    nxt = _seg_dot(lo, s, row(ap[q]))
                    sa1[q] = nxt + row(beta[q]) * sa0 + row(kappa[q]) * vb[q][i]
                    state[q] = s * row(w8[q]) + sa0 * row(b8[q]) + vb[q][i] * row(k8[q])
                y8 = emit_y(i, y8)
                for q in range(N_BLK):
                    s = state[q]
                    sv_ref[base + i + 1, :, sls[q]] = s
                    state[q] = s * row(w8[q], 1) + sa1[q] * row(b8[q], 1) + vb[q][i + 1] * row(k8[q], 1)
                y8 = emit_y(i + 1, y8)
            for q in range(N_BLK):
                y_ref[pl.ds(base, ROW_GROUP), sls[q]] = y8[q]
            return tuple(state)

        init = tuple(st_ref[:, q * LANES:(q + 1) * LANES] for q in range(N_BLK))
        fin = lax.fori_loop(0, T // ROW_GROUP, group, init)
        for q in range(N_BLK):
            st_ref[:, q * LANES:(q + 1) * LANES] = fin[q]

        if ng:
            @pl.when(pl.program_id(0) == nchunk - 1)
            def _():
                _ag_finish(g_in, g_out, *sems)

    blk = pl.BlockSpec((T, R_WIDTH), lambda i: (i, 0))
    res = pl.pallas_call(
        body, name="scan_fwd", grid=(nchunk,), in_specs=[blk] * 6 + [ANY] * ng,
        out_specs=[blk, pl.BlockSpec((T, R_HEAD, R_WIDTH), lambda i: (i, 0, 0))] + [ANY] * ng,
        out_shape=[jax.ShapeDtypeStruct((S, R_WIDTH), F32),
                   jax.ShapeDtypeStruct((S, R_HEAD, R_WIDTH), F32)] + _ag_out_shapes(gather),
        scratch_shapes=[pltpu.VMEM((R_HEAD, R_WIDTH), F32)] + (_ag_sems(ng) if ng else []),
        compiler_params=pltpu.CompilerParams(
            dimension_semantics=("arbitrary",), vmem_limit_bytes=VMEM_LIMIT),
    )(r, w, k, v, a, b, *gather)
    return res[0], res[1], list(res[2:])


def _scan_bwd(r, w, k, v, a, b, states, dy, scatter=()):
    S = r.shape[0]
    T = min(SCAN_T, S)
    nchunk = S // T
    ns = len(scatter)

    def body(*refs):
        r_ref, w_ref, k_ref, v_ref, a_ref, b_ref, sv_ref, dy_ref = refs[:8]
        x_in, x_out = refs[8:8 + ns], refs[14 + ns:14 + 2 * ns]
        dr_ref, dw_ref, dk_ref, dv_ref, da_ref, db_ref = refs[8 + ns:14 + ns]
        ds_ref, sems = refs[14 + 2 * ns], refs[15 + 2 * ns:]
        lo, eye = _scan_consts()
        ones = _head_ones()

        @pl.when(pl.program_id(0) == 0)
        def _():
            ds_ref[...] = jnp.zeros(ds_ref.shape, F32)
            if ns:
                _rsc_start(x_in, x_out, *sems)

        sub8 = lax.broadcasted_iota(jnp.int32, (ROW_GROUP, LANES), 0)

        def bgroup(n, dstate, chunk_end):
            base = (T // ROW_GROUP - 1 - n) * ROW_GROUP
            if not isinstance(n, int):
                base = pl.multiple_of(base, ROW_GROUP)
            dstate = list(dstate)
            sls = [slice(q * LANES, (q + 1) * LANES) for q in range(N_BLK)]
            ld = lambda ref: [ref[pl.ds(base, ROW_GROUP), sl] for sl in sls]
            r8, w8, k8, v8, a8, b8, dy8 = (ld(r_ref), ld(w_ref), ld(k_ref), ld(v_ref), ld(a_ref), ld(b_ref),
                                           ld(dy_ref))
            vbs = [_cols_of_rows_mxu(eye, ones, v8[q]) for q in range(N_BLK)]
            dycs = [_cols_of_rows_mxu(eye, ones, dy8[q]) for q in range(N_BLK)]
            acc = [{n_: jnp.zeros((ROW_GROUP, LANES), F32) for n_ in ("r", "w", "k", "v", "a", "b")}
                   for _ in range(N_BLK)]
            for i in reversed(range(ROW_GROUP)):
                row = lambda t8: t8[i:i + 1, :]
                put = lambda q_, n_, val: acc[q_].__setitem__(n_, jnp.where(sub8 == i, val, acc[q_][n_]))
                dks = []
                for q in range(N_BLK):
                    sp, vb = sv_ref[base + i, :, sls[q]], vbs[q][i]
                    wr, ar, br, kr, rr = row(w8[q]), row(a8[q]), row(b8[q]), row(k8[q]), row(r8[q])
                    sa = _seg_dot(lo, sp, ar)
                    if chunk_end and i == ROW_GROUP - 1:
                        st = sp * wr + sa * br + vb * kr
                    else:
                        st = sv_ref[base + i + 1, :, sls[q]]
                    dyc = dycs[q][i]
                    ds = dstate[q] + dyc * rr
                    put(q, "r", _colsum(st * dyc))
                    put(q, "w", _colsum(ds * sp))
                    put(q, "b", _colsum(ds * sa))
                    put(q, "k", _colsum(ds * vb))
                    dsa = _seg_dot(lo, ds, br)
                    dks.append(ds * kr)
                    put(q, "a", _colsum(sp * dsa))
                    dstate[q] = ds * wr + dsa * ar
                dvc = _seg_sums_mxu(ones, dks)
                for q in range(N_BLK):
                    put(q, "v", _row_of_col(eye, dvc[q]))
            for q in range(N_BLK):
                for n_, ref in (("r", dr_ref), ("w", dw_ref), ("k", dk_ref), ("v", dv_ref), ("a", da_ref), ("b", db_ref)):
                    ref[pl.ds(base, ROW_GROUP), sls[q]] = acc[q][n_]
            return tuple(dstate)

        fin = bgroup(0, tuple(ds_ref[:, q * LANES:(q + 1) * LANES] for q in range(N_BLK)), True)
        fin = lax.fori_loop(1, T // ROW_GROUP, lambda n, d: bgroup(n, d, False), fin)
        for q in range(N_BLK):
            ds_ref[:, q * LANES:(q + 1) * LANES] = fin[q]

        if ns:
            @pl.when(pl.program_id(0) == nchunk - 1)
            def _():
                _rsc_finish(x_in, x_out, *sems)

    blk = pl.BlockSpec((T, R_WIDTH), lambda i: (nchunk - 1 - i, 0))
    svb = pl.BlockSpec((T, R_HEAD, R_WIDTH), lambda i: (nchunk - 1 - i, 0, 0))
    return pl.pallas_call(
        body, name="scan_bwd", grid=(nchunk,), in_specs=[blk] * 6 + [svb, blk] + [ANY] * ns,
        out_specs=[blk] * 6 + [ANY] * ns,
        out_shape=[jax.ShapeDtypeStruct((S, R_WIDTH), F32)] * 6 + [jax.ShapeDtypeStruct(p.shape, p.dtype) for p in scatter],
        scratch_shapes=[pltpu.VMEM((R_HEAD, R_WIDTH), F32)] + (_rsc_sems(ns) if ns else []),
        compiler_params=pltpu.CompilerParams(
            dimension_semantics=("arbitrary",), vmem_limit_bytes=VMEM_LIMIT),
    )(r, w, k, v, a, b, states, dy, *scatter)


def _pad_in_cols(t, axis):
    cut = 2 * G_WIDTH + R_COLS
    lo, hi = lax.slice_in_dim(t, 0, cut, axis=axis), lax.slice_in_dim(t, cut, t.shape[axis], axis=axis)
    zshape = list(t.shape)
    zshape[axis] = LORA_PAD - LORA
    return jnp.concatenate([lo, jnp.zeros(zshape, t.dtype), hi], axis=axis)


def _unpad_in_cols(t, axis):
    cut = 2 * G_WIDTH + R_COLS
    return jnp.concatenate([lax.slice_in_dim(t, 0, cut, axis=axis),
                            lax.slice_in_dim(t, cut + LORA_PAD - LORA, t.shape[axis], axis=axis)], axis=axis)


def _pad_rows(t, lo, n):
    return jnp.zeros((n, t.shape[1]), t.dtype).at[lo:lo + t.shape[0]].set(t)


def _join_cols(w3):
    p, k, n = w3.shape
    return jnp.transpose(w3, (1, 0, 2)).reshape(k, p * n)


def _split_cols(w):
    k, n = w.shape
    return jnp.transpose(w.reshape(k, N_CHIPS, n // N_CHIPS), (1, 0, 2))


def _ada_dw(ccol, dmod):
    n = dmod.shape[1] // N_CHIPS
    tile = 256

    def body(c_ref, d_ref, o_ref):
        cc = c_ref[...]
        o_ref[0] = ((cc * _sigmoid(cc)) * d_ref[...]).astype(o_ref.dtype)

    return pl.pallas_call(
        body, name="ada_dw", grid=(N_CHIPS, D_MODEL // tile),
        in_specs=[pl.BlockSpec((tile, 1), lambda q, i: (i, 0)), pl.BlockSpec((1, n), lambda q, i: (0, q))],
        out_specs=pl.BlockSpec((1, tile, n), lambda q, i: (q, i, 0)),
        out_shape=jax.ShapeDtypeStruct((N_CHIPS, D_MODEL, n), WIRE_DT),
        compiler_params=pltpu.CompilerParams(dimension_semantics=("parallel", "parallel")),
    )(ccol, dmod)


def _shift_down(t):
    return jnp.concatenate([jnp.zeros((1, t.shape[1]), t.dtype), t[:-1]], axis=0)


def _shift_up(t):
    return jnp.concatenate([t[1:], jnp.zeros((1, t.shape[1]), t.dtype)], axis=0)


LATE = ("w_branch_a", "w_branch_b", "w_out", "w_ff1", "w_ff2")


def _local_step(x, c, tgt, W):
    S = x.shape[0]
    bf = MXU_DT
    G = {}

    hl = np.arange(R_WIDTH) // R_HEAD
    bd = jnp.asarray(hl[:, None] == hl[None, :], jnp.bfloat16)
    gsel = jnp.asarray(np.arange(LANES)[:, None] == (np.arange(G_WIDTH) // (G_WIDTH // G_GROUPS))[None, :],
                       jnp.bfloat16)
    w_in_p = _pad_in_cols(_join_cols(W["w_in"]), 1)
    b_in_p = _pad_in_cols(W["b_in"], 1)
    c_g, c_r = 2 * G_WIDTH, 2 * G_WIDTH + R_COLS_PAD
    w_g, w_r, w_gate = w_in_p[:, :c_g], w_in_p[:, c_g:c_r], w_in_p[:, c_r:]
    b_g, b_r, b_gate = b_in_p[:, :c_g], b_in_p[:, c_g:c_r], b_in_p[:, c_r:]
    mu_p = jnp.concatenate([W["mu_shift"], jnp.zeros((1, LORA_PAD - LORA), F32)], axis=1)
    wd_p = _pad_rows(W["w_decay_up"].astype(F32), 0, LORA_PAD)
    wa_p = _pad_rows(W["w_aaa_up"].astype(F32), DECAY_LORA, LORA_PAD)
    wg_p = _pad_rows(W["w_gate_up"].astype(F32), DECAY_LORA + AAA_LORA, LORA_PAD)
    ws2 = W["w_spatial"].reshape(G_GROUPS * CHUNK, CHUNK)
    bst = jnp.zeros((CHUNK, LANES), F32).at[:, :G_GROUPS].set(W["b_spatial"].T)
    rk = W["r_k"].reshape(1, R_WIDTH)

    c8 = jnp.broadcast_to(c, (8, D_MODEL))
    (ca8,) = _rows("ada_silu", lambda cc: cc * _sigmoid(cc), [c8], [], [(D_MODEL, bf)], tile=8)
    mod_raw = _mm("ada_mm", ca8, W["w_ada"], "nn")
    (mod8,) = _rows("ada_bias", lambda m, bb: m + bb, [mod_raw], [W["b_ada"]], [(6 * D_MODEL, F32)], tile=8)
    sh1, sc1, gt1, sh2, sc2, gt2 = [mod8[0:1, i * D_MODEL:(i + 1) * D_MODEL] for i in range(6)]

    (h,) = _rows("mod1", _modulate, [x], [sc1, sh1], [(D_MODEL, bf)])
    proj_g = _mm("proj_g", h, w_g, "nn")
    proj_r = _mm("proj_r", h, w_r, "nn")
    proj_gate = _mm("proj_gate", h, w_gate, "nn")

    def split2(t):
        return t[:, :G_WIDTH], t[:, G_WIDTH:]

    def gmlp_fwd(z, bz, g, b, bst_, gsel_, ws_):
        gmask, tril = _gmlp_consts()
        (zu, zv), (bu, bv) = split2(z), split2(bz)
        wsl = [ws_[i * CHUNK:(i + 1) * CHUNK] for i in range(G_GROUPS)]
        return _gmlp_core(gmask, tril, gsel_, zu, zv, bu, bv, g, b, bst_, *wsl)

    (y_a,) = _rows("gmlp_fwd", gmlp_fwd, [proj_g], [b_g, W["g_ln_v"], W["b_ln_v"], bst, gsel, ws2],
                   [(G_WIDTH, bf)], tile=CHUNK)

    r_cuts = (0, R_WIDTH, 2 * R_WIDTH, 3 * R_WIDTH, R_COLS_PAD)

    def split4(t):
        return [t[:, r_cuts[i]:r_cuts[i + 1]] for i in range(4)]

    tile_pre = min(256, S)

    def rowmask_of():
        grow = pl.program_id(0) * tile_pre + lax.broadcasted_iota(jnp.int32, (tile_pre, 1), 0)
        return (grow > 0).astype(F32)

    pre_params = [b_r, mu_p, W["w0"], wd_p, W["a0"], wa_p, wg_p, W["k_k"], W["k_a"], bd]

    def pre_fwd(z, p, bz, mu, w0, wd, a0, wa, wg, k_k, k_a, bd_):
        return _pre_core(rowmask_of(), bd_, *split4(z), *split4(p), *split4(bz), *split4(mu),
                         w0, wd, a0, wa, wg, k_k, k_a)

    proj_r_prev = _shift_down(proj_r)
    s_r, s_w, s_k, s_v, s_a, s_b, s_g = _rows(
        "rwkv_pre_fwd", pre_fwd, [proj_r, proj_r_prev], pre_params, [(R_WIDTH, F32)] * 7, tile=tile_pre)
    y_scan, states, late = yield "fwd", (s_r, s_w, s_k, s_v, s_a, s_b)
    W = {**W, **late}

    def post_fwd(y, r, k2, v, g, gain, bias, rk_, bd_):
        return _post_core(bd_, y, r, k2, v, g, gain, bias, rk_)

    post_params = [W["gn_gain"], W["gn_bias"], rk, bd]
    (y_b,) = _rows("rwkv_post_fwd", post_fwd, [y_scan, s_r, s_k, s_v, s_g], post_params, [(R_WIDTH, bf)])
    p_a = _mm("branch_a", y_a, W["w_branch_a"], "nn")
    p_b = _mm("branch_b", y_b, W["w_branch_b"], "nn")

    def merge_fwd(pa, pb, gz, bgz):
        return _merge_core(pa, pb, gz[:, :D_MODEL], gz[:, D_MODEL:], bgz[:, :D_MODEL], bgz[:, D_MODEL:])

    (merged,) = _rows("merge_fwd", merge_fwd, [p_a, p_b, proj_gate], [b_gate], [(D_MODEL, bf)])
    mix = _mm("out_proj", merged, W["w_out"], "nn")
    ln1_params = [gt1, W["b_out"], W["ln1_g"], W["ln1_b"], sc2, sh2]
    h1, h2 = _rows("ln1_fwd", _ln1_core, [x, mix], ln1_params, [(D_MODEL, F32), (D_MODEL, bf)])

    def relu2(acc, bb):
        ra = jnp.maximum(acc + bb, 0.0)
        return ra * ra, ra

    act, ra = _mm("ff1", h2, W["w_ff1"], "nn", (bf, bf), epilogue=relu2, side=[W["b_ff1"]])
    ff = _mm("ff2", act, W["w_ff2"], "nn")

    def ln2_loss(h1_, ff_, tg, gt2_, bff2, g, b):
        loss, vjp = jax.vjp(functools.partial(_ln2_loss_core, tg), h1_, ff_, gt2_, bff2, g, b)
        return vjp(jnp.ones((), F32)) + (loss,)

    ln2_params = [gt2, W["b_ff2"], W["ln2_g"], W["ln2_b"]]
    dh1, dff, dgt2, G["b_ff2"], G["ln2_g"], G["ln2_b"], loss_acc = _rows(
        "ln2_loss", ln2_loss, [h1, ff, tgt], ln2_params, [(D_MODEL, F32), (D_MODEL, bf)],
        accs=[(1, D_MODEL)] * 4 + [(1, LANES)])
    loss = loss_acc[0, 0]

    da1 = _mm("ff2_dx", dff, W["w_ff2"], "nt", bf, epilogue=lambda acc, r_: acc * (2.0 * r_.astype(F32)), side=[ra])
    G["w_ff2"] = _mm("ff2_dw", act, dff, "tn", WIRE_DT).reshape(N_CHIPS, D_FF // N_CHIPS, D_MODEL)
    (G["b_ff1"],) = _rows("ff_db", lambda d: _colsum(d.astype(F32)), [da1], [], [], accs=[(1, D_FF)], tile=512)
    dh2 = _mm("ff1_dx", da1, W["w_ff1"], "nt")
    G["w_ff1"] = _mm("ff1_dw", h2, da1, "tn", WIRE_DT, out_split=N_CHIPS)

    def ln1_bwd(x_, mix_, dh1_, dh2_, *ps):
        _, vjp = jax.vjp(_ln1_core, x_, mix_, *ps)
        return vjp((dh1_, dh2_))

    dx_res, dmix, dgt1, G["b_out"], G["ln1_g"], G["ln1_b"], dsc2, dsh2 = _rows(
        "ln1_bwd", ln1_bwd, [x, mix, dh1, dh2], ln1_params, [(D_MODEL, F32), (D_MODEL, bf)],
        accs=[(1, D_MODEL)] * 6)

    dmerged = _mm("out_proj_dx", dmix, W["w_out"], "nt")
    G["w_out"] = _mm("out_proj_dw", merged, dmix, "tn", WIRE_DT).reshape(N_CHIPS, D_MODEL // N_CHIPS, D_MODEL)

    def merge_bwd(pa, pb, gz, dm, bgz):
        args = (pa.astype(F32), pb.astype(F32), gz[:, :D_MODEL], gz[:, D_MODEL:], bgz[:, :D_MODEL], bgz[:, D_MODEL:])
        _, vjp = jax.vjp(_merge_core, *args)
        dpa, dpb, dga, dgb, dbga, dbgb = vjp(dm)
        return dpa, dpb, jnp.concatenate([dga, dgb], axis=1), jnp.concatenate([dbga, dbgb], axis=1)

    dp_a, dp_b, dgates, db_gate = _rows(
        "merge_bwd", merge_bwd, [p_a, p_b, proj_gate, dmerged], [b_gate],
        [(D_MODEL, bf), (D_MODEL, bf), (2 * D_MODEL, bf)], accs=[(1, 2 * D_MODEL)])
    dy_a = _mm("branch_a_dx", dp_a, W["w_branch_a"], "nt")
    G["w_branch_a"] = _mm("branch_a_dw", y_a, dp_a, "tn", WIRE_DT, out_split=N_CHIPS)
    dy_b = _mm("branch_b_dx", dp_b, W["w_branch_b"], "nt")
    G["w_branch_b"] = _mm("branch_b_dw", y_b, dp_b, "tn", WIRE_DT, out_split=N_CHIPS)

    def post_bwd(y, r, k2, v, g, dyb, gain, bias, rk_, bd_):
        _, vjp = jax.vjp(functools.partial(_post_core, bd_), y, r, k2, v, g, gain, bias, rk_)
        return vjp(dyb)

    dy_scan, dr_p, dk_p, dv_p, dg_p, G["gn_gain"], G["gn_bias"], drk = _rows(
        "rwkv_post_bwd", post_bwd, [y_scan, s_r, s_k, s_v, s_g, dy_b], post_params,
        [(R_WIDTH, F32)] * 5, accs=[(1, R_WIDTH)] * 3)
    G["r_k"] = drk.reshape(R_HEADS, R_HEAD)
    dr_s, dw_s, dk_s, dv_s, da_s, db_s = yield "bwd", (s_r, s_w, s_k, s_v, s_a, s_b, states, dy_scan), G

    def pre_bwd(z, p, dr1, dr2, dw, dk1, dk2, dv1, dv2, da, db, dg,
                bz, mu, w0, wd, a0, wa, wg, k_k, k_a, bd_):
        prim = (*split4(z), *split4(p), *split4(bz), *split4(mu), w0, wd, a0, wa, wg, k_k, k_a)
        _, vjp = jax.vjp(functools.partial(_pre_core, rowmask_of(), bd_), *prim)
        d = vjp((dr1 + dr2, dw, dk1 + dk2, dv1 + dv2, da, db, dg))
        cat = lambda parts: jnp.concatenate(parts, axis=1)
        return (cat(d[0:4]), cat(d[4:8]), cat(d[8:12]), cat(d[12:16])) + tuple(d[16:])

    dz_r, dprev, db_r, dmu_p, G["w0"], dwd_p, G["a0"], dwa_p, dwg_p, G["k_k"], G["k_a"] = _rows(
        "rwkv_pre_bwd", pre_bwd,
        [proj_r, proj_r_prev, dr_s, dr_p, dw_s, dk_s, dk_p, dv_s, dv_p, da_s, db_s, dg_p],
        pre_params, [(R_COLS_PAD, bf)] * 2,
        accs=[(1, R_COLS_PAD), (1, R_COLS_PAD), (1, R_WIDTH), (LORA_PAD, R_WIDTH), (1, R_WIDTH),
              (LORA_PAD, R_WIDTH), (LORA_PAD, R_WIDTH), (1, R_WIDTH), (1, R_WIDTH)],
        tile=tile_pre)
    G["mu_shift"] = dmu_p[:, :R_COLS]
    G["w_decay_up"] = dwd_p[:DECAY_LORA]
    G["w_aaa_up"] = dwa_p[DECAY_LORA:DECAY_LORA + AAA_LORA]
    G["w_gate_up"] = dwg_p[DECAY_LORA + AAA_LORA:LORA]

    def gmlp_bwd(z, dya, bz, g, b, bst_, gsel_, ws_):
        gmask, tril = _gmlp_consts()
        (zu, zv), (bu, bv) = split2(z), split2(bz)
        wsl = [ws_[i * CHUNK:(i + 1) * CHUNK] for i in range(G_GROUPS)]
        _, vjp = jax.vjp(functools.partial(_gmlp_core, gmask, tril, gsel_), zu, zv, bu, bv, g, b, bst_, *wsl)
        d = vjp(dya)
        return (jnp.concatenate(d[0:2], axis=1), jnp.concatenate(d[2:4], axis=1), d[4], d[5], d[6],
                jnp.concatenate(d[7:], axis=0))

    dz_g, db_g, G["g_ln_v"], G["b_ln_v"], dbst, dws2 = _rows(
        "gmlp_bwd", gmlp_bwd, [proj_g, dy_a], [b_g, W["g_ln_v"], W["b_ln_v"], bst, gsel, ws2],
        [(2 * G_WIDTH, bf)],
        accs=[(1, 2 * G_WIDTH), (1, G_WIDTH), (1, G_WIDTH), (CHUNK, LANES), (G_GROUPS * CHUNK, CHUNK)],
        tile=CHUNK)
    G["w_spatial"] = dws2.reshape(G_GROUPS, CHUNK, CHUNK)
    G["b_spatial"] = dbst[:, :G_GROUPS].T

    def dproj_cat(dzg, dzr, dpv, dgz):
        return jnp.concatenate([dzg, (dzr.astype(F32) + dpv.astype(F32)).astype(dzg.dtype), dgz], axis=1)

    (dproj,) = _rows("dproj_cat", dproj_cat, [dz_g, dz_r, _shift_up(dprev), dgates], [],
                     [(2 * G_WIDTH + R_COLS_PAD + 2 * D_MODEL, bf)])
    dh = _mm("proj_dx", dproj, w_in_p, "nt", tk=2432)
    G["w_in"] = _split_cols(_unpad_in_cols(_mm("proj_dw", h, dproj, "tn", WIRE_DT, tm=512, tn=2432), 1))
    G["b_in"] = _unpad_in_cols(jnp.concatenate([db_g, db_r, db_gate], axis=1), 1)

    def mod1_bwd(x_, dh_, dxr, sc):
        return dh_ * (1.0 + sc) + dxr, _colsum(dh_ * x_), _colsum(dh_)

    grad_x, dsc1, dsh1 = _rows("mod1_bwd", mod1_bwd, [x, dh, dx_res], [sc1], [(D_MODEL, F32)],
                               accs=[(1, D_MODEL)] * 2)

    dmod = jnp.concatenate([dsh1, dsc1, dgt1, dsh2, dsc2, dgt2], axis=1)
    G["b_ada"] = dmod
    G["w_ada"] = _ada_dw(c.reshape(D_MODEL, 1), dmod)
    return loss, grad_x, G


BIG = (("w_ada", (D_MODEL, 6 * D_MODEL), 1), ("w_in", (D_MODEL, 2 * G_WIDTH + R_COLS + 2 * D_MODEL), 1),
       ("w_branch_a", (G_WIDTH, D_MODEL), 1), ("w_branch_b", (R_WIDTH, D_MODEL), 1),
       ("w_out", (D_MODEL, D_MODEL), 0), ("w_ff1", (D_MODEL, D_FF), 1), ("w_ff2", (D_FF, D_MODEL), 0))
LORAS = (("w_decay_up", (DECAY_LORA, R_WIDTH), 1), ("w_aaa_up", (AAA_LORA, R_WIDTH), 1),
         ("w_gate_up", (GATE_LORA, R_WIDTH), 1))
SHARDED = BIG + LORAS
SMALL = (("b_ada", (1, 6 * D_MODEL)), ("b_in", (1, 2 * G_WIDTH + R_COLS + 2 * D_MODEL)),
         ("g_ln_v", (1, G_WIDTH)), ("b_ln_v", (1, G_WIDTH)), ("w_spatial", (G_GROUPS, CHUNK, CHUNK)),
         ("b_spatial", (G_GROUPS, CHUNK)), ("mu_shift", (1, R_COLS)), ("w0", (1, R_WIDTH)),
         ("a0", (1, R_WIDTH)), ("k_k", (1, R_WIDTH)), ("k_a", (1, R_WIDTH)), ("r_k", (R_HEADS, R_HEAD)),
         ("gn_gain", (1, R_WIDTH)), ("gn_bias", (1, R_WIDTH)), ("b_out", (1, D_MODEL)),
         ("ln1_g", (1, D_MODEL)), ("ln1_b", (1, D_MODEL)), ("b_ff1", (1, D_FF)), ("b_ff2", (1, D_MODEL)),
         ("ln2_g", (1, D_MODEL)), ("ln2_b", (1, D_MODEL)))
WEIGHT_ORDER = ("w_ada", "b_ada", "w_in", "b_in", "g_ln_v", "b_ln_v", "w_spatial", "b_spatial", "mu_shift",
                "w0", "w_decay_up", "a0", "w_aaa_up", "w_gate_up", "k_k", "k_a", "r_k", "gn_gain", "gn_bias",
                "w_branch_a", "w_branch_b", "w_out", "b_out", "ln1_g", "ln1_b", "w_ff1", "b_ff1", "w_ff2",
                "b_ff2", "ln2_g", "ln2_b")
N_CHIPS = 4


def _shard_shape(shape, axis):
    s = list(shape)
    s[axis] //= N_CHIPS
    return tuple(s)


def _numel(shape):
    return int(np.prod(shape))


def _round_up(n, q):
    return -(-n // q) * q


PIECE_Q = 1


def _piece(shape):
    return _round_up(_numel(shape), PIECE_Q)


N_LORA = sum(_piece(_shard_shape(s, a)) for _, s, a in LORAS)
N_SMALL = sum(_piece(s) for _, s in SMALL)
ROWS_SW = _round_up(N_LORA, PACK_Q) // PACK_W
ROWS_SG = _round_up(N_LORA + N_SMALL, PACK_Q) // PACK_W


def _flat_pieces(parts, dtype, total=None):
    out, n = [], 0
    for p in parts:
        out.append(p.reshape(-1).astype(dtype))
        pad = _piece(p.shape) - out[-1].shape[0]
        if pad:
            out.append(jnp.zeros((pad,), dtype))
        n += _piece(p.shape)
    if total is not None and total > n:
        out.append(jnp.zeros((total - n,), dtype))
    return jnp.concatenate(out)


def _pack_small(loras, small, rows, dtype):
    parts = [loras[n] for n, _, _ in LORAS] + ([small[n] for n, _ in SMALL] if small is not None else [])
    return _flat_pieces(parts, dtype, rows * PACK_W).reshape(rows, PACK_W)


def _unpack_small(pack, with_small):
    flat = pack.reshape(-1)
    out, off = {}, 0
    for n, s, a in LORAS:
        ss = _shard_shape(s, a)
        out[n] = flat[off:off + _numel(ss)].reshape(ss)
        off += _piece(ss)
    if with_small:
        for n, s in SMALL:
            out[n] = flat[off:off + _numel(s)].reshape(s)
            off += _piece(s)
    return out


def _pack_small_grads(G):
    common = _flat_pieces([G[n] for n, _ in SMALL], F32, ROWS_SG * PACK_W - N_LORA)
    segs = []
    for q in range(N_CHIPS):
        loras = [G[n][:, q * (s[1] // N_CHIPS):(q + 1) * (s[1] // N_CHIPS)] for n, s, _ in LORAS]
        segs.append(jnp.concatenate([_flat_pieces(loras, F32), common]).reshape(ROWS_SG, PACK_W))
    return jnp.stack(segs)


ANY = pl.BlockSpec(memory_space=pl.ANY)
MESH = pl.DeviceIdType.MESH


def _place():
    x, y, c = lax.axis_index("x"), lax.axis_index("y"), lax.axis_index("c")
    chips = [(1 - x, y), (x, 1 - y), (1 - x, 1 - y)]
    return x, y, c, chips


def _remote(src, dst, send_sem, recv_sem, to):
    return pltpu.make_async_remote_copy(src_ref=src, dst_ref=dst, send_sem=send_sem, recv_sem=recv_sem,
                                        device_id=to, device_id_type=MESH)


def _ag_copies(ins, outs, send_sems, recv_sems):
    x, y, c, chips = _place()
    s = 2 * x + y
    cps = []
    for a in range(len(ins)):
        H = ins[a].shape[0] // 2
        mine = ins[a].at[pl.ds(c * H, H)]
        cps += [_remote(mine, outs[a].at[2 * s + c], send_sems.at[6 * a + j], recv_sems.at[6 * a + j], (*chip, c))
                for j, chip in enumerate(chips)]
    return cps


def _ag_start(ins, outs, send_sems, recv_sems):
    for cp in _ag_copies(ins, outs, send_sems, recv_sems):
        cp.start()


def _ag_finish(ins, outs, send_sems, recv_sems):
    x, y, c, chips = _place()
    sibling = (x, y, 1 - c)
    slot = [2 * chip[0] + chip[1] for chip in chips]
    passed = []
    for a in range(len(ins)):
        for j in range(3):
            landed = outs[a].at[2 * slot[j] + c]
            _remote(landed, landed, send_sems.at[6 * a + j], recv_sems.at[6 * a + j], sibling).wait_recv()
            cp = _remote(landed, landed, send_sems.at[6 * a + 3 + j], recv_sems.at[6 * a + 3 + j], sibling)
            cp.start()
            passed.append(cp)
    for a in range(len(ins)):
        for j in range(3):
            got = outs[a].at[2 * slot[j] + 1 - c]
            _remote(got, got, send_sems.at[6 * a + 3 + j], recv_sems.at[6 * a + 3 + j], sibling).wait_recv()
    for cp in _ag_copies(ins, outs, send_sems, recv_sems) + passed:
        cp.wait_send()


def _ag_out_shapes(shards):
    return [jax.ShapeDtypeStruct((2 * N_CHIPS, a.shape[0] // 2, a.shape[1]), a.dtype) for a in shards]


def _ag_sems(n):
    return [pltpu.SemaphoreType.DMA((6 * n,)), pltpu.SemaphoreType.DMA((6 * n,))]


def _all_gather(shards):
    n = len(shards)

    def body(*refs):
        ins, outs, sems = refs[:n], refs[n:2 * n], refs[2 * n:]
        _ag_start(ins, outs, *sems)
        _ag_finish(ins, outs, *sems)

    return pl.pallas_call(
        body, name="ag_weights", in_specs=[ANY] * n, out_specs=[ANY] * n,
        out_shape=_ag_out_shapes(shards), scratch_shapes=_ag_sems(n),
    )(*shards)


def _rs_sibling_in(gps):
    n = len(gps)

    def body(*refs):
        ins, outs = refs[:n], refs[n:2 * n]
        send_sems, recv_sems = refs[2 * n:]
        x, y, c, _ = _place()
        cps = []
        for a in range(n):
            H = gps[a].shape[1] // 2
            cps += [_remote(ins[a].at[q, pl.ds((1 - c) * H, H)], outs[a].at[q], send_sems.at[N_CHIPS * a + q],
                            recv_sems.at[N_CHIPS * a + q], (x, y, 1 - c)) for q in range(N_CHIPS)]
        for cp in cps:
            cp.start()
        for cp in cps:
            cp.wait()

    return pl.pallas_call(
        body, name="rs_sibling_in", in_specs=[ANY] * n, out_specs=[ANY] * n,
        out_shape=[jax.ShapeDtypeStruct((N_CHIPS, g.shape[1] // 2, g.shape[2]), g.dtype) for g in gps],
        scratch_shapes=[pltpu.SemaphoreType.DMA((N_CHIPS * n,)), pltpu.SemaphoreType.DMA((N_CHIPS * n,))],
    )(*gps)


def _rs_add_own(name, gp, got, c_arr, tr=256):
    H, C = got.shape[1:]
    tr = _pick(H, tr, 8)
    nb = H // tr

    def body(c_ref, g_ref, r_ref, o_ref):
        o_ref[...] = (g_ref[...].astype(F32) + r_ref[...].astype(F32)).astype(o_ref.dtype)

    return pl.pallas_call(
        body, name="rs_add_own_" + name,
        grid_spec=pltpu.PrefetchScalarGridSpec(
            num_scalar_prefetch=1, grid=(N_CHIPS, nb),
            in_specs=[pl.BlockSpec((1, tr, C), lambda q, i, c_ref: (q, c_ref[0] * nb + i, 0)),
                      pl.BlockSpec((1, tr, C), lambda q, i, c_ref: (q, i, 0))],
            out_specs=pl.BlockSpec((1, tr, C), lambda q, i, c_ref: (q, i, 0))),
        out_shape=jax.ShapeDtypeStruct((N_CHIPS, H, C), gp.dtype),
        compiler_params=pltpu.CompilerParams(dimension_semantics=("arbitrary", "arbitrary")),
    )(c_arr, gp, got)


def _rs_chips(parts):
    n = len(parts)

    def body(*refs):
        ins, outs, sems = refs[:n], refs[n:2 * n], refs[2 * n:]
        _rsc_start(ins, outs, *sems)
        _rsc_finish(ins, outs, *sems)

    return pl.pallas_call(
        body, name="rs_chips", in_specs=[ANY] * n, out_specs=[ANY] * n,
        out_shape=[jax.ShapeDtypeStruct(p.shape, p.dtype) for p in parts], scratch_shapes=_rsc_sems(n),
    )(*parts)


def _rsc_copies(ins, outs, send_sems, recv_sems):
    x, y, c, chips = _place()
    s = 2 * x + y
    return [_remote(ins[a].at[2 * chip[0] + chip[1]], outs[a].at[s], send_sems.at[3 * a + j], recv_sems.at[3 * a + j],
                    (*chip, c)) for a in range(len(ins)) for j, chip in enumerate(chips)]


def _rsc_start(ins, outs, send_sems, recv_sems):
    for cp in _rsc_copies(ins, outs, send_sems, recv_sems):
        cp.start()


def _rsc_finish(ins, outs, send_sems, recv_sems):
    x, y, c, chips = _place()
    for cp in _rsc_copies(ins, outs, send_sems, recv_sems):
        cp.wait_send()
    for a in range(len(ins)):
        for j, chip in enumerate(chips):
            got = outs[a].at[2 * chip[0] + chip[1]]
            _remote(got, got, send_sems.at[3 * a + j], recv_sems.at[3 * a + j], (*chip, c)).wait_recv()


def _rsc_sems(n):
    return [pltpu.SemaphoreType.DMA((3 * n,)), pltpu.SemaphoreType.DMA((3 * n,))]


def _rs_add_chips(name, part, slots, sc_arr, tr=128):
    H, C = slots.shape[1:]
    tr = _pick(H, tr, 8)
    nb = H // tr

    def body(sc_ref, p_ref, s_ref, o_ref):
        acc = None
        for q in range(N_CHIPS):
            term = lax.cond(sc_ref[0] == q, lambda: p_ref[0].astype(F32), lambda q=q: s_ref[q].astype(F32))
            acc = term if acc is None else acc + term
        o_ref[...] = acc

    return pl.pallas_call(
        body, name="rs_add_chips_" + name,
        grid_spec=pltpu.PrefetchScalarGridSpec(
            num_scalar_prefetch=1, grid=(nb,),
            in_specs=[pl.BlockSpec((1, tr, C), lambda i, sc: (sc[0], i, 0)),
                      pl.BlockSpec((N_CHIPS, tr, C), lambda i, sc: (0, i, 0))],
            out_specs=pl.BlockSpec((tr, C), lambda i, sc: (sc[1] * nb + i, 0))),
        out_shape=jax.ShapeDtypeStruct((2 * H, C), F32),
        compiler_params=pltpu.CompilerParams(dimension_semantics=("arbitrary",)),
    )(sc_arr, part, slots)


def _rs_sibling_out(wholes):
    n = len(wholes)

    def body(*refs):
        ins, outs = refs[:n], refs[n:2 * n]
        send_sems, recv_sems = refs[2 * n:]
        x, y, c, _ = _place()
        cps = []
        for a in range(n):
            H = wholes[a].shape[0] // 2
            cps.append(_remote(ins[a].at[pl.ds(c * H, H)], outs[a].at[pl.ds(c * H, H)], send_sems.at[a], recv_sems.at[a],
                               (x, y, 1 - c)))
        for cp in cps:
            cp.start()
        for a in range(n):
            H = wholes[a].shape[0] // 2
            cps[a].wait_send()
            got = outs[a].at[pl.ds((1 - c) * H, H)]
            _remote(got, got, send_sems.at[a], recv_sems.at[a], (x, y, 1 - c)).wait_recv()

    return pl.pallas_call(
        body, name="rs_sibling_out", in_specs=[ANY] * n, out_specs=[ANY] * n,
        out_shape=[jax.ShapeDtypeStruct(w.shape, w.dtype) for w in wholes],
        input_output_aliases={a: a for a in range(n)},
        scratch_shapes=[pltpu.SemaphoreType.DMA((n,)), pltpu.SemaphoreType.DMA((n,))],
    )(*wholes)


def _adamw_math(w_, g_, m_, v_):
    m2 = ADAM_B1 * m_ + (1.0 - ADAM_B1) * g_
    v2 = ADAM_B2 * v_ + (1.0 - ADAM_B2) * (g_ * g_)
    m_hat = m2 / (1.0 - ADAM_B1 ** ADAM_STEP)
    v_hat = v2 / (1.0 - ADAM_B2 ** ADAM_STEP)
    return -ADAM_LR * (m_hat / (jnp.sqrt(v_hat) + ADAM_EPS) + ADAM_WD * w_), m2, v2


def _adamw(name, w, g, m, v):
    return _rows("adamw_" + name, _adamw_math, [w, g, m, v], [], [(w.shape[1], F32)] * 3,
                 tile=_pick(w.shape[0], 256, 8))


def _adamw_small(ws, gs, ms, vs):
    n = len(ws)

    def body(*refs):
        for i in range(n):
            res = _adamw_math(*[refs[j * n + i][...] for j in range(4)])
            for j in range(3):
                refs[(4 + j) * n + i][...] = res[j]

    out = pl.pallas_call(
        body, name="adamw_small",
        out_shape=[jax.ShapeDtypeStruct(w.shape, F32) for _ in range(3) for w in ws],
        compiler_params=pltpu.CompilerParams(vmem_limit_bytes=VMEM_LIMIT),
    )(*ws, *gs, *ms, *vs)
    return out[:n], out[n:2 * n], out[2 * n:]


def kernel(x, c, w_ada, b_ada, w_in, b_in, g_ln_v, b_ln_v, w_spatial, b_spatial, mu_shift, w0, w_decay_up, a0, w_aaa_up, w_gate_up, k_k, k_a, r_k, gn_gain, gn_bias, w_branch_a, w_branch_b, w_out, b_out, ln1_g, ln1_b, w_ff1, b_ff1, w_ff2, b_ff2, ln2_g, ln2_b, loss_target, m_w_ada, m_b_ada, m_w_in, m_b_in, m_g_ln_v, m_b_ln_v, m_w_spatial, m_b_spatial, m_mu_shift, m_w0, m_w_decay_up, m_a0, m_w_aaa_up, m_w_gate_up, m_k_k, m_k_a, m_r_k, m_gn_gain, m_gn_bias, m_w_branch_a, m_w_branch_b, m_w_out, m_b_out, m_ln1_g, m_ln1_b, m_w_ff1, m_b_ff1, m_w_ff2, m_b_ff2, m_ln2_g, m_ln2_b, v_w_ada, v_b_ada, v_w_in, v_b_in, v_g_ln_v, v_b_ln_v, v_w_spatial, v_b_spatial, v_mu_shift, v_w0, v_w_decay_up, v_a0, v_w_aaa_up, v_w_gate_up, v_k_k, v_k_a, v_r_k, v_gn_gain, v_gn_bias, v_w_branch_a, v_w_branch_b, v_w_out, v_b_out, v_ln1_g, v_ln1_b, v_w_ff1, v_b_ff1, v_w_ff2, v_b_ff2, v_ln2_g, v_ln2_b):
    args = dict(locals())
    local_shape = {n: _shard_shape(s, a) for n, s, a in SHARDED}
    local_shape.update(dict(SMALL))
    wts = {n: args[n].reshape(local_shape[n]) for n in WEIGHT_ORDER}
    mom = {n: args["m_" + n].reshape(local_shape[n]) for n in WEIGHT_ORDER}
    var = {n: args["v_" + n].reshape(local_shape[n]) for n in WEIGHT_ORDER}
    big = [n for n, _, _ in BIG]
    late, early = [n for n in big if n in LATE], [n for n in big if n not in LATE]
    chip = 2 * lax.axis_index("x") + lax.axis_index("y")
    c_arr = lax.axis_index("c").astype(jnp.int32).reshape(1)
    sc_arr = jnp.stack([chip, lax.axis_index("c")]).astype(jnp.int32)
    mine = {n: wts[n].astype(MXU_DT) for n in big}

    def whole(n, got):
        _, s, a = next(t for t in BIG if t[0] == n)
        g = lax.dynamic_update_slice(got.reshape((N_CHIPS,) + mine[n].shape), mine[n][None], (chip, 0, 0))
        return g if a == 1 else g.reshape(s)

    def sibling_sums(names, G, small):
        gps = [G[n] for n in names] + ([_pack_small_grads(G)] if small else [])
        names = names + (["small"] if small else [])
        return [_rs_add_own(n, g, r, c_arr) for n, g, r in zip(names, gps, _rs_sibling_in(gps))]

    small_w = _pack_small(wts, None, ROWS_SW, MXU_DT)
    gath = _all_gather([mine[n] for n in early] + [small_w])
    W = {n: wts[n] for n, _ in SMALL}
    W.update({n: whole(n, g) for n, g in zip(early, gath)})
    small_g = lax.dynamic_update_slice(gath[-1].reshape((N_CHIPS,) + small_w.shape), small_w[None], (chip, 0, 0))
    lora_q = [_unpack_small(small_g[q], False) for q in range(N_CHIPS)]
    for n, _, _ in LORAS:
        W[n] = jnp.concatenate([lora_q[q][n] for q in range(N_CHIPS)], axis=1)

    step = _local_step(x[0], c, loss_target[0], W)
    _, scan_in = next(step)
    y_scan, states, got = _scan_fwd(*scan_in, gather=[mine[n] for n in late])
    _, scan_in, G = step.send((y_scan, states, {n: whole(n, g) for n, g in zip(late, got)}))
    parts_late = sibling_sums(late, G, False)
    res = _scan_bwd(*scan_in, scatter=parts_late)
    try:
        step.send(tuple(res[:6]))
    except StopIteration as done:
        loss, grad_x, G = done.value
    loss = lax.psum(loss, MESH_AXES)
    parts_early = sibling_sums(early, G, True)
    names = late + early + ["small"]
    parts, slots = parts_late + parts_early, list(res[6:]) + list(_rs_chips(parts_early))
    segs = _rs_sibling_out([_rs_add_chips(n, p, s, sc_arr) for n, p, s in zip(names, parts, slots)])

    out = {}
    for n, g in zip(names[:-1], segs[:-1]):
        out[n] = (g,) + tuple(_adamw(n, wts[n], g, mom[n], var[n]))
    small = [n for n, _, _ in LORAS] + [n for n, _ in SMALL]
    g_small = _unpack_small(segs[-1], True)
    upd = _adamw_small(*[[t[n] for n in small] for t in (wts, g_small, mom, var)])
    for i, n in enumerate(small):
        out[n] = (g_small[n], upd[0][i], upd[1][i], upd[2][i])
    res = [loss, grad_x[None]]
    for k in range(4):
        res += [out[n][k].reshape(args[n].shape) for n in WEIGHT_ORDER]
    return tuple(res)
```

```python
import functools

import numpy as np
import jax
import jax.numpy as jnp
from jax import lax
from jax.experimental import pallas as pl
from jax.experimental.pallas import tpu as pltpu

F32 = jnp.float32
MXU_DT = jnp.bfloat16
WIRE_DT = jnp.bfloat16

D_MODEL = 1024
G_GROUPS = 8
G_WIDTH = 512
CHUNK = 128
R_WIDTH = 512
R_HEAD = 64
R_HEADS = 8
DECAY_LORA = 32
AAA_LORA = 32
GATE_LORA = 96
LORA = DECAY_LORA + AAA_LORA + GATE_LORA
LORA_PAD = 256
R_COLS = 3 * R_WIDTH + LORA
R_COLS_PAD = 3 * R_WIDTH + LORA_PAD
D_FF = 4 * D_MODEL
ALPHA = 2.0 ** 0.25
LN_EPS = 1e-5
GN_EPS = 64e-5
ADAM_LR = 0.001
ADAM_B1 = 0.9
ADAM_B2 = 0.999
ADAM_EPS = 1e-08
ADAM_WD = 0.01
ADAM_STEP = 10

LANES = 128
PACK_W = 512
PACK_Q = 2 * 16 * PACK_W
VMEM_LIMIT = 48 * 1024 * 1024
SCAN_T = 64

MESH_AXES = ("x", "y", "c")


def _dg(a, b, dims):
    return lax.dot_general(a.astype(MXU_DT), b.astype(MXU_DT), (dims, ((), ())),
                           preferred_element_type=F32)


@jax.custom_vjp
def _bdot(a, b):
    return _dg(a, b, ((1,), (0,)))


def _bdot_fwd(a, b):
    return _bdot(a, b), (a, b)


def _bdot_bwd(res, g):
    a, b = res
    return (_dg(g, b, ((1,), (1,))).astype(a.dtype), _dg(a, g, ((0,), (0,))).astype(b.dtype))


_bdot.defvjp(_bdot_fwd, _bdot_bwd)


def _split_dot(x, m, dims):
    hi = x.astype(jnp.bfloat16)
    lo = (x - hi.astype(F32)).astype(jnp.bfloat16)
    dn = (dims, ((), ()))
    return (lax.dot_general(hi, m, dn, preferred_element_type=F32)
            + lax.dot_general(lo, m, dn, preferred_element_type=F32))


@jax.custom_vjp
def _pdot(x, m):
    return _split_dot(x, m, ((1,), (0,)))


def _pdot_fwd(x, m):
    return _pdot(x, m), m


def _pdot_bwd(m, g):
    return _split_dot(g, m, ((1,), (1,))), None


_pdot.defvjp(_pdot_fwd, _pdot_bwd)


def _sigmoid(x):
    return 1.0 / (1.0 + jnp.exp(-x))


def _softplus(x):
    return jnp.maximum(x, 0.0) + jnp.log(1.0 + jnp.exp(-jnp.maximum(x, -x)))


def _gelu(x):
    return 0.5 * x * (1.0 + jnp.tanh(0.7978845608028654 * (x + 0.044715 * (x * x * x))))


def _ln(x, g, b, eps):
    mu = jnp.mean(x, axis=-1, keepdims=True)
    xc = x - mu
    var = jnp.mean(xc * xc, axis=-1, keepdims=True)
    return xc * lax.rsqrt(var + eps) * g + b


def _colsum(x):
    return jnp.sum(x, axis=0, keepdims=True)


def _pick(n, target, q=LANES):
    if n <= target:
        return n
    best = None
    for t in range(q, target + 1, q):
        if n % t == 0:
            best = t
    assert best is not None, (n, target)
    return best


def _mm(name, a, b, mode, out_dtype=F32, out_split=1, tm=1024, tn=1024, tk=1024, epilogue=None, side=()):
    bs = b.shape[0] if b.ndim == 3 else 1
    br, bc = b.shape[-2:]
    if mode == "nn":
        (M, K), K2, N = a.shape, br, bc * bs
    elif mode == "nt":
        (M, K), N, K2 = a.shape, br, bc * bs
    else:
        assert bs == 1
        (K, M), K2, N = a.shape, br, bc
    assert K == K2, (name, a.shape, b.shape, mode)
    n_piece = N // max(bs if mode == "nn" else 1, out_split)
    k_piece = K // (bs if mode == "nt" else 1)
    tm, tn, tk = _pick(M, tm, 8 if M < LANES else LANES), _pick(n_piece, tn), _pick(k_piece, tk)
    nk, npj, npk = K // tk, n_piece // tn, k_piece // tk
    dims = {"nn": ((1,), (0,)), "nt": ((1,), (1,)), "tn": ((0,), (0,))}[mode]

    ns = len(side)
    multi = isinstance(out_dtype, (tuple, list))
    out_dtypes = tuple(out_dtype) if multi else (out_dtype,)
    assert out_split == 1 or (epilogue is None and not multi)

    def body(a_ref, b_ref, *rest):
        side_refs, o_refs, acc_ref = rest[:ns], rest[ns:-1], rest[-1]
        o_ref = o_refs[0]
        k = pl.program_id(2)

        @pl.when(k == 0)
        def _():
            acc_ref[...] = jnp.zeros(acc_ref.shape, F32)

        acc_ref[...] += _dg(a_ref[...], b_ref[0] if bs > 1 else b_ref[...], dims)

        @pl.when(k == nk - 1)
        def _():
            if out_split > 1:
                o_ref[0] = acc_ref[...].astype(o_ref.dtype)
            elif epilogue is None:
                o_ref[...] = acc_ref[...].astype(o_ref.dtype)
            else:
                vals = epilogue(acc_ref[...], *[r[...] for r in side_refs])
                for ref, val in zip(o_refs, vals if isinstance(vals, (tuple, list)) else (vals,)):
                    ref[...] = val.astype(ref.dtype)

    if mode == "nn":
        a_spec = pl.BlockSpec((tm, tk), lambda i, j, k: (i, k))
        b_spec = (pl.BlockSpec((tk, tn), lambda i, j, k: (k, j)) if bs == 1 else
                  pl.BlockSpec((1, tk, tn), lambda i, j, k: (j // npj, k, j % npj)))
    elif mode == "nt":
        a_spec = pl.BlockSpec((tm, tk), lambda i, j, k: (i, k))
        b_spec = (pl.BlockSpec((tn, tk), lambda i, j, k: (j, k)) if bs == 1 else
                  pl.BlockSpec((1, tn, tk), lambda i, j, k: (k // npk, j, k % npk)))
    else:
        a_spec = pl.BlockSpec((tk, tm), lambda i, j, k: (k, i))
        b_spec = pl.BlockSpec((tk, tn), lambda i, j, k: (k, j))
    if out_split > 1:
        o_spec = pl.BlockSpec((1, tm, tn), lambda i, j, k: (j // npj, i, j % npj))
        o_shape = jax.ShapeDtypeStruct((out_split, M, n_piece), out_dtype)
    else:
        o_spec = [pl.BlockSpec((tm, tn), lambda i, j, k: (i, j)) for _ in out_dtypes]
        o_shape = [jax.ShapeDtypeStruct((M, N), dt) for dt in out_dtypes]
    side_specs = [pl.BlockSpec((1, tn), lambda i, j, k: (0, j)) if t.shape[0] == 1 else
                  pl.BlockSpec((tm, tn), lambda i, j, k: (i, j)) for t in side]
    res = pl.pallas_call(
        body, name=name, grid=(M // tm, N // tn, nk),
        in_specs=[a_spec, b_spec] + side_specs, out_specs=o_spec, out_shape=o_shape,
        scratch_shapes=[pltpu.VMEM((tm, tn), F32)],
        compiler_params=pltpu.CompilerParams(
            dimension_semantics=("parallel", "parallel", "arbitrary"), vmem_limit_bytes=VMEM_LIMIT),
    )(a, b, *side)
    return res if (multi or out_split > 1) else res[0]


def _rows(name, fn, rows, params, outs, accs=(), tile=256):
    S = rows[0].shape[0]
    tile = min(tile, S)
    assert S % tile == 0, (name, S, tile)
    nr, npar, no, na = len(rows), len(params), len(outs), len(accs)

    def body(*refs):
        rin, pin = refs[:nr], refs[nr:nr + npar]
        oref, aref = refs[nr + npar:nr + npar + no], refs[nr + npar + no:]
        res = fn(*[r[...] for r in rin], *[p[...] for p in pin])
        if not isinstance(res, (tuple, list)):
            res = (res,)
        assert len(res) == no + na, (name, len(res), no, na)
        for ref, val in zip(oref, res[:no]):
            ref[...] = val.astype(ref.dtype)
        if na:
            @pl.when(pl.program_id(0) == 0)
            def _():
                for ref in aref:
                    ref[...] = jnp.zeros(ref.shape, ref.dtype)

            for ref, val in zip(aref, res[no:]):
                ref[...] += jnp.broadcast_to(val, ref.shape).astype(ref.dtype)

    def whole(shape):
        nd = len(shape)
        return pl.BlockSpec(tuple(shape), lambda i: (0,) * nd)

    in_specs = ([pl.BlockSpec((tile, r.shape[1]), lambda i: (i, 0)) for r in rows]
                + [whole(p.shape) for p in params])
    out_specs = ([pl.BlockSpec((tile, n), lambda i: (i, 0)) for n, _ in outs]
                 + [whole(s) for s in accs])
    out_shape = ([jax.ShapeDtypeStruct((S, n), dt) for n, dt in outs]
                 + [jax.ShapeDtypeStruct(tuple(s), F32) for s in accs])
    res = pl.pallas_call(
        body, name=name, grid=(S // tile,), in_specs=in_specs, out_specs=out_specs,
        out_shape=out_shape,
        compiler_params=pltpu.CompilerParams(
            dimension_semantics=("arbitrary",), vmem_limit_bytes=VMEM_LIMIT),
    )(*rows, *params)
    return res


def _modulate(x, sc, sh):
    return x * (1.0 + sc) + sh


def _gmlp_consts():
    lane = lax.broadcasted_iota(jnp.int32, (1, G_WIDTH), 1)
    gmask = [(lane // (G_WIDTH // G_GROUPS) == g).astype(F32) for g in range(G_GROUPS)]
    tril = (lax.broadcasted_iota(jnp.int32, (CHUNK, CHUNK), 0)
            >= lax.broadcasted_iota(jnp.int32, (CHUNK, CHUNK), 1))
    return gmask, tril


def _gmlp_core(gmask, tril, gsel, zu, zv, bu, bv, g, b, bst, *ws):
    u = _gelu(zu + bu)
    v = _ln(_gelu(zv + bv), g, b, LN_EPS)
    s = _pdot(bst, gsel)
    for gi in range(G_GROUPS):
        s = s + _bdot(jnp.where(tril, ws[gi], 0.0), v * gmask[gi])
    return u * s


def _pre_core(rowmask, bd, zr, zk, zv, zl, pr, pk, pv, pq, br, bk, bv, bl, mr, mk, mv, ml,
              w0, wd, a0, wa, wg, k_k, k_a):
    def mix(z, p, b, mu):
        zz = z + b
        return zz + ((p + b) * rowmask - zz) * mu

    r, k, v, l = mix(zr, pr, br, mr), mix(zk, pk, bk, mk), mix(zv, pv, bv, mv), mix(zl, pq, bl, ml)
    w_log = -_softplus(-(w0 + _bdot(jnp.tanh(l), wd))) - 0.5
    decay = jnp.exp(-jnp.exp(w_log))
    a = _sigmoid(a0 + _bdot(l, wa))
    g = _bdot(_sigmoid(l), wg)
    kk = k * k_k
    kkn = kk / jnp.maximum(jnp.sqrt(_pdot(kk * kk, bd)), 1e-12)
    k2 = k * (1.0 + (a - 1.0) * k_a)
    return r, decay, k2, v, -kkn, kkn * a, g


def _post_core(bd, y, r, k2, v, g, gain, bias, rk):
    inv = 1.0 / R_HEAD
    mu = _pdot(y, bd) * inv
    yc = y - mu
    var = _pdot(yc * yc, bd) * inv
    yn = yc * lax.rsqrt(var + GN_EPS) * gain + bias
    bonus = _pdot(r * k2 * rk, bd) * v
    return (yn + bonus) * g


def _merge_core(pa, pb, ga, gb, bga, bgb):
    return _sigmoid(ga + bga) * pa + _sigmoid(gb + bgb) * pb


def _ln1_core(x, mix, gt1, bout, g, b, sc2, sh2):
    h1 = _ln(ALPHA * x + gt1 * (mix + bout), g, b, LN_EPS)
    return h1, h1 * (1.0 + sc2) + sh2


def _ln2_loss_core(tgt, h1, ff, gt2, bff2, g, b):
    out = _ln(ALPHA * h1 + gt2 * (ff + bff2), g, b, LN_EPS)
    err = out - tgt
    return 0.5 * jnp.sum(err * err) * (1.0 / D_MODEL)


def _scan_consts():
    sub = lax.broadcasted_iota(jnp.int32, (R_HEAD, LANES), 0)
    lane = lax.broadcasted_iota(jnp.int32, (R_HEAD, LANES), 1)
    return lane < R_HEAD, sub == (lane & (R_HEAD - 1))


def _seg_sum(lo, xb):
    s_lo = jnp.sum(jnp.where(lo, xb, 0.0), axis=1, keepdims=True)
    s_hi = jnp.sum(jnp.where(lo, 0.0, xb), axis=1, keepdims=True)
    return jnp.where(lo, s_lo, s_hi)


def _seg_dot(lo, s, row):
    lo_row = lo[0:1, :]
    s_lo = jnp.sum(s * jnp.where(lo_row, row, 0.0), axis=1, keepdims=True)
    s_hi = jnp.sum(s * jnp.where(lo_row, 0.0, row), axis=1, keepdims=True)
    return jnp.where(lo, s_lo, s_hi)


def _row_of_col(eye, colb):
    return jnp.sum(jnp.where(eye, colb, 0.0), axis=0, keepdims=True)


def _head_ones():
    i = lax.broadcasted_iota(jnp.int32, (LANES, LANES), 0) // R_HEAD
    j = lax.broadcasted_iota(jnp.int32, (LANES, LANES), 1) // R_HEAD
    return (i == j).astype(jnp.bfloat16)


N_SPLIT = 1


def _split(x):
    parts, r = [], x
    for u in range(N_SPLIT):
        p = r.astype(jnp.bfloat16)
        parts.append(p)
        if u + 1 < N_SPLIT:
            r = r - p.astype(F32)
    return parts


def _ones_dot(parts, ones, n):
    res = lax.dot_general(jnp.concatenate(parts, axis=0), ones, (((1,), (0,)), ((), ())),
                          preferred_element_type=F32)
    out = []
    for m in range(n):
        t = [res[(N_SPLIT * m + u) * R_HEAD:(N_SPLIT * m + u + 1) * R_HEAD] for u in range(N_SPLIT)]
        out.append(functools.reduce(lambda p, q: p + q, t))
    return out


def _seg_sums_mxu(ones, mats):
    return _ones_dot([p for m in mats for p in _split(m)], ones, len(mats))


def _cols_of_rows_mxu(eye, ones, rows8):
    terms = [p.astype(F32) for p in _split(rows8)]
    parts = [jnp.where(eye, jnp.broadcast_to(t[i:i + 1, :], eye.shape), 0.0).astype(jnp.bfloat16)
             for i in range(rows8.shape[0]) for t in terms]
    return _ones_dot(parts, ones, rows8.shape[0])


N_BLK = R_WIDTH // LANES
ROW_GROUP = 8


def _scan_fwd(r, w, k, v, a, b, gather=()):
    S = r.shape[0]
    T = min(SCAN_T, S)
    nchunk = S // T
    ng = len(gather)

    def body(*refs):
        r_ref, w_ref, k_ref, v_ref, a_ref, b_ref = refs[:6]
        g_in, (y_ref, sv_ref), g_out = refs[6:6 + ng], refs[6 + ng:8 + ng], refs[8 + ng:8 + 2 * ng]
        st_ref, sems = refs[8 + 2 * ng], refs[9 + 2 * ng:]
        lo, eye = _scan_consts()
        ones = _head_ones()

        @pl.when(pl.program_id(0) == 0)
        def _():
            st_ref[...] = jnp.zeros(st_ref.shape, F32)
            if ng:
                _ag_start(g_in, g_out, *sems)

        sub8 = lax.broadcasted_iota(jnp.int32, (ROW_GROUP, LANES), 0)

        def group(gi, state):
            base = pl.multiple_of(gi * ROW_GROUP, ROW_GROUP)
            state = list(state)
            sls = [slice(q * LANES, (q + 1) * LANES) for q in range(N_BLK)]
            ld = lambda ref: [ref[pl.ds(base, ROW_GROUP), sl] for sl in sls]
            r8, w8, k8, v8, a8, b8 = ld(r_ref), ld(w_ref), ld(k_ref), ld(v_ref), ld(a_ref), ld(b_ref)
            vb = [_cols_of_rows_mxu(eye, ones, v8[q]) for q in range(N_BLK)]
            an = [pltpu.roll(a8[q], ROW_GROUP - 1, 0) for q in range(N_BLK)]
            ap = [w8[q] * an[q] for q in range(N_BLK)]
            beta = [_seg_sum(lo[:ROW_GROUP], b8[q] * an[q]) for q in range(N_BLK)]
            kappa = [_seg_sum(lo[:ROW_GROUP], k8[q] * an[q]) for q in range(N_BLK)]
            y8 = [jnp.zeros((ROW_GROUP, LANES), F32)] * N_BLK

            def emit_y(i, y8_):
                ycol = _seg_sums_mxu(ones, [state[q] * r8[q][i:i + 1, :] for q in range(N_BLK)])
                return [jnp.where(sub8 == i, _row_of_col(eye, ycol[q]), y8_[q]) for q in range(N_BLK)]

            for i in range(0, ROW_GROUP, 2):
                row = lambda t8, d=0: t8[i + d:i + d + 1, :]
                sa1 = [None] * N_BLK
                for q in range(N_BLK):
                    s = state[q]
                    sv_ref[base + i, :, sls[q]] = s
                    sa0 = _seg_dot(lo, s, row(a8[q]))
                    nxt = _seg_dot(lo, s, row(ap[q]))
                    sa1[q] = nxt + row(beta[q]) * sa0 + row(kappa[q]) * vb[q][i]
                    state[q] = s * row(w8[q]) + sa0 * row(b8[q]) + vb[q][i] * row(k8[q])
                y8 = emit_y(i, y8)
                for q in range(N_BLK):
                    s = state[q]
                    sv_ref[base + i + 1, :, sls[q]] = s
                    state[q] = s * row(w8[q], 1) + sa1[q] * row(b8[q], 1) + vb[q][i + 1] * row(k8[q], 1)
                y8 = emit_y(i + 1, y8)
            for q in range(N_BLK):
                y_ref[pl.ds(base, ROW_GROUP), sls[q]] = y8[q]
            return tuple(state)

        init = tuple(st_ref[:, q * LANES:(q + 1) * LANES] for q in range(N_BLK))
        fin = lax.fori_loop(0, T // ROW_GROUP, group, init)
        for q in range(N_BLK):
            st_ref[:, q * LANES:(q + 1) * LANES] = fin[q]

        if ng:
            @pl.when(pl.program_id(0) == nchunk - 1)
            def _():
                _ag_finish(g_in, g_out, *sems)

    blk = pl.BlockSpec((T, R_WIDTH), lambda i: (i, 0))
    res = pl.pallas_call(
        body, name="scan_fwd", grid=(nchunk,), in_specs=[blk] * 6 + [ANY] * ng,
        out_specs=[blk, pl.BlockSpec((T, R_HEAD, R_WIDTH), lambda i: (i, 0, 0))] + [ANY] * ng,
        out_shape=[jax.ShapeDtypeStruct((S, R_WIDTH), F32),
                   jax.ShapeDtypeStruct((S, R_HEAD, R_WIDTH), F32)] + _ag_out_shapes(gather),
        scratch_shapes=[pltpu.VMEM((R_HEAD, R_WIDTH), F32)] + (_ag_sems(ng) if ng else []),
        compiler_params=pltpu.CompilerParams(
            dimension_semantics=("arbitrary",), vmem_limit_bytes=VMEM_LIMIT),
    )(r, w, k, v, a, b, *gather)
    return res[0], res[1], list(res[2:])


def _scan_bwd(r, w, k, v, a, b, states, dy, scatter=()):
    S = r.shape[0]
    T = min(SCAN_T, S)
    nchunk = S // T
    ns = len(scatter)

    def body(*refs):
        r_ref, w_ref, k_ref, v_ref, a_ref, b_ref, sv_ref, dy_ref = refs[:8]
        x_in, x_out = refs[8:8 + ns], refs[14 + ns:14 + 2 * ns]
        dr_ref, dw_ref, dk_ref, dv_ref, da_ref, db_ref = refs[8 + ns:14 + ns]
        ds_ref, sems = refs[14 + 2 * ns], refs[15 + 2 * ns:]
        lo, eye = _scan_consts()
        ones = _head_ones()

        @pl.when(pl.program_id(0) == 0)
        def _():
            ds_ref[...] = jnp.zeros(ds_ref.shape, F32)
            if ns:
                _rsc_start(x_in, x_out, *sems)

        sub8 = lax.broadcasted_iota(jnp.int32, (ROW_GROUP, LANES), 0)

        def bgroup(n, dstate, chunk_end):
            base = (T // ROW_GROUP - 1 - n) * ROW_GROUP
            if not isinstance(n, int):
                base = pl.multiple_of(base, ROW_GROUP)
            dstate = list(dstate)
            sls = [slice(q * LANES, (q + 1) * LANES) for q in range(N_BLK)]
            ld = lambda ref: [ref[pl.ds(base, ROW_GROUP), sl] for sl in sls]
            r8, w8, k8, v8, a8, b8, dy8 = (ld(r_ref), ld(w_ref), ld(k_ref), ld(v_ref), ld(a_ref), ld(b_ref),
                                           ld(dy_ref))
            vbs = [_cols_of_rows_mxu(eye, ones, v8[q]) for q in range(N_BLK)]
            dycs = [_cols_of_rows_mxu(eye, ones, dy8[q]) for q in range(N_BLK)]
            acc = [{n_: jnp.zeros((ROW_GROUP, LANES), F32) for n_ in ("r", "w", "k", "v", "a", "b")}
                   for _ in range(N_BLK)]
            for i in reversed(range(ROW_GROUP)):
                row = lambda t8: t8[i:i + 1, :]
                put = lambda q_, n_, val: acc[q_].__setitem__(n_, jnp.where(sub8 == i, val, acc[q_][n_]))
                dks = []
                for q in range(N_BLK):
                    sp, vb = sv_ref[base + i, :, sls[q]], vbs[q][i]
                    wr, ar, br, kr, rr = row(w8[q]), row(a8[q]), row(b8[q]), row(k8[q]), row(r8[q])
                    sa = _seg_dot(lo, sp, ar)
                    if chunk_end and i == ROW_GROUP - 1:
                        st = sp * wr + sa * br + vb * kr
                    else:
                        st = sv_ref[base + i + 1, :, sls[q]]
                    dyc = dycs[q][i]
                    ds = dstate[q] + dyc * rr
                    put(q, "r", _colsum(st * dyc))
                    put(q, "w", _colsum(ds * sp))
                    put(q, "b", _colsum(ds * sa))
                    put(q, "k", _colsum(ds * vb))
                    dsa = _seg_dot(lo, ds, br)
                    dks.append(ds * kr)
                    put(q, "a", _colsum(sp * dsa))
                    dstate[q] = ds * wr + dsa * ar
                dvc = _seg_sums_mxu(ones, dks)
                for q in range(N_BLK):
                    put(q, "v", _row_of_col(eye, dvc[q]))
            for q in range(N_BLK):
                for n_, ref in (("r", dr_ref), ("w", dw_ref), ("k", dk_ref), ("v", dv_ref), ("a", da_ref), ("b", db_ref)):
                    ref[pl.ds(base, ROW_GROUP), sls[q]] = acc[q][n_]
            return tuple(dstate)

        fin = bgroup(0, tuple(ds_ref[:, q * LANES:(q + 1) * LANES] for q in range(N_BLK)), True)
        fin = lax.fori_loop(1, T // ROW_GROUP, lambda n, d: bgroup(n, d, False), fin)
        for q in range(N_BLK):
            ds_ref[:, q * LANES:(q + 1) * LANES] = fin[q]

        if ns:
            @pl.when(pl.program_id(0) == nchunk - 1)
            def _():
                _rsc_finish(x_in, x_out, *sems)

    blk = pl.BlockSpec((T, R_WIDTH), lambda i: (nchunk - 1 - i, 0))
    svb = pl.BlockSpec((T, R_HEAD, R_WIDTH), lambda i: (nchunk - 1 - i, 0, 0))
    return pl.pallas_call(
        body, name="scan_bwd", grid=(nchunk,), in_specs=[blk] * 6 + [svb, blk] + [ANY] * ns,
        out_specs=[blk] * 6 + [ANY] * ns,
        out_shape=[jax.ShapeDtypeStruct((S, R_WIDTH), F32)] * 6 + [jax.ShapeDtypeStruct(p.shape, p.dtype) for p in scatter],
        scratch_shapes=[pltpu.VMEM((R_HEAD, R_WIDTH), F32)] + (_rsc_sems(ns) if ns else []),
        compiler_params=pltpu.CompilerParams(
            dimension_semantics=("arbitrary",), vmem_limit_bytes=VMEM_LIMIT),
    )(r, w, k, v, a, b, states, dy, *scatter)


def _pad_in_cols(t, axis):
    cut = 2 * G_WIDTH + R_COLS
    lo, hi = lax.slice_in_dim(t, 0, cut, axis=axis), lax.slice_in_dim(t, cut, t.shape[axis], axis=axis)
    zshape = list(t.shape)
    zshape[axis] = LORA_PAD - LORA
    return jnp.concatenate([lo, jnp.zeros(zshape, t.dtype), hi], axis=axis)


def _unpad_in_cols(t, axis):
    cut = 2 * G_WIDTH + R_COLS
    return jnp.concatenate([lax.slice_in_dim(t, 0, cut, axis=axis),
                            lax.slice_in_dim(t, cut + LORA_PAD - LORA, t.shape[axis], axis=axis)], axis=axis)


def _pad_rows(t, lo, n):
    return jnp.zeros((n, t.shape[1]), t.dtype).at[lo:lo + t.shape[0]].set(t)


def _join_cols(w3):
    p, k, n = w3.shape
    return jnp.transpose(w3, (1, 0, 2)).reshape(k, p * n)


def _split_cols(w):
    k, n = w.shape
    return jnp.transpose(w.reshape(k, N_CHIPS, n // N_CHIPS), (1, 0, 2))


N_DEV = 8
BF16_ROWS = 16


def _ada_grads(c_all, dmod_all, chip):
    n = dmod_all.shape[1] // N_CHIPS
    pad = lambda t: jnp.concatenate([t, jnp.zeros((BF16_ROWS - N_DEV, t.shape[1]), t.dtype)], axis=0)
    (ca,) = _rows("ada_silu_all", lambda cc: cc * _sigmoid(cc), [pad(c_all)], [], [(D_MODEL, MXU_DT)], tile=BF16_ROWS)
    mine = lax.dynamic_slice(dmod_all, (0, chip * n), (N_DEV, n))
    g_w = _mm("ada_dw", ca, pad(mine), "tn")
    (g_b,) = _rows("ada_db", _colsum, [dmod_all], [], [], accs=[(1, dmod_all.shape[1])], tile=N_DEV)
    return g_w, g_b


def _shift_down(t):
    return jnp.concatenate([jnp.zeros((1, t.shape[1]), t.dtype), t[:-1]], axis=0)


def _shift_up(t):
    return jnp.concatenate([t[1:], jnp.zeros((1, t.shape[1]), t.dtype)], axis=0)


LATE = ("w_branch_a", "w_branch_b", "w_out", "w_ff1", "w_ff2")


def _local_step(x, c, tgt, W):
    S = x.shape[0]
    bf = MXU_DT
    G = {}

    hl = np.arange(R_WIDTH) // R_HEAD
    bd = jnp.asarray(hl[:, None] == hl[None, :], jnp.bfloat16)
    gsel = jnp.asarray(np.arange(LANES)[:, None] == (np.arange(G_WIDTH) // (G_WIDTH // G_GROUPS))[None, :],
                       jnp.bfloat16)
    w_in_p = _pad_in_cols(_join_cols(W["w_in"]), 1)
    b_in_p = _pad_in_cols(W["b_in"], 1)
    c_g, c_r = 2 * G_WIDTH, 2 * G_WIDTH + R_COLS_PAD
    w_g, w_r, w_gate = w_in_p[:, :c_g], w_in_p[:, c_g:c_r], w_in_p[:, c_r:]
    b_g, b_r, b_gate = b_in_p[:, :c_g], b_in_p[:, c_g:c_r], b_in_p[:, c_r:]
    mu_p = jnp.concatenate([W["mu_shift"], jnp.zeros((1, LORA_PAD - LORA), F32)], axis=1)
    wd_p = _pad_rows(W["w_decay_up"].astype(F32), 0, LORA_PAD)
    wa_p = _pad_rows(W["w_aaa_up"].astype(F32), DECAY_LORA, LORA_PAD)
    wg_p = _pad_rows(W["w_gate_up"].astype(F32), DECAY_LORA + AAA_LORA, LORA_PAD)
    ws2 = W["w_spatial"].reshape(G_GROUPS * CHUNK, CHUNK)
    bst = jnp.zeros((CHUNK, LANES), F32).at[:, :G_GROUPS].set(W["b_spatial"].T)
    rk = W["r_k"].reshape(1, R_WIDTH)

    c8 = jnp.broadcast_to(c, (8, D_MODEL))
    (ca8,) = _rows("ada_silu", lambda cc: cc * _sigmoid(cc), [c8], [], [(D_MODEL, bf)], tile=8)
    mod_raw = _mm("ada_mm", ca8, W["w_ada"], "nn")
    (mod8,) = _rows("ada_bias", lambda m, bb: m + bb, [mod_raw], [W["b_ada"]], [(6 * D_MODEL, F32)], tile=8)
    sh1, sc1, gt1, sh2, sc2, gt2 = [mod8[0:1, i * D_MODEL:(i + 1) * D_MODEL] for i in range(6)]

    (h,) = _rows("mod1", _modulate, [x], [sc1, sh1], [(D_MODEL, bf)])
    proj_g = _mm("proj_g", h, w_g, "nn")
    proj_r = _mm("proj_r", h, w_r, "nn")
    proj_gate = _mm("proj_gate", h, w_gate, "nn")

    def split2(t):
        return t[:, :G_WIDTH], t[:, G_WIDTH:]

    def gmlp_fwd(z, bz, g, b, bst_, gsel_, ws_):
        gmask, tril = _gmlp_consts()
        (zu, zv), (bu, bv) = split2(z), split2(bz)
        wsl = [ws_[i * CHUNK:(i + 1) * CHUNK] for i in range(G_GROUPS)]
        return _gmlp_core(gmask, tril, gsel_, zu, zv, bu, bv, g, b, bst_, *wsl)

    (y_a,) = _rows("gmlp_fwd", gmlp_fwd, [proj_g], [b_g, W["g_ln_v"], W["b_ln_v"], bst, gsel, ws2],
                   [(G_WIDTH, bf)], tile=CHUNK)

    r_cuts = (0, R_WIDTH, 2 * R_WIDTH, 3 * R_WIDTH, R_COLS_PAD)

    def split4(t):
        return [t[:, r_cuts[i]:r_cuts[i + 1]] for i in range(4)]

    tile_pre = min(256, S)

    def rowmask_of():
        grow = pl.program_id(0) * tile_pre + lax.broadcasted_iota(jnp.int32, (tile_pre, 1), 0)
        return (grow > 0).astype(F32)

    pre_params = [b_r, mu_p, W["w0"], wd_p, W["a0"], wa_p, wg_p, W["k_k"], W["k_a"], bd]

    def pre_fwd(z, p, bz, mu, w0, wd, a0, wa, wg, k_k, k_a, bd_):
        return _pre_core(rowmask_of(), bd_, *split4(z), *split4(p), *split4(bz), *split4(mu),
                         w0, wd, a0, wa, wg, k_k, k_a)

    proj_r_prev = _shift_down(proj_r)
    s_r, s_w, s_k, s_v, s_a, s_b, s_g = _rows(
        "rwkv_pre_fwd", pre_fwd, [proj_r, proj_r_prev], pre_params, [(R_WIDTH, F32)] * 7, tile=tile_pre)
    y_scan, states, late = yield "fwd", (s_r, s_w, s_k, s_v, s_a, s_b)
    W = {**W, **late}

    def post_fwd(y, r, k2, v, g, gain, bias, rk_, bd_):
        return _post_core(bd_, y, r, k2, v, g, gain, bias, rk_)

    post_params = [W["gn_gain"], W["gn_bias"], rk, bd]
    (y_b,) = _rows("rwkv_post_fwd", post_fwd, [y_scan, s_r, s_k, s_v, s_g], post_params, [(R_WIDTH, bf)])
    p_a = _mm("branch_a", y_a, W["w_branch_a"], "nn")
    p_b = _mm("branch_b", y_b, W["w_branch_b"], "nn")

    def merge_fwd(pa, pb, gz, bgz):
        return _merge_core(pa, pb, gz[:, :D_MODEL], gz[:, D_MODEL:], bgz[:, :D_MODEL], bgz[:, D_MODEL:])

    (merged,) = _rows("merge_fwd", merge_fwd, [p_a, p_b, proj_gate], [b_gate], [(D_MODEL, bf)])
    mix = _mm("out_proj", merged, W["w_out"], "nn")
    ln1_params = [gt1, W["b_out"], W["ln1_g"], W["ln1_b"], sc2, sh2]
    h1, h2 = _rows("ln1_fwd", _ln1_core, [x, mix], ln1_params, [(D_MODEL, F32), (D_MODEL, bf)])

    def relu2(acc, bb):
        ra = jnp.maximum(acc + bb, 0.0)
        return ra * ra, ra

    act, ra = _mm("ff1", h2, W["w_ff1"], "nn", (bf, bf), epilogue=relu2, side=[W["b_ff1"]])
    ff = _mm("ff2", act, W["w_ff2"], "nn")

    def ln2_loss(h1_, ff_, tg, gt2_, bff2, g, b):
        loss, vjp = jax.vjp(functools.partial(_ln2_loss_core, tg), h1_, ff_, gt2_, bff2, g, b)
        return vjp(jnp.ones((), F32)) + (loss,)

    ln2_params = [gt2, W["b_ff2"], W["ln2_g"], W["ln2_b"]]
    dh1, dff, dgt2, G["b_ff2"], G["ln2_g"], G["ln2_b"], loss_acc = _rows(
        "ln2_loss", ln2_loss, [h1, ff, tgt], ln2_params, [(D_MODEL, F32), (D_MODEL, bf)],
        accs=[(1, D_MODEL)] * 4 + [(1, LANES)])
    loss = loss_acc[0, 0]

    da1 = _mm("ff2_dx", dff, W["w_ff2"], "nt", bf, epilogue=lambda acc, r_: acc * (2.0 * r_.astype(F32)), side=[ra])
    G["w_ff2"] = _mm("ff2_dw", act, dff, "tn", WIRE_DT).reshape(N_CHIPS, D_FF // N_CHIPS, D_MODEL)
    (G["b_ff1"],) = _rows("ff_db", lambda d: _colsum(d.astype(F32)), [da1], [], [], accs=[(1, D_FF)], tile=512)
    dh2 = _mm("ff1_dx", da1, W["w_ff1"], "nt")
    G["w_ff1"] = _mm("ff1_dw", h2, da1, "tn", WIRE_DT, out_split=N_CHIPS)

    def ln1_bwd(x_, mix_, dh1_, dh2_, *ps):
        _, vjp = jax.vjp(_ln1_core, x_, mix_, *ps)
        return vjp((dh1_, dh2_))

    dx_res, dmix, dgt1, G["b_out"], G["ln1_g"], G["ln1_b"], dsc2, dsh2 = _rows(
        "ln1_bwd", ln1_bwd, [x, mix, dh1, dh2], ln1_params, [(D_MODEL, F32), (D_MODEL, bf)],
        accs=[(1, D_MODEL)] * 6)

    dmerged = _mm("out_proj_dx", dmix, W["w_out"], "nt")
    G["w_out"] = _mm("out_proj_dw", merged, dmix, "tn", WIRE_DT).reshape(N_CHIPS, D_MODEL // N_CHIPS, D_MODEL)

    def merge_bwd(pa, pb, gz, dm, bgz):
        args = (pa.astype(F32), pb.astype(F32), gz[:, :D_MODEL], gz[:, D_MODEL:], bgz[:, :D_MODEL], bgz[:, D_MODEL:])
        _, vjp = jax.vjp(_merge_core, *args)
        dpa, dpb, dga, dgb, dbga, dbgb = vjp(dm)
        return dpa, dpb, jnp.concatenate([dga, dgb], axis=1), jnp.concatenate([dbga, dbgb], axis=1)

    dp_a, dp_b, dgates, db_gate = _rows(
        "merge_bwd", merge_bwd, [p_a, p_b, proj_gate, dmerged], [b_gate],
        [(D_MODEL, bf), (D_MODEL, bf), (2 * D_MODEL, bf)], accs=[(1, 2 * D_MODEL)])
    dy_a = _mm("branch_a_dx", dp_a, W["w_branch_a"], "nt")
    G["w_branch_a"] = _mm("branch_a_dw", y_a, dp_a, "tn", WIRE_DT, out_split=N_CHIPS)
    dy_b = _mm("branch_b_dx", dp_b, W["w_branch_b"], "nt")
    G["w_branch_b"] = _mm("branch_b_dw", y_b, dp_b, "tn", WIRE_DT, out_split=N_CHIPS)

    def post_bwd(y, r, k2, v, g, dyb, gain, bias, rk_, bd_):
        _, vjp = jax.vjp(functools.partial(_post_core, bd_), y, r, k2, v, g, gain, bias, rk_)
        return vjp(dyb)

    dy_scan, dr_p, dk_p, dv_p, dg_p, G["gn_gain"], G["gn_bias"], drk = _rows(
        "rwkv_post_bwd", post_bwd, [y_scan, s_r, s_k, s_v, s_g, dy_b], post_params,
        [(R_WIDTH, F32)] * 5, accs=[(1, R_WIDTH)] * 3)
    G["r_k"] = drk.reshape(R_HEADS, R_HEAD)
    dr_s, dw_s, dk_s, dv_s, da_s, db_s = yield "bwd", (s_r, s_w, s_k, s_v, s_a, s_b, states, dy_scan), G

    def pre_bwd(z, p, dr1, dr2, dw, dk1, dk2, dv1, dv2, da, db, dg,
                bz, mu, w0, wd, a0, wa, wg, k_k, k_a, bd_):
        prim = (*split4(z), *split4(p), *split4(bz), *split4(mu), w0, wd, a0, wa, wg, k_k, k_a)
        _, vjp = jax.vjp(functools.partial(_pre_core, rowmask_of(), bd_), *prim)
        d = vjp((dr1 + dr2, dw, dk1 + dk2, dv1 + dv2, da, db, dg))
        cat = lambda parts: jnp.concatenate(parts, axis=1)
        return (cat(d[0:4]), cat(d[4:8]), cat(d[8:12]), cat(d[12:16])) + tuple(d[16:])

    dz_r, dprev, db_r, dmu_p, G["w0"], dwd_p, G["a0"], dwa_p, dwg_p, G["k_k"], G["k_a"] = _rows(
        "rwkv_pre_bwd", pre_bwd,
        [proj_r, proj_r_prev, dr_s, dr_p, dw_s, dk_s, dk_p, dv_s, dv_p, da_s, db_s, dg_p],
        pre_params, [(R_COLS_PAD, bf)] * 2,
        accs=[(1, R_COLS_PAD), (1, R_COLS_PAD), (1, R_WIDTH), (LORA_PAD, R_WIDTH), (1, R_WIDTH),
              (LORA_PAD, R_WIDTH), (LORA_PAD, R_WIDTH), (1, R_WIDTH), (1, R_WIDTH)],
        tile=tile_pre)
    G["mu_shift"] = dmu_p[:, :R_COLS]
    G["w_decay_up"] = dwd_p[:DECAY_LORA]
    G["w_aaa_up"] = dwa_p[DECAY_LORA:DECAY_LORA + AAA_LORA]
    G["w_gate_up"] = dwg_p[DECAY_LORA + AAA_LORA:LORA]

    def gmlp_bwd(z, dya, bz, g, b, bst_, gsel_, ws_):
        gmask, tril = _gmlp_consts()
        (zu, zv), (bu, bv) = split2(z), split2(bz)
        wsl = [ws_[i * CHUNK:(i + 1) * CHUNK] for i in range(G_GROUPS)]
        _, vjp = jax.vjp(functools.partial(_gmlp_core, gmask, tril, gsel_), zu, zv, bu, bv, g, b, bst_, *wsl)
        d = vjp(dya)
        return (jnp.concatenate(d[0:2], axis=1), jnp.concatenate(d[2:4], axis=1), d[4], d[5], d[6],
                jnp.concatenate(d[7:], axis=0))

    dz_g, db_g, G["g_ln_v"], G["b_ln_v"], dbst, dws2 = _rows(
        "gmlp_bwd", gmlp_bwd, [proj_g, dy_a], [b_g, W["g_ln_v"], W["b_ln_v"], bst, gsel, ws2],
        [(2 * G_WIDTH, bf)],
        accs=[(1, 2 * G_WIDTH), (1, G_WIDTH), (1, G_WIDTH), (CHUNK, LANES), (G_GROUPS * CHUNK, CHUNK)],
        tile=CHUNK)
    G["w_spatial"] = dws2.reshape(G_GROUPS, CHUNK, CHUNK)
    G["b_spatial"] = dbst[:, :G_GROUPS].T

    def dproj_cat(dzg, dzr, dpv, dgz):
        return jnp.concatenate([dzg, (dzr.astype(F32) + dpv.astype(F32)).astype(dzg.dtype), dgz], axis=1)

    (dproj,) = _rows("dproj_cat", dproj_cat, [dz_g, dz_r, _shift_up(dprev), dgates], [],
                     [(2 * G_WIDTH + R_COLS_PAD + 2 * D_MODEL, bf)])
    dh = _mm("proj_dx", dproj, w_in_p, "nt", tk=2432)
    G["w_in"] = _split_cols(_unpad_in_cols(_mm("proj_dw", h, dproj, "tn", WIRE_DT, tm=512, tn=2432), 1))
    G["b_in"] = _unpad_in_cols(jnp.concatenate([db_g, db_r, db_gate], axis=1), 1)

    def mod1_bwd(x_, dh_, dxr, sc):
        return dh_ * (1.0 + sc) + dxr, _colsum(dh_ * x_), _colsum(dh_)

    grad_x, dsc1, dsh1 = _rows("mod1_bwd", mod1_bwd, [x, dh, dx_res], [sc1], [(D_MODEL, F32)],
                               accs=[(1, D_MODEL)] * 2)

    dmod = jnp.concatenate([dsh1, dsc1, dgt1, dsh2, dsc2, dgt2], axis=1)
    return loss, grad_x, G, dmod


BIG = (("w_ada", (D_MODEL, 6 * D_MODEL), 1), ("w_in", (D_MODEL, 2 * G_WIDTH + R_COLS + 2 * D_MODEL), 1),
       ("w_branch_a", (G_WIDTH, D_MODEL), 1), ("w_branch_b", (R_WIDTH, D_MODEL), 1),
       ("w_out", (D_MODEL, D_MODEL), 0), ("w_ff1", (D_MODEL, D_FF), 1), ("w_ff2", (D_FF, D_MODEL), 0))
LORAS = (("w_decay_up", (DECAY_LORA, R_WIDTH), 1), ("w_aaa_up", (AAA_LORA, R_WIDTH), 1),
         ("w_gate_up", (GATE_LORA, R_WIDTH), 1))
SHARDED = BIG + LORAS
SMALL = (("b_ada", (1, 6 * D_MODEL)), ("b_in", (1, 2 * G_WIDTH + R_COLS + 2 * D_MODEL)),
         ("g_ln_v", (1, G_WIDTH)), ("b_ln_v", (1, G_WIDTH)), ("w_spatial", (G_GROUPS, CHUNK, CHUNK)),
         ("b_spatial", (G_GROUPS, CHUNK)), ("mu_shift", (1, R_COLS)), ("w0", (1, R_WIDTH)),
         ("a0", (1, R_WIDTH)), ("k_k", (1, R_WIDTH)), ("k_a", (1, R_WIDTH)), ("r_k", (R_HEADS, R_HEAD)),
         ("gn_gain", (1, R_WIDTH)), ("gn_bias", (1, R_WIDTH)), ("b_out", (1, D_MODEL)),
         ("ln1_g", (1, D_MODEL)), ("ln1_b", (1, D_MODEL)), ("b_ff1", (1, D_FF)), ("b_ff2", (1, D_MODEL)),
         ("ln2_g", (1, D_MODEL)), ("ln2_b", (1, D_MODEL)))
WEIGHT_ORDER = ("w_ada", "b_ada", "w_in", "b_in", "g_ln_v", "b_ln_v", "w_spatial", "b_spatial", "mu_shift",
                "w0", "w_decay_up", "a0", "w_aaa_up", "w_gate_up", "k_k", "k_a", "r_k", "gn_gain", "gn_bias",
                "w_branch_a", "w_branch_b", "w_out", "b_out", "ln1_g", "ln1_b", "w_ff1", "b_ff1", "w_ff2",
                "b_ff2", "ln2_g", "ln2_b")
N_CHIPS = 4


def _shard_shape(shape, axis):
    s = list(shape)
    s[axis] //= N_CHIPS
    return tuple(s)


def _numel(shape):
    return int(np.prod(shape))


def _round_up(n, q):
    return -(-n // q) * q


PIECE_Q = 1


def _piece(shape):
    return _round_up(_numel(shape), PIECE_Q)


N_LORA = sum(_piece(_shard_shape(s, a)) for _, s, a in LORAS)
PACKED = tuple(t for t in SMALL if t[0] != "b_ada")
N_SMALL = sum(_piece(s) for _, s in PACKED)
ROWS_SW = _round_up(N_LORA, PACK_Q) // PACK_W
ROWS_SG = _round_up(N_LORA + N_SMALL, PACK_Q) // PACK_W


def _flat_pieces(parts, dtype, total=None):
    out, n = [], 0
    for p in parts:
        out.append(p.reshape(-1).astype(dtype))
        pad = _piece(p.shape) - out[-1].shape[0]
        if pad:
            out.append(jnp.zeros((pad,), dtype))
        n += _piece(p.shape)
    if total is not None and total > n:
        out.append(jnp.zeros((total - n,), dtype))
    return jnp.concatenate(out)


def _pack_small(loras, small, rows, dtype):
    parts = [loras[n] for n, _, _ in LORAS] + ([small[n] for n, _ in PACKED] if small is not None else [])
    return _flat_pieces(parts, dtype, rows * PACK_W).reshape(rows, PACK_W)


def _unpack_small(pack, with_small):
    flat = pack.reshape(-1)
    out, off = {}, 0
    for n, s, a in LORAS:
        ss = _shard_shape(s, a)
        out[n] = flat[off:off + _numel(ss)].reshape(ss)
        off += _piece(ss)
    if with_small:
        for n, s in PACKED:
            out[n] = flat[off:off + _numel(s)].reshape(s)
            off += _piece(s)
    return out


def _pack_small_grads(G):
    common = _flat_pieces([G[n] for n, _ in PACKED], F32, ROWS_SG * PACK_W - N_LORA)
    segs = []
    for q in range(N_CHIPS):
        loras = [G[n][:, q * (s[1] // N_CHIPS):(q + 1) * (s[1] // N_CHIPS)] for n, s, _ in LORAS]
        segs.append(jnp.concatenate([_flat_pieces(loras, F32), common]).reshape(ROWS_SG, PACK_W))
    return jnp.stack(segs)


ANY = pl.BlockSpec(memory_space=pl.ANY)
MESH = pl.DeviceIdType.MESH


def _place():
    x, y, c = lax.axis_index("x"), lax.axis_index("y"), lax.axis_index("c")
    chips = [(1 - x, y), (x, 1 - y), (1 - x, 1 - y)]
    return x, y, c, chips


def _remote(src, dst, send_sem, recv_sem, to):
    return pltpu.make_async_remote_copy(src_ref=src, dst_ref=dst, send_sem=send_sem, recv_sem=recv_sem,
                                        device_id=to, device_id_type=MESH)


def _gather_rows(row):
    n = row.shape[1]
    VM = pl.BlockSpec(memory_space=pltpu.VMEM)

    def body(r_ref, o_ref, send_sems, recv_sems, own_sem):
        x, y, c = lax.axis_index("x"), lax.axis_index("y"), lax.axis_index("c")
        flip = lambda v, bit: 1 - v if bit else v
        me = 4 * x + 2 * y + c
        own = pltpu.make_async_copy(r_ref, o_ref.at[me], own_sem)
        own.start()
        peers = [(flip(x, k >> 2 & 1), flip(y, k >> 1 & 1), flip(c, k & 1)) for k in range(1, N_DEV)]
        cps = [_remote(r_ref, o_ref.at[me], send_sems.at[k], recv_sems.at[k], peer) for k, peer in enumerate(peers)]
        for cp in cps:
            cp.start()
        for k, (px, py, pc) in enumerate(peers):
            cps[k].wait_send()
            got = o_ref.at[4 * px + 2 * py + pc]
            _remote(got, got, send_sems.at[k], recv_sems.at[k], (px, py, pc)).wait_recv()
        own.wait()

    return pl.pallas_call(
        body, name="gather_rows", in_specs=[VM], out_specs=VM,
        out_shape=jax.ShapeDtypeStruct((N_DEV, 1, n), row.dtype),
        scratch_shapes=[pltpu.SemaphoreType.DMA((N_DEV - 1,)), pltpu.SemaphoreType.DMA((N_DEV - 1,)),
                        pltpu.SemaphoreType.DMA],
    )(row)


def _ag_copies(ins, outs, send_sems, recv_sems):
    x, y, c, chips = _place()
    s = 2 * x + y
    cps = []
    for a in range(len(ins)):
        H = ins[a].shape[0] // 2
        mine = ins[a].at[pl.ds(c * H, H)]
        cps += [_remote(mine, outs[a].at[2 * s + c], send_sems.at[6 * a + j], recv_sems.at[6 * a + j], (*chip, c))
                for j, chip in enumerate(chips)]
    return cps


def _ag_start(ins, outs, send_sems, recv_sems):
    for cp in _ag_copies(ins, outs, send_sems, recv_sems):
        cp.start()


def _ag_finish(ins, outs, send_sems, recv_sems):
    x, y, c, chips = _place()
    sibling = (x, y, 1 - c)
    slot = [2 * chip[0] + chip[1] for chip in chips]
    passed = []
    for a in range(len(ins)):
        for j in range(3):
            landed = outs[a].at[2 * slot[j] + c]
            _remote(landed, landed, send_sems.at[6 * a + j], recv_sems.at[6 * a + j], sibling).wait_recv()
            cp = _remote(landed, landed, send_sems.at[6 * a + 3 + j], recv_sems.at[6 * a + 3 + j], sibling)
            cp.start()
            passed.append(cp)
    for a in range(len(ins)):
        for j in range(3):
            got = outs[a].at[2 * slot[j] + 1 - c]
            _remote(got, got, send_sems.at[6 * a + 3 + j], recv_sems.at[6 * a + 3 + j], sibling).wait_recv()
    for cp in _ag_copies(ins, outs, send_sems, recv_sems) + passed:
        cp.wait_send()


def _ag_out_shapes(shards):
    return [jax.ShapeDtypeStruct((2 * N_CHIPS, a.shape[0] // 2, a.shape[1]), a.dtype) for a in shards]


def _ag_sems(n):
    return [pltpu.SemaphoreType.DMA((6 * n,)), pltpu.SemaphoreType.DMA((6 * n,))]


def _all_gather(shards):
    n = len(shards)

    def body(*refs):
        ins, outs, sems = refs[:n], refs[n:2 * n], refs[2 * n:]
        _ag_start(ins, outs, *sems)
        _ag_finish(ins, outs, *sems)

    return pl.pallas_call(
        body, name="ag_weights", in_specs=[ANY] * n, out_specs=[ANY] * n,
        out_shape=_ag_out_shapes(shards), scratch_shapes=_ag_sems(n),
    )(*shards)


def _rs_sibling_in(gps):
    n = len(gps)

    def body(*refs):
        ins, outs = refs[:n], refs[n:2 * n]
        send_sems, recv_sems = refs[2 * n:]
        x, y, c, _ = _place()
        cps = []
        for a in range(n):
            H = gps[a].shape[1] // 2
            cps += [_remote(ins[a].at[q, pl.ds((1 - c) * H, H)], outs[a].at[q], send_sems.at[N_CHIPS * a + q],
                            recv_sems.at[N_CHIPS * a + q], (x, y, 1 - c)) for q in range(N_CHIPS)]
        for cp in cps:
            cp.start()
        for cp in cps:
            cp.wait()

    return pl.pallas_call(
        body, name="rs_sibling_in", in_specs=[ANY] * n, out_specs=[ANY] * n,
        out_shape=[jax.ShapeDtypeStruct((N_CHIPS, g.shape[1] // 2, g.shape[2]), g.dtype) for g in gps],
        scratch_shapes=[pltpu.SemaphoreType.DMA((N_CHIPS * n,)), pltpu.SemaphoreType.DMA((N_CHIPS * n,))],
    )(*gps)


def _rs_add_own(name, gp, got, c_arr, tr=256):
    H, C = got.shape[1:]
    tr = _pick(H, tr, 8)
    nb = H // tr

    def body(c_ref, g_ref, r_ref, o_ref):
        o_ref[...] = (g_ref[...].astype(F32) + r_ref[...].astype(F32)).astype(o_ref.dtype)

    return pl.pallas_call(
        body, name="rs_add_own_" + name,
        grid_spec=pltpu.PrefetchScalarGridSpec(
            num_scalar_prefetch=1, grid=(N_CHIPS, nb),
            in_specs=[pl.BlockSpec((1, tr, C), lambda q, i, c_ref: (q, c_ref[0] * nb + i, 0)),
                      pl.BlockSpec((1, tr, C), lambda q, i, c_ref: (q, i, 0))],
            out_specs=pl.BlockSpec((1, tr, C), lambda q, i, c_ref: (q, i, 0))),
        out_shape=jax.ShapeDtypeStruct((N_CHIPS, H, C), gp.dtype),
        compiler_params=pltpu.CompilerParams(dimension_semantics=("arbitrary", "arbitrary")),
    )(c_arr, gp, got)


def _rs_chips(parts):
    n = len(parts)

    def body(*refs):
        ins, outs, sems = refs[:n], refs[n:2 * n], refs[2 * n:]
        _rsc_start(ins, outs, *sems)
        _rsc_finish(ins, outs, *sems)

    return pl.pallas_call(
        body, name="rs_chips", in_specs=[ANY] * n, out_specs=[ANY] * n,
        out_shape=[jax.ShapeDtypeStruct(p.shape, p.dtype) for p in parts], scratch_shapes=_rsc_sems(n),
    )(*parts)


def _rsc_copies(ins, outs, send_sems, recv_sems):
    x, y, c, chips = _place()
    s = 2 * x + y
    return [_remote(ins[a].at[2 * chip[0] + chip[1]], outs[a].at[s], send_sems.at[3 * a + j], recv_sems.at[3 * a + j],
                    (*chip, c)) for a in range(len(ins)) for j, chip in enumerate(chips)]


def _rsc_start(ins, outs, send_sems, recv_sems):
    for cp in _rsc_copies(ins, outs, send_sems, recv_sems):
        cp.start()


def _rsc_finish(ins, outs, send_sems, recv_sems):
    x, y, c, chips = _place()
    for cp in _rsc_copies(ins, outs, send_sems, recv_sems):
        cp.wait_send()
    for a in range(len(ins)):
        for j, chip in enumerate(chips):
            got = outs[a].at[2 * chip[0] + chip[1]]
            _remote(got, got, send_sems.at[3 * a + j], recv_sems.at[3 * a + j], (*chip, c)).wait_recv()


def _rsc_sems(n):
    return [pltpu.SemaphoreType.DMA((3 * n,)), pltpu.SemaphoreType.DMA((3 * n,))]


def _rs_add_chips(name, part, slots, sc_arr, tr=128):
    H, C = slots.shape[1:]
    tr = _pick(H, tr, 8)
    nb = H // tr

    def body(sc_ref, p_ref, s_ref, o_ref):
        acc = None
        for q in range(N_CHIPS):
            term = lax.cond(sc_ref[0] == q, lambda: p_ref[0].astype(F32), lambda q=q: s_ref[q].astype(F32))
            acc = term if acc is None else acc + term
        o_ref[...] = acc

    return pl.pallas_call(
        body, name="rs_add_chips_" + name,
        grid_spec=pltpu.PrefetchScalarGridSpec(
            num_scalar_prefetch=1, grid=(nb,),
            in_specs=[pl.BlockSpec((1, tr, C), lambda i, sc: (sc[0], i, 0)),
                      pl.BlockSpec((N_CHIPS, tr, C), lambda i, sc: (0, i, 0))],
            out_specs=pl.BlockSpec((tr, C), lambda i, sc: (sc[1] * nb + i, 0))),
        out_shape=jax.ShapeDtypeStruct((2 * H, C), F32),
        compiler_params=pltpu.CompilerParams(dimension_semantics=("arbitrary",)),
    )(sc_arr, part, slots)


def _rs_sibling_out(wholes):
    n = len(wholes)

    def body(*refs):
        ins, outs = refs[:n], refs[n:2 * n]
        send_sems, recv_sems = refs[2 * n:]
        x, y, c, _ = _place()
        cps = []
        for a in range(n):
            H = wholes[a].shape[0] // 2
            cps.append(_remote(ins[a].at[pl.ds(c * H, H)], outs[a].at[pl.ds(c * H, H)], send_sems.at[a], recv_sems.at[a],
                               (x, y, 1 - c)))
        for cp in cps:
            cp.start()
        for a in range(n):
            H = wholes[a].shape[0] // 2
            cps[a].wait_send()
            got = outs[a].at[pl.ds((1 - c) * H, H)]
            _remote(got, got, send_sems.at[a], recv_sems.at[a], (x, y, 1 - c)).wait_recv()

    return pl.pallas_call(
        body, name="rs_sibling_out", in_specs=[ANY] * n, out_specs=[ANY] * n,
        out_shape=[jax.ShapeDtypeStruct(w.shape, w.dtype) for w in wholes],
        input_output_aliases={a: a for a in range(n)},
        scratch_shapes=[pltpu.SemaphoreType.DMA((n,)), pltpu.SemaphoreType.DMA((n,))],
    )(*wholes)


def _adamw_math(w_, g_, m_, v_):
    m2 = ADAM_B1 * m_ + (1.0 - ADAM_B1) * g_
    v2 = ADAM_B2 * v_ + (1.0 - ADAM_B2) * (g_ * g_)
    m_hat = m2 / (1.0 - ADAM_B1 ** ADAM_STEP)
    v_hat = v2 / (1.0 - ADAM_B2 ** ADAM_STEP)
    return -ADAM_LR * (m_hat / (jnp.sqrt(v_hat) + ADAM_EPS) + ADAM_WD * w_), m2, v2


def _adamw(name, w, g, m, v):
    return _rows("adamw_" + name, _adamw_math, [w, g, m, v], [], [(w.shape[1], F32)] * 3,
                 tile=_pick(w.shape[0], 256, 8))


def _adamw_small(ws, gs, ms, vs):
    n = len(ws)

    def body(*refs):
        for i in range(n):
            res = _adamw_math(*[refs[j * n + i][...] for j in range(4)])
            for j in range(3):
                refs[(4 + j) * n + i][...] = res[j]

    out = pl.pallas_call(
        body, name="adamw_small",
        out_shape=[jax.ShapeDtypeStruct(w.shape, F32) for _ in range(3) for w in ws],
        compiler_params=pltpu.CompilerParams(vmem_limit_bytes=VMEM_LIMIT),
    )(*ws, *gs, *ms, *vs)
    return out[:n], out[n:2 * n], out[2 * n:]


def kernel(x, c, w_ada, b_ada, w_in, b_in, g_ln_v, b_ln_v, w_spatial, b_spatial, mu_shift, w0, w_decay_up, a0, w_aaa_up, w_gate_up, k_k, k_a, r_k, gn_gain, gn_bias, w_branch_a, w_branch_b, w_out, b_out, ln1_g, ln1_b, w_ff1, b_ff1, w_ff2, b_ff2, ln2_g, ln2_b, loss_target, m_w_ada, m_b_ada, m_w_in, m_b_in, m_g_ln_v, m_b_ln_v, m_w_spatial, m_b_spatial, m_mu_shift, m_w0, m_w_decay_up, m_a0, m_w_aaa_up, m_w_gate_up, m_k_k, m_k_a, m_r_k, m_gn_gain, m_gn_bias, m_w_branch_a, m_w_branch_b, m_w_out, m_b_out, m_ln1_g, m_ln1_b, m_w_ff1, m_b_ff1, m_w_ff2, m_b_ff2, m_ln2_g, m_ln2_b, v_w_ada, v_b_ada, v_w_in, v_b_in, v_g_ln_v, v_b_ln_v, v_w_spatial, v_b_spatial, v_mu_shift, v_w0, v_w_decay_up, v_a0, v_w_aaa_up, v_w_gate_up, v_k_k, v_k_a, v_r_k, v_gn_gain, v_gn_bias, v_w_branch_a, v_w_branch_b, v_w_out, v_b_out, v_ln1_g, v_ln1_b, v_w_ff1, v_b_ff1, v_w_ff2, v_b_ff2, v_ln2_g, v_ln2_b):
    args = dict(locals())
    local_shape = {n: _shard_shape(s, a) for n, s, a in SHARDED}
    local_shape.update(dict(SMALL))
    wts = {n: args[n].reshape(local_shape[n]) for n in WEIGHT_ORDER}
    mom = {n: args["m_" + n].reshape(local_shape[n]) for n in WEIGHT_ORDER}
    var = {n: args["v_" + n].reshape(local_shape[n]) for n in WEIGHT_ORDER}
    big = [n for n, _, _ in BIG]
    late, early = [n for n in big if n in LATE], [n for n in big if n not in LATE]
    chip = 2 * lax.axis_index("x") + lax.axis_index("y")
    c_arr = lax.axis_index("c").astype(jnp.int32).reshape(1)
    sc_arr = jnp.stack([chip, lax.axis_index("c")]).astype(jnp.int32)
    mine = {n: wts[n].astype(MXU_DT) for n in big}

    def whole(n, got):
        _, s, a = next(t for t in BIG if t[0] == n)
        g = lax.dynamic_update_slice(got.reshape((N_CHIPS,) + mine[n].shape), mine[n][None], (chip, 0, 0))
        return g if a == 1 else g.reshape(s)

    def sibling_sums(names, G, small):
        gps = [G[n] for n in names] + ([_pack_small_grads(G)] if small else [])
        names = names + (["small"] if small else [])
        return [_rs_add_own(n, g, r, c_arr) for n, g, r in zip(names, gps, _rs_sibling_in(gps))]

    small_w = _pack_small(wts, None, ROWS_SW, MXU_DT)
    gath = _all_gather([mine[n] for n in early] + [small_w])
    W = {n: wts[n] for n, _ in SMALL}
    W.update({n: whole(n, g) for n, g in zip(early, gath)})
    small_g = lax.dynamic_update_slice(gath[-1].reshape((N_CHIPS,) + small_w.shape), small_w[None], (chip, 0, 0))
    lora_q = [_unpack_small(small_g[q], False) for q in range(N_CHIPS)]
    for n, _, _ in LORAS:
        W[n] = jnp.concatenate([lora_q[q][n] for q in range(N_CHIPS)], axis=1)

    step = _local_step(x[0], c, loss_target[0], W)
    _, scan_in = next(step)
    y_scan, states, got = _scan_fwd(*scan_in, gather=[mine[n] for n in late])
    _, scan_in, G = step.send((y_scan, states, {n: whole(n, g) for n, g in zip(late, got)}))
    parts_late = sibling_sums(late, G, False)
    res = _scan_bwd(*scan_in, scatter=parts_late)
    try:
        step.send(tuple(res[:6]))
    except StopIteration as done:
        loss, grad_x, G, dmod = done.value
    loss = lax.psum(loss, MESH_AXES)
    rows = _gather_rows(jnp.concatenate([c, dmod], axis=1)).reshape(N_DEV, -1)
    g_w_ada, g_b_ada = _ada_grads(rows[:, :D_MODEL], rows[:, D_MODEL:], chip)
    tail = [n for n in early if n != "w_ada"]
    parts_tail = sibling_sums(tail, G, True)
    names = late + tail + ["small"]
    parts, slots = parts_late + parts_tail, list(res[6:]) + list(_rs_chips(parts_tail))
    segs = _rs_sibling_out([_rs_add_chips(n, p, s, sc_arr) for n, p, s in zip(names, parts, slots)])

    out = {}
    for n, g in list(zip(names[:-1], segs[:-1])) + [("w_ada", g_w_ada)]:
        out[n] = (g,) + tuple(_adamw(n, wts[n], g, mom[n], var[n]))
    small = [n for n, _, _ in LORAS] + [n for n, _ in SMALL]
    g_small = {**_unpack_small(segs[-1], True), "b_ada": g_b_ada}
    upd = _adamw_small(*[[t[n] for n in small] for t in (wts, g_small, mom, var)])
    for i, n in enumerate(small):
        out[n] = (g_small[n], upd[0][i], upd[1][i], upd[2][i])
    res = [loss, grad_x[None]]
    for k in range(4):
        res += [out[n][k].reshape(args[n].shape) for n in WEIGHT_ORDER]
    return tuple(res)
```
